```python
import math
import jax, jax.numpy as jnp
from jax import lax
import numpy as np

D_MODEL = 1024
BATCH = 16
SEQ = 2048
DEPTH = 2
DEC_BATCH = 128
DEC_SEQ = 1
PAST_LEN = 16384
PAGE_SIZE = 128

WINDOW = 128
HA_Q = 16
HA_KV = 4
HD_A = D_MODEL // HA_Q
NUM_BUCKETS = 32
MAX_DISTANCE = WINDOW
HB = 8
DK_B = (D_MODEL // 2) // HB
DV_B = D_MODEL // HB
RET_CHUNK = 128
D_INNER_C = D_MODEL
HD_C = 64
HC = D_INNER_C // HD_C
N_C = 128
G_C = 2
CONV_W = 4
CONV_DIM = D_INNER_C + 2 * G_C * N_C
SSD_CHUNK = 128
D_FF = 4 * D_MODEL
EPS = 1e-6
SPLIT_SIZES = (HA_Q * HD_A, HA_KV * HD_A, HA_KV * HD_A,
               HB * DK_B, HB * DK_B, HB * DV_B, HB * DV_B,
               D_INNER_C, CONV_DIM, HC, 3 * D_MODEL)
D_IN = sum(SPLIT_SIZES)

kernel_name = "hybrid_swa_retention_ssd_decoder_step"

F32 = jnp.float32


def _rmsnorm(x, w):
    xf = x.astype(F32)
    y = xf * lax.rsqrt(jnp.mean(xf * xf, axis=-1, keepdims=True) + EPS)
    return (y * w.astype(F32)).astype(x.dtype)


def _rms(x):
    xf = x.astype(F32)
    return xf * lax.rsqrt(jnp.mean(xf * xf, axis=-1, keepdims=True) + EPS)


def _split_proj(z):
    out = []
    start = 0
    for s in SPLIT_SIZES:
        out.append(z[..., start:start + s])
        start += s
    return out


def _t5_bucket(dist):
    max_exact = NUM_BUCKETS // 2
    n = jnp.maximum(dist, 0)
    nf = jnp.maximum(n, 1).astype(F32)
    large = max_exact + (jnp.log(nf / max_exact) / math.log(MAX_DISTANCE / max_exact)
                         * (NUM_BUCKETS - max_exact)).astype(jnp.int32)
    large = jnp.minimum(large, NUM_BUCKETS - 1)
    return jnp.where(n < max_exact, n, large)


def _window_attn(q, k, v, dist, valid, sinks, rel_table):
    G = HA_Q // HA_KV
    qg = q.reshape(q.shape[:-2] + (HA_KV, G, HD_A)).astype(F32)
    s = jnp.einsum('...qkgd,...skd->...kgqs', qg, k.astype(F32)) * (HD_A ** -0.5)
    bias = rel_table.astype(F32)[_t5_bucket(dist)]
    bias = jnp.moveaxis(bias, -1, -3)
    bias = bias.reshape((HA_KV, G) + bias.shape[-2:])
    s = jnp.where(valid[..., None, None, :, :], s + bias, -jnp.inf)
    sink = sinks.astype(F32).reshape(HA_KV, G, 1, 1)
    m = jnp.maximum(jnp.max(s, axis=-1, keepdims=True), sink)
    p = jnp.exp(s - m)
    p = p / (jnp.sum(p, axis=-1, keepdims=True) + jnp.exp(sink - m))
    o = jnp.einsum('...kgqs,...skd->...qkgd', p, v.astype(F32))
    return o.reshape(o.shape[:-3] + (HA_Q * HD_A,))


def _attn_prompt(q, k, v, sinks, rel_table):
    B, T = q.shape[0], q.shape[1]
    nb = T // WINDOW
    qb = q.reshape(B, nb, WINDOW, HA_Q, HD_A)
    pad = jnp.zeros((B, WINDOW, HA_KV, HD_A), k.dtype)
    kb = jnp.concatenate([pad, k], axis=1).reshape(B, nb + 1, WINDOW, HA_KV, HD_A)
    vb = jnp.concatenate([pad.astype(v.dtype), v], axis=1).reshape(B, nb + 1, WINDOW, HA_KV, HD_A)
    kk = jnp.concatenate([kb[:, :-1], kb[:, 1:]], axis=2)
    vv = jnp.concatenate([vb[:, :-1], vb[:, 1:]], axis=2)
    qi = jnp.arange(WINDOW)
    kj = jnp.arange(2 * WINDOW) - WINDOW
    dist = qi[:, None] - kj[None, :]
    kpos = jnp.arange(nb)[:, None] * WINDOW + kj[None, :]
    valid = (dist >= 0) & (dist < WINDOW) & (kpos[:, None, :] >= 0)
    o = _window_attn(qb, kk, vv, dist, valid, sinks, rel_table)
    return o.reshape(B, T, HA_Q * HD_A)


def _attn_sample(q, k_new, v_new, k_buf, v_buf, sinks, rel_table):
    Wb = k_buf.shape[1]
    T = q.shape[1]
    kk = jnp.concatenate([k_buf.astype(k_new.dtype), k_new], axis=1)
    vv = jnp.concatenate([v_buf.astype(v_new.dtype), v_new], axis=1)
    qpos = PAST_LEN + jnp.arange(T)
    kpos = PAST_LEN - Wb + jnp.arange(Wb + T)
    dist = qpos[:, None] - kpos[None, :]
    valid = (dist >= 0) & (dist < WINDOW)
    o = _window_attn(q, kk, vv, dist, valid, sinks, rel_table)
    return o, kk[:, T:], vv[:, T:]


def _xpos_rotate(x, pos):
    d = x.shape[-1]
    theta = 1.0 / (10000.0 ** jnp.linspace(0.0, 1.0, d // 2, dtype=F32))
    ang = pos.astype(F32)[:, None] * theta[None, :]
    sin = jnp.sin(ang)[:, None, :]
    cos = jnp.cos(ang)[:, None, :]
    x1, x2 = x[..., 0::2], x[..., 1::2]
    return jnp.stack([x1 * cos - x2 * sin, x1 * sin + x2 * cos], axis=-1).reshape(x.shape)


def _retention_chunk(S, q, k, v, log_gamma):
    L = q.shape[1]
    i = jnp.arange(L, dtype=F32)
    diff = i[:, None] - i[None, :]
    decay = jnp.where(diff >= 0, jnp.exp(log_gamma[:, None, None] * jnp.maximum(diff, 0.0)), 0.0)
    inner = jnp.einsum('blhd,bmhd->bhlm', q, k) * decay
    o = jnp.einsum('bhlm,bmhe->blhe', inner, v)
    q_dec = jnp.exp(log_gamma[None, :] * (i[:, None] + 1.0))
    o = o + jnp.einsum('blhd,bhde->blhe', q * q_dec[None, :, :, None], S)
    k_dec = jnp.exp(log_gamma[None, :] * (L - 1.0 - i[:, None]))
    S_new = (jnp.exp(log_gamma * L)[None, :, None, None] * S
             + jnp.einsum('blhd,blhe->bhde', k * k_dec[None, :, :, None], v))
    return S_new, o


def _ssd_chunk(h, x, dt, Bm, Cm, A):
    L = x.shape[1]
    acum = jnp.cumsum(dt * A, axis=1)
    seg = acum[:, :, None, :] - acum[:, None, :, :]
    causal = (jnp.arange(L)[:, None] >= jnp.arange(L)[None, :])[None, :, :, None]
    Lmat = jnp.where(causal, jnp.exp(jnp.where(causal, seg, 0.0)), 0.0)
    Bh = jnp.repeat(Bm, HC // G_C, axis=2)
    Ch = jnp.repeat(Cm, HC // G_C, axis=2)
    cb = jnp.einsum('bihn,bjhn->bijh', Ch, Bh)
    y = jnp.einsum('bijh,bjh,bjhp->bihp', cb * Lmat, dt, x)
    y = y + jnp.einsum('bihn,bhpn->bihp', Ch * jnp.exp(acum)[..., None], h)
    w = jnp.exp(acum[:, -1:, :] - acum) * dt
    h_new = (jnp.exp(acum[:, -1, :])[:, :, None, None] * h
             + jnp.einsum('bjh,bjhp,bjhn->bhpn', w, x, Bh))
    return h_new, y


def _scan_chunks(step, carry, xs, chunk):
    B, T = xs[0].shape[0], xs[0].shape[1]
    nc = T // chunk
    xs_c = tuple(jnp.moveaxis(a.reshape((B, nc, chunk) + a.shape[2:]), 1, 0) for a in xs)
    carry, ys = lax.scan(lambda c, xx: step(c, *xx), carry, xs_c)
    ys = jnp.moveaxis(ys, 0, 1)
    return carry, ys.reshape((B, T) + ys.shape[3:])


def _dwconv(xpad, w, b):
    T = xpad.shape[1] - (CONV_W - 1)
    out = b
    for i in range(CONV_W):
        out = out + xpad[:, i:i + T] * w[i]
    return out


def _layer(x, c, p, rel_table, state, is_prompt):
    (sinks, n1, n2, aw, ab, w_in, cw, cbias, dtb, alog, dsk, snw, w_out, w_up, w_down) = p
    B_, T = x.shape[0], x.shape[1]
    xdt = x.dtype
    mod = jnp.dot(jax.nn.silu(c), aw) + ab
    sh1, sc1, g1, sh2, sc2, g2 = jnp.split(mod[:, None, :], 6, axis=-1)
    h = _rmsnorm(x, n1) * (1 + sc1) + sh1
    aq, ak, av, bq, bk, bv, bg, cz, cxbc, cdt, gts = _split_proj(jnp.dot(h, w_in))

    aq = aq.reshape(B_, T, HA_Q, HD_A)
    ak = ak.reshape(B_, T, HA_KV, HD_A)
    av = av.reshape(B_, T, HA_KV, HD_A)
    if is_prompt:
        oa = _attn_prompt(aq, ak, av, sinks, rel_table)
        kbuf, vbuf = ak[:, T - WINDOW:], av[:, T - WINDOW:]
    else:
        oa, kbuf, vbuf = _attn_sample(aq, ak, av, state[0], state[1], sinks, rel_table)

    pos = (0 if is_prompt else PAST_LEN) + jnp.arange(T)
    log_gamma = jnp.log(1.0 - 2.0 ** (-5.0 - jnp.arange(HB, dtype=F32)))
    qb = _xpos_rotate(bq.reshape(B_, T, HB, DK_B).astype(F32), pos)
    kb = _xpos_rotate(bk.reshape(B_, T, HB, DK_B).astype(F32) * (DK_B ** -0.5), pos)
    vb = bv.reshape(B_, T, HB, DV_B).astype(F32)
    ret_step = lambda S, q, k, v: _retention_chunk(S, q, k, v, log_gamma)
    if is_prompt:
        S_ret, ob = _scan_chunks(ret_step, jnp.zeros((B_, HB, DK_B, DV_B), F32), (qb, kb, vb), RET_CHUNK)
    else:
        S_ret, ob = ret_step(state[2].astype(F32), qb, kb, vb)
    ob = jax.nn.silu(bg.astype(F32)) * _rms(ob).reshape(B_, T, HB * DV_B)

    if is_prompt:
        hist = jnp.zeros((B_, CONV_W - 1, CONV_DIM), cxbc.dtype)
    else:
        hist = state[4].astype(cxbc.dtype)
    xpad = jnp.concatenate([hist, cxbc], axis=1)
    conv_new = xpad[:, T:]
    xbc = jax.nn.silu(_dwconv(xpad, cw, cbias).astype(F32))
    xc = xbc[..., :D_INNER_C].reshape(B_, T, HC, HD_C)
    Bc = xbc[..., D_INNER_C:D_INNER_C + G_C * N_C].reshape(B_, T, G_C, N_C)
    Cc = xbc[..., D_INNER_C + G_C * N_C:].reshape(B_, T, G_C, N_C)
    dtc = jax.nn.softplus(cdt.astype(F32) + dtb.astype(F32))
    A = -jnp.exp(alog.astype(F32))
    ssd_step = lambda hs, xx, dd, bb, cc: _ssd_chunk(hs, xx, dd, bb, cc, A)
    if is_prompt:
        h_ssm, yc = _scan_chunks(ssd_step, jnp.zeros((B_, HC, HD_C, N_C), F32), (xc, dtc, Bc, Cc), SSD_CHUNK)
    else:
        h_ssm, yc = ssd_step(state[3].astype(F32), xc, dtc, Bc, Cc)
    yc = yc + dsk.astype(F32)[:, None] * xc
    yc = yc.reshape(B_, T, D_INNER_C) * jax.nn.silu(cz.astype(F32))
    oc = _rms(yc.reshape(B_, T, G_C, D_INNER_C // G_C)).reshape(B_, T, D_INNER_C) * snw.astype(F32)

    ga, gb, gc = jnp.split(jax.nn.sigmoid(gts.astype(F32)), 3, axis=-1)
    mix = (ga * oa + gb * ob + gc * oc).astype(xdt)
    x = x + g1 * jnp.dot(mix, w_out)

    h2 = _rmsnorm(x, n2) * (1 + sc2) + sh2
    x = x + g2 * jnp.dot(jnp.square(jax.nn.relu(jnp.dot(h2, w_up))), w_down)

    if is_prompt:
        dts = (xdt, xdt, xdt, xdt, xdt)
    else:
        dts = tuple(s.dtype for s in state)
    new_state = (kbuf.astype(dts[0]), vbuf.astype(dts[1]), S_ret.astype(dts[2]),
                 h_ssm.astype(dts[3]), conv_new.astype(dts[4]))
    return x, new_state


def setup_inputs(seed: int = 0) -> dict:
    key = jax.random.key(seed)
    ks = jax.random.split(key, 32)
    nrm = lambda k, shape, s: jax.random.normal(k, shape, F32) * s
    win_buf = min(WINDOW, PAST_LEN)
    dt0 = jnp.exp(jax.random.uniform(ks[20], (DEPTH, HC), F32, math.log(1e-3), math.log(1e-1)))
    return {
        "x_prompt": nrm(ks[0], (BATCH, SEQ, D_MODEL), 1.0),
        "x_sample": nrm(ks[1], (DEC_BATCH, DEC_SEQ, D_MODEL), 1.0),
        "cache_win_k": nrm(ks[2], (DEPTH, DEC_BATCH, win_buf, HA_KV, HD_A), 1.0),
        "cache_win_v": nrm(ks[3], (DEPTH, DEC_BATCH, win_buf, HA_KV, HD_A), 1.0),
        "state_ret": nrm(ks[4], (DEPTH, DEC_BATCH, HB, DK_B, DV_B), 0.1),
        "state_ssm": nrm(ks[5], (DEPTH, DEC_BATCH, HC, HD_C, N_C), 0.1),
        "state_conv": nrm(ks[6], (DEPTH, DEC_BATCH, CONV_W - 1, CONV_DIM), 1.0),
        "c_prompt": nrm(ks[7], (BATCH, D_MODEL), 1.0),
        "c_sample": nrm(ks[8], (DEC_BATCH, D_MODEL), 1.0),
        "rel_bias_table": nrm(ks[9], (NUM_BUCKETS, HA_Q), 0.5),
        "attn_sinks": nrm(ks[10], (DEPTH, HA_Q), 1.0),
        "norm1_w": 1.0 + nrm(ks[11], (DEPTH, D_MODEL), 0.01),
        "norm2_w": 1.0 + nrm(ks[12], (DEPTH, D_MODEL), 0.01),
        "ada_w": nrm(ks[13], (DEPTH, D_MODEL, 6 * D_MODEL), 0.5 * D_MODEL ** -0.5),
        "ada_b": nrm(ks[14], (DEPTH, 6 * D_MODEL), 0.02),
        "w_in": nrm(ks[15], (DEPTH, D_MODEL, D_IN), D_MODEL ** -0.5),
        "conv_w": nrm(ks[16], (DEPTH, CONV_W, CONV_DIM), CONV_W ** -0.5),
        "conv_b": nrm(ks[17], (DEPTH, CONV_DIM), 0.02),
        "dt_bias": dt0 + jnp.log(-jnp.expm1(-dt0)),
        "A_log": jnp.log(jax.random.uniform(ks[18], (DEPTH, HC), F32, 1.0, 16.0)),
        "D_skip": 1.0 + nrm(ks[19], (DEPTH, HC), 0.1),
        "ssm_norm_w": 1.0 + nrm(ks[21], (DEPTH, D_INNER_C), 0.01),
        "w_out": nrm(ks[22], (DEPTH, D_MODEL, D_MODEL), D_MODEL ** -0.5),
        "w_up": nrm(ks[23], (DEPTH, D_MODEL, D_FF), D_MODEL ** -0.5),
        "w_down": nrm(ks[24], (DEPTH, D_FF, D_MODEL), D_FF ** -0.5),
        "final_norm_w": 1.0 + nrm(ks[25], (D_MODEL,), 0.01),
    }


def reference(x_prompt, x_sample, cache_win_k, cache_win_v, state_ret, state_ssm, state_conv,
              c_prompt, c_sample, rel_bias_table, attn_sinks, norm1_w, norm2_w, ada_w, ada_b,
              w_in, conv_w, conv_b, dt_bias, A_log, D_skip, ssm_norm_w, w_out, w_up, w_down,
              final_norm_w):
    xp, xs = x_prompt, x_sample
    sp, ss = [], []
    for l in range(DEPTH):
        p = (attn_sinks[l], norm1_w[l], norm2_w[l], ada_w[l], ada_b[l], w_in[l], conv_w[l], conv_b[l],
             dt_bias[l], A_log[l], D_skip[l], ssm_norm_w[l], w_out[l], w_up[l], w_down[l])
        xp, st_p = _layer(xp, c_prompt, p, rel_bias_table, None, True)
        xs, st_s = _layer(xs, c_sample, p, rel_bias_table,
                          (cache_win_k[l], cache_win_v[l], state_ret[l], state_ssm[l], state_conv[l]), False)
        sp.append(st_p)
        ss.append(st_s)
    stk = lambda sts, i: jnp.stack([s[i] for s in sts], axis=0)
    y_prompt = _rmsnorm(xp, final_norm_w)
    y_sample = _rmsnorm(xs, final_norm_w)
    return (y_prompt, y_sample,
            stk(sp, 0), stk(sp, 1), stk(sp, 2), stk(sp, 3), stk(sp, 4),
            stk(ss, 0), stk(ss, 1), stk(ss, 2), stk(ss, 3), stk(ss, 4))
```

```python
import functools
import math

import numpy as np
import jax
import jax.numpy as jnp
from jax import lax
from jax.experimental import pallas as pl
from jax.experimental.pallas import tpu as pltpu

F32 = jnp.float32
BF16 = jnp.bfloat16

D_MODEL = 1024
DEPTH = 2
PAST_LEN = 16384
WINDOW = 128
HA_Q = 16
HA_KV = 4
HD_A = 64
NUM_BUCKETS = 32
MAX_DISTANCE = WINDOW
HB = 8
DK_B = 64
DV_B = 128
HC = 16
HD_C = 64
N_C = 128
G_C = 2
CONV_W = 4
CONV_DIM = D_MODEL + 2 * G_C * N_C
D_FF = 4 * D_MODEL
EPS = 1e-6
NEG = -1e30

O_AQ, O_AK, O_AV = 0, 1024, 1280
O_BQ, O_BK, O_BV, O_BG = 1536, 2048, 2560, 3584
O_CZ, O_CXBC, O_CDT, O_GTS = 4608, 5632, 7168, 7184
SEG_A = 2560
SEG_B = 4096
SEG_C = 3712
C_DT = 2560
C_GC = 2688

VMEM_LIMIT_V7X = 56 * 1024 * 1024
ROWS_PER_SAMPLE_STEP = 8


def _cparams(n_axes):
    return pltpu.CompilerParams(dimension_semantics=("arbitrary",) * n_axes,
                                vmem_limit_bytes=VMEM_LIMIT_V7X)


def _dot(a, b):
    return jnp.dot(a, b, preferred_element_type=F32)


def _dot_nt(a, b):
    return lax.dot_general(a, b, (((1,), (1,)), ((), ())), preferred_element_type=F32)


def _dot_tn(a, b):
    return lax.dot_general(a, b, (((0,), (0,)), ((), ())), preferred_element_type=F32)


def _sigmoid(x):
    return 0.5 * (jnp.tanh(0.5 * x) + 1.0)


def _silu(x):
    return x * _sigmoid(x)


def _softplus(x):
    return jnp.maximum(x, 0.0) + jnp.log1p(jnp.exp(-jnp.abs(x)))


def _modnorm(x, nw, sc, sh):
    ms = jnp.mean(x * x, axis=-1, keepdims=True)
    return (x * lax.rsqrt(ms + EPS) * nw) * (1.0 + sc) + sh


def _rms_lanes(x):
    ms = jnp.mean(x * x, axis=-1, keepdims=True)
    return x * lax.rsqrt(ms + EPS)


def _pairswap(x):
    ax = x.ndim - 1
    n = x.shape[ax]
    lane = lax.broadcasted_iota(jnp.int32, x.shape, ax)
    nxt = pltpu.roll(x, n - 1, ax)
    prv = pltpu.roll(x, 1, ax)
    return jnp.where((lane & 1) == 0, nxt, prv)


def _split3(x):
    hi = x.astype(BF16)
    r1 = x - hi.astype(F32)
    mid = r1.astype(BF16)
    lo = (r1 - mid.astype(F32)).astype(BF16)
    return hi, mid, lo


def _gammas():
    return 1.0 - 2.0 ** (-5.0 - np.arange(HB, dtype=np.float64))


def _rot_tables(pos):
    theta = 1.0 / (10000.0 ** np.linspace(0.0, 1.0, DK_B // 2))
    ang = np.asarray(pos, np.float64)[:, None] * theta[None, :]
    cos = np.repeat(np.cos(ang), 2, axis=1)
    sin = np.repeat(np.sin(ang), 2, axis=1)
    sin[:, 0::2] *= -1.0
    return (np.tile(cos, (1, HB)).astype(np.float32), np.tile(sin, (1, HB)).astype(np.float32))


def _ret_tables():
    g = _gammas()
    L = WINDOW
    i = np.arange(L, dtype=np.float64)
    diff = i[:, None] - i[None, :]
    dm = np.where(diff >= 0, g[:, None, None] ** np.maximum(diff, 0.0), 0.0)
    qdec = np.repeat(g[None, :] ** (i[:, None] + 1.0), DK_B, axis=1)
    kdec = np.repeat(g[None, :] ** (L - 1.0 - i[:, None]), DK_B, axis=1)
    return dm.astype(np.float32), qdec.astype(np.float32), kdec.astype(np.float32)


def _t5_bias_by_dist(rel_table):
    max_exact = NUM_BUCKETS // 2
    n = jnp.arange(WINDOW)
    nf = jnp.maximum(n, 1).astype(F32)
    large = max_exact + (jnp.log(nf / max_exact) / math.log(MAX_DISTANCE / max_exact)
                         * (NUM_BUCKETS - max_exact)).astype(jnp.int32)
    large = jnp.minimum(large, NUM_BUCKETS - 1)
    bucket = jnp.where(n < max_exact, n, large)
    return rel_table.astype(F32)[bucket]


def _ada_kernel(c_ref, w_ref, b_ref, o_ref):
    s = _silu(c_ref[...])
    o_ref[0] = _dot(s.astype(BF16), w_ref[0].astype(BF16)) + b_ref[0]


def _ada_call(c_all, ada_w, ada_b):
    n = c_all.shape[0]
    nb = 6
    return pl.pallas_call(
        _ada_kernel,
        grid=(DEPTH, nb),
        in_specs=[pl.BlockSpec((n, D_MODEL), lambda l, j: (0, 0)),
                  pl.BlockSpec((1, D_MODEL, D_MODEL), lambda l, j: (l, 0, j)),
                  pl.BlockSpec((1, 1, D_MODEL), lambda l, j: (l, 0, j))],
        out_specs=pl.BlockSpec((1, n, D_MODEL), lambda l, j: (l, 0, j)),
        out_shape=jax.ShapeDtypeStruct((DEPTH, n, 6 * D_MODEL), F32),
        compiler_params=_cparams(2),
        name="ada_mod",
    )(c_all, ada_w, ada_b.reshape(DEPTH, 1, 6 * D_MODEL))


def _pa_kernel(sinks_ref, x_ref, mod_ref, n1_ref, w_ref, bp_ref, bc_ref,
               mix_ref, ko_ref, vo_ref, z_ref, kprev_ref, vprev_ref, pen_ref, *, tt):
    t = pl.program_id(1)

    @pl.when(t == 0)
    def _():
        kprev_ref[...] = jnp.zeros_like(kprev_ref)
        vprev_ref[...] = jnp.zeros_like(vprev_ref)
        pen_ref[...] = jnp.full(pen_ref.shape, NEG, F32)

    mod = mod_ref[0]
    h = _modnorm(x_ref[0], n1_ref[...], mod[:, D_MODEL:2 * D_MODEL], mod[:, 0:D_MODEL])
    z_ref[...] = _dot(h.astype(BF16), w_ref[...])

    def chunk(c, carry):
        r0 = pl.multiple_of(c * WINDOW, WINDOW)
        rows = pl.ds(r0, WINDOW)
        prev_pen = pen_ref[0:1, :]
        q = (z_ref[rows, 0:1024] * (HD_A ** -0.5)).astype(BF16)
        kc = z_ref[rows, 1024:1280]
        vc = z_ref[rows, 1280:1536]
        kcb = kc.astype(BF16)
        vcb = vc.astype(BF16)
        kp = kprev_ref[...].astype(BF16)
        vp = vprev_ref[...].astype(BF16)
        outs = []
        for hh in range(HA_Q):
            g = hh // (HA_Q // HA_KV)
            gs = slice(HD_A * g, HD_A * (g + 1))
            qh = q[:, HD_A * hh:HD_A * (hh + 1)]
            sp = _dot_nt(qh, kp[:, gs]) + bp_ref[hh] + prev_pen
            sc = _dot_nt(qh, kcb[:, gs]) + bc_ref[hh]
            sink = sinks_ref[hh]
            m = jnp.maximum(jnp.max(jnp.maximum(sp, sc), axis=-1, keepdims=True), sink)
            pp = jnp.exp(sp - m)
            pc = jnp.exp(sc - m)
            den = jnp.sum(pp + pc, axis=-1, keepdims=True) + jnp.exp(sink - m)
            o = _dot(pp.astype(BF16), vp[:, gs]) + _dot(pc.astype(BF16), vcb[:, gs])
            outs.append(o / den)
        oa = jnp.concatenate(outs, axis=1)
        ga = _sigmoid(z_ref[rows, 1536:2560])
        mix_ref[0, rows, :] = ga * oa
        kprev_ref[...] = kc
        vprev_ref[...] = vc
        pen_ref[...] = jnp.zeros_like(pen_ref)
        ko_ref[0] = kc
        vo_ref[0] = vc
        return carry

    lax.fori_loop(0, tt // WINDOW, chunk, 0)


def _pa_call(x, mod3, n1, wa, bias_prev, bias_cur, sinks, tt):
    B, T, _ = x.shape
    kern = functools.partial(_pa_kernel, tt=tt)
    return pl.pallas_call(
        kern,
        grid=(B, T // tt),
        in_specs=[pl.BlockSpec(memory_space=pltpu.SMEM),
                  pl.BlockSpec((1, tt, D_MODEL), lambda b, t: (b, t, 0)),
                  pl.BlockSpec((1, 1, 6 * D_MODEL), lambda b, t: (b, 0, 0)),
                  pl.BlockSpec((1, D_MODEL), lambda b, t: (0, 0)),
                  pl.BlockSpec((D_MODEL, SEG_A), lambda b, t: (0, 0)),
                  pl.BlockSpec((HA_Q, WINDOW, WINDOW), lambda b, t: (0, 0, 0)),
                  pl.BlockSpec((HA_Q, WINDOW, WINDOW), lambda b, t: (0, 0, 0))],
        out_specs=[pl.BlockSpec((1, tt, D_MODEL), lambda b, t: (b, t, 0)),
                   pl.BlockSpec((1, WINDOW, HA_KV * HD_A), lambda b, t: (b, 0, 0)),
                   pl.BlockSpec((1, WINDOW, HA_KV * HD_A), lambda b, t: (b, 0, 0))],
        out_shape=[jax.ShapeDtypeStruct((B, T, D_MODEL), F32),
                   jax.ShapeDtypeStruct((B, WINDOW, HA_KV * HD_A), F32),
                   jax.ShapeDtypeStruct((B, WINDOW, HA_KV * HD_A), F32)],
        scratch_shapes=[pltpu.VMEM((tt, SEG_A), F32),
                        pltpu.VMEM((WINDOW, HA_KV * HD_A), F32),
                        pltpu.VMEM((WINDOW, HA_KV * HD_A), F32),
                        pltpu.VMEM((8, WINDOW), F32)],
        compiler_params=_cparams(2),
        name="prompt_attn",
    )(sinks, x, mod3, n1, wa, bias_prev, bias_cur)


def _pb_kernel(x_ref, mod_ref, n1_ref, w_ref, cos_ref, sin_ref, qdec_ref, kdec_ref, dm_ref, mixa_ref,
               mix_ref, so_ref, z_ref, s_ref, *, tt, glast):
    t = pl.program_id(1)

    @pl.when(t == 0)
    def _():
        s_ref[...] = jnp.zeros_like(s_ref)

    mod = mod_ref[0]
    h = _modnorm(x_ref[0], n1_ref[...], mod[:, D_MODEL:2 * D_MODEL], mod[:, 0:D_MODEL])
    z_ref[...] = _dot(h.astype(BF16), w_ref[...])

    def chunk(c, carry):
        r0 = pl.multiple_of(c * WINDOW, WINDOW)
        rows = pl.ds(r0, WINDOW)
        cos = cos_ref[rows, :]
        sin = sin_ref[rows, :]
        qf = z_ref[rows, 0:512]
        kf = z_ref[rows, 512:1024]
        qr = qf * cos + _pairswap(qf) * sin
        kr = (kf * cos + _pairswap(kf) * sin) * (DK_B ** -0.5)
        qb = qr.astype(BF16)
        kb = kr.astype(BF16)
        qd = (qr * qdec_ref[...]).astype(BF16)
        kd = (kr * kdec_ref[...]).astype(BF16)
        vb = z_ref[rows, 1024:2048].astype(BF16)
        outs = []
        for hh in range(HB):
            ks = slice(DK_B * hh, DK_B * (hh + 1))
            vh = vb[:, DV_B * hh:DV_B * (hh + 1)]
            inner = _dot_nt(qb[:, ks], kb[:, ks]) * dm_ref[hh]
            s_old = s_ref[ks, :]
            o = _dot(inner.astype(BF16), vh) + _dot(qd[:, ks], s_old.astype(BF16))
            s_ref[ks, :] = glast[hh] * s_old + _dot_tn(kd[:, ks], vh)
            outs.append(_rms_lanes(o))
        ob = jnp.concatenate(outs, axis=1)
        bg = z_ref[rows, 2048:3072]
        gb = z_ref[rows, 3072:4096]
        mix_ref[0, rows, :] = mixa_ref[0, rows, :] + _sigmoid(gb) * (_silu(bg) * ob)
        return carry

    lax.fori_loop(0, tt // WINDOW, chunk, 0)
    so_ref[0] = s_ref[...]


def _pb_call(x, mod3, n1, wb, mixa, tt):
    B, T, _ = x.shape
    cos, sin = _rot_tables(np.arange(T))
    dm, qdec, kdec = _ret_tables()
    glast = tuple(float(v) for v in (_gammas() ** WINDOW))
    kern = functools.partial(_pb_kernel, tt=tt, glast=glast)
    full2 = lambda b, t: (0, 0)
    return pl.pallas_call(
        kern,
        grid=(B, T // tt),
        in_specs=[pl.BlockSpec((1, tt, D_MODEL), lambda b, t: (b, t, 0)),
                  pl.BlockSpec((1, 1, 6 * D_MODEL), lambda b, t: (b, 0, 0)),
                  pl.BlockSpec((1, D_MODEL), full2),
                  pl.BlockSpec((D_MODEL, SEG_B), full2),
                  pl.BlockSpec((tt, HB * DK_B), lambda b, t: (t, 0)),
                  pl.BlockSpec((tt, HB * DK_B), lambda b, t: (t, 0)),
                  pl.BlockSpec((WINDOW, HB * DK_B), full2),
                  pl.BlockSpec((WINDOW, HB * DK_B), full2),
                  pl.BlockSpec((HB, WINDOW, WINDOW), lambda b, t: (0, 0, 0)),
                  pl.BlockSpec((1, tt, D_MODEL), lambda b, t: (b, t, 0))],
        out_specs=[pl.BlockSpec((1, tt, D_MODEL), lambda b, t: (b, t, 0)),
                   pl.BlockSpec((1, HB * DK_B, DV_B), lambda b, t: (b, 0, 0))],
        out_shape=[jax.ShapeDtypeStruct((B, T, D_MODEL), F32),
                   jax.ShapeDtypeStruct((B, HB * DK_B, DV_B), F32)],
        scratch_shapes=[pltpu.VMEM((tt, SEG_B), F32),
                        pltpu.VMEM((HB * DK_B, DV_B), F32)],
        compiler_params=_cparams(2),
        name="prompt_retention",
    )(x, mod3, n1, wb, jnp.asarray(cos), jnp.asarray(sin), jnp.asarray(qdec), jnp.asarray(kdec),
      jnp.asarray(dm), mixa)


def _pc_kernel(x_ref, mod_ref, n1_ref, w_ref, cw_ref, cb_ref, dtb_ref, a_ref, dsk_ref, snw_ref, tri_ref,
               mixab_ref, wo_ref,
               xo_ref, ho_ref, co_ref,
               z_ref, xbuf_ref, xbc_ref, dt_ref, hst_ref, mixs_ref, *, tt):
    t = pl.program_id(1)

    @pl.when(t == 0)
    def _():
        xbuf_ref[0:8, :] = jnp.zeros((8, CONV_DIM), F32)
        hst_ref[...] = jnp.zeros_like(hst_ref)

    x = x_ref[0]
    mod = mod_ref[0]
    h = _modnorm(x, n1_ref[...], mod[:, D_MODEL:2 * D_MODEL], mod[:, 0:D_MODEL])
    z_ref[...] = _dot(h.astype(BF16), w_ref[...])

    xbuf_ref[8:8 + tt, :] = z_ref[:, 1024:2560]
    acc = cb_ref[...]
    for i in range(CONV_W):
        acc = acc + xbuf_ref[5 + i:5 + i + tt, :] * cw_ref[i:i + 1, :]
    xbc_ref[...] = _silu(acc)
    co_ref[0] = xbuf_ref[tt + 5:tt + 8, :]
    xbuf_ref[0:8, :] = xbuf_ref[tt:tt + 8, :]
    dt_ref[...] = _softplus(z_ref[:, C_DT:C_DT + 128] + dtb_ref[...])

    ii = lax.broadcasted_iota(jnp.int32, (WINDOW, WINDOW), 0)
    jj = lax.broadcasted_iota(jnp.int32, (WINDOW, WINDOW), 1)
    causal = ii >= jj
    hpg = HC // G_C

    def chunk(c, carry):
        r0 = pl.multiple_of(c * WINDOW, WINDOW)
        rows = pl.ds(r0, WINDOW)
        xc = xbc_ref[rows, 0:1024]
        bmat = xbc_ref[rows, 1024:1280]
        cmat = xbc_ref[rows, 1280:1536]
        dtc = dt_ref[rows, :]
        acum = jnp.dot(tri_ref[...], dtc * a_ref[...], precision=lax.Precision.HIGHEST,
                       preferred_element_type=F32)
        acum_t = acum.T
        dt_t = dtc.T
        x_t = xc.T
        xb = xc.astype(BF16)
        bb = bmat.astype(BF16)
        cb16 = cmat.astype(BF16)
        ys = []
        for g in range(G_C):
            ns = slice(N_C * g, N_C * (g + 1))
            cbg = _dot_nt(cb16[:, ns], bb[:, ns])
            for hh in range(hpg * g, hpg * (g + 1)):
                ps = slice(HD_C * hh, HD_C * (hh + 1))
                colb = jnp.broadcast_to(acum[:, hh:hh + 1], (WINDOW, WINDOW))
                row = acum_t[hh:hh + 1, :]
                dtrow = dt_t[hh:hh + 1, :]
                lmat = jnp.exp(jnp.where(causal, colb - row, NEG))
                m = cbg * lmat * dtrow
                hs = hst_ref[ps, :]
                cs = cmat[:, ns] * jnp.exp(colb)
                y = _dot(m.astype(BF16), xb[:, ps]) + _dot_nt(cs.astype(BF16), hs.astype(BF16))
                last = colb[WINDOW - 1:WINDOW, :]
                wrow = jnp.exp(last - row) * dtrow
                xw = (x_t[ps, :] * wrow).astype(BF16)
                hst_ref[ps, :] = jnp.exp(last) * hs + _dot(xw, bb[:, ns])
                ys.append(y)
        y = jnp.concatenate(ys, axis=1) + dsk_ref[...] * xc
        yc = y * _silu(z_ref[rows, 0:1024])
        gw = D_MODEL // G_C
        oc = jnp.concatenate([_rms_lanes(yc[:, gw * g:gw * (g + 1)]) for g in range(G_C)], axis=1) * snw_ref[...]
        gc = _sigmoid(z_ref[rows, C_GC:C_GC + 1024])
        mixs_ref[rows, :] = (mixab_ref[0, rows, :] + gc * oc).astype(BF16)
        return carry

    lax.fori_loop(0, tt // WINDOW, chunk, 0)
    ho_ref[0] = hst_ref[...]
    g1 = mod[:, 2 * D_MODEL:3 * D_MODEL]
    xo_ref[0] = x + g1 * _dot(mixs_ref[...], wo_ref[...])


def _pc_call(x, mod3, n1, wc, cw, cb, dtb_pad, a_pad, dsk_full, snw, mixab, wo, tt):
    B, T, _ = x.shape
    tri = jnp.asarray(np.tril(np.ones((WINDOW, WINDOW), np.float32)))
    kern = functools.partial(_pc_kernel, tt=tt)
    full2 = lambda b, t: (0, 0)
    return pl.pallas_call(
        kern,
        grid=(B, T // tt),
        in_specs=[pl.BlockSpec((1, tt, D_MODEL), lambda b, t: (b, t, 0)),
                  pl.BlockSpec((1, 1, 6 * D_MODEL), lambda b, t: (b, 0, 0)),
                  pl.BlockSpec((1, D_MODEL), full2),
                  pl.BlockSpec((D_MODEL, SEG_C), full2),
                  pl.BlockSpec((CONV_W, CONV_DIM), full2),
                  pl.BlockSpec((1, CONV_DIM), full2),
                  pl.BlockSpec((1, 128), full2),
                  pl.BlockSpec((1, 128), full2),
                  pl.BlockSpec((1, D_MODEL), full2),
                  pl.BlockSpec((1, D_MODEL), full2),
                  pl.BlockSpec((WINDOW, WINDOW), full2),
                  pl.BlockSpec((1, tt, D_MODEL), lambda b, t: (b, t, 0)),
                  pl.BlockSpec((D_MODEL, D_MODEL), full2)],
        out_specs=[pl.BlockSpec((1, tt, D_MODEL), lambda b, t: (b, t, 0)),
                   pl.BlockSpec((1, HC * HD_C, N_C), lambda b, t: (b, 0, 0)),
                   pl.BlockSpec((1, CONV_W - 1, CONV_DIM), lambda b, t: (b, 0, 0))],
        out_shape=[jax.ShapeDtypeStruct((B, T, D_MODEL), F32),
                   jax.ShapeDtypeStruct((B, HC * HD_C, N_C), F32),
                   jax.ShapeDtypeStruct((B, CONV_W - 1, CONV_DIM), F32)],
        scratch_shapes=[pltpu.VMEM((tt, SEG_C), F32),
                        pltpu.VMEM((tt + 8, CONV_DIM), F32),
                        pltpu.VMEM((tt, CONV_DIM), F32),
                        pltpu.VMEM((tt, 128), F32),
                        pltpu.VMEM((HC * HD_C, N_C), F32),
                        pltpu.VMEM((tt, D_MODEL), BF16)],
        compiler_params=_cparams(2),
        name="prompt_ssd_out",
    )(x, mod3, n1, wc, cw, cb, dtb_pad, a_pad, dsk_full, snw, tri, mixab, wo)


def _mlp_kernel(x_ref, sh_ref, sc_ref, g_ref, n2_ref, wu_ref, wd_ref, fn_ref, o_ref, h_ref, acc_ref,
                *, nf, final, per_row):
    f = pl.program_id(1)
    rd = (lambda r: r[...]) if per_row else (lambda r: r[0])

    @pl.when(f == 0)
    def _():
        h_ref[...] = _modnorm(x_ref[...], n2_ref[...], rd(sc_ref), rd(sh_ref)).astype(BF16)
        acc_ref[...] = jnp.zeros_like(acc_ref)

    u = _dot(h_ref[...], wu_ref[0])
    u = jnp.square(jnp.maximum(u, 0.0)).astype(BF16)
    acc_ref[...] += _dot(u, wd_ref[0])

    @pl.when(f == nf - 1)
    def _():
        y = x_ref[...] + rd(g_ref) * acc_ref[...]
        if final:
            y = _rms_lanes(y) * fn_ref[...]
        o_ref[...] = y


def _mlp_call(x2, mod, n2, wu, wd, fn, layer, tm, tf, rows_per_mod, final):
    M = x2.shape[0]
    nf = D_FF // tf
    per_row = rows_per_mod == 1
    if per_row:
        mspec = lambda j: pl.BlockSpec((tm, D_MODEL), lambda m, f: (m, j))
    else:
        mspec = lambda j: pl.BlockSpec((1, 1, D_MODEL), lambda m, f: ((m * tm) // rows_per_mod, 0, j))
    kern = functools.partial(_mlp_kernel, nf=nf, final=final, per_row=per_row)
    return pl.pallas_call(
        kern,
        grid=(M // tm, nf),
        in_specs=[pl.BlockSpec((tm, D_MODEL), lambda m, f: (m, 0)),
                  mspec(3), mspec(4), mspec(5),
                  pl.BlockSpec((1, D_MODEL), lambda m, f: (0, 0)),
                  pl.BlockSpec((1, D_MODEL, tf), lambda m, f: (layer, 0, f)),
                  pl.BlockSpec((1, tf, D_MODEL), lambda m, f: (layer, f, 0)),
                  pl.BlockSpec((1, D_MODEL), lambda m, f: (0, 0))],
        out_specs=pl.BlockSpec((tm, D_MODEL), lambda m, f: (m, 0)),
        out_shape=jax.ShapeDtypeStruct((M, D_MODEL), F32),
        scratch_shapes=[pltpu.VMEM((tm, D_MODEL), BF16),
                        pltpu.VMEM((tm, D_MODEL), F32)],
        compiler_params=_cparams(2),
        name="mlp",
    )(x2, mod, mod, mod, n2, wu, wd, fn)


def _sproj_kernel(x_ref, sh_ref, sc_ref, n1_ref, w_ref, o_ref):
    h = _modnorm(x_ref[...], n1_ref[...], sc_ref[...], sh_ref[...])
    o_ref[...] = _dot(h.astype(BF16), w_ref[...])


def _sproj_call(xs, mod_s, n1, w):
    n, width = xs.shape[0], w.shape[1]
    return pl.pallas_call(
        _sproj_kernel,
        grid=(1,),
        in_specs=[pl.BlockSpec((n, D_MODEL), lambda i: (0, 0)),
                  pl.BlockSpec((n, D_MODEL), lambda i: (0, 0)),
                  pl.BlockSpec((n, D_MODEL), lambda i: (0, 1)),
                  pl.BlockSpec((1, D_MODEL), lambda i: (0, 0)),
                  pl.BlockSpec((D_MODEL, width), lambda i: (0, 0))],
        out_specs=pl.BlockSpec((n, width), lambda i: (0, 0)),
        out_shape=jax.ShapeDtypeStruct((n, width), F32),
        compiler_params=_cparams(1),
        name="sample_proj",
    )(xs, mod_s, mod_s, n1, w)


def _sa_kernel(q_ref, kv_ref, ck_ref, cv_ref, bias_ref, sink_ref, ko_ref, vo_ref, oa_ref):
    nb = ROWS_PER_SAMPLE_STEP
    gsz = HA_Q // HA_KV
    rg = lax.broadcasted_iota(jnp.int32, (HA_Q, HD_A), 0) // gsz
    sink = sink_ref[...]
    kw = HA_KV * HD_A
    for i in range(nb):
        ko_ref[0, i, 0:WINDOW - 1, :] = ck_ref[0, i, 1:WINDOW, :]
        ko_ref[0, i, WINDOW - 1:WINDOW, :] = kv_ref[i:i + 1, 0:kw]
        vo_ref[0, i, 0:WINDOW - 1, :] = cv_ref[0, i, 1:WINDOW, :]
        vo_ref[0, i, WINDOW - 1:WINDOW, :] = kv_ref[i:i + 1, kw:2 * kw]
        kmat = ko_ref[0, i].astype(BF16)
        vmat = vo_ref[0, i].astype(BF16)
        q = q_ref[i] * (HD_A ** -0.5)
        qe = jnp.concatenate([jnp.where(rg == g, q, 0.0) for g in range(HA_KV)], axis=1).astype(BF16)
        s = _dot_nt(qe, kmat) + bias_ref[...]
        m = jnp.maximum(jnp.max(s, axis=-1, keepdims=True), sink)
        p = jnp.exp(s - m)
        den = jnp.sum(p, axis=-1, keepdims=True) + jnp.exp(sink - m)
        o = _dot(p.astype(BF16), vmat) / den
        o16 = jnp.zeros((HA_Q, HD_A), F32)
        for g in range(HA_KV):
            o16 = o16 + jnp.where(rg == g, o[:, HD_A * g:HD_A * (g + 1)], 0.0)
        oa_ref[i] = o16


def _sa_call(q3, kv, cache_k, cache_v, bias_s, sink_col, layer):
    n = q3.shape[0]
    nb = ROWS_PER_SAMPLE_STEP
    kw = HA_KV * HD_A
    cspec = pl.BlockSpec((1, nb, WINDOW, kw), lambda i: (layer, i, 0, 0))
    ospec = pl.BlockSpec((1, nb, WINDOW, kw), lambda i: (0, i, 0, 0))
    return pl.pallas_call(
        _sa_kernel,
        grid=(n // nb,),
        in_specs=[pl.BlockSpec((nb, HA_Q, HD_A), lambda i: (i, 0, 0)),
                  pl.BlockSpec((nb, 2 * kw), lambda i: (i, 0)),
                  cspec, cspec,
                  pl.BlockSpec((HA_Q, WINDOW), lambda i: (0, 0)),
                  pl.BlockSpec((HA_Q, 1), lambda i: (0, 0))],
        out_specs=[ospec, ospec, pl.BlockSpec((nb, HA_Q, HD_A), lambda i: (i, 0, 0))],
        out_shape=[jax.ShapeDtypeStruct((1, n, WINDOW, kw), F32),
                   jax.ShapeDtypeStruct((1, n, WINDOW, kw), F32),
                   jax.ShapeDtypeStruct((n, HA_Q, HD_A), F32)],
        compiler_params=_cparams(1),
        name="sample_attn",
    )(q3, kv, cache_k, cache_v, bias_s, sink_col)


def _sr_kernel(zb_ref, cos_ref, sin_ref, gcol_ref, s_ref, so_ref, o_ref):
    nb = ROWS_PER_SAMPLE_STEP
    wq = HB * DK_B
    cos = cos_ref[...]
    sin = sin_ref[...]
    qf = zb_ref[:, 0:wq]
    kf = zb_ref[:, wq:2 * wq]
    qr = qf * cos + _pairswap(qf) * sin
    kr = (kf * cos + _pairswap(kf) * sin) * (DK_B ** -0.5)
    v = zb_ref[:, 2 * wq:2 * wq + HB * DV_B]
    r8 = lax.broadcasted_iota(jnp.int32, (HB, wq), 0)
    hl = lax.broadcasted_iota(jnp.int32, (HB, wq), 1) // DK_B
    rv = lax.broadcasted_iota(jnp.int32, (HB, DV_B), 0)
    for i in range(nb):
        k8 = jnp.where(hl == r8, jnp.broadcast_to(kr[i:i + 1, :], (HB, wq)), 0.0).astype(BF16)
        q8 = jnp.where(hl == r8, jnp.broadcast_to(qr[i:i + 1, :], (HB, wq)), 0.0).astype(BF16)
        v8 = jnp.zeros((HB, DV_B), F32)
        for r in range(HB):
            v8 = jnp.where(rv == r, jnp.broadcast_to(v[i:i + 1, DV_B * r:DV_B * (r + 1)], (HB, DV_B)), v8)
        s_new = gcol_ref[...] * s_ref[0, i] + _dot_tn(k8, v8.astype(BF16))
        so_ref[0, i] = s_new
        o_ref[i] = _dot(q8, s_new.astype(BF16))


def _sr_call(zb, state, layer):
    n = zb.shape[0]
    nb = ROWS_PER_SAMPLE_STEP
    wq = HB * DK_B
    cos, sin = _rot_tables(np.array([PAST_LEN]))
    gcol = np.repeat(_gammas(), DK_B)[:, None] * np.ones((1, DV_B))
    return pl.pallas_call(
        _sr_kernel,
        grid=(n // nb,),
        in_specs=[pl.BlockSpec((nb, SEG_B), lambda i: (i, 0)),
                  pl.BlockSpec((1, wq), lambda i: (0, 0)),
                  pl.BlockSpec((1, wq), lambda i: (0, 0)),
                  pl.BlockSpec((wq, DV_B), lambda i: (0, 0)),
                  pl.BlockSpec((1, nb, wq, DV_B), lambda i: (layer, i, 0, 0))],
        out_specs=[pl.BlockSpec((1, nb, wq, DV_B), lambda i: (0, i, 0, 0)),
                   pl.BlockSpec((nb, HB, DV_B), lambda i: (i, 0, 0))],
        out_shape=[jax.ShapeDtypeStruct((1, n, wq, DV_B), F32),
                   jax.ShapeDtypeStruct((n, HB, DV_B), F32)],
        compiler_params=_cparams(1),
        name="sample_retention",
    )(zb, jnp.asarray(cos), jnp.asarray(sin), jnp.asarray(gcol.astype(np.float32)), state)


def _ss_kernel(zc_ref, cs_ref, cw_ref, cb_ref, dtb_ref, a_ref, e3_ref, h_ref,
               ho_ref, cso_ref, y_ref, xc_ref):
    nb = ROWS_PER_SAMPLE_STEP
    cx = zc_ref[:, 1024:2560]
    taps = [cs_ref[0, :, CONV_DIM * i:CONV_DIM * (i + 1)] for i in range(CONV_W - 1)] + [cx]
    acc = cb_ref[...]
    for i in range(CONV_W):
        acc = acc + taps[i] * cw_ref[i:i + 1, :]
    xbc = _silu(acc)
    cso_ref[...] = jnp.concatenate(taps[1:], axis=1)
    xc = xbc[:, 0:1024]
    bmat = xbc[:, 1024:1280]
    cmat = xbc[:, 1280:1536]
    dt = _softplus(zc_ref[:, C_DT:C_DT + 128] + dtb_ref[...])
    da = jnp.exp(dt * a_ref[...])
    dt_e = _dot(jnp.concatenate(_split3(dt), axis=1), e3_ref[...])
    da_e = _dot(jnp.concatenate(_split3(da), axis=1), e3_ref[...])
    dtx = dt_e * xc
    gw = (HC // G_C) * HD_C
    r8 = lax.broadcasted_iota(jnp.int32, (nb, gw), 0)
    rn = lax.broadcasted_iota(jnp.int32, (nb, N_C), 0)
    ones8 = jnp.ones((nb, N_C), BF16)
    ycols = []
    for g in range(G_C):
        ws = slice(gw * g, gw * (g + 1))
        ns = slice(N_C * g, N_C * (g + 1))
        bg16 = bmat[:, ns].astype(BF16)
        yacc = jnp.zeros((nb, gw), F32)
        for i in range(nb):
            x8 = jnp.where(r8 == i, dtx[:, ws], 0.0).astype(BF16)
            outer = _dot_tn(x8, bg16)
            hi, mid, lo = (p.astype(F32) for p in _split3(jnp.broadcast_to(da_e[i:i + 1, ws], (nb, gw))))
            l3 = jnp.where(r8 == 0, hi, jnp.where(r8 == 1, mid, jnp.where(r8 == 2, lo, 0.0)))
            dacol = _dot_tn(l3.astype(BF16), ones8)
            h_new = dacol * h_ref[0, i, ws, :] + outer
            ho_ref[0, i, ws, :] = h_new
            c8 = jnp.where(rn == i, cmat[:, ns], 0.0).astype(BF16)
            yacc = yacc + _dot_nt(c8, h_new.astype(BF16))
        ycols.append(yacc)
    y_ref[...] = jnp.concatenate(ycols, axis=1)
    xc_ref[...] = xc


def _ss_call(zc, conv_state, hstate, cw, cb, dtb_pad, a_pad, layer):
    n = zc.shape[0]
    nb = ROWS_PER_SAMPLE_STEP
    e = np.zeros((128, D_MODEL), np.float32)
    for hh in range(HC):
        e[hh, HD_C * hh:HD_C * (hh + 1)] = 1.0
    e3 = jnp.asarray(np.concatenate([e, e, e], axis=0), dtype=BF16)
    cwid = (CONV_W - 1) * CONV_DIM
    full = lambda i: (0, 0)
    return pl.pallas_call(
        _ss_kernel,
        grid=(n // nb,),
        in_specs=[pl.BlockSpec((nb, SEG_C), lambda i: (i, 0)),
                  pl.BlockSpec((1, nb, cwid), lambda i: (layer, i, 0)),
                  pl.BlockSpec((CONV_W, CONV_DIM), full),
                  pl.BlockSpec((1, CONV_DIM), full),
                  pl.BlockSpec((1, 128), full),
                  pl.BlockSpec((1, 128), full),
                  pl.BlockSpec((3 * 128, D_MODEL), full),
                  pl.BlockSpec((1, nb, HC * HD_C, N_C), lambda i: (layer, i, 0, 0))],
        out_specs=[pl.BlockSpec((1, nb, HC * HD_C, N_C), lambda i: (0, i, 0, 0)),
                   pl.BlockSpec((nb, cwid), lambda i: (i, 0)),
                   pl.BlockSpec((nb, D_MODEL), lambda i: (i, 0)),
                   pl.BlockSpec((nb, D_MODEL), lambda i: (i, 0))],
        out_shape=[jax.ShapeDtypeStruct((1, n, HC * HD_C, N_C), F32),
                   jax.ShapeDtypeStruct((n, cwid), F32),
                   jax.ShapeDtypeStruct((n, D_MODEL), F32),
                   jax.ShapeDtypeStruct((n, D_MODEL), F32)],
        compiler_params=_cparams(1),
        name="sample_ssd",
    )(zc, conv_state, cw, cb, dtb_pad, a_pad, e3, hstate)


def _sm_kernel(x_ref, g1_ref, oa_ref, oret_ref, y_ref, xc_ref, za_ref, zb_ref, zc_ref,
               dsk_ref, snw_ref, wo_ref, o_ref):
    ob = jnp.concatenate([_rms_lanes(oret_ref[:, DV_B * hh:DV_B * (hh + 1)]) for hh in range(HB)], axis=1)
    ob = _silu(zb_ref[:, 2048:3072]) * ob
    yc = (y_ref[...] + dsk_ref[...] * xc_ref[...]) * _silu(zc_ref[:, 0:1024])
    gw = D_MODEL // G_C
    oc = jnp.concatenate([_rms_lanes(yc[:, gw * g:gw * (g + 1)]) for g in range(G_C)], axis=1) * snw_ref[...]
    mix = (_sigmoid(za_ref[:, 1536:2560]) * oa_ref[...] + _sigmoid(zb_ref[:, 3072:4096]) * ob
           + _sigmoid(zc_ref[:, C_GC:C_GC + 1024]) * oc)
    o_ref[...] = x_ref[...] + g1_ref[...] * _dot(mix.astype(BF16), wo_ref[...])


def _sm_call(xs, mod_s, oa, oret, y, xc, za, zb, zc, dsk_full, snw, wo):
    n = xs.shape[0]
    full = lambda i: (0, 0)
    row = lambda w: pl.BlockSpec((n, w), full)
    return pl.pallas_call(
        _sm_kernel,
        grid=(1,),
        in_specs=[row(D_MODEL),
                  pl.BlockSpec((n, D_MODEL), lambda i: (0, 2)),
                  row(D_MODEL), row(D_MODEL), row(D_MODEL), row(D_MODEL),
                  row(SEG_A), row(SEG_B), row(SEG_C),
                  pl.BlockSpec((1, D_MODEL), full),
                  pl.BlockSpec((1, D_MODEL), full),
                  pl.BlockSpec((D_MODEL, D_MODEL), full)],
        out_specs=row(D_MODEL),
        out_shape=jax.ShapeDtypeStruct((n, D_MODEL), F32),
        compiler_params=_cparams(1),
        name="sample_merge_out",
    )(xs, mod_s, oa, oret, y, xc, za, zb, zc, dsk_full, snw, wo)


PROMPT_TILE = 512
MLP_ROWS = 1024
MLP_FF = 512


def _prep_w_in(w):
    wa = jnp.concatenate([w[:, O_AQ:O_BQ], w[:, O_GTS:O_GTS + 1024]], axis=1)
    wb = jnp.concatenate([w[:, O_BQ:O_CZ], w[:, O_GTS + 1024:O_GTS + 2048]], axis=1)
    wc = jnp.concatenate([w[:, O_CZ:O_CDT], jnp.pad(w[:, O_CDT:O_GTS], ((0, 0), (0, 128 - HC))),
                          w[:, O_GTS + 2048:O_GTS + 3072]], axis=1)
    return wa.astype(BF16), wb.astype(BF16), wc.astype(BF16)


def _forward(x_prompt, x_sample, cache_win_k, cache_win_v, state_ret, state_ssm, state_conv,
             c_prompt, c_sample, rel_bias_table, attn_sinks, norm1_w, norm2_w, ada_w, ada_b,
             w_in, conv_w, conv_b, dt_bias, A_log, D_skip, ssm_norm_w, w_out, w_up, w_down,
             final_norm_w, *, prompt_tile, mlp_rows, mlp_ff):
    B, T, _ = x_prompt.shape
    DB = x_sample.shape[0]
    kw = HA_KV * HD_A

    mod_all = _ada_call(jnp.concatenate([c_prompt, c_sample], axis=0), ada_w, ada_b)

    tbl = _t5_bias_by_dist(rel_bias_table)
    qi = np.arange(WINDOW)[:, None]
    kj = np.arange(WINDOW)[None, :]
    d_prev = qi + WINDOW - kj
    d_cur = qi - kj
    bias_prev = jnp.where(jnp.asarray(kj > qi)[None], jnp.moveaxis(tbl[np.clip(d_prev, 0, WINDOW - 1)], -1, 0), NEG)
    bias_cur = jnp.where(jnp.asarray(kj <= qi)[None], jnp.moveaxis(tbl[np.clip(d_cur, 0, WINDOW - 1)], -1, 0), NEG)
    bias_s = tbl[::-1].T

    wu16 = w_up.astype(BF16)
    wd16 = w_down.astype(BF16)
    wo16 = w_out.astype(BF16)
    fn = final_norm_w.reshape(1, D_MODEL)
    ck = cache_win_k.reshape(DEPTH, DB, WINDOW, kw)
    cv = cache_win_v.reshape(DEPTH, DB, WINDOW, kw)
    sret = state_ret.reshape(DEPTH, DB, HB * DK_B, DV_B)
    sssm = state_ssm.reshape(DEPTH, DB, HC * HD_C, N_C)
    sconv = state_conv.reshape(DEPTH, DB, (CONV_W - 1) * CONV_DIM)

    xp = x_prompt
    xs = x_sample.reshape(DB, D_MODEL)
    outs_p = [[] for _ in range(5)]
    outs_s = [[] for _ in range(5)]
    for l in range(DEPTH):
        wa, wb, wc = _prep_w_in(w_in[l])
        n1 = norm1_w[l].reshape(1, D_MODEL)
        n2 = norm2_w[l].reshape(1, D_MODEL)
        cw = conv_w[l]
        cb = conv_b[l].reshape(1, CONV_DIM)
        dtb_pad = jnp.pad(dt_bias[l], (0, 128 - HC)).reshape(1, 128)
        a_pad = jnp.pad(-jnp.exp(A_log[l].astype(F32)), (0, 128 - HC)).reshape(1, 128)
        dsk_full = jnp.repeat(D_skip[l], HD_C).reshape(1, D_MODEL)
        snw = ssm_norm_w[l].reshape(1, D_MODEL)
        final = l == DEPTH - 1
        mod_p = mod_all[l, :B].reshape(B, 1, 6 * D_MODEL)
        mod_s = mod_all[l, B:]

        mixa, kbuf, vbuf = _pa_call(xp, mod_p, n1, wa, bias_prev, bias_cur, attn_sinks[l], prompt_tile)
        mixab, s_ret = _pb_call(xp, mod_p, n1, wb, mixa, prompt_tile)
        x1, h_ssm, conv_new = _pc_call(xp, mod_p, n1, wc, cw, cb, dtb_pad, a_pad, dsk_full, snw,
                                       mixab, wo16[l], prompt_tile)
        xp = _mlp_call(x1.reshape(B * T, D_MODEL), mod_p, n2, wu16, wd16, fn, l,
                       mlp_rows, mlp_ff, T, final).reshape(B, T, D_MODEL)
        for lst, v in zip(outs_p, (kbuf.reshape(B, WINDOW, HA_KV, HD_A), vbuf.reshape(B, WINDOW, HA_KV, HD_A),
                                   s_ret.reshape(B, HB, DK_B, DV_B), h_ssm.reshape(B, HC, HD_C, N_C), conv_new)):
            lst.append(v)

        za = _sproj_call(xs, mod_s, n1, wa)
        zb = _sproj_call(xs, mod_s, n1, wb)
        zc = _sproj_call(xs, mod_s, n1, wc)
        q3 = za[:, 0:HA_Q * HD_A].reshape(DB, HA_Q, HD_A)
        kv = za[:, HA_Q * HD_A:HA_Q * HD_A + 2 * kw]
        ck_new, cv_new, oa3 = _sa_call(q3, kv, ck, cv, bias_s, attn_sinks[l].reshape(HA_Q, 1), l)
        s_new, o3 = _sr_call(zb, sret, l)
        h_new, cs_new, y_s, xc_s = _ss_call(zc, sconv, sssm, cw, cb, dtb_pad, a_pad, l)
        xs1 = _sm_call(xs, mod_s, oa3.reshape(DB, D_MODEL), o3.reshape(DB, D_MODEL), y_s, xc_s,
                       za, zb, zc, dsk_full, snw, wo16[l])
        xs = _mlp_call(xs1, mod_s, n2, wu16, wd16, fn, l, DB, mlp_ff, 1, final)
        for lst, v in zip(outs_s, (ck_new.reshape(DB, WINDOW, HA_KV, HD_A), cv_new.reshape(DB, WINDOW, HA_KV, HD_A),
                                   s_new.reshape(DB, HB, DK_B, DV_B), h_new.reshape(DB, HC, HD_C, N_C),
                                   cs_new.reshape(DB, CONV_W - 1, CONV_DIM))):
            lst.append(v)

    stk = lambda lst: jnp.stack(lst, axis=0)
    return (xp, xs.reshape(DB, 1, D_MODEL),
            stk(outs_p[0]), stk(outs_p[1]), stk(outs_p[2]), stk(outs_p[3]), stk(outs_p[4]),
            stk(outs_s[0]), stk(outs_s[1]), stk(outs_s[2]), stk(outs_s[3]), stk(outs_s[4]))


def kernel(x_prompt, x_sample, cache_win_k, cache_win_v, state_ret, state_ssm, state_conv, c_prompt, c_sample,
           rel_bias_table, attn_sinks, norm1_w, norm2_w, ada_w, ada_b, w_in, conv_w, conv_b, dt_bias, A_log,
           D_skip, ssm_norm_w, w_out, w_up, w_down, final_norm_w):
    return _forward(x_prompt, x_sample, cache_win_k, cache_win_v, state_ret, state_ssm, state_conv,
                    c_prompt, c_sample, rel_bias_table, attn_sinks, norm1_w, norm2_w, ada_w, ada_b,
                    w_in, conv_w, conv_b, dt_bias, A_log, D_skip, ssm_norm_w, w_out, w_up, w_down,
                    final_norm_w, prompt_tile=PROMPT_TILE, mlp_rows=MLP_ROWS, mlp_ff=MLP_FF)
```

```python
import functools
import math

import numpy as np
import jax
import jax.numpy as jnp
from jax import lax
from jax.experimental import pallas as pl
from jax.experimental.pallas import tpu as pltpu

F32 = jnp.float32
BF16 = jnp.bfloat16

D_MODEL = 1024
DEPTH = 2
PAST_LEN = 16384
WINDOW = 128
HA_Q = 16
HA_KV = 4
HD_A = 64
NUM_BUCKETS = 32
MAX_DISTANCE = WINDOW
HB = 8
DK_B = 64
DV_B = 128
HC = 16
HD_C = 64
N_C = 128
G_C = 2
CONV_W = 4
CONV_DIM = D_MODEL + 2 * G_C * N_C
D_FF = 4 * D_MODEL
EPS = 1e-6
NEG = -1e30

O_AQ, O_AK, O_AV = 0, 1024, 1280
O_BQ, O_BK, O_BV, O_BG = 1536, 2048, 2560, 3584
O_CZ, O_CXBC, O_CDT, O_GTS = 4608, 5632, 7168, 7184
SEG_A = 2560
SEG_B = 4096
SEG_C = 3712
C_DT = 2560
C_GC = 2688

VMEM_LIMIT_V7X = 56 * 1024 * 1024
ROWS_PER_SAMPLE_STEP = 8


def _cparams(n_axes):
    return pltpu.CompilerParams(dimension_semantics=("arbitrary",) * n_axes,
                                vmem_limit_bytes=VMEM_LIMIT_V7X)


def _dot(a, b):
    return jnp.dot(a, b, preferred_element_type=F32)


def _dot_nt(a, b):
    return lax.dot_general(a, b, (((1,), (1,)), ((), ())), preferred_element_type=F32)


def _dot_tn(a, b):
    return lax.dot_general(a, b, (((0,), (0,)), ((), ())), preferred_element_type=F32)


def _sigmoid(x):
    return 0.5 * (jnp.tanh(0.5 * x) + 1.0)


def _silu(x):
    return x * _sigmoid(x)


def _softplus(x):
    return jnp.maximum(x, 0.0) + jnp.log1p(jnp.exp(-jnp.abs(x)))


def _modnorm(x, nw, sc, sh):
    ms = jnp.mean(x * x, axis=-1, keepdims=True)
    return (x * lax.rsqrt(ms + EPS) * nw) * (1.0 + sc) + sh


def _rms_lanes(x):
    ms = jnp.mean(x * x, axis=-1, keepdims=True)
    return x * lax.rsqrt(ms + EPS)


def _pairswap(x):
    ax = x.ndim - 1
    n = x.shape[ax]
    lane = lax.broadcasted_iota(jnp.int32, x.shape, ax)
    nxt = pltpu.roll(x, n - 1, ax)
    prv = pltpu.roll(x, 1, ax)
    return jnp.where((lane & 1) == 0, nxt, prv)


def _split3(x):
    hi = x.astype(BF16)
    r1 = x - hi.astype(F32)
    mid = r1.astype(BF16)
    lo = (r1 - mid.astype(F32)).astype(BF16)
    return hi, mid, lo


def _gammas():
    return 1.0 - 2.0 ** (-5.0 - np.arange(HB, dtype=np.float64))


def _rot_tables(pos):
    theta = 1.0 / (10000.0 ** np.linspace(0.0, 1.0, DK_B // 2))
    ang = np.asarray(pos, np.float64)[:, None] * theta[None, :]
    cos = np.repeat(np.cos(ang), 2, axis=1)
    sin = np.repeat(np.sin(ang), 2, axis=1)
    sin[:, 0::2] *= -1.0
    return (np.tile(cos, (1, HB)).astype(np.float32), np.tile(sin, (1, HB)).astype(np.float32))


def _ret_tables():
    g = _gammas()
    L = WINDOW
    i = np.arange(L, dtype=np.float64)
    diff = i[:, None] - i[None, :]
    dm = np.where(diff >= 0, g[:, None, None] ** np.maximum(diff, 0.0), 0.0)
    qdec = np.repeat(g[None, :] ** (i[:, None] + 1.0), DK_B, axis=1)
    kdec = np.repeat(g[None, :] ** (L - 1.0 - i[:, None]), DK_B, axis=1)
    return dm.astype(np.float32), qdec.astype(np.float32), kdec.astype(np.float32)


def _t5_bias_by_dist(rel_table):
    max_exact = NUM_BUCKETS // 2
    n = jnp.arange(WINDOW)
    nf = jnp.maximum(n, 1).astype(F32)
    large = max_exact + (jnp.log(nf / max_exact) / math.log(MAX_DISTANCE / max_exact)
                         * (NUM_BUCKETS - max_exact)).astype(jnp.int32)
    large = jnp.minimum(large, NUM_BUCKETS - 1)
    bucket = jnp.where(n < max_exact, n, large)
    return rel_table.astype(F32)[bucket]


def _ada_kernel(c_ref, w_ref, b_ref, o_ref):
    s = _silu(c_ref[...])
    o_ref[0] = _dot(s.astype(BF16), w_ref[0].astype(BF16)) + b_ref[0]


def _ada_call(c_all, ada_w, ada_b):
    n = c_all.shape[0]
    nb = 6
    return pl.pallas_call(
        _ada_kernel,
        grid=(DEPTH, nb),
        in_specs=[pl.BlockSpec((n, D_MODEL), lambda l, j: (0, 0)),
                  pl.BlockSpec((1, D_MODEL, D_MODEL), lambda l, j: (l, 0, j)),
                  pl.BlockSpec((1, 1, D_MODEL), lambda l, j: (l, 0, j))],
        out_specs=pl.BlockSpec((1, n, D_MODEL), lambda l, j: (l, 0, j)),
        out_shape=jax.ShapeDtypeStruct((DEPTH, n, 6 * D_MODEL), F32),
        compiler_params=_cparams(2),
        name="ada_mod",
    )(c_all, ada_w, ada_b.reshape(DEPTH, 1, 6 * D_MODEL))


def _pa_kernel(x_ref, mod_ref, n1_ref, wt_ref, wkv_ref, bias_ref, sink_ref,
               mix_ref, ko_ref, vo_ref, zt_ref, kv_ref, kprev_ref, vtprev_ref, pen_ref, *, tt):
    t = pl.program_id(1)
    nchunk = tt // WINDOW
    kw = HA_KV * HD_A
    gsz = HA_Q // HA_KV

    @pl.when(t == 0)
    def _():
        kprev_ref[...] = jnp.zeros_like(kprev_ref)
        vtprev_ref[...] = jnp.zeros_like(vtprev_ref)
        pen_ref[...] = jnp.full(pen_ref.shape, NEG, F32)

    mod = mod_ref[0]
    h = _modnorm(x_ref[0], n1_ref[...], mod[:, D_MODEL:2 * D_MODEL], mod[:, 0:D_MODEL]).astype(BF16)
    zt = _dot_nt(wt_ref[...], h)
    for c in range(nchunk):
        zt_ref[c] = zt[:, WINDOW * c:WINDOW * (c + 1)]
    kv_ref[...] = _dot(h, wkv_ref[...])

    def chunk(c, carry):
        r0 = pl.multiple_of(c * WINDOW, WINDOW)
        rows = pl.ds(r0, WINDOW)
        kc = kv_ref[rows, 0:kw]
        vc = kv_ref[rows, kw:2 * kw]
        vt = vc.T
        kk = jnp.concatenate([kprev_ref[...], kc], axis=0).astype(BF16)
        vvt = jnp.concatenate([vtprev_ref[...], vt], axis=1).astype(BF16)
        qt = zt_ref[c, 0:D_MODEL, :].astype(BF16)
        pen = pen_ref[0:1, :]
        pieces = []
        for g in range(HA_KV):
            gs = slice(HD_A * g, HD_A * (g + 1))
            qcat = jnp.concatenate([qt[HD_A * (gsz * g + j):HD_A * (gsz * g + j + 1), :] for j in range(gsz)], axis=1)
            s = _dot(kk[:, gs], qcat) + bias_ref[g]
            sp = s[0:WINDOW, :] + pen
            sc = s[WINDOW:2 * WINDOW, :]
            sink = sink_ref[g]
            m = jnp.maximum(jnp.maximum(jnp.max(sp, axis=0, keepdims=True), jnp.max(sc, axis=0, keepdims=True)), sink)
            pp = jnp.exp(sp - m)
            pc = jnp.exp(sc - m)
            den = jnp.sum(pp, axis=0, keepdims=True) + jnp.sum(pc, axis=0, keepdims=True) + jnp.exp(sink - m)
            p = jnp.concatenate([pp, pc], axis=0).astype(BF16)
            ot = _dot(vvt[gs, :], p) * (1.0 / den)
            pieces += [ot[:, WINDOW * j:WINDOW * (j + 1)] for j in range(gsz)]
        oat = jnp.concatenate(pieces, axis=0)
        mixt = _sigmoid(zt_ref[c, D_MODEL:2 * D_MODEL, :]) * oat
        mix_ref[0, rows, :] = mixt.T
        kprev_ref[...] = kc
        vtprev_ref[...] = vt
        pen_ref[...] = jnp.zeros_like(pen_ref)
        ko_ref[0] = kc
        vo_ref[0] = vc
        return carry

    lax.fori_loop(0, nchunk, chunk, 0)


def _pa_call(x, mod3, n1, wqgt, wkv, bias_t, sink_rows, tt):
    B, T, _ = x.shape
    kw = HA_KV * HD_A
    qw = (HA_Q // HA_KV) * WINDOW
    kern = functools.partial(_pa_kernel, tt=tt)
    return pl.pallas_call(
        kern,
        grid=(B, T // tt),
        in_specs=[pl.BlockSpec((1, tt, D_MODEL), lambda b, t: (b, t, 0)),
                  pl.BlockSpec((1, 1, 6 * D_MODEL), lambda b, t: (b, 0, 0)),
                  pl.BlockSpec((1, D_MODEL), lambda b, t: (0, 0)),
                  pl.BlockSpec((2 * D_MODEL, D_MODEL), lambda b, t: (0, 0)),
                  pl.BlockSpec((D_MODEL, 2 * kw), lambda b, t: (0, 0)),
                  pl.BlockSpec((HA_KV, 2 * WINDOW, qw), lambda b, t: (0, 0, 0)),
                  pl.BlockSpec((HA_KV, 1, qw), lambda b, t: (0, 0, 0))],
        out_specs=[pl.BlockSpec((1, tt, D_MODEL), lambda b, t: (b, t, 0)),
                   pl.BlockSpec((1, WINDOW, kw), lambda b, t: (b, 0, 0)),
                   pl.BlockSpec((1, WINDOW, kw), lambda b, t: (b, 0, 0))],
        out_shape=[jax.ShapeDtypeStruct((B, T, D_MODEL), F32),
                   jax.ShapeDtypeStruct((B, WINDOW, kw), F32),
                   jax.ShapeDtypeStruct((B, WINDOW, kw), F32)],
        scratch_shapes=[pltpu.VMEM((tt // WINDOW, 2 * D_MODEL, WINDOW), F32),
                        pltpu.VMEM((tt, 2 * kw), F32),
                        pltpu.VMEM((WINDOW, kw), F32),
                        pltpu.VMEM((kw, WINDOW), F32),
                        pltpu.VMEM((8, qw), F32)],
        compiler_params=_cparams(2),
        name="prompt_attn",
    )(x, mod3, n1, wqgt, wkv, bias_t, sink_rows)


def _pb_kernel(x_ref, mod_ref, n1_ref, w_ref, cos_ref, sin_ref, qdec_ref, kdec_ref, dm_ref, mixa_ref,
               mix_ref, so_ref, z_ref, s_ref, *, tt, glast):
    t = pl.program_id(1)

    @pl.when(t == 0)
    def _():
        s_ref[...] = jnp.zeros_like(s_ref)

    mod = mod_ref[0]
    h = _modnorm(x_ref[0], n1_ref[...], mod[:, D_MODEL:2 * D_MODEL], mod[:, 0:D_MODEL])
    z_ref[...] = _dot(h.astype(BF16), w_ref[...])

    def chunk(c, carry):
        r0 = pl.multiple_of(c * WINDOW, WINDOW)
        rows = pl.ds(r0, WINDOW)
        cos = cos_ref[rows, :]
        sin = sin_ref[rows, :]
        qf = z_ref[rows, 0:512]
        kf = z_ref[rows, 512:1024]
        qr = qf * cos + _pairswap(qf) * sin
        kr = (kf * cos + _pairswap(kf) * sin) * (DK_B ** -0.5)
        qb = qr.astype(BF16)
        kb = kr.astype(BF16)
        qd = (qr * qdec_ref[...]).astype(BF16)
        kd = (kr * kdec_ref[...]).astype(BF16)
        vb = z_ref[rows, 1024:2048].astype(BF16)
        outs = []
        for hh in range(HB):
            ks = slice(DK_B * hh, DK_B * (hh + 1))
            vh = vb[:, DV_B * hh:DV_B * (hh + 1)]
            inner = _dot_nt(qb[:, ks], kb[:, ks]) * dm_ref[hh]
            s_old = s_ref[ks, :]
            o = _dot(inner.astype(BF16), vh) + _dot(qd[:, ks], s_old.astype(BF16))
            s_ref[ks, :] = glast[hh] * s_old + _dot_tn(kd[:, ks], vh)
            outs.append(_rms_lanes(o))
        ob = jnp.concatenate(outs, axis=1)
        bg = z_ref[rows, 2048:3072]
        gb = z_ref[rows, 3072:4096]
        mix_ref[0, rows, :] = mixa_ref[0, rows, :] + _sigmoid(gb) * (_silu(bg) * ob)
        return carry

    lax.fori_loop(0, tt // WINDOW, chunk, 0)
    so_ref[0] = s_ref[...]


def _pb_call(x, mod3, n1, wb, mixa, tt):
    B, T, _ = x.shape
    cos, sin = _rot_tables(np.arange(T))
    dm, qdec, kdec = _ret_tables()
    glast = tuple(float(v) for v in (_gammas() ** WINDOW))
    kern = functools.partial(_pb_kernel, tt=tt, glast=glast)
    full2 = lambda b, t: (0, 0)
    return pl.pallas_call(
        kern,
        grid=(B, T // tt),
        in_specs=[pl.BlockSpec((1, tt, D_MODEL), lambda b, t: (b, t, 0)),
                  pl.BlockSpec((1, 1, 6 * D_MODEL), lambda b, t: (b, 0, 0)),
                  pl.BlockSpec((1, D_MODEL), full2),
                  pl.BlockSpec((D_MODEL, SEG_B), full2),
                  pl.BlockSpec((tt, HB * DK_B), lambda b, t: (t, 0)),
                  pl.BlockSpec((tt, HB * DK_B), lambda b, t: (t, 0)),
                  pl.BlockSpec((WINDOW, HB * DK_B), full2),
                  pl.BlockSpec((WINDOW, HB * DK_B), full2),
                  pl.BlockSpec((HB, WINDOW, WINDOW), lambda b, t: (0, 0, 0)),
                  pl.BlockSpec((1, tt, D_MODEL), lambda b, t: (b, t, 0))],
        out_specs=[pl.BlockSpec((1, tt, D_MODEL), lambda b, t: (b, t, 0)),
                   pl.BlockSpec((1, HB * DK_B, DV_B), lambda b, t: (b, 0, 0))],
        out_shape=[jax.ShapeDtypeStruct((B, T, D_MODEL), F32),
                   jax.ShapeDtypeStruct((B, HB * DK_B, DV_B), F32)],
        scratch_shapes=[pltpu.VMEM((tt, SEG_B), F32),
                        pltpu.VMEM((HB * DK_B, DV_B), F32)],
        compiler_params=_cparams(2),
        name="prompt_retention",
    )(x, mod3, n1, wb, jnp.asarray(cos), jnp.asarray(sin), jnp.asarray(qdec), jnp.asarray(kdec),
      jnp.asarray(dm), mixa)


def _pc_kernel(x_ref, mod_ref, n1_ref, w_ref, cw_ref, cb_ref, dtb_ref, a_ref, dsk_ref, snw_ref, tri_ref,
               mixab_ref, wo_ref,
               xo_ref, ho_ref, co_ref,
               z_ref, xbuf_ref, xbc_ref, dt_ref, hst_ref, mixs_ref, *, tt):
    t = pl.program_id(1)

    @pl.when(t == 0)
    def _():
        xbuf_ref[0:8, :] = jnp.zeros((8, CONV_DIM), F32)
        hst_ref[...] = jnp.zeros_like(hst_ref)

    x = x_ref[0]
    mod = mod_ref[0]
    h = _modnorm(x, n1_ref[...], mod[:, D_MODEL:2 * D_MODEL], mod[:, 0:D_MODEL])
    z_ref[...] = _dot(h.astype(BF16), w_ref[...])

    xbuf_ref[8:8 + tt, :] = z_ref[:, 1024:2560]
    acc = cb_ref[...]
    for i in range(CONV_W):
        acc = acc + xbuf_ref[5 + i:5 + i + tt, :] * cw_ref[i:i + 1, :]
    xbc_ref[...] = _silu(acc)
    co_ref[0] = xbuf_ref[tt + 5:tt + 8, :]
    xbuf_ref[0:8, :] = xbuf_ref[tt:tt + 8, :]
    dt_ref[...] = _softplus(z_ref[:, C_DT:C_DT + 128] + dtb_ref[...])

    ii = lax.broadcasted_iota(jnp.int32, (WINDOW, WINDOW), 0)
    jj = lax.broadcasted_iota(jnp.int32, (WINDOW, WINDOW), 1)
    causal = ii >= jj
    hpg = HC // G_C

    def chunk(c, carry):
        r0 = pl.multiple_of(c * WINDOW, WINDOW)
        rows = pl.ds(r0, WINDOW)
        xc = xbc_ref[rows, 0:1024]
        bmat = xbc_ref[rows, 1024:1280]
        cmat = xbc_ref[rows, 1280:1536]
        dtc = dt_ref[rows, :]
        acum = jnp.dot(tri_ref[...], dtc * a_ref[...], precision=lax.Precision.HIGHEST,
                       preferred_element_type=F32)
        acum_t = acum.T
        dt_t = dtc.T
        x_t = xc.T
        xb = xc.astype(BF16)
        bb = bmat.astype(BF16)
        cb16 = cmat.astype(BF16)
        ys = []
        for g in range(G_C):
            ns = slice(N_C * g, N_C * (g + 1))
            cbg = _dot_nt(cb16[:, ns], bb[:, ns])
            for hh in range(hpg * g, hpg * (g + 1)):
                ps = slice(HD_C * hh, HD_C * (hh + 1))
                colb = jnp.broadcast_to(acum[:, hh:hh + 1], (WINDOW, WINDOW))
                row = acum_t[hh:hh + 1, :]
                dtrow = dt_t[hh:hh + 1, :]
                lmat = jnp.exp(jnp.where(causal, colb - row, NEG))
                m = cbg * lmat * dtrow
                hs = hst_ref[ps, :]
                cs = cmat[:, ns] * jnp.exp(colb)
                y = _dot(m.astype(BF16), xb[:, ps]) + _dot_nt(cs.astype(BF16), hs.astype(BF16))
                last = colb[WINDOW - 1:WINDOW, :]
                wrow = jnp.exp(last - row) * dtrow
                xw = (x_t[ps, :] * wrow).astype(BF16)
                hst_ref[ps, :] = jnp.exp(last) * hs + _dot(xw, bb[:, ns])
                ys.append(y)
        y = jnp.concatenate(ys, axis=1) + dsk_ref[...] * xc
        yc = y * _silu(z_ref[rows, 0:1024])
        gw = D_MODEL // G_C
        oc = jnp.concatenate([_rms_lanes(yc[:, gw * g:gw * (g + 1)]) for g in range(G_C)], axis=1) * snw_ref[...]
        gc = _sigmoid(z_ref[rows, C_GC:C_GC + 1024])
        mixs_ref[rows, :] = (mixab_ref[0, rows, :] + gc * oc).astype(BF16)
        return carry

    lax.fori_loop(0, tt // WINDOW, chunk, 0)
    ho_ref[0] = hst_ref[...]
    g1 = mod[:, 2 * D_MODEL:3 * D_MODEL]
    xo_ref[0] = x + g1 * _dot(mixs_ref[...], wo_ref[...])


def _pc_call(x, mod3, n1, wc, cw, cb, dtb_pad, a_pad, dsk_full, snw, mixab, wo, tt):
    B, T, _ = x.shape
    tri = jnp.asarray(np.tril(np.ones((WINDOW, WINDOW), np.float32)))
    kern = functools.partial(_pc_kernel, tt=tt)
    full2 = lambda b, t: (0, 0)
    return pl.pallas_call(
        kern,
        grid=(B, T // tt),
        in_specs=[pl.BlockSpec((1, tt, D_MODEL), lambda b, t: (b, t, 0)),
                  pl.BlockSpec((1, 1, 6 * D_MODEL), lambda b, t: (b, 0, 0)),
                  pl.BlockSpec((1, D_MODEL), full2),
                  pl.BlockSpec((D_MODEL, SEG_C), full2),
                  pl.BlockSpec((CONV_W, CONV_DIM), full2),
                  pl.BlockSpec((1, CONV_DIM), full2),
                  pl.BlockSpec((1, 128), full2),
                  pl.BlockSpec((1, 128), full2),
                  pl.BlockSpec((1, D_MODEL), full2),
                  pl.BlockSpec((1, D_MODEL), full2),
                  pl.BlockSpec((WINDOW, WINDOW), full2),
                  pl.BlockSpec((1, tt, D_MODEL), lambda b, t: (b, t, 0)),
                  pl.BlockSpec((D_MODEL, D_MODEL), full2)],
        out_specs=[pl.BlockSpec((1, tt, D_MODEL), lambda b, t: (b, t, 0)),
                   pl.BlockSpec((1, HC * HD_C, N_C), lambda b, t: (b, 0, 0)),
                   pl.BlockSpec((1, CONV_W - 1, CONV_DIM), lambda b, t: (b, 0, 0))],
        out_shape=[jax.ShapeDtypeStruct((B, T, D_MODEL), F32),
                   jax.ShapeDtypeStruct((B, HC * HD_C, N_C), F32),
                   jax.ShapeDtypeStruct((B, CONV_W - 1, CONV_DIM), F32)],
        scratch_shapes=[pltpu.VMEM((tt, SEG_C), F32),
                        pltpu.VMEM((tt + 8, CONV_DIM), F32),
                        pltpu.VMEM((tt, CONV_DIM), F32),
                        pltpu.VMEM((tt, 128), F32),
                        pltpu.VMEM((HC * HD_C, N_C), F32),
                        pltpu.VMEM((tt, D_MODEL), BF16)],
        compiler_params=_cparams(2),
        name="prompt_ssd_out",
    )(x, mod3, n1, wc, cw, cb, dtb_pad, a_pad, dsk_full, snw, tri, mixab, wo)


def _mlp_kernel(x_ref, sh_ref, sc_ref, g_ref, n2_ref, wu_ref, wd_ref, fn_ref, o_ref, h_ref, acc_ref,
                *, nf, final, per_row):
    f = pl.program_id(1)
    rd = (lambda r: r[...]) if per_row else (lambda r: r[0])

    @pl.when(f == 0)
    def _():
        h_ref[...] = _modnorm(x_ref[...], n2_ref[...], rd(sc_ref), rd(sh_ref)).astype(BF16)
        acc_ref[...] = jnp.zeros_like(acc_ref)

    u = _dot(h_ref[...], wu_ref[0])
    u = jnp.square(jnp.maximum(u, 0.0)).astype(BF16)
    acc_ref[...] += _dot(u, wd_ref[0])

    @pl.when(f == nf - 1)
    def _():
        y = x_ref[...] + rd(g_ref) * acc_ref[...]
        if final:
            y = _rms_lanes(y) * fn_ref[...]
        o_ref[...] = y


def _mlp_call(x2, mod, n2, wu, wd, fn, layer, tm, tf, rows_per_mod, final):
    M = x2.shape[0]
    nf = D_FF // tf
    per_row = rows_per_mod == 1
    if per_row:
        mspec = lambda j: pl.BlockSpec((tm, D_MODEL), lambda m, f: (m, j))
    else:
        mspec = lambda j: pl.BlockSpec((1, 1, D_MODEL), lambda m, f: ((m * tm) // rows_per_mod, 0, j))
    kern = functools.partial(_mlp_kernel, nf=nf, final=final, per_row=per_row)
    return pl.pallas_call(
        kern,
        grid=(M // tm, nf),
        in_specs=[pl.BlockSpec((tm, D_MODEL), lambda m, f: (m, 0)),
                  mspec(3), mspec(4), mspec(5),
                  pl.BlockSpec((1, D_MODEL), lambda m, f: (0, 0)),
                  pl.BlockSpec((1, D_MODEL, tf), lambda m, f: (layer, 0, f)),
                  pl.BlockSpec((1, tf, D_MODEL), lambda m, f: (layer, f, 0)),
                  pl.BlockSpec((1, D_MODEL), lambda m, f: (0, 0))],
        out_specs=pl.BlockSpec((tm, D_MODEL), lambda m, f: (m, 0)),
        out_shape=jax.ShapeDtypeStruct((M, D_MODEL), F32),
        scratch_shapes=[pltpu.VMEM((tm, D_MODEL), BF16),
                        pltpu.VMEM((tm, D_MODEL), F32)],
        compiler_params=_cparams(2),
        name="mlp",
    )(x2, mod, mod, mod, n2, wu, wd, fn)


def _sproj_kernel(x_ref, sh_ref, sc_ref, n1_ref, w_ref, o_ref):
    h = _modnorm(x_ref[...], n1_ref[...], sc_ref[...], sh_ref[...])
    o_ref[...] = _dot(h.astype(BF16), w_ref[...])


def _sproj_call(xs, mod_s, n1, w):
    n, width = xs.shape[0], w.shape[1]
    return pl.pallas_call(
        _sproj_kernel,
        grid=(1,),
        in_specs=[pl.BlockSpec((n, D_MODEL), lambda i: (0, 0)),
                  pl.BlockSpec((n, D_MODEL), lambda i: (0, 0)),
                  pl.BlockSpec((n, D_MODEL), lambda i: (0, 1)),
                  pl.BlockSpec((1, D_MODEL), lambda i: (0, 0)),
                  pl.BlockSpec((D_MODEL, width), lambda i: (0, 0))],
        out_specs=pl.BlockSpec((n, width), lambda i: (0, 0)),
        out_shape=jax.ShapeDtypeStruct((n, width), F32),
        compiler_params=_cparams(1),
        name="sample_proj",
    )(xs, mod_s, mod_s, n1, w)


def _sa_kernel(q_ref, kv_ref, ck_ref, cv_ref, bias_ref, sink_ref, ko_ref, vo_ref, oa_ref):
    nb = ROWS_PER_SAMPLE_STEP
    gsz = HA_Q // HA_KV
    rg = lax.broadcasted_iota(jnp.int32, (HA_Q, HD_A), 0) // gsz
    sink = sink_ref[...]
    kw = HA_KV * HD_A
    for i in range(nb):
        ko_ref[0, i, 0:WINDOW - 1, :] = ck_ref[0, i, 1:WINDOW, :]
        ko_ref[0, i, WINDOW - 1:WINDOW, :] = kv_ref[i:i + 1, 0:kw]
        vo_ref[0, i, 0:WINDOW - 1, :] = cv_ref[0, i, 1:WINDOW, :]
        vo_ref[0, i, WINDOW - 1:WINDOW, :] = kv_ref[i:i + 1, kw:2 * kw]
        kmat = ko_ref[0, i].astype(BF16)
        vmat = vo_ref[0, i].astype(BF16)
        q = q_ref[i] * (HD_A ** -0.5)
        qe = jnp.concatenate([jnp.where(rg == g, q, 0.0) for g in range(HA_KV)], axis=1).astype(BF16)
        s = _dot_nt(qe, kmat) + bias_ref[...]
        m = jnp.maximum(jnp.max(s, axis=-1, keepdims=True), sink)
        p = jnp.exp(s - m)
        den = jnp.sum(p, axis=-1, keepdims=True) + jnp.exp(sink - m)
        o = _dot(p.astype(BF16), vmat) / den
        o16 = jnp.zeros((HA_Q, HD_A), F32)
        for g in range(HA_KV):
            o16 = o16 + jnp.where(rg == g, o[:, HD_A * g:HD_A * (g + 1)], 0.0)
        oa_ref[i] = o16


def _sa_call(q3, kv, cache_k, cache_v, bias_s, sink_col, layer):
    n = q3.shape[0]
    nb = ROWS_PER_SAMPLE_STEP
    kw = HA_KV * HD_A
    cspec = pl.BlockSpec((1, nb, WINDOW, kw), lambda i: (layer, i, 0, 0))
    ospec = pl.BlockSpec((1, nb, WINDOW, kw), lambda i: (0, i, 0, 0))
    return pl.pallas_call(
        _sa_kernel,
        grid=(n // nb,),
        in_specs=[pl.BlockSpec((nb, HA_Q, HD_A), lambda i: (i, 0, 0)),
                  pl.BlockSpec((nb, 2 * kw), lambda i: (i, 0)),
                  cspec, cspec,
                  pl.BlockSpec((HA_Q, WINDOW), lambda i: (0, 0)),
                  pl.BlockSpec((HA_Q, 1), lambda i: (0, 0))],
        out_specs=[ospec, ospec, pl.BlockSpec((nb, HA_Q, HD_A), lambda i: (i, 0, 0))],
        out_shape=[jax.ShapeDtypeStruct((1, n, WINDOW, kw), F32),
                   jax.ShapeDtypeStruct((1, n, WINDOW, kw), F32),
                   jax.ShapeDtypeStruct((n, HA_Q, HD_A), F32)],
        compiler_params=_cparams(1),
        name="sample_attn",
    )(q3, kv, cache_k, cache_v, bias_s, sink_col)


def _sr_kernel(zb_ref, cos_ref, sin_ref, gcol_ref, s_ref, so_ref, o_ref):
    nb = ROWS_PER_SAMPLE_STEP
    wq = HB * DK_B
    cos = cos_ref[...]
    sin = sin_ref[...]
    qf = zb_ref[:, 0:wq]
    kf = zb_ref[:, wq:2 * wq]
    qr = qf * cos + _pairswap(qf) * sin
    kr = (kf * cos + _pairswap(kf) * sin) * (DK_B ** -0.5)
    v = zb_ref[:, 2 * wq:2 * wq + HB * DV_B]
    r8 = lax.broadcasted_iota(jnp.int32, (HB, wq), 0)
    hl = lax.broadcasted_iota(jnp.int32, (HB, wq), 1) // DK_B
    rv = lax.broadcasted_iota(jnp.int32, (HB, DV_B), 0)
    for i in range(nb):
        k8 = jnp.where(hl == r8, jnp.broadcast_to(kr[i:i + 1, :], (HB, wq)), 0.0).astype(BF16)
        q8 = jnp.where(hl == r8, jnp.broadcast_to(qr[i:i + 1, :], (HB, wq)), 0.0).astype(BF16)
        v8 = jnp.zeros((HB, DV_B), F32)
        for r in range(HB):
            v8 = jnp.where(rv == r, jnp.broadcast_to(v[i:i + 1, DV_B * r:DV_B * (r + 1)], (HB, DV_B)), v8)
        s_new = gcol_ref[...] * s_ref[0, i] + _dot_tn(k8, v8.astype(BF16))
        so_ref[0, i] = s_new
        o_ref[i] = _dot(q8, s_new.astype(BF16))


def _sr_call(zb, state, layer):
    n = zb.shape[0]
    nb = ROWS_PER_SAMPLE_STEP
    wq = HB * DK_B
    cos, sin = _rot_tables(np.array([PAST_LEN]))
    gcol = np.repeat(_gammas(), DK_B)[:, None] * np.ones((1, DV_B))
    return pl.pallas_call(
        _sr_kernel,
        grid=(n // nb,),
        in_specs=[pl.BlockSpec((nb, SEG_B), lambda i: (i, 0)),
                  pl.BlockSpec((1, wq), lambda i: (0, 0)),
                  pl.BlockSpec((1, wq), lambda i: (0, 0)),
                  pl.BlockSpec((wq, DV_B), lambda i: (0, 0)),
                  pl.BlockSpec((1, nb, wq, DV_B), lambda i: (layer, i, 0, 0))],
        out_specs=[pl.BlockSpec((1, nb, wq, DV_B), lambda i: (0, i, 0, 0)),
                   pl.BlockSpec((nb, HB, DV_B), lambda i: (i, 0, 0))],
        out_shape=[jax.ShapeDtypeStruct((1, n, wq, DV_B), F32),
                   jax.ShapeDtypeStruct((n, HB, DV_B), F32)],
        compiler_params=_cparams(1),
        name="sample_retention",
    )(zb, jnp.asarray(cos), jnp.asarray(sin), jnp.asarray(gcol.astype(np.float32)), state)


def _ss_kernel(zc_ref, cs_ref, cw_ref, cb_ref, dtb_ref, a_ref, e3_ref, h_ref,
               ho_ref, cso_ref, y_ref, xc_ref):
    nb = ROWS_PER_SAMPLE_STEP
    cx = zc_ref[:, 1024:2560]
    taps = [cs_ref[0, :, CONV_DIM * i:CONV_DIM * (i + 1)] for i in range(CONV_W - 1)] + [cx]
    acc = cb_ref[...]
    for i in range(CONV_W):
        acc = acc + taps[i] * cw_ref[i:i + 1, :]
    xbc = _silu(acc)
    cso_ref[...] = jnp.concatenate(taps[1:], axis=1)
    xc = xbc[:, 0:1024]
    bmat = xbc[:, 1024:1280]
    cmat = xbc[:, 1280:1536]
    dt = _softplus(zc_ref[:, C_DT:C_DT + 128] + dtb_ref[...])
    da = jnp.exp(dt * a_ref[...])
    dt_e = _dot(jnp.concatenate(_split3(dt), axis=1), e3_ref[...])
    da_e = _dot(jnp.concatenate(_split3(da), axis=1), e3_ref[...])
    dtx = dt_e * xc
    gw = (HC // G_C) * HD_C
    r8 = lax.broadcasted_iota(jnp.int32, (nb, gw), 0)
    rn = lax.broadcasted_iota(jnp.int32, (nb, N_C), 0)
    ones8 = jnp.ones((nb, N_C), BF16)
    ycols = []
    for g in range(G_C):
        ws = slice(gw * g, gw * (g + 1))
        ns = slice(N_C * g, N_C * (g + 1))
        bg16 = bmat[:, ns].astype(BF16)
        yacc = jnp.zeros((nb, gw), F32)
        for i in range(nb):
            x8 = jnp.where(r8 == i, dtx[:, ws], 0.0).astype(BF16)
            outer = _dot_tn(x8, bg16)
            hi, mid, lo = (p.astype(F32) for p in _split3(jnp.broadcast_to(da_e[i:i + 1, ws], (nb, gw))))
            l3 = jnp.where(r8 == 0, hi, jnp.where(r8 == 1, mid, jnp.where(r8 == 2, lo, 0.0)))
            dacol = _dot_tn(l3.astype(BF16), ones8)
            h_new = dacol * h_ref[0, i, ws, :] + outer
            ho_ref[0, i, ws, :] = h_new
            c8 = jnp.where(rn == i, cmat[:, ns], 0.0).astype(BF16)
            yacc = yacc + _dot_nt(c8, h_new.astype(BF16))
        ycols.append(yacc)
    y_ref[...] = jnp.concatenate(ycols, axis=1)
    xc_ref[...] = xc


def _ss_call(zc, conv_state, hstate, cw, cb, dtb_pad, a_pad, layer):
    n = zc.shape[0]
    nb = ROWS_PER_SAMPLE_STEP
    e = np.zeros((128, D_MODEL), np.float32)
    for hh in range(HC):
        e[hh, HD_C * hh:HD_C * (hh + 1)] = 1.0
    e3 = jnp.asarray(np.concatenate([e, e, e], axis=0), dtype=BF16)
    cwid = (CONV_W - 1) * CONV_DIM
    full = lambda i: (0, 0)
    return pl.pallas_call(
        _ss_kernel,
        grid=(n // nb,),
        in_specs=[pl.BlockSpec((nb, SEG_C), lambda i: (i, 0)),
                  pl.BlockSpec((1, nb, cwid), lambda i: (layer, i, 0)),
                  pl.BlockSpec((CONV_W, CONV_DIM), full),
                  pl.BlockSpec((1, CONV_DIM), full),
                  pl.BlockSpec((1, 128), full),
                  pl.BlockSpec((1, 128), full),
                  pl.BlockSpec((3 * 128, D_MODEL), full),
                  pl.BlockSpec((1, nb, HC * HD_C, N_C), lambda i: (layer, i, 0, 0))],
        out_specs=[pl.BlockSpec((1, nb, HC * HD_C, N_C), lambda i: (0, i, 0, 0)),
                   pl.BlockSpec((nb, cwid), lambda i: (i, 0)),
                   pl.BlockSpec((nb, D_MODEL), lambda i: (i, 0)),
                   pl.BlockSpec((nb, D_MODEL), lambda i: (i, 0))],
        out_shape=[jax.ShapeDtypeStruct((1, n, HC * HD_C, N_C), F32),
                   jax.ShapeDtypeStruct((n, cwid), F32),
                   jax.ShapeDtypeStruct((n, D_MODEL), F32),
                   jax.ShapeDtypeStruct((n, D_MODEL), F32)],
        compiler_params=_cparams(1),
        name="sample_ssd",
    )(zc, conv_state, cw, cb, dtb_pad, a_pad, e3, hstate)


def _sm_kernel(x_ref, g1_ref, oa_ref, oret_ref, y_ref, xc_ref, za_ref, zb_ref, zc_ref,
               dsk_ref, snw_ref, wo_ref, o_ref):
    ob = jnp.concatenate([_rms_lanes(oret_ref[:, DV_B * hh:DV_B * (hh + 1)]) for hh in range(HB)], axis=1)
    ob = _silu(zb_ref[:, 2048:3072]) * ob
    yc = (y_ref[...] + dsk_ref[...] * xc_ref[...]) * _silu(zc_ref[:, 0:1024])
    gw = D_MODEL // G_C
    oc = jnp.concatenate([_rms_lanes(yc[:, gw * g:gw * (g + 1)]) for g in range(G_C)], axis=1) * snw_ref[...]
    mix = (_sigmoid(za_ref[:, 1536:2560]) * oa_ref[...] + _sigmoid(zb_ref[:, 3072:4096]) * ob
           + _sigmoid(zc_ref[:, C_GC:C_GC + 1024]) * oc)
    o_ref[...] = x_ref[...] + g1_ref[...] * _dot(mix.astype(BF16), wo_ref[...])


def _sm_call(xs, mod_s, oa, oret, y, xc, za, zb, zc, dsk_full, snw, wo):
    n = xs.shape[0]
    full = lambda i: (0, 0)
    row = lambda w: pl.BlockSpec((n, w), full)
    return pl.pallas_call(
        _sm_kernel,
        grid=(1,),
        in_specs=[row(D_MODEL),
                  pl.BlockSpec((n, D_MODEL), lambda i: (0, 2)),
                  row(D_MODEL), row(D_MODEL), row(D_MODEL), row(D_MODEL),
                  row(SEG_A), row(SEG_B), row(SEG_C),
                  pl.BlockSpec((1, D_MODEL), full),
                  pl.BlockSpec((1, D_MODEL), full),
                  pl.BlockSpec((D_MODEL, D_MODEL), full)],
        out_specs=row(D_MODEL),
        out_shape=jax.ShapeDtypeStruct((n, D_MODEL), F32),
        compiler_params=_cparams(1),
        name="sample_merge_out",
    )(xs, mod_s, oa, oret, y, xc, za, zb, zc, dsk_full, snw, wo)


PROMPT_TILE = 512
MLP_ROWS = 1024
MLP_FF = 512


def _prep_w_in(w):
    wa = jnp.concatenate([w[:, O_AQ:O_BQ], w[:, O_GTS:O_GTS + 1024]], axis=1)
    wb = jnp.concatenate([w[:, O_BQ:O_CZ], w[:, O_GTS + 1024:O_GTS + 2048]], axis=1)
    wc = jnp.concatenate([w[:, O_CZ:O_CDT], jnp.pad(w[:, O_CDT:O_GTS], ((0, 0), (0, 128 - HC))),
                          w[:, O_GTS + 2048:O_GTS + 3072]], axis=1)
    return wa.astype(BF16), wb.astype(BF16), wc.astype(BF16)


def _forward(x_prompt, x_sample, cache_win_k, cache_win_v, state_ret, state_ssm, state_conv,
             c_prompt, c_sample, rel_bias_table, attn_sinks, norm1_w, norm2_w, ada_w, ada_b,
             w_in, conv_w, conv_b, dt_bias, A_log, D_skip, ssm_norm_w, w_out, w_up, w_down,
             final_norm_w, *, prompt_tile, mlp_rows, mlp_ff):
    B, T, _ = x_prompt.shape
    DB = x_sample.shape[0]
    kw = HA_KV * HD_A

    mod_all = _ada_call(jnp.concatenate([c_prompt, c_sample], axis=0), ada_w, ada_b)

    tbl = _t5_bias_by_dist(rel_bias_table)
    qi = np.arange(WINDOW)[:, None]
    kj = np.arange(WINDOW)[None, :]
    d_prev = qi + WINDOW - kj
    d_cur = qi - kj
    bias_prev = jnp.where(jnp.asarray(kj > qi)[None], jnp.moveaxis(tbl[np.clip(d_prev, 0, WINDOW - 1)], -1, 0), NEG)
    bias_cur = jnp.where(jnp.asarray(kj <= qi)[None], jnp.moveaxis(tbl[np.clip(d_cur, 0, WINDOW - 1)], -1, 0), NEG)
    bias_s = tbl[::-1].T
    bias_kq = jnp.concatenate([jnp.swapaxes(bias_prev, 1, 2), jnp.swapaxes(bias_cur, 1, 2)], axis=1)
    bias_t = bias_kq.reshape(HA_KV, HA_Q // HA_KV, 2 * WINDOW, WINDOW).transpose(0, 2, 1, 3).reshape(
        HA_KV, 2 * WINDOW, (HA_Q // HA_KV) * WINDOW)

    wu16 = w_up.astype(BF16)
    wd16 = w_down.astype(BF16)
    wo16 = w_out.astype(BF16)
    fn = final_norm_w.reshape(1, D_MODEL)
    ck = cache_win_k.reshape(DEPTH, DB, WINDOW, kw)
    cv = cache_win_v.reshape(DEPTH, DB, WINDOW, kw)
    sret = state_ret.reshape(DEPTH, DB, HB * DK_B, DV_B)
    sssm = state_ssm.reshape(DEPTH, DB, HC * HD_C, N_C)
    sconv = state_conv.reshape(DEPTH, DB, (CONV_W - 1) * CONV_DIM)

    xp = x_prompt
    xs = x_sample.reshape(DB, D_MODEL)
    outs_p = [[] for _ in range(5)]
    outs_s = [[] for _ in range(5)]
    for l in range(DEPTH):
        wa, wb, wc = _prep_w_in(w_in[l])
        n1 = norm1_w[l].reshape(1, D_MODEL)
        n2 = norm2_w[l].reshape(1, D_MODEL)
        cw = conv_w[l]
        cb = conv_b[l].reshape(1, CONV_DIM)
        dtb_pad = jnp.pad(dt_bias[l], (0, 128 - HC)).reshape(1, 128)
        a_pad = jnp.pad(-jnp.exp(A_log[l].astype(F32)), (0, 128 - HC)).reshape(1, 128)
        dsk_full = jnp.repeat(D_skip[l], HD_C).reshape(1, D_MODEL)
        snw = ssm_norm_w[l].reshape(1, D_MODEL)
        final = l == DEPTH - 1
        mod_p = mod_all[l, :B].reshape(B, 1, 6 * D_MODEL)
        mod_s = mod_all[l, B:]

        wl = w_in[l]
        wqgt = jnp.concatenate([wl[:, O_AQ:O_AK] * (HD_A ** -0.5), wl[:, O_GTS:O_GTS + D_MODEL]], axis=1).T.astype(BF16)
        wkv = wl[:, O_AK:O_BQ].astype(BF16)
        sink_rows = jnp.repeat(attn_sinks[l], WINDOW).reshape(HA_KV, 1, (HA_Q // HA_KV) * WINDOW)
        mixa, kbuf, vbuf = _pa_call(xp, mod_p, n1, wqgt, wkv, bias_t, sink_rows, prompt_tile)
        mixab, s_ret = _pb_call(xp, mod_p, n1, wb, mixa, prompt_tile)
        x1, h_ssm, conv_new = _pc_call(xp, mod_p, n1, wc, cw, cb, dtb_pad, a_pad, dsk_full, snw,
                                       mixab, wo16[l], prompt_tile)
        xp = _mlp_call(x1.reshape(B * T, D_MODEL), mod_p, n2, wu16, wd16, fn, l,
                       mlp_rows, mlp_ff, T, final).reshape(B, T, D_MODEL)
        for lst, v in zip(outs_p, (kbuf.reshape(B, WINDOW, HA_KV, HD_A), vbuf.reshape(B, WINDOW, HA_KV, HD_A),
                                   s_ret.reshape(B, HB, DK_B, DV_B), h_ssm.reshape(B, HC, HD_C, N_C), conv_new)):
            lst.append(v)

        za = _sproj_call(xs, mod_s, n1, wa)
        zb = _sproj_call(xs, mod_s, n1, wb)
        zc = _sproj_call(xs, mod_s, n1, wc)
        q3 = za[:, 0:HA_Q * HD_A].reshape(DB, HA_Q, HD_A)
        kv = za[:, HA_Q * HD_A:HA_Q * HD_A + 2 * kw]
        ck_new, cv_new, oa3 = _sa_call(q3, kv, ck, cv, bias_s, attn_sinks[l].reshape(HA_Q, 1), l)
        s_new, o3 = _sr_call(zb, sret, l)
        h_new, cs_new, y_s, xc_s = _ss_call(zc, sconv, sssm, cw, cb, dtb_pad, a_pad, l)
        xs1 = _sm_call(xs, mod_s, oa3.reshape(DB, D_MODEL), o3.reshape(DB, D_MODEL), y_s, xc_s,
                       za, zb, zc, dsk_full, snw, wo16[l])
        xs = _mlp_call(xs1, mod_s, n2, wu16, wd16, fn, l, DB, mlp_ff, 1, final)
        for lst, v in zip(outs_s, (ck_new.reshape(DB, WINDOW, HA_KV, HD_A), cv_new.reshape(DB, WINDOW, HA_KV, HD_A),
                                   s_new.reshape(DB, HB, DK_B, DV_B), h_new.reshape(DB, HC, HD_C, N_C),
                                   cs_new.reshape(DB, CONV_W - 1, CONV_DIM))):
            lst.append(v)

    stk = lambda lst: jnp.stack(lst, axis=0)
    return (xp, xs.reshape(DB, 1, D_MODEL),
            stk(outs_p[0]), stk(outs_p[1]), stk(outs_p[2]), stk(outs_p[3]), stk(outs_p[4]),
            stk(outs_s[0]), stk(outs_s[1]), stk(outs_s[2]), stk(outs_s[3]), stk(outs_s[4]))


def kernel(x_prompt, x_sample, cache_win_k, cache_win_v, state_ret, state_ssm, state_conv, c_prompt, c_sample,
           rel_bias_table, attn_sinks, norm1_w, norm2_w, ada_w, ada_b, w_in, conv_w, conv_b, dt_bias, A_log,
           D_skip, ssm_norm_w, w_out, w_up, w_down, final_norm_w):
    return _forward(x_prompt, x_sample, cache_win_k, cache_win_v, state_ret, state_ssm, state_conv,
                    c_prompt, c_sample, rel_bias_table, attn_sinks, norm1_w, norm2_w, ada_w, ada_b,
                    w_in, conv_w, conv_b, dt_bias, A_log, D_skip, ssm_norm_w, w_out, w_up, w_down,
                    final_norm_w, prompt_tile=PROMPT_TILE, mlp_rows=MLP_ROWS, mlp_ff=MLP_FF)
```

```python
import functools
import math

import numpy as np
import jax
import jax.numpy as jnp
from jax import lax
from jax.experimental import pallas as pl
from jax.experimental.pallas import tpu as pltpu

F32 = jnp.float32
BF16 = jnp.bfloat16

D_MODEL = 1024
DEPTH = 2
PAST_LEN = 16384
WINDOW = 128
HA_Q = 16
HA_KV = 4
HD_A = 64
NUM_BUCKETS = 32
MAX_DISTANCE = WINDOW
HB = 8
DK_B = 64
DV_B = 128
HC = 16
HD_C = 64
N_C = 128
G_C = 2
CONV_W = 4
CONV_DIM = D_MODEL + 2 * G_C * N_C
D_FF = 4 * D_MODEL
EPS = 1e-6
NEG = -1e30

O_AQ, O_AK, O_AV = 0, 1024, 1280
O_BQ, O_BK, O_BV, O_BG = 1536, 2048, 2560, 3584
O_CZ, O_CXBC, O_CDT, O_GTS = 4608, 5632, 7168, 7184
SEG_A = 2560
SEG_B = 4096
SEG_C = 3712
C_DT = 2560
C_GC = 2688

VMEM_LIMIT_V7X = 56 * 1024 * 1024
ROWS_PER_SAMPLE_STEP = 8


def _cparams(n_axes):
    return pltpu.CompilerParams(dimension_semantics=("arbitrary",) * n_axes,
                                vmem_limit_bytes=VMEM_LIMIT_V7X)


def _dot(a, b):
    return jnp.dot(a, b, preferred_element_type=F32)


def _dot_nt(a, b):
    return lax.dot_general(a, b, (((1,), (1,)), ((), ())), preferred_element_type=F32)


def _dot_tn(a, b):
    return lax.dot_general(a, b, (((0,), (0,)), ((), ())), preferred_element_type=F32)


def _sigmoid(x):
    return 0.5 * (jnp.tanh(0.5 * x) + 1.0)


def _silu(x):
    return x * _sigmoid(x)


def _softplus(x):
    return jnp.maximum(x, 0.0) + jnp.log1p(jnp.exp(-jnp.abs(x)))


def _modnorm(x, nw, sc, sh):
    ms = jnp.mean(x * x, axis=-1, keepdims=True)
    return (x * lax.rsqrt(ms + EPS) * nw) * (1.0 + sc) + sh


def _rms_lanes(x):
    ms = jnp.mean(x * x, axis=-1, keepdims=True)
    return x * lax.rsqrt(ms + EPS)


def _pairswap(x):
    ax = x.ndim - 1
    n = x.shape[ax]
    lane = lax.broadcasted_iota(jnp.int32, x.shape, ax)
    nxt = pltpu.roll(x, n - 1, ax)
    prv = pltpu.roll(x, 1, ax)
    return jnp.where((lane & 1) == 0, nxt, prv)


def _split3(x):
    hi = x.astype(BF16)
    r1 = x - hi.astype(F32)
    mid = r1.astype(BF16)
    lo = (r1 - mid.astype(F32)).astype(BF16)
    return hi, mid, lo


def _gammas():
    return 1.0 - 2.0 ** (-5.0 - np.arange(HB, dtype=np.float64))


def _rot_tables(pos):
    theta = 1.0 / (10000.0 ** np.linspace(0.0, 1.0, DK_B // 2))
    ang = np.asarray(pos, np.float64)[:, None] * theta[None, :]
    cos = np.repeat(np.cos(ang), 2, axis=1)
    sin = np.repeat(np.sin(ang), 2, axis=1)
    sin[:, 0::2] *= -1.0
    return (np.tile(cos, (1, HB)).astype(np.float32), np.tile(sin, (1, HB)).astype(np.float32))


def _ret_tables():
    g = _gammas()
    L = WINDOW
    i = np.arange(L, dtype=np.float64)
    diff = i[:, None] - i[None, :]
    dm = np.where(diff >= 0, g[:, None, None] ** np.maximum(diff, 0.0), 0.0)
    qdec = np.repeat(g[None, :] ** (i[:, None] + 1.0), DK_B, axis=1)
    kdec = np.repeat(g[None, :] ** (L - 1.0 - i[:, None]), DK_B, axis=1)
    return dm.astype(np.float32), qdec.astype(np.float32), kdec.astype(np.float32)


def _t5_bucket_np(dist):
    max_exact = NUM_BUCKETS // 2
    n = np.maximum(dist, 0)
    nf = np.maximum(n, 1).astype(np.float32)
    large = max_exact + (np.log(nf / np.float32(max_exact)) / np.float32(math.log(MAX_DISTANCE / max_exact))
                         * np.float32(NUM_BUCKETS - max_exact)).astype(np.int32)
    large = np.minimum(large, NUM_BUCKETS - 1)
    return np.where(n < max_exact, n, large)


def _bias_tables(rel_table):
    gsz = HA_Q // HA_KV
    qi = np.arange(WINDOW)[None, :]
    kj = np.arange(WINDOW)[:, None]
    dist = np.concatenate([qi + WINDOW - kj, qi - kj], axis=0)
    valid = (dist >= 0) & (dist < WINDOW)
    onehot = (_t5_bucket_np(dist)[..., None] == np.arange(NUM_BUCKETS)) & valid[..., None]
    tab = rel_table.astype(F32)
    bias_kq = jnp.einsum('kqb,bh->hkq', jnp.asarray(onehot, F32), tab, precision=lax.Precision.HIGHEST)
    bias_kq = bias_kq + jnp.asarray(np.where(valid, 0.0, NEG), F32)[None]
    bias_t = bias_kq.reshape(HA_KV, gsz, 2 * WINDOW, WINDOW).transpose(0, 2, 1, 3).reshape(HA_KV, 2 * WINDOW, gsz * WINDOW)
    oh_s = _t5_bucket_np(WINDOW - 1 - np.arange(WINDOW))[:, None] == np.arange(NUM_BUCKETS)
    bias_s = jnp.einsum('jb,bh->hj', jnp.asarray(oh_s, F32), tab, precision=lax.Precision.HIGHEST)
    return bias_t, bias_s


def _ada_kernel(c_ref, w_ref, b_ref, o_ref):
    s = _silu(c_ref[...])
    o_ref[0] = _dot(s.astype(BF16), w_ref[0].astype(BF16)) + b_ref[0]


def _ada_call(c_all, ada_w, ada_b):
    n = c_all.shape[0]
    nb = 6
    return pl.pallas_call(
        _ada_kernel,
        grid=(DEPTH, nb),
        in_specs=[pl.BlockSpec((n, D_MODEL), lambda l, j: (0, 0)),
                  pl.BlockSpec((1, D_MODEL, D_MODEL), lambda l, j: (l, 0, j)),
                  pl.BlockSpec((1, 1, D_MODEL), lambda l, j: (l, 0, j))],
        out_specs=pl.BlockSpec((1, n, D_MODEL), lambda l, j: (l, 0, j)),
        out_shape=jax.ShapeDtypeStruct((DEPTH, n, 6 * D_MODEL), F32),
        compiler_params=_cparams(2),
        name="ada_mod",
    )(c_all, ada_w, ada_b.reshape(DEPTH, 1, 6 * D_MODEL))


def _pa_kernel(x_ref, mod_ref, n1_ref, wt_ref, wkv_ref, bias_ref, sink_ref,
               mix_ref, ko_ref, vo_ref, zt_ref, kv_ref, kprev_ref, vtprev_ref, pen_ref, *, tt):
    t = pl.program_id(1)
    nchunk = tt // WINDOW
    kw = HA_KV * HD_A
    gsz = HA_Q // HA_KV

    @pl.when(t == 0)
    def _():
        kprev_ref[...] = jnp.zeros_like(kprev_ref)
        vtprev_ref[...] = jnp.zeros_like(vtprev_ref)
        pen_ref[...] = jnp.full(pen_ref.shape, NEG, F32)

    mod = mod_ref[0]
    h = _modnorm(x_ref[0], n1_ref[...], mod[:, D_MODEL:2 * D_MODEL], mod[:, 0:D_MODEL]).astype(BF16)
    zt = _dot_nt(wt_ref[...], h)
    for c in range(nchunk):
        zt_ref[c] = zt[:, WINDOW * c:WINDOW * (c + 1)]
    kv_ref[...] = _dot(h, wkv_ref[...])

    def chunk(c, carry):
        r0 = pl.multiple_of(c * WINDOW, WINDOW)
        rows = pl.ds(r0, WINDOW)
        kc = kv_ref[rows, 0:kw]
        vc = kv_ref[rows, kw:2 * kw]
        vt = vc.T
        kk = jnp.concatenate([kprev_ref[...], kc], axis=0).astype(BF16)
        vvt = jnp.concatenate([vtprev_ref[...], vt], axis=1).astype(BF16)
        qt = zt_ref[c, 0:D_MODEL, :].astype(BF16)
        pen = pen_ref[0:1, :]
        s_all = []
        for g in range(HA_KV):
            gs = slice(HD_A * g, HD_A * (g + 1))
            qcat = jnp.concatenate([qt[HD_A * (gsz * g + j):HD_A * (gsz * g + j + 1), :] for j in range(gsz)], axis=1)
            s_all.append(_dot(kk[:, gs], qcat))
        p_all = []
        for g in range(HA_KV):
            s = s_all[g] + bias_ref[g]
            sp = s[0:WINDOW, :] + pen
            sc = s[WINDOW:2 * WINDOW, :]
            sink = sink_ref[g]
            m = jnp.maximum(jnp.maximum(jnp.max(sp, axis=0, keepdims=True), jnp.max(sc, axis=0, keepdims=True)), sink)
            pp = jnp.exp(sp - m)
            pc = jnp.exp(sc - m)
            den = jnp.sum(pp, axis=0, keepdims=True) + jnp.sum(pc, axis=0, keepdims=True) + jnp.exp(sink - m)
            p_all.append((jnp.concatenate([pp, pc], axis=0).astype(BF16), 1.0 / den))
        pieces = []
        for g in range(HA_KV):
            gs = slice(HD_A * g, HD_A * (g + 1))
            p, rden = p_all[g]
            ot = _dot(vvt[gs, :], p) * rden
            pieces += [ot[:, WINDOW * j:WINDOW * (j + 1)] for j in range(gsz)]
        oat = jnp.concatenate(pieces, axis=0)
        mixt = _sigmoid(zt_ref[c, D_MODEL:2 * D_MODEL, :]) * oat
        mix_ref[0, rows, :] = mixt.T
        kprev_ref[...] = kc
        vtprev_ref[...] = vt
        pen_ref[...] = jnp.zeros_like(pen_ref)
        ko_ref[0] = kc
        vo_ref[0] = vc
        return carry

    lax.fori_loop(0, nchunk, chunk, 0)


def _pa_call(x, mod3, n1, wqgt, wkv, bias_t, sink_rows, tt):
    B, T, _ = x.shape
    kw = HA_KV * HD_A
    qw = (HA_Q // HA_KV) * WINDOW
    kern = functools.partial(_pa_kernel, tt=tt)
    return pl.pallas_call(
        kern,
        grid=(B, T // tt),
        in_specs=[pl.BlockSpec((1, tt, D_MODEL), lambda b, t: (b, t, 0)),
                  pl.BlockSpec((1, 1, 6 * D_MODEL), lambda b, t: (b, 0, 0)),
                  pl.BlockSpec((1, D_MODEL), lambda b, t: (0, 0)),
                  pl.BlockSpec((2 * D_MODEL, D_MODEL), lambda b, t: (0, 0)),
                  pl.BlockSpec((D_MODEL, 2 * kw), lambda b, t: (0, 0)),
                  pl.BlockSpec((HA_KV, 2 * WINDOW, qw), lambda b, t: (0, 0, 0)),
                  pl.BlockSpec((HA_KV, 1, qw), lambda b, t: (0, 0, 0))],
        out_specs=[pl.BlockSpec((1, tt, D_MODEL), lambda b, t: (b, t, 0)),
                   pl.BlockSpec((1, WINDOW, kw), lambda b, t: (b, 0, 0)),
                   pl.BlockSpec((1, WINDOW, kw), lambda b, t: (b, 0, 0))],
        out_shape=[jax.ShapeDtypeStruct((B, T, D_MODEL), F32),
                   jax.ShapeDtypeStruct((B, WINDOW, kw), F32),
                   jax.ShapeDtypeStruct((B, WINDOW, kw), F32)],
        scratch_shapes=[pltpu.VMEM((tt // WINDOW, 2 * D_MODEL, WINDOW), F32),
                        pltpu.VMEM((tt, 2 * kw), F32),
                        pltpu.VMEM((WINDOW, kw), F32),
                        pltpu.VMEM((kw, WINDOW), F32),
                        pltpu.VMEM((8, qw), F32)],
        compiler_params=_cparams(2),
        name="prompt_attn",
    )(x, mod3, n1, wqgt, wkv, bias_t, sink_rows)


def _pb_kernel(x_ref, mod_ref, n1_ref, wt_ref, cos_ref, sin_ref, qdec_ref, kdec_ref, dm_ref, mixa_ref,
               mix_ref, so_ref, zt_ref, s_ref, *, tt, glast):
    t = pl.program_id(1)
    nchunk = tt // WINDOW
    hw = HB * DK_B // 2
    hp = DK_B // 2

    @pl.when(t == 0)
    def _():
        s_ref[...] = jnp.zeros_like(s_ref)

    mod = mod_ref[0]
    h = _modnorm(x_ref[0], n1_ref[...], mod[:, D_MODEL:2 * D_MODEL], mod[:, 0:D_MODEL]).astype(BF16)
    for cc in range(nchunk // 2):
        z2 = _dot_nt(wt_ref[...], h[2 * WINDOW * cc:2 * WINDOW * (cc + 1), :])
        zt_ref[2 * cc] = z2[:, 0:WINDOW]
        zt_ref[2 * cc + 1] = z2[:, WINDOW:2 * WINDOW]

    def head_rows(pair, hh):
        return jnp.concatenate([pair[0][hp * hh:hp * (hh + 1), :], pair[1][hp * hh:hp * (hh + 1), :]], axis=0)

    def chunk(c, carry):
        r0 = pl.multiple_of(c * WINDOW, WINDOW)
        rows = pl.ds(r0, WINDOW)
        cos = cos_ref[c]
        sin = sin_ref[c]
        q1 = zt_ref[c, 0:hw, :]
        q2 = zt_ref[c, hw:2 * hw, :]
        k1 = zt_ref[c, 2 * hw:3 * hw, :]
        k2 = zt_ref[c, 3 * hw:4 * hw, :]
        rq = (q1 * cos - q2 * sin, q1 * sin + q2 * cos)
        rk = (k1 * cos - k2 * sin, k1 * sin + k2 * cos)
        qdec = qdec_ref[...]
        kdec = kdec_ref[...]
        qb = tuple(a.astype(BF16) for a in rq)
        kb = tuple(a.astype(BF16) for a in rk)
        qd = tuple((a * qdec).astype(BF16) for a in rq)
        kd = tuple((a * kdec).astype(BF16) for a in rk)
        vt = zt_ref[c, 4 * hw:4 * hw + HB * DV_B, :].astype(BF16)
        inner, cross, supd = [], [], []
        for hh in range(HB):
            s_old = s_ref[DK_B * hh:DK_B * (hh + 1), :]
            inner.append(_dot_tn(head_rows(qb, hh), head_rows(kb, hh)))
            cross.append(_dot_tn(s_old.astype(BF16), head_rows(qd, hh)))
            supd.append(glast[hh] * s_old + _dot_nt(head_rows(kd, hh), vt[DV_B * hh:DV_B * (hh + 1), :]))
        innd = [(inner[hh] * dm_ref[hh]).astype(BF16) for hh in range(HB)]
        outs = []
        for hh in range(HB):
            ot = _dot_nt(vt[DV_B * hh:DV_B * (hh + 1), :], innd[hh]) + cross[hh]
            s_ref[DK_B * hh:DK_B * (hh + 1), :] = supd[hh]
            ms = jnp.mean(ot * ot, axis=0, keepdims=True)
            outs.append(ot * lax.rsqrt(ms + EPS))
        obt = jnp.concatenate(outs, axis=0)
        bgt = zt_ref[c, 4 * hw + D_MODEL:4 * hw + 2 * D_MODEL, :]
        gbt = zt_ref[c, 4 * hw + 2 * D_MODEL:4 * hw + 3 * D_MODEL, :]
        mixt = _sigmoid(gbt) * (_silu(bgt) * obt)
        mix_ref[0, rows, :] = mixa_ref[0, rows, :] + mixt.T
        return carry

    lax.fori_loop(0, nchunk, chunk, 0)
    so_ref[0] = s_ref[...]


def _pair_perm():
    p = np.arange(2)[:, None, None]
    hh = np.arange(HB)[None, :, None]
    i = np.arange(DK_B // 2)[None, None, :]
    return (hh * DK_B + 2 * i + p).reshape(-1)


def _rot_tables_t(T):
    theta = 1.0 / (10000.0 ** np.linspace(0.0, 1.0, DK_B // 2))
    ang = theta[:, None] * np.arange(T, dtype=np.float64)[None, :]

    def lay(a):
        a = np.tile(a, (HB, 1))
        return np.ascontiguousarray(a.reshape(a.shape[0], T // WINDOW, WINDOW).transpose(1, 0, 2)).astype(np.float32)

    return lay(np.cos(ang)), lay(np.sin(ang))


def _pb_call(x, mod3, n1, wbt, mixa, tt):
    B, T, _ = x.shape
    hw = HB * DK_B // 2
    cos, sin = _rot_tables_t(T)
    dm, qdec, kdec = _ret_tables()
    qdec_t = np.ascontiguousarray(qdec[:, ::2].T)
    kdec_t = np.ascontiguousarray(kdec[:, ::2].T)
    glast = tuple(float(v) for v in (_gammas() ** WINDOW))
    kern = functools.partial(_pb_kernel, tt=tt, glast=glast)
    full2 = lambda b, t: (0, 0)
    nct = tt // WINDOW
    return pl.pallas_call(
        kern,
        grid=(B, T // tt),
        in_specs=[pl.BlockSpec((1, tt, D_MODEL), lambda b, t: (b, t, 0)),
                  pl.BlockSpec((1, 1, 6 * D_MODEL), lambda b, t: (b, 0, 0)),
                  pl.BlockSpec((1, D_MODEL), full2),
                  pl.BlockSpec((SEG_B, D_MODEL), full2),
                  pl.BlockSpec((nct, hw, WINDOW), lambda b, t: (t, 0, 0)),
                  pl.BlockSpec((nct, hw, WINDOW), lambda b, t: (t, 0, 0)),
                  pl.BlockSpec((hw, WINDOW), full2),
                  pl.BlockSpec((hw, WINDOW), full2),
                  pl.BlockSpec((HB, WINDOW, WINDOW), lambda b, t: (0, 0, 0)),
                  pl.BlockSpec((1, tt, D_MODEL), lambda b, t: (b, t, 0))],
        out_specs=[pl.BlockSpec((1, tt, D_MODEL), lambda b, t: (b, t, 0)),
                   pl.BlockSpec((1, HB * DK_B, DV_B), lambda b, t: (b, 0, 0))],
        out_shape=[jax.ShapeDtypeStruct((B, T, D_MODEL), F32),
                   jax.ShapeDtypeStruct((B, HB * DK_B, DV_B), F32)],
        scratch_shapes=[pltpu.VMEM((nct, SEG_B, WINDOW), F32),
                        pltpu.VMEM((HB * DK_B, DV_B), F32)],
        compiler_params=_cparams(2),
        name="prompt_retention",
    )(x, mod3, n1, wbt, jnp.asarray(cos), jnp.asarray(sin), jnp.asarray(qdec_t), jnp.asarray(kdec_t),
      jnp.asarray(dm), mixa)


def _pc_kernel(x_ref, mod_ref, n1_ref, w_ref, cw_ref, cb_ref, dtb_ref, a_ref, dsk_ref, snw_ref, tri_ref,
               mixab_ref, wo_ref,
               xo_ref, ho_ref, co_ref,
               z_ref, xbuf_ref, xbc_ref, dt_ref, hst_ref, mixs_ref, *, tt):
    t = pl.program_id(1)

    @pl.when(t == 0)
    def _():
        xbuf_ref[0:8, :] = jnp.zeros((8, CONV_DIM), F32)
        hst_ref[...] = jnp.zeros_like(hst_ref)

    x = x_ref[0]
    mod = mod_ref[0]
    h = _modnorm(x, n1_ref[...], mod[:, D_MODEL:2 * D_MODEL], mod[:, 0:D_MODEL])
    z_ref[...] = _dot(h.astype(BF16), w_ref[...])

    xbuf_ref[8:8 + tt, :] = z_ref[:, 1024:2560]
    acc = cb_ref[...]
    for i in range(CONV_W):
        acc = acc + xbuf_ref[5 + i:5 + i + tt, :] * cw_ref[i:i + 1, :]
    xbc_ref[...] = _silu(acc)
    co_ref[0] = xbuf_ref[tt + 5:tt + 8, :]
    xbuf_ref[0:8, :] = xbuf_ref[tt:tt + 8, :]
    dt_ref[...] = _softplus(z_ref[:, C_DT:C_DT + 128] + dtb_ref[...])

    ii = lax.broadcasted_iota(jnp.int32, (WINDOW, WINDOW), 0)
    jj = lax.broadcasted_iota(jnp.int32, (WINDOW, WINDOW), 1)
    causal = ii >= jj
    hpg = HC // G_C

    def chunk(c, carry):
        r0 = pl.multiple_of(c * WINDOW, WINDOW)
        rows = pl.ds(r0, WINDOW)
        xc = xbc_ref[rows, 0:1024]
        bmat = xbc_ref[rows, 1024:1280]
        cmat = xbc_ref[rows, 1280:1536]
        dtc = dt_ref[rows, :]
        acum = jnp.dot(tri_ref[...], dtc * a_ref[...], precision=lax.Precision.HIGHEST,
                       preferred_element_type=F32)
        acum_t = acum.T
        dt_t = dtc.T
        x_t = xc.T
        xb = xc.astype(BF16)
        bb = bmat.astype(BF16)
        cb16 = cmat.astype(BF16)
        ys = []
        for g in range(G_C):
            ns = slice(N_C * g, N_C * (g + 1))
            cbg = _dot_nt(cb16[:, ns], bb[:, ns])
            for hh in range(hpg * g, hpg * (g + 1)):
                ps = slice(HD_C * hh, HD_C * (hh + 1))
                colb = jnp.broadcast_to(acum[:, hh:hh + 1], (WINDOW, WINDOW))
                row = acum_t[hh:hh + 1, :]
                dtrow = dt_t[hh:hh + 1, :]
                lmat = jnp.exp(jnp.where(causal, colb - row, NEG))
                m = cbg * lmat * dtrow
                hs = hst_ref[ps, :]
                cs = cmat[:, ns] * jnp.exp(colb)
                y = _dot(m.astype(BF16), xb[:, ps]) + _dot_nt(cs.astype(BF16), hs.astype(BF16))
                last = colb[WINDOW - 1:WINDOW, :]
                wrow = jnp.exp(last - row) * dtrow
                xw = (x_t[ps, :] * wrow).astype(BF16)
                hst_ref[ps, :] = jnp.exp(last) * hs + _dot(xw, bb[:, ns])
                ys.append(y)
        y = jnp.concatenate(ys, axis=1) + dsk_ref[...] * xc
        yc = y * _silu(z_ref[rows, 0:1024])
        gw = D_MODEL // G_C
        oc = jnp.concatenate([_rms_lanes(yc[:, gw * g:gw * (g + 1)]) for g in range(G_C)], axis=1) * snw_ref[...]
        gc = _sigmoid(z_ref[rows, C_GC:C_GC + 1024])
        mixs_ref[rows, :] = (mixab_ref[0, rows, :] + gc * oc).astype(BF16)
        return carry

    lax.fori_loop(0, tt // WINDOW, chunk, 0)
    ho_ref[0] = hst_ref[...]
    g1 = mod[:, 2 * D_MODEL:3 * D_MODEL]
    xo_ref[0] = x + g1 * _dot(mixs_ref[...], wo_ref[...])


def _pc_call(x, mod3, n1, wc, cw, cb, dtb_pad, a_pad, dsk_full, snw, mixab, wo, tt):
    B, T, _ = x.shape
    tri = jnp.asarray(np.tril(np.ones((WINDOW, WINDOW), np.float32)))
    kern = functools.partial(_pc_kernel, tt=tt)
    full2 = lambda b, t: (0, 0)
    return pl.pallas_call(
        kern,
        grid=(B, T // tt),
        in_specs=[pl.BlockSpec((1, tt, D_MODEL), lambda b, t: (b, t, 0)),
                  pl.BlockSpec((1, 1, 6 * D_MODEL), lambda b, t: (b, 0, 0)),
                  pl.BlockSpec((1, D_MODEL), full2),
                  pl.BlockSpec((D_MODEL, SEG_C), full2),
                  pl.BlockSpec((CONV_W, CONV_DIM), full2),
                  pl.BlockSpec((1, CONV_DIM), full2),
                  pl.BlockSpec((1, 128), full2),
                  pl.BlockSpec((1, 128), full2),
                  pl.BlockSpec((1, D_MODEL), full2),
                  pl.BlockSpec((1, D_MODEL), full2),
                  pl.BlockSpec((WINDOW, WINDOW), full2),
                  pl.BlockSpec((1, tt, D_MODEL), lambda b, t: (b, t, 0)),
                  pl.BlockSpec((D_MODEL, D_MODEL), full2)],
        out_specs=[pl.BlockSpec((1, tt, D_MODEL), lambda b, t: (b, t, 0)),
                   pl.BlockSpec((1, HC * HD_C, N_C), lambda b, t: (b, 0, 0)),
                   pl.BlockSpec((1, CONV_W - 1, CONV_DIM), lambda b, t: (b, 0, 0))],
        out_shape=[jax.ShapeDtypeStruct((B, T, D_MODEL), F32),
                   jax.ShapeDtypeStruct((B, HC * HD_C, N_C), F32),
                   jax.ShapeDtypeStruct((B, CONV_W - 1, CONV_DIM), F32)],
        scratch_shapes=[pltpu.VMEM((tt, SEG_C), F32),
                        pltpu.VMEM((tt + 8, CONV_DIM), F32),
                        pltpu.VMEM((tt, CONV_DIM), F32),
                        pltpu.VMEM((tt, 128), F32),
                        pltpu.VMEM((HC * HD_C, N_C), F32),
                        pltpu.VMEM((tt, D_MODEL), BF16)],
        compiler_params=_cparams(2),
        name="prompt_ssd_out",
    )(x, mod3, n1, wc, cw, cb, dtb_pad, a_pad, dsk_full, snw, tri, mixab, wo)


def _mlp_kernel(x_ref, sh_ref, sc_ref, g_ref, n2_ref, wu_ref, wd_ref, fn_ref, o_ref, h_ref, acc_ref,
                *, nf, final, per_row):
    f = pl.program_id(1)
    rd = (lambda r: r[...]) if per_row else (lambda r: r[0])

    @pl.when(f == 0)
    def _():
        h_ref[...] = _modnorm(x_ref[...], n2_ref[...], rd(sc_ref), rd(sh_ref)).astype(BF16)
        acc_ref[...] = jnp.zeros_like(acc_ref)

    u = _dot(h_ref[...], wu_ref[0])
    u = jnp.square(jnp.maximum(u, 0.0)).astype(BF16)
    acc_ref[...] += _dot(u, wd_ref[0])

    @pl.when(f == nf - 1)
    def _():
        y = x_ref[...] + rd(g_ref) * acc_ref[...]
        if final:
            y = _rms_lanes(y) * fn_ref[...]
        o_ref[...] = y


def _mlp_call(x2, mod, n2, wu, wd, fn, layer, tm, tf, rows_per_mod, final):
    M = x2.shape[0]
    nf = D_FF // tf
    per_row = rows_per_mod == 1
    if per_row:
        mspec = lambda j: pl.BlockSpec((tm, D_MODEL), lambda m, f: (m, j))
    else:
        mspec = lambda j: pl.BlockSpec((1, 1, D_MODEL), lambda m, f: ((m * tm) // rows_per_mod, 0, j))
    kern = functools.partial(_mlp_kernel, nf=nf, final=final, per_row=per_row)
    return pl.pallas_call(
        kern,
        grid=(M // tm, nf),
        in_specs=[pl.BlockSpec((tm, D_MODEL), lambda m, f: (m, 0)),
                  mspec(3), mspec(4), mspec(5),
                  pl.BlockSpec((1, D_MODEL), lambda m, f: (0, 0)),
                  pl.BlockSpec((1, D_MODEL, tf), lambda m, f: (layer, 0, f)),
                  pl.BlockSpec((1, tf, D_MODEL), lambda m, f: (layer, f, 0)),
                  pl.BlockSpec((1, D_MODEL), lambda m, f: (0, 0))],
        out_specs=pl.BlockSpec((tm, D_MODEL), lambda m, f: (m, 0)),
        out_shape=jax.ShapeDtypeStruct((M, D_MODEL), F32),
        scratch_shapes=[pltpu.VMEM((tm, D_MODEL), BF16),
                        pltpu.VMEM((tm, D_MODEL), F32)],
        compiler_params=_cparams(2),
        name="mlp",
    )(x2, mod, mod, mod, n2, wu, wd, fn)


def _sproj_kernel(x_ref, sh_ref, sc_ref, n1_ref, w_ref, o_ref):
    h = _modnorm(x_ref[...], n1_ref[...], sc_ref[...], sh_ref[...])
    o_ref[...] = _dot(h.astype(BF16), w_ref[...])


def _sproj_call(xs, mod_s, n1, w):
    n, width = xs.shape[0], w.shape[1]
    return pl.pallas_call(
        _sproj_kernel,
        grid=(1,),
        in_specs=[pl.BlockSpec((n, D_MODEL), lambda i: (0, 0)),
                  pl.BlockSpec((n, D_MODEL), lambda i: (0, 0)),
                  pl.BlockSpec((n, D_MODEL), lambda i: (0, 1)),
                  pl.BlockSpec((1, D_MODEL), lambda i: (0, 0)),
                  pl.BlockSpec((D_MODEL, width), lambda i: (0, 0))],
        out_specs=pl.BlockSpec((n, width), lambda i: (0, 0)),
        out_shape=jax.ShapeDtypeStruct((n, width), F32),
        compiler_params=_cparams(1),
        name="sample_proj",
    )(xs, mod_s, mod_s, n1, w)


def _carry_outputs(prev, n_in):
    extra = [pl.BlockSpec(memory_space=pl.ANY) for _ in prev]
    alias = {n_in + k: k for k in range(len(prev))}
    return extra, alias


def _sa_kernel(q_ref, kv_ref, ck_ref, cv_ref, bias_ref, sink_ref, *rest):
    ko_ref, vo_ref, oa_ref = rest[-3:]
    nb = ROWS_PER_SAMPLE_STEP
    gsz = HA_Q // HA_KV
    rg = lax.broadcasted_iota(jnp.int32, (HA_Q, HD_A), 0) // gsz
    sink = sink_ref[...]
    kw = HA_KV * HD_A
    for i in range(nb):
        ko_ref[0, i, 0:WINDOW - 1, :] = ck_ref[0, i, 1:WINDOW, :]
        ko_ref[0, i, WINDOW - 1:WINDOW, :] = kv_ref[i:i + 1, 0:kw]
        vo_ref[0, i, 0:WINDOW - 1, :] = cv_ref[0, i, 1:WINDOW, :]
        vo_ref[0, i, WINDOW - 1:WINDOW, :] = kv_ref[i:i + 1, kw:2 * kw]
        kmat = ko_ref[0, i].astype(BF16)
        vmat = vo_ref[0, i].astype(BF16)
        q = q_ref[i] * (HD_A ** -0.5)
        qe = jnp.concatenate([jnp.where(rg == g, q, 0.0) for g in range(HA_KV)], axis=1).astype(BF16)
        s = _dot_nt(qe, kmat) + bias_ref[...]
        m = jnp.maximum(jnp.max(s, axis=-1, keepdims=True), sink)
        p = jnp.exp(s - m)
        den = jnp.sum(p, axis=-1, keepdims=True) + jnp.exp(sink - m)
        o = _dot(p.astype(BF16), vmat) / den
        o16 = jnp.zeros((HA_Q, HD_A), F32)
        for g in range(HA_KV):
            o16 = o16 + jnp.where(rg == g, o[:, HD_A * g:HD_A * (g + 1)], 0.0)
        oa_ref[i] = o16


def _sa_call(q3, kv, cache_k, cache_v, bias_s, sink_col, layer, prev):
    n = q3.shape[0]
    nb = ROWS_PER_SAMPLE_STEP
    kw = HA_KV * HD_A
    cspec = pl.BlockSpec((1, nb, WINDOW, kw), lambda i: (layer, i, 0, 0))
    extra, alias = _carry_outputs(prev, 6)
    return pl.pallas_call(
        _sa_kernel,
        grid=(n // nb,),
        in_specs=[pl.BlockSpec((nb, HA_Q, HD_A), lambda i: (i, 0, 0)),
                  pl.BlockSpec((nb, 2 * kw), lambda i: (i, 0)),
                  cspec, cspec,
                  pl.BlockSpec((HA_Q, WINDOW), lambda i: (0, 0)),
                  pl.BlockSpec((HA_Q, 1), lambda i: (0, 0))] + extra,
        out_specs=[cspec, cspec, pl.BlockSpec((nb, HA_Q, HD_A), lambda i: (i, 0, 0))],
        out_shape=[jax.ShapeDtypeStruct((DEPTH, n, WINDOW, kw), F32),
                   jax.ShapeDtypeStruct((DEPTH, n, WINDOW, kw), F32),
                   jax.ShapeDtypeStruct((n, HA_Q, HD_A), F32)],
        input_output_aliases=alias,
        compiler_params=_cparams(1),
        name="sample_attn",
    )(q3, kv, cache_k, cache_v, bias_s, sink_col, *prev)


def _sr_kernel(zb_ref, cos_ref, sin_ref, gcol_ref, s_ref, *rest):
    so_ref, o_ref = rest[-2:]
    nb = ROWS_PER_SAMPLE_STEP
    wq = HB * DK_B
    cos = cos_ref[...]
    sin = sin_ref[...]
    qf = zb_ref[:, 0:wq]
    kf = zb_ref[:, wq:2 * wq]
    qr = qf * cos + _pairswap(qf) * sin
    kr = (kf * cos + _pairswap(kf) * sin) * (DK_B ** -0.5)
    v = zb_ref[:, 2 * wq:2 * wq + HB * DV_B]
    r8 = lax.broadcasted_iota(jnp.int32, (HB, wq), 0)
    hl = lax.broadcasted_iota(jnp.int32, (HB, wq), 1) // DK_B
    rv = lax.broadcasted_iota(jnp.int32, (HB, DV_B), 0)
    for i in range(nb):
        k8 = jnp.where(hl == r8, jnp.broadcast_to(kr[i:i + 1, :], (HB, wq)), 0.0).astype(BF16)
        q8 = jnp.where(hl == r8, jnp.broadcast_to(qr[i:i + 1, :], (HB, wq)), 0.0).astype(BF16)
        v8 = jnp.zeros((HB, DV_B), F32)
        for r in range(HB):
            v8 = jnp.where(rv == r, jnp.broadcast_to(v[i:i + 1, DV_B * r:DV_B * (r + 1)], (HB, DV_B)), v8)
        s_new = gcol_ref[...] * s_ref[0, i] + _dot_tn(k8, v8.astype(BF16))
        so_ref[0, i] = s_new
        o_ref[i] = _dot(q8, s_new.astype(BF16))


def _sr_call(zb, state, layer, prev):
    n = zb.shape[0]
    nb = ROWS_PER_SAMPLE_STEP
    wq = HB * DK_B
    cos, sin = _rot_tables(np.array([PAST_LEN]))
    gcol = np.repeat(_gammas(), DK_B)[:, None] * np.ones((1, DV_B))
    sspec = pl.BlockSpec((1, nb, wq, DV_B), lambda i: (layer, i, 0, 0))
    extra, alias = _carry_outputs(prev, 5)
    return pl.pallas_call(
        _sr_kernel,
        grid=(n // nb,),
        in_specs=[pl.BlockSpec((nb, SEG_B), lambda i: (i, 0)),
                  pl.BlockSpec((1, wq), lambda i: (0, 0)),
                  pl.BlockSpec((1, wq), lambda i: (0, 0)),
                  pl.BlockSpec((wq, DV_B), lambda i: (0, 0)),
                  sspec] + extra,
        out_specs=[sspec, pl.BlockSpec((nb, HB, DV_B), lambda i: (i, 0, 0))],
        out_shape=[jax.ShapeDtypeStruct((DEPTH, n, wq, DV_B), F32),
                   jax.ShapeDtypeStruct((n, HB, DV_B), F32)],
        input_output_aliases=alias,
        compiler_params=_cparams(1),
        name="sample_retention",
    )(zb, jnp.asarray(cos), jnp.asarray(sin), jnp.asarray(gcol.astype(np.float32)), state, *prev)


def _ss_kernel(zc_ref, cs_ref, cw_ref, cb_ref, dtb_ref, a_ref, e3_ref, h_ref, *rest):
    ho_ref, cso_ref, y_ref, xc_ref = rest[-4:]
    nb = ROWS_PER_SAMPLE_STEP
    cx = zc_ref[:, 1024:2560]
    taps = [cs_ref[0, :, CONV_DIM * i:CONV_DIM * (i + 1)] for i in range(CONV_W - 1)] + [cx]
    acc = cb_ref[...]
    for i in range(CONV_W):
        acc = acc + taps[i] * cw_ref[i:i + 1, :]
    xbc = _silu(acc)
    cso_ref[...] = jnp.concatenate(taps[1:], axis=1)
    xc = xbc[:, 0:1024]
    bmat = xbc[:, 1024:1280]
    cmat = xbc[:, 1280:1536]
    dt = _softplus(zc_ref[:, C_DT:C_DT + 128] + dtb_ref[...])
    da = jnp.exp(dt * a_ref[...])
    dt_e = _dot(jnp.concatenate(_split3(dt), axis=1), e3_ref[...])
    da_e = _dot(jnp.concatenate(_split3(da), axis=1), e3_ref[...])
    dtx = dt_e * xc
    gw = (HC // G_C) * HD_C
    r8 = lax.broadcasted_iota(jnp.int32, (nb, gw), 0)
    rn = lax.broadcasted_iota(jnp.int32, (nb, N_C), 0)
    ones8 = jnp.ones((nb, N_C), BF16)
    ycols = []
    for g in range(G_C):
        ws = slice(gw * g, gw * (g + 1))
        ns = slice(N_C * g, N_C * (g + 1))
        bg16 = bmat[:, ns].astype(BF16)
        yacc = jnp.zeros((nb, gw), F32)
        for i in range(nb):
            x8 = jnp.where(r8 == i, dtx[:, ws], 0.0).astype(BF16)
            outer = _dot_tn(x8, bg16)
            hi, mid, lo = (p.astype(F32) for p in _split3(jnp.broadcast_to(da_e[i:i + 1, ws], (nb, gw))))
            l3 = jnp.where(r8 == 0, hi, jnp.where(r8 == 1, mid, jnp.where(r8 == 2, lo, 0.0)))
            dacol = _dot_tn(l3.astype(BF16), ones8)
            h_new = dacol * h_ref[0, i, ws, :] + outer
            ho_ref[0, i, ws, :] = h_new
            c8 = jnp.where(rn == i, cmat[:, ns], 0.0).astype(BF16)
            yacc = yacc + _dot_nt(c8, h_new.astype(BF16))
        ycols.append(yacc)
    y_ref[...] = jnp.concatenate(ycols, axis=1)
    xc_ref[...] = xc


def _ss_call(zc, conv_state, hstate, cw, cb, dtb_pad, a_pad, layer, prev):
    n = zc.shape[0]
    nb = ROWS_PER_SAMPLE_STEP
    e = np.zeros((128, D_MODEL), np.float32)
    for hh in range(HC):
        e[hh, HD_C * hh:HD_C * (hh + 1)] = 1.0
    e3 = jnp.asarray(np.concatenate([e, e, e], axis=0), dtype=BF16)
    cwid = (CONV_W - 1) * CONV_DIM
    full = lambda i: (0, 0)
    hspec = pl.BlockSpec((1, nb, HC * HD_C, N_C), lambda i: (layer, i, 0, 0))
    extra, alias = _carry_outputs(prev, 8)
    return pl.pallas_call(
        _ss_kernel,
        grid=(n // nb,),
        in_specs=[pl.BlockSpec((nb, SEG_C), lambda i: (i, 0)),
                  pl.BlockSpec((1, nb, cwid), lambda i: (layer, i, 0)),
                  pl.BlockSpec((CONV_W, CONV_DIM), full),
                  pl.BlockSpec((1, CONV_DIM), full),
                  pl.BlockSpec((1, 128), full),
                  pl.BlockSpec((1, 128), full),
                  pl.BlockSpec((3 * 128, D_MODEL), full),
                  hspec] + extra,
        out_specs=[hspec,
                   pl.BlockSpec((nb, cwid), lambda i: (i, 0)),
                   pl.BlockSpec((nb, D_MODEL), lambda i: (i, 0)),
                   pl.BlockSpec((nb, D_MODEL), lambda i: (i, 0))],
        out_shape=[jax.ShapeDtypeStruct((DEPTH, n, HC * HD_C, N_C), F32),
                   jax.ShapeDtypeStruct((n, cwid), F32),
                   jax.ShapeDtypeStruct((n, D_MODEL), F32),
                   jax.ShapeDtypeStruct((n, D_MODEL), F32)],
        input_output_aliases=alias,
        compiler_params=_cparams(1),
        name="sample_ssd",
    )(zc, conv_state, cw, cb, dtb_pad, a_pad, e3, hstate, *prev)


def _sm_kernel(x_ref, g1_ref, oa_ref, oret_ref, y_ref, xc_ref, za_ref, zb_ref, zc_ref,
               dsk_ref, snw_ref, wo_ref, o_ref):
    ob = jnp.concatenate([_rms_lanes(oret_ref[:, DV_B * hh:DV_B * (hh + 1)]) for hh in range(HB)], axis=1)
    ob = _silu(zb_ref[:, 2048:3072]) * ob
    yc = (y_ref[...] + dsk_ref[...] * xc_ref[...]) * _silu(zc_ref[:, 0:1024])
    gw = D_MODEL // G_C
    oc = jnp.concatenate([_rms_lanes(yc[:, gw * g:gw * (g + 1)]) for g in range(G_C)], axis=1) * snw_ref[...]
    mix = (_sigmoid(za_ref[:, 1536:2560]) * oa_ref[...] + _sigmoid(zb_ref[:, 3072:4096]) * ob
           + _sigmoid(zc_ref[:, C_GC:C_GC + 1024]) * oc)
    o_ref[...] = x_ref[...] + g1_ref[...] * _dot(mix.astype(BF16), wo_ref[...])


def _sm_call(xs, mod_s, oa, oret, y, xc, za, zb, zc, dsk_full, snw, wo):
    n = xs.shape[0]
    full = lambda i: (0, 0)
    row = lambda w: pl.BlockSpec((n, w), full)
    return pl.pallas_call(
        _sm_kernel,
        grid=(1,),
        in_specs=[row(D_MODEL),
                  pl.BlockSpec((n, D_MODEL), lambda i: (0, 2)),
                  row(D_MODEL), row(D_MODEL), row(D_MODEL), row(D_MODEL),
                  row(SEG_A), row(SEG_B), row(SEG_C),
                  pl.BlockSpec((1, D_MODEL), full),
                  pl.BlockSpec((1, D_MODEL), full),
                  pl.BlockSpec((D_MODEL, D_MODEL), full)],
        out_specs=row(D_MODEL),
        out_shape=jax.ShapeDtypeStruct((n, D_MODEL), F32),
        compiler_params=_cparams(1),
        name="sample_merge_out",
    )(xs, mod_s, oa, oret, y, xc, za, zb, zc, dsk_full, snw, wo)


PROMPT_TILE = 512
MLP_ROWS = 1024
MLP_FF = 512


def _prep_w_in(w):
    wa = jnp.concatenate([w[:, O_AQ:O_BQ], w[:, O_GTS:O_GTS + 1024]], axis=1)
    wb = jnp.concatenate([w[:, O_BQ:O_CZ], w[:, O_GTS + 1024:O_GTS + 2048]], axis=1)
    wc = jnp.concatenate([w[:, O_CZ:O_CDT], jnp.pad(w[:, O_CDT:O_GTS], ((0, 0), (0, 128 - HC))),
                          w[:, O_GTS + 2048:O_GTS + 3072]], axis=1)
    return wa.astype(BF16), wb.astype(BF16), wc.astype(BF16)


def _forward(x_prompt, x_sample, cache_win_k, cache_win_v, state_ret, state_ssm, state_conv,
             c_prompt, c_sample, rel_bias_table, attn_sinks, norm1_w, norm2_w, ada_w, ada_b,
             w_in, conv_w, conv_b, dt_bias, A_log, D_skip, ssm_norm_w, w_out, w_up, w_down,
             final_norm_w, *, prompt_tile, mlp_rows, mlp_ff):
    B, T, _ = x_prompt.shape
    DB = x_sample.shape[0]
    kw = HA_KV * HD_A

    mod_all = _ada_call(jnp.concatenate([c_prompt, c_sample], axis=0), ada_w, ada_b)

    bias_t, bias_s = _bias_tables(rel_bias_table)

    wu16 = w_up.astype(BF16)
    wd16 = w_down.astype(BF16)
    wo16 = w_out.astype(BF16)
    fn = final_norm_w.reshape(1, D_MODEL)
    ck = cache_win_k.reshape(DEPTH, DB, WINDOW, kw)
    cv = cache_win_v.reshape(DEPTH, DB, WINDOW, kw)
    sret = state_ret.reshape(DEPTH, DB, HB * DK_B, DV_B)
    sssm = state_ssm.reshape(DEPTH, DB, HC * HD_C, N_C)
    sconv = state_conv.reshape(DEPTH, DB, (CONV_W - 1) * CONV_DIM)

    xp = x_prompt
    xs = x_sample.reshape(DB, D_MODEL)
    outs_p = [[] for _ in range(5)]
    conv_s = []
    win_kv, ret_all, ssm_all = (), (), ()
    perm = _pair_perm()
    for l in range(DEPTH):
        wl = w_in[l]
        wa, wb, wc = _prep_w_in(wl)
        n1 = norm1_w[l].reshape(1, D_MODEL)
        n2 = norm2_w[l].reshape(1, D_MODEL)
        cw = conv_w[l]
        cb = conv_b[l].reshape(1, CONV_DIM)
        dtb_pad = jnp.pad(dt_bias[l], (0, 128 - HC)).reshape(1, 128)
        a_pad = jnp.pad(-jnp.exp(A_log[l].astype(F32)), (0, 128 - HC)).reshape(1, 128)
        dsk_full = jnp.repeat(D_skip[l], HD_C).reshape(1, D_MODEL)
        snw = ssm_norm_w[l].reshape(1, D_MODEL)
        final = l == DEPTH - 1
        mod_p = mod_all[l, :B].reshape(B, 1, 6 * D_MODEL)
        mod_s = mod_all[l, B:]

        wqgt = jnp.concatenate([wl[:, O_AQ:O_AK] * (HD_A ** -0.5), wl[:, O_GTS:O_GTS + D_MODEL]], axis=1).T.astype(BF16)
        wkv = wl[:, O_AK:O_BQ].astype(BF16)
        sink_rows = jnp.repeat(attn_sinks[l], WINDOW).reshape(HA_KV, 1, (HA_Q // HA_KV) * WINDOW)
        mixa, kbuf, vbuf = _pa_call(xp, mod_p, n1, wqgt, wkv, bias_t, sink_rows, prompt_tile)
        wbt = jnp.concatenate([wl[:, O_BQ + perm], wl[:, O_BK + perm] * (DK_B ** -0.5), wl[:, O_BV:O_CZ],
                               wl[:, O_GTS + D_MODEL:O_GTS + 2 * D_MODEL]], axis=1).T.astype(BF16)
        mixab, s_perm = _pb_call(xp, mod_p, n1, wbt, mixa, prompt_tile)
        s_ret = s_perm.reshape(B, HB, 2, DK_B // 2, DV_B).transpose(0, 1, 3, 2, 4)
        x1, h_ssm, conv_new = _pc_call(xp, mod_p, n1, wc, cw, cb, dtb_pad, a_pad, dsk_full, snw,
                                       mixab, wo16[l], prompt_tile)
        xp = _mlp_call(x1.reshape(B * T, D_MODEL), mod_p, n2, wu16, wd16, fn, l,
                       mlp_rows, mlp_ff, T, final).reshape(B, T, D_MODEL)
        for lst, v in zip(outs_p, (kbuf.reshape(B, WINDOW, HA_KV, HD_A), vbuf.reshape(B, WINDOW, HA_KV, HD_A),
                                   s_ret.reshape(B, HB, DK_B, DV_B), h_ssm.reshape(B, HC, HD_C, N_C), conv_new)):
            lst.append(v)

        za = _sproj_call(xs, mod_s, n1, wa)
        zb = _sproj_call(xs, mod_s, n1, wb)
        zc = _sproj_call(xs, mod_s, n1, wc)
        q3 = za[:, 0:HA_Q * HD_A].reshape(DB, HA_Q, HD_A)
        kv = za[:, HA_Q * HD_A:HA_Q * HD_A + 2 * kw]
        ck_new, cv_new, oa3 = _sa_call(q3, kv, ck, cv, bias_s, attn_sinks[l].reshape(HA_Q, 1), l, win_kv)
        win_kv = (ck_new, cv_new)
        s_new, o3 = _sr_call(zb, sret, l, ret_all)
        ret_all = (s_new,)
        h_new, cs_new, y_s, xc_s = _ss_call(zc, sconv, sssm, cw, cb, dtb_pad, a_pad, l, ssm_all)
        ssm_all = (h_new,)
        xs1 = _sm_call(xs, mod_s, oa3.reshape(DB, D_MODEL), o3.reshape(DB, D_MODEL), y_s, xc_s,
                       za, zb, zc, dsk_full, snw, wo16[l])
        xs = _mlp_call(xs1, mod_s, n2, wu16, wd16, fn, l, DB, mlp_ff, 1, final)
        conv_s.append(cs_new.reshape(DB, CONV_W - 1, CONV_DIM))

    stk = lambda lst: jnp.stack(lst, axis=0)
    return (xp, xs.reshape(DB, 1, D_MODEL),
            stk(outs_p[0]), stk(outs_p[1]), stk(outs_p[2]), stk(outs_p[3]), stk(outs_p[4]),
            win_kv[0].reshape(DEPTH, DB, WINDOW, HA_KV, HD_A), win_kv[1].reshape(DEPTH, DB, WINDOW, HA_KV, HD_A),
            ret_all[0].reshape(DEPTH, DB, HB, DK_B, DV_B), ssm_all[0].reshape(DEPTH, DB, HC, HD_C, N_C),
            stk(conv_s))


def kernel(x_prompt, x_sample, cache_win_k, cache_win_v, state_ret, state_ssm, state_conv, c_prompt, c_sample,
           rel_bias_table, attn_sinks, norm1_w, norm2_w, ada_w, ada_b, w_in, conv_w, conv_b, dt_bias, A_log,
           D_skip, ssm_norm_w, w_out, w_up, w_down, final_norm_w):
    return _forward(x_prompt, x_sample, cache_win_k, cache_win_v, state_ret, state_ssm, state_conv,
                    c_prompt, c_sample, rel_bias_table, attn_sinks, norm1_w, norm2_w, ada_w, ada_b,
                    w_in, conv_w, conv_b, dt_bias, A_log, D_skip, ssm_norm_w, w_out, w_up, w_down,
                    final_norm_w, prompt_tile=PROMPT_TILE, mlp_rows=MLP_ROWS, mlp_ff=MLP_FF)
```

```python
import functools
import math

import numpy as np
import jax
import jax.numpy as jnp
from jax import lax
from jax.experimental import pallas as pl
from jax.experimental.pallas import tpu as pltpu

F32 = jnp.float32
BF16 = jnp.bfloat16

D_MODEL = 1024
DEPTH = 2
PAST_LEN = 16384
WINDOW = 128
HA_Q = 16
HA_KV = 4
HD_A = 64
NUM_BUCKETS = 32
MAX_DISTANCE = WINDOW
HB = 8
DK_B = 64
DV_B = 128
HC = 16
HD_C = 64
N_C = 128
G_C = 2
CONV_W = 4
CONV_DIM = D_MODEL + 2 * G_C * N_C
D_FF = 4 * D_MODEL
EPS = 1e-6
NEG = -1e30

O_AQ, O_AK, O_AV = 0, 1024, 1280
O_BQ, O_BK, O_BV, O_BG = 1536, 2048, 2560, 3584
O_CZ, O_CXBC, O_CDT, O_GTS = 4608, 5632, 7168, 7184
SEG_A = 2560
SEG_B = 4096
SEG_C = 3712
C_DT = 2560
C_GC = 2688

VMEM_LIMIT_V7X = 56 * 1024 * 1024
ROWS_PER_SAMPLE_STEP = 8


def _cparams(n_axes):
    return pltpu.CompilerParams(dimension_semantics=("arbitrary",) * n_axes,
                                vmem_limit_bytes=VMEM_LIMIT_V7X)


def _dot(a, b):
    return jnp.dot(a, b, preferred_element_type=F32)


def _dot_nt(a, b):
    return lax.dot_general(a, b, (((1,), (1,)), ((), ())), preferred_element_type=F32)


def _dot_tn(a, b):
    return lax.dot_general(a, b, (((0,), (0,)), ((), ())), preferred_element_type=F32)


def _sigmoid(x):
    return 0.5 * (jnp.tanh(0.5 * x) + 1.0)


def _silu(x):
    return x * _sigmoid(x)


def _softplus(x):
    return jnp.maximum(x, 0.0) + jnp.log1p(jnp.exp(-jnp.abs(x)))


def _modnorm(x, nw, sc, sh):
    ms = jnp.mean(x * x, axis=-1, keepdims=True)
    return (x * lax.rsqrt(ms + EPS) * nw) * (1.0 + sc) + sh


def _rms_lanes(x):
    ms = jnp.mean(x * x, axis=-1, keepdims=True)
    return x * lax.rsqrt(ms + EPS)


def _pairswap(x):
    ax = x.ndim - 1
    n = x.shape[ax]
    lane = lax.broadcasted_iota(jnp.int32, x.shape, ax)
    nxt = pltpu.roll(x, n - 1, ax)
    prv = pltpu.roll(x, 1, ax)
    return jnp.where((lane & 1) == 0, nxt, prv)


def _split3(x):
    hi = x.astype(BF16)
    r1 = x - hi.astype(F32)
    mid = r1.astype(BF16)
    lo = (r1 - mid.astype(F32)).astype(BF16)
    return hi, mid, lo


def _gammas():
    return 1.0 - 2.0 ** (-5.0 - np.arange(HB, dtype=np.float64))


def _rot_tables(pos):
    theta = 1.0 / (10000.0 ** np.linspace(0.0, 1.0, DK_B // 2))
    ang = np.asarray(pos, np.float64)[:, None] * theta[None, :]
    cos = np.repeat(np.cos(ang), 2, axis=1)
    sin = np.repeat(np.sin(ang), 2, axis=1)
    sin[:, 0::2] *= -1.0
    return (np.tile(cos, (1, HB)).astype(np.float32), np.tile(sin, (1, HB)).astype(np.float32))


def _ret_tables():
    g = _gammas()
    L = WINDOW
    i = np.arange(L, dtype=np.float64)
    diff = i[:, None] - i[None, :]
    dm = np.where(diff >= 0, g[:, None, None] ** np.maximum(diff, 0.0), 0.0)
    qdec = np.repeat(g[None, :] ** (i[:, None] + 1.0), DK_B, axis=1)
    kdec = np.repeat(g[None, :] ** (L - 1.0 - i[:, None]), DK_B, axis=1)
    return dm.astype(np.float32), qdec.astype(np.float32), kdec.astype(np.float32)


def _t5_bucket_np(dist):
    max_exact = NUM_BUCKETS // 2
    n = np.maximum(dist, 0)
    nf = np.maximum(n, 1).astype(np.float32)
    large = max_exact + (np.log(nf / np.float32(max_exact)) / np.float32(math.log(MAX_DISTANCE / max_exact))
                         * np.float32(NUM_BUCKETS - max_exact)).astype(np.int32)
    large = np.minimum(large, NUM_BUCKETS - 1)
    return np.where(n < max_exact, n, large)


def _bias_tables(rel_table):
    gsz = HA_Q // HA_KV
    qi = np.arange(WINDOW)[None, :]
    kj = np.arange(WINDOW)[:, None]
    dist = np.where(kj > qi, qi + WINDOW - kj, qi - kj)
    onehot = _t5_bucket_np(dist)[..., None] == np.arange(NUM_BUCKETS)
    tab = rel_table.astype(F32)
    bias_kq = jnp.einsum('kqb,bh->hkq', jnp.asarray(onehot, F32), tab, precision=lax.Precision.HIGHEST)
    bias_t = bias_kq.reshape(HA_KV, gsz, WINDOW, WINDOW).transpose(0, 2, 1, 3).reshape(HA_KV, WINDOW, gsz * WINDOW)
    oh_s = _t5_bucket_np(WINDOW - 1 - np.arange(WINDOW))[:, None] == np.arange(NUM_BUCKETS)
    bias_s = jnp.einsum('jb,bh->hj', jnp.asarray(oh_s, F32), tab, precision=lax.Precision.HIGHEST)
    return bias_t, bias_s


def _ada_kernel(c_ref, w_ref, b_ref, o_ref):
    s = _silu(c_ref[...])
    o_ref[0] = _dot(s.astype(BF16), w_ref[0].astype(BF16)) + b_ref[0]


def _ada_call(c_all, ada_w, ada_b):
    n = c_all.shape[0]
    nb = 6
    return pl.pallas_call(
        _ada_kernel,
        grid=(DEPTH, nb),
        in_specs=[pl.BlockSpec((n, D_MODEL), lambda l, j: (0, 0)),
                  pl.BlockSpec((1, D_MODEL, D_MODEL), lambda l, j: (l, 0, j)),
                  pl.BlockSpec((1, 1, D_MODEL), lambda l, j: (l, 0, j))],
        out_specs=pl.BlockSpec((1, n, D_MODEL), lambda l, j: (l, 0, j)),
        out_shape=jax.ShapeDtypeStruct((DEPTH, n, 6 * D_MODEL), F32),
        compiler_params=_cparams(2),
        name="ada_mod",
    )(c_all, ada_w, ada_b.reshape(DEPTH, 1, 6 * D_MODEL))


def _pa_kernel(x_ref, mod_ref, n1_ref, wt_ref, wkv_ref, bias_ref, sink_ref, lowm_ref,
               mix_ref, ko_ref, vo_ref, zt_ref, kv_ref, kprev_ref, vtprev_ref, pen_ref, *, tt):
    t = pl.program_id(1)
    nchunk = tt // WINDOW
    kw = HA_KV * HD_A
    gsz = HA_Q // HA_KV

    @pl.when(t == 0)
    def _():
        kprev_ref[...] = jnp.zeros_like(kprev_ref)
        vtprev_ref[...] = jnp.zeros_like(vtprev_ref)
        pen_ref[...] = jnp.full(pen_ref.shape, NEG, F32)

    mod = mod_ref[0]
    h = _modnorm(x_ref[0], n1_ref[...], mod[:, D_MODEL:2 * D_MODEL], mod[:, 0:D_MODEL]).astype(BF16)
    zt = _dot_nt(wt_ref[...], h)
    for c in range(nchunk):
        zt_ref[c] = zt[:, WINDOW * c:WINDOW * (c + 1)]
    kv_ref[...] = _dot(h, wkv_ref[...])
    qw = gsz * WINDOW
    lower = (lax.broadcasted_iota(jnp.int32, (WINDOW, qw), 0)
             > (lax.broadcasted_iota(jnp.int32, (WINDOW, qw), 1) & (WINDOW - 1)))

    def chunk(c, carry):
        r0 = pl.multiple_of(c * WINDOW, WINDOW)
        rows = pl.ds(r0, WINDOW)
        kc = kv_ref[rows, 0:kw]
        vc = kv_ref[rows, kw:2 * kw]
        vt = vc.T
        kk = jnp.concatenate([kprev_ref[...], kc], axis=0).astype(BF16)
        vvt = jnp.concatenate([vtprev_ref[...], vt], axis=1).astype(BF16)
        qt = zt_ref[c, 0:D_MODEL, :].astype(BF16)
        pen = pen_ref[0:1, :]
        s_all = []
        for g in range(HA_KV):
            gs = slice(HD_A * g, HD_A * (g + 1))
            qcat = jnp.concatenate([qt[HD_A * (gsz * g + j):HD_A * (gsz * g + j + 1), :] for j in range(gsz)], axis=1)
            s_all.append(_dot(kk[:, gs], qcat))
        p_all = []
        for g in range(HA_KV):
            sg = jnp.where(lower, s_all[g][0:WINDOW, :] + pen, s_all[g][WINDOW:2 * WINDOW, :]) + bias_ref[g]
            sink = sink_ref[g]
            m = jnp.maximum(jnp.max(sg, axis=0, keepdims=True), sink)
            pw = jnp.exp(sg - m)
            den = jnp.sum(pw, axis=0, keepdims=True) + jnp.exp(sink - m)
            pb = pw.astype(BF16)
            p_prev = pb * lowm_ref[...]
            p_all.append((jnp.concatenate([p_prev, pb - p_prev], axis=0), 1.0 / den))
        pieces = []
        for g in range(HA_KV):
            gs = slice(HD_A * g, HD_A * (g + 1))
            p, rden = p_all[g]
            ot = _dot(vvt[gs, :], p) * rden
            pieces += [ot[:, WINDOW * j:WINDOW * (j + 1)] for j in range(gsz)]
        oat = jnp.concatenate(pieces, axis=0)
        mixt = _sigmoid(zt_ref[c, D_MODEL:2 * D_MODEL, :]) * oat
        mix_ref[0, rows, :] = mixt.T
        kprev_ref[...] = kc
        vtprev_ref[...] = vt
        pen_ref[...] = jnp.zeros_like(pen_ref)
        ko_ref[0] = kc
        vo_ref[0] = vc
        return carry

    lax.fori_loop(0, nchunk, chunk, 0)


def _pa_call(x, mod3, n1, wqgt, wkv, bias_t, sink_rows, tt):
    B, T, _ = x.shape
    kw = HA_KV * HD_A
    qw = (HA_Q // HA_KV) * WINDOW
    kern = functools.partial(_pa_kernel, tt=tt)
    lowm = (np.arange(WINDOW)[:, None] > (np.arange(qw)[None, :] % WINDOW)).astype(np.float32)
    return pl.pallas_call(
        kern,
        grid=(B, T // tt),
        in_specs=[pl.BlockSpec((1, tt, D_MODEL), lambda b, t: (b, t, 0)),
                  pl.BlockSpec((1, 1, 6 * D_MODEL), lambda b, t: (b, 0, 0)),
                  pl.BlockSpec((1, D_MODEL), lambda b, t: (0, 0)),
                  pl.BlockSpec((2 * D_MODEL, D_MODEL), lambda b, t: (0, 0)),
                  pl.BlockSpec((D_MODEL, 2 * kw), lambda b, t: (0, 0)),
                  pl.BlockSpec((HA_KV, WINDOW, qw), lambda b, t: (0, 0, 0)),
                  pl.BlockSpec((HA_KV, 1, qw), lambda b, t: (0, 0, 0)),
                  pl.BlockSpec((WINDOW, qw), lambda b, t: (0, 0))],
        out_specs=[pl.BlockSpec((1, tt, D_MODEL), lambda b, t: (b, t, 0)),
                   pl.BlockSpec((1, WINDOW, kw), lambda b, t: (b, 0, 0)),
                   pl.BlockSpec((1, WINDOW, kw), lambda b, t: (b, 0, 0))],
        out_shape=[jax.ShapeDtypeStruct((B, T, D_MODEL), F32),
                   jax.ShapeDtypeStruct((B, WINDOW, kw), F32),
                   jax.ShapeDtypeStruct((B, WINDOW, kw), F32)],
        scratch_shapes=[pltpu.VMEM((tt // WINDOW, 2 * D_MODEL, WINDOW), F32),
                        pltpu.VMEM((tt, 2 * kw), F32),
                        pltpu.VMEM((WINDOW, kw), F32),
                        pltpu.VMEM((kw, WINDOW), F32),
                        pltpu.VMEM((8, qw), F32)],
        compiler_params=_cparams(2),
        name="prompt_attn",
    )(x, mod3, n1, wqgt, wkv, bias_t, sink_rows, jnp.asarray(lowm, BF16))


def _pb_kernel(x_ref, mod_ref, n1_ref, wt_ref, cos_ref, sin_ref, qdec_ref, kdec_ref, dm_ref, mixa_ref,
               mix_ref, so_ref, zt_ref, s_ref, *, tt, glast):
    t = pl.program_id(1)
    nchunk = tt // WINDOW
    hw = HB * DK_B // 2
    hp = DK_B // 2

    @pl.when(t == 0)
    def _():
        s_ref[...] = jnp.zeros_like(s_ref)

    mod = mod_ref[0]
    h = _modnorm(x_ref[0], n1_ref[...], mod[:, D_MODEL:2 * D_MODEL], mod[:, 0:D_MODEL]).astype(BF16)
    for cc in range(nchunk // 2):
        z2 = _dot_nt(wt_ref[...], h[2 * WINDOW * cc:2 * WINDOW * (cc + 1), :])
        zt_ref[2 * cc] = z2[:, 0:WINDOW]
        zt_ref[2 * cc + 1] = z2[:, WINDOW:2 * WINDOW]

    def head_rows(pair, hh):
        return jnp.concatenate([pair[0][hp * hh:hp * (hh + 1), :], pair[1][hp * hh:hp * (hh + 1), :]], axis=0)

    def chunk(c, carry):
        r0 = pl.multiple_of(c * WINDOW, WINDOW)
        rows = pl.ds(r0, WINDOW)
        cos = cos_ref[c]
        sin = sin_ref[c]
        q1 = zt_ref[c, 0:hw, :]
        q2 = zt_ref[c, hw:2 * hw, :]
        k1 = zt_ref[c, 2 * hw:3 * hw, :]
        k2 = zt_ref[c, 3 * hw:4 * hw, :]
        rq = (q1 * cos - q2 * sin, q1 * sin + q2 * cos)
        rk = (k1 * cos - k2 * sin, k1 * sin + k2 * cos)
        qdec = qdec_ref[...]
        kdec = kdec_ref[...]
        qb = tuple(a.astype(BF16) for a in rq)
        kb = tuple(a.astype(BF16) for a in rk)
        qd = tuple((a * qdec).astype(BF16) for a in rq)
        kd = tuple((a * kdec).astype(BF16) for a in rk)
        vt = zt_ref[c, 4 * hw:4 * hw + HB * DV_B, :].astype(BF16)
        inner, cross, supd = [], [], []
        for hh in range(HB):
            s_old = s_ref[DK_B * hh:DK_B * (hh + 1), :]
            inner.append(_dot_tn(head_rows(qb, hh), head_rows(kb, hh)))
            cross.append(_dot_tn(s_old.astype(BF16), head_rows(qd, hh)))
            supd.append(glast[hh] * s_old + _dot_nt(head_rows(kd, hh), vt[DV_B * hh:DV_B * (hh + 1), :]))
        innd = [(inner[hh] * dm_ref[hh]).astype(BF16) for hh in range(HB)]
        outs = []
        for hh in range(HB):
            ot = _dot_nt(vt[DV_B * hh:DV_B * (hh + 1), :], innd[hh]) + cross[hh]
            s_ref[DK_B * hh:DK_B * (hh + 1), :] = supd[hh]
            ms = jnp.mean(ot * ot, axis=0, keepdims=True)
            outs.append(ot * lax.rsqrt(ms + EPS))
        obt = jnp.concatenate(outs, axis=0)
        bgt = zt_ref[c, 4 * hw + D_MODEL:4 * hw + 2 * D_MODEL, :]
        gbt = zt_ref[c, 4 * hw + 2 * D_MODEL:4 * hw + 3 * D_MODEL, :]
        mixt = _sigmoid(gbt) * (_silu(bgt) * obt)
        mix_ref[0, rows, :] = mixa_ref[0, rows, :] + mixt.T
        return carry

    lax.fori_loop(0, nchunk, chunk, 0)
    so_ref[0] = s_ref[...]


def _deinterleave_pairs(w):
    return w.reshape(w.shape[0], HB, DK_B // 2, 2).transpose(0, 3, 1, 2).reshape(w.shape[0], HB * DK_B)


def _rot_tables_t(T):
    theta = 1.0 / (10000.0 ** np.linspace(0.0, 1.0, DK_B // 2))
    ang = theta[:, None] * np.arange(T, dtype=np.float64)[None, :]

    def lay(a):
        a = np.tile(a, (HB, 1))
        return np.ascontiguousarray(a.reshape(a.shape[0], T // WINDOW, WINDOW).transpose(1, 0, 2)).astype(np.float32)

    return lay(np.cos(ang)), lay(np.sin(ang))


def _pb_call(x, mod3, n1, wbt, mixa, tt):
    B, T, _ = x.shape
    hw = HB * DK_B // 2
    cos, sin = _rot_tables_t(T)
    dm, qdec, kdec = _ret_tables()
    qdec_t = np.ascontiguousarray(qdec[:, ::2].T)
    kdec_t = np.ascontiguousarray(kdec[:, ::2].T)
    glast = tuple(float(v) for v in (_gammas() ** WINDOW))
    kern = functools.partial(_pb_kernel, tt=tt, glast=glast)
    full2 = lambda b, t: (0, 0)
    nct = tt // WINDOW
    return pl.pallas_call(
        kern,
        grid=(B, T // tt),
        in_specs=[pl.BlockSpec((1, tt, D_MODEL), lambda b, t: (b, t, 0)),
                  pl.BlockSpec((1, 1, 6 * D_MODEL), lambda b, t: (b, 0, 0)),
                  pl.BlockSpec((1, D_MODEL), full2),
                  pl.BlockSpec((SEG_B, D_MODEL), full2),
                  pl.BlockSpec((nct, hw, WINDOW), lambda b, t: (t, 0, 0)),
                  pl.BlockSpec((nct, hw, WINDOW), lambda b, t: (t, 0, 0)),
                  pl.BlockSpec((hw, WINDOW), full2),
                  pl.BlockSpec((hw, WINDOW), full2),
                  pl.BlockSpec((HB, WINDOW, WINDOW), lambda b, t: (0, 0, 0)),
                  pl.BlockSpec((1, tt, D_MODEL), lambda b, t: (b, t, 0))],
        out_specs=[pl.BlockSpec((1, tt, D_MODEL), lambda b, t: (b, t, 0)),
                   pl.BlockSpec((1, HB * DK_B, DV_B), lambda b, t: (b, 0, 0))],
        out_shape=[jax.ShapeDtypeStruct((B, T, D_MODEL), F32),
                   jax.ShapeDtypeStruct((B, HB * DK_B, DV_B), F32)],
        scratch_shapes=[pltpu.VMEM((nct, SEG_B, WINDOW), F32),
                        pltpu.VMEM((HB * DK_B, DV_B), F32)],
        compiler_params=_cparams(2),
        name="prompt_retention",
    )(x, mod3, n1, wbt, jnp.asarray(cos), jnp.asarray(sin), jnp.asarray(qdec_t), jnp.asarray(kdec_t),
      jnp.asarray(dm), mixa)


def _pc_kernel(x_ref, mod_ref, n1_ref, w_ref, cw_ref, cb_ref, dtb_ref, a_ref, dsk_ref, snw_ref, tri_ref,
               mixab_ref, wo_ref,
               xo_ref, ho_ref, co_ref,
               z_ref, xbuf_ref, xbc_ref, dt_ref, hst_ref, mixs_ref, *, tt):
    t = pl.program_id(1)

    @pl.when(t == 0)
    def _():
        xbuf_ref[0:8, :] = jnp.zeros((8, CONV_DIM), F32)
        hst_ref[...] = jnp.zeros_like(hst_ref)

    x = x_ref[0]
    mod = mod_ref[0]
    h = _modnorm(x, n1_ref[...], mod[:, D_MODEL:2 * D_MODEL], mod[:, 0:D_MODEL])
    z_ref[...] = _dot(h.astype(BF16), w_ref[...])

    xbuf_ref[8:8 + tt, :] = z_ref[:, 1024:2560]
    acc = cb_ref[...]
    for i in range(CONV_W):
        acc = acc + xbuf_ref[5 + i:5 + i + tt, :] * cw_ref[i:i + 1, :]
    xbc_ref[...] = _silu(acc)
    co_ref[0] = xbuf_ref[tt + 5:tt + 8, :]
    xbuf_ref[0:8, :] = xbuf_ref[tt:tt + 8, :]
    dt_ref[...] = _softplus(z_ref[:, C_DT:C_DT + 128] + dtb_ref[...])

    ii = lax.broadcasted_iota(jnp.int32, (WINDOW, WINDOW), 0)
    jj = lax.broadcasted_iota(jnp.int32, (WINDOW, WINDOW), 1)
    causal = ii >= jj
    hpg = HC // G_C

    def chunk(c, carry):
        r0 = pl.multiple_of(c * WINDOW, WINDOW)
        rows = pl.ds(r0, WINDOW)
        xc = xbc_ref[rows, 0:1024]
        bmat = xbc_ref[rows, 1024:1280]
        cmat = xbc_ref[rows, 1280:1536]
        dtc = dt_ref[rows, :]
        acum = jnp.dot(tri_ref[...], dtc * a_ref[...], precision=lax.Precision.HIGHEST,
                       preferred_element_type=F32)
        acum_t = acum.T
        dt_t = dtc.T
        x_t = xc.T
        xb = xc.astype(BF16)
        bb = bmat.astype(BF16)
        cb16 = cmat.astype(BF16)
        ys = []
        for g in range(G_C):
            ns = slice(N_C * g, N_C * (g + 1))
            cbg = _dot_nt(cb16[:, ns], bb[:, ns])
            for hh in range(hpg * g, hpg * (g + 1)):
                ps = slice(HD_C * hh, HD_C * (hh + 1))
                colb = jnp.broadcast_to(acum[:, hh:hh + 1], (WINDOW, WINDOW))
                row = acum_t[hh:hh + 1, :]
                dtrow = dt_t[hh:hh + 1, :]
                lmat = jnp.exp(jnp.where(causal, colb - row, NEG))
                m = cbg * lmat * dtrow
                hs = hst_ref[ps, :]
                cs = cmat[:, ns] * jnp.exp(colb)
                y = _dot(m.astype(BF16), xb[:, ps]) + _dot_nt(cs.astype(BF16), hs.astype(BF16))
                last = colb[WINDOW - 1:WINDOW, :]
                wrow = jnp.exp(last - row) * dtrow
                xw = (x_t[ps, :] * wrow).astype(BF16)
                hst_ref[ps, :] = jnp.exp(last) * hs + _dot(xw, bb[:, ns])
                ys.append(y)
        y = jnp.concatenate(ys, axis=1) + dsk_ref[...] * xc
        yc = y * _silu(z_ref[rows, 0:1024])
        gw = D_MODEL // G_C
        oc = jnp.concatenate([_rms_lanes(yc[:, gw * g:gw * (g + 1)]) for g in range(G_C)], axis=1) * snw_ref[...]
        gc = _sigmoid(z_ref[rows, C_GC:C_GC + 1024])
        mixs_ref[rows, :] = (mixab_ref[0, rows, :] + gc * oc).astype(BF16)
        return carry

    lax.fori_loop(0, tt // WINDOW, chunk, 0)
    ho_ref[0] = hst_ref[...]
    g1 = mod[:, 2 * D_MODEL:3 * D_MODEL]
    xo_ref[0] = x + g1 * _dot(mixs_ref[...], wo_ref[...])


def _pc_call(x, mod3, n1, wc, cw, cb, dtb_pad, a_pad, dsk_full, snw, mixab, wo, tt):
    B, T, _ = x.shape
    tri = jnp.asarray(np.tril(np.ones((WINDOW, WINDOW), np.float32)))
    kern = functools.partial(_pc_kernel, tt=tt)
    full2 = lambda b, t: (0, 0)
    return pl.pallas_call(
        kern,
        grid=(B, T // tt),
        in_specs=[pl.BlockSpec((1, tt, D_MODEL), lambda b, t: (b, t, 0)),
                  pl.BlockSpec((1, 1, 6 * D_MODEL), lambda b, t: (b, 0, 0)),
                  pl.BlockSpec((1, D_MODEL), full2),
                  pl.BlockSpec((D_MODEL, SEG_C), full2),
                  pl.BlockSpec((CONV_W, CONV_DIM), full2),
                  pl.BlockSpec((1, CONV_DIM), full2),
                  pl.BlockSpec((1, 128), full2),
                  pl.BlockSpec((1, 128), full2),
                  pl.BlockSpec((1, D_MODEL), full2),
                  pl.BlockSpec((1, D_MODEL), full2),
                  pl.BlockSpec((WINDOW, WINDOW), full2),
                  pl.BlockSpec((1, tt, D_MODEL), lambda b, t: (b, t, 0)),
                  pl.BlockSpec((D_MODEL, D_MODEL), full2)],
        out_specs=[pl.BlockSpec((1, tt, D_MODEL), lambda b, t: (b, t, 0)),
                   pl.BlockSpec((1, HC * HD_C, N_C), lambda b, t: (b, 0, 0)),
                   pl.BlockSpec((1, CONV_W - 1, CONV_DIM), lambda b, t: (b, 0, 0))],
        out_shape=[jax.ShapeDtypeStruct((B, T, D_MODEL), F32),
                   jax.ShapeDtypeStruct((B, HC * HD_C, N_C), F32),
                   jax.ShapeDtypeStruct((B, CONV_W - 1, CONV_DIM), F32)],
        scratch_shapes=[pltpu.VMEM((tt, SEG_C), F32),
                        pltpu.VMEM((tt + 8, CONV_DIM), F32),
                        pltpu.VMEM((tt, CONV_DIM), F32),
                        pltpu.VMEM((tt, 128), F32),
                        pltpu.VMEM((HC * HD_C, N_C), F32),
                        pltpu.VMEM((tt, D_MODEL), BF16)],
        compiler_params=_cparams(2),
        name="prompt_ssd_out",
    )(x, mod3, n1, wc, cw, cb, dtb_pad, a_pad, dsk_full, snw, tri, mixab, wo)


def _mlp_kernel(x_ref, sh_ref, sc_ref, g_ref, n2_ref, wu_ref, wd_ref, fn_ref, o_ref, h_ref, acc_ref,
                *, nf, final, per_row):
    f = pl.program_id(1)
    rd = (lambda r: r[...]) if per_row else (lambda r: r[0])

    @pl.when(f == 0)
    def _():
        h_ref[...] = _modnorm(x_ref[...], n2_ref[...], rd(sc_ref), rd(sh_ref)).astype(BF16)
        acc_ref[...] = jnp.zeros_like(acc_ref)

    u = _dot(h_ref[...], wu_ref[0])
    u = jnp.square(jnp.maximum(u, 0.0)).astype(BF16)
    acc_ref[...] += _dot(u, wd_ref[0])

    @pl.when(f == nf - 1)
    def _():
        y = x_ref[...] + rd(g_ref) * acc_ref[...]
        if final:
            y = _rms_lanes(y) * fn_ref[...]
        o_ref[...] = y


def _mlp_call(x2, mod, n2, wu, wd, fn, layer, tm, tf, rows_per_mod, final):
    M = x2.shape[0]
    nf = D_FF // tf
    per_row = rows_per_mod == 1
    if per_row:
        mspec = lambda j: pl.BlockSpec((tm, D_MODEL), lambda m, f: (m, j))
    else:
        mspec = lambda j: pl.BlockSpec((1, 1, D_MODEL), lambda m, f: ((m * tm) // rows_per_mod, 0, j))
    kern = functools.partial(_mlp_kernel, nf=nf, final=final, per_row=per_row)
    return pl.pallas_call(
        kern,
        grid=(M // tm, nf),
        in_specs=[pl.BlockSpec((tm, D_MODEL), lambda m, f: (m, 0)),
                  mspec(3), mspec(4), mspec(5),
                  pl.BlockSpec((1, D_MODEL), lambda m, f: (0, 0)),
                  pl.BlockSpec((1, D_MODEL, tf), lambda m, f: (layer, 0, f)),
                  pl.BlockSpec((1, tf, D_MODEL), lambda m, f: (layer, f, 0)),
                  pl.BlockSpec((1, D_MODEL), lambda m, f: (0, 0))],
        out_specs=pl.BlockSpec((tm, D_MODEL), lambda m, f: (m, 0)),
        out_shape=jax.ShapeDtypeStruct((M, D_MODEL), F32),
        scratch_shapes=[pltpu.VMEM((tm, D_MODEL), BF16),
                        pltpu.VMEM((tm, D_MODEL), F32)],
        compiler_params=_cparams(2),
        name="mlp",
    )(x2, mod, mod, mod, n2, wu, wd, fn)


def _sproj_kernel(x_ref, sh_ref, sc_ref, n1_ref, w_ref, o_ref):
    h = _modnorm(x_ref[...], n1_ref[...], sc_ref[...], sh_ref[...])
    o_ref[...] = _dot(h.astype(BF16), w_ref[...])


def _sproj_call(xs, mod_s, n1, w):
    n, width = xs.shape[0], w.shape[1]
    return pl.pallas_call(
        _sproj_kernel,
        grid=(1,),
        in_specs=[pl.BlockSpec((n, D_MODEL), lambda i: (0, 0)),
                  pl.BlockSpec((n, D_MODEL), lambda i: (0, 0)),
                  pl.BlockSpec((n, D_MODEL), lambda i: (0, 1)),
                  pl.BlockSpec((1, D_MODEL), lambda i: (0, 0)),
                  pl.BlockSpec((D_MODEL, width), lambda i: (0, 0))],
        out_specs=pl.BlockSpec((n, width), lambda i: (0, 0)),
        out_shape=jax.ShapeDtypeStruct((n, width), F32),
        compiler_params=_cparams(1),
        name="sample_proj",
    )(xs, mod_s, mod_s, n1, w)


def _carry_outputs(prev, n_in):
    extra = [pl.BlockSpec(memory_space=pl.ANY) for _ in prev]
    alias = {n_in + k: k for k in range(len(prev))}
    return extra, alias


def _sa_kernel(q_ref, kv_ref, ck_ref, cv_ref, bias_ref, sink_ref, *rest):
    ko_ref, vo_ref, oa_ref = rest[-3:]
    nb = ROWS_PER_SAMPLE_STEP
    gsz = HA_Q // HA_KV
    rg = lax.broadcasted_iota(jnp.int32, (HA_Q, HD_A), 0) // gsz
    sink = sink_ref[...]
    kw = HA_KV * HD_A
    scores, vmats = [], []
    for i in range(nb):
        ko_ref[0, i, 0:WINDOW - 1, :] = ck_ref[0, i, 1:WINDOW, :]
        ko_ref[0, i, WINDOW - 1:WINDOW, :] = kv_ref[i:i + 1, 0:kw]
        vo_ref[0, i, 0:WINDOW - 1, :] = cv_ref[0, i, 1:WINDOW, :]
        vo_ref[0, i, WINDOW - 1:WINDOW, :] = kv_ref[i:i + 1, kw:2 * kw]
        kmat = ko_ref[0, i].astype(BF16)
        vmats.append(vo_ref[0, i].astype(BF16))
        q = q_ref[i] * (HD_A ** -0.5)
        qe = jnp.concatenate([jnp.where(rg == g, q, 0.0) for g in range(HA_KV)], axis=1).astype(BF16)
        scores.append(_dot_nt(qe, kmat))
    probs = []
    for i in range(nb):
        sc = scores[i] + bias_ref[...]
        m = jnp.maximum(jnp.max(sc, axis=-1, keepdims=True), sink)
        p = jnp.exp(sc - m)
        den = jnp.sum(p, axis=-1, keepdims=True) + jnp.exp(sink - m)
        probs.append((p.astype(BF16), den))
    outs = [_dot(probs[i][0], vmats[i]) for i in range(nb)]
    for i in range(nb):
        o = outs[i] / probs[i][1]
        o16 = jnp.zeros((HA_Q, HD_A), F32)
        for g in range(HA_KV):
            o16 = o16 + jnp.where(rg == g, o[:, HD_A * g:HD_A * (g + 1)], 0.0)
        oa_ref[i] = o16


def _sa_call(q3, kv, cache_k, cache_v, bias_s, sink_col, layer, prev):
    n = q3.shape[0]
    nb = ROWS_PER_SAMPLE_STEP
    kw = HA_KV * HD_A
    cspec = pl.BlockSpec((1, nb, WINDOW, kw), lambda i: (layer, i, 0, 0))
    extra, alias = _carry_outputs(prev, 6)
    return pl.pallas_call(
        _sa_kernel,
        grid=(n // nb,),
        in_specs=[pl.BlockSpec((nb, HA_Q, HD_A), lambda i: (i, 0, 0)),
                  pl.BlockSpec((nb, 2 * kw), lambda i: (i, 0)),
                  cspec, cspec,
                  pl.BlockSpec((HA_Q, WINDOW), lambda i: (0, 0)),
                  pl.BlockSpec((HA_Q, 1), lambda i: (0, 0))] + extra,
        out_specs=[cspec, cspec, pl.BlockSpec((nb, HA_Q, HD_A), lambda i: (i, 0, 0))],
        out_shape=[jax.ShapeDtypeStruct((DEPTH, n, WINDOW, kw), F32),
                   jax.ShapeDtypeStruct((DEPTH, n, WINDOW, kw), F32),
                   jax.ShapeDtypeStruct((n, HA_Q, HD_A), F32)],
        input_output_aliases=alias,
        compiler_params=_cparams(1),
        name="sample_attn",
    )(q3, kv, cache_k, cache_v, bias_s, sink_col, *prev)


def _sr_kernel(zb_ref, cos_ref, sin_ref, gcol_ref, s_ref, *rest):
    so_ref, o_ref = rest[-2:]
    nb = ROWS_PER_SAMPLE_STEP
    wq = HB * DK_B
    cos = cos_ref[...]
    sin = sin_ref[...]
    qf = zb_ref[:, 0:wq]
    kf = zb_ref[:, wq:2 * wq]
    qr = qf * cos + _pairswap(qf) * sin
    kr = (kf * cos + _pairswap(kf) * sin) * (DK_B ** -0.5)
    v = zb_ref[:, 2 * wq:2 * wq + HB * DV_B]
    r8 = lax.broadcasted_iota(jnp.int32, (HB, wq), 0)
    hl = lax.broadcasted_iota(jnp.int32, (HB, wq), 1) // DK_B
    rv = lax.broadcasted_iota(jnp.int32, (HB, DV_B), 0)
    outers, q8s = [], []
    for i in range(nb):
        k8 = jnp.where(hl == r8, jnp.broadcast_to(kr[i:i + 1, :], (HB, wq)), 0.0).astype(BF16)
        q8s.append(jnp.where(hl == r8, jnp.broadcast_to(qr[i:i + 1, :], (HB, wq)), 0.0).astype(BF16))
        v8 = jnp.zeros((HB, DV_B), F32)
        for r in range(HB):
            v8 = jnp.where(rv == r, jnp.broadcast_to(v[i:i + 1, DV_B * r:DV_B * (r + 1)], (HB, DV_B)), v8)
        outers.append(_dot_tn(k8, v8.astype(BF16)))
    s_news = []
    for i in range(nb):
        s_new = gcol_ref[...] * s_ref[0, i] + outers[i]
        so_ref[0, i] = s_new
        s_news.append(s_new.astype(BF16))
    for i in range(nb):
        o_ref[i] = _dot(q8s[i], s_news[i])


def _sr_call(zb, state, layer, prev):
    n = zb.shape[0]
    nb = ROWS_PER_SAMPLE_STEP
    wq = HB * DK_B
    cos, sin = _rot_tables(np.array([PAST_LEN]))
    gcol = np.repeat(_gammas(), DK_B)[:, None] * np.ones((1, DV_B))
    sspec = pl.BlockSpec((1, nb, wq, DV_B), lambda i: (layer, i, 0, 0))
    extra, alias = _carry_outputs(prev, 5)
    return pl.pallas_call(
        _sr_kernel,
        grid=(n // nb,),
        in_specs=[pl.BlockSpec((nb, SEG_B), lambda i: (i, 0)),
                  pl.BlockSpec((1, wq), lambda i: (0, 0)),
                  pl.BlockSpec((1, wq), lambda i: (0, 0)),
                  pl.BlockSpec((wq, DV_B), lambda i: (0, 0)),
                  sspec] + extra,
        out_specs=[sspec, pl.BlockSpec((nb, HB, DV_B), lambda i: (i, 0, 0))],
        out_shape=[jax.ShapeDtypeStruct((DEPTH, n, wq, DV_B), F32),
                   jax.ShapeDtypeStruct((n, HB, DV_B), F32)],
        input_output_aliases=alias,
        compiler_params=_cparams(1),
        name="sample_retention",
    )(zb, jnp.asarray(cos), jnp.asarray(sin), jnp.asarray(gcol.astype(np.float32)), state, *prev)


def _ss_kernel(zc_ref, cs_ref, cw_ref, cb_ref, dtb_ref, a_ref, e3_ref, h_ref, *rest):
    ho_ref, cso_ref, y_ref, xc_ref = rest[-4:]
    nb = ROWS_PER_SAMPLE_STEP
    cx = zc_ref[:, 1024:2560]
    taps = [cs_ref[0, :, CONV_DIM * i:CONV_DIM * (i + 1)] for i in range(CONV_W - 1)] + [cx]
    acc = cb_ref[...]
    for i in range(CONV_W):
        acc = acc + taps[i] * cw_ref[i:i + 1, :]
    xbc = _silu(acc)
    cso_ref[...] = jnp.concatenate(taps[1:], axis=1)
    xc = xbc[:, 0:1024]
    bmat = xbc[:, 1024:1280]
    cmat = xbc[:, 1280:1536]
    dt = _softplus(zc_ref[:, C_DT:C_DT + 128] + dtb_ref[...])
    da = jnp.exp(dt * a_ref[...])
    dt_e = _dot(jnp.concatenate(_split3(dt), axis=1), e3_ref[...])
    da_e = _dot(jnp.concatenate(_split3(da), axis=1), e3_ref[...])
    dtx = dt_e * xc
    gw = (HC // G_C) * HD_C
    r8 = lax.broadcasted_iota(jnp.int32, (nb, gw), 0)
    rn = lax.broadcasted_iota(jnp.int32, (nb, N_C), 0)
    ones8 = jnp.ones((nb, N_C), BF16)
    prods = {}
    for g in range(G_C):
        ws = slice(gw * g, gw * (g + 1))
        bg16 = bmat[:, N_C * g:N_C * (g + 1)].astype(BF16)
        for i in range(nb):
            x8 = jnp.where(r8 == i, dtx[:, ws], 0.0).astype(BF16)
            outer = _dot_tn(x8, bg16)
            hi, mid, lo = (p.astype(F32) for p in _split3(jnp.broadcast_to(da_e[i:i + 1, ws], (nb, gw))))
            l3 = jnp.where(r8 == 0, hi, jnp.where(r8 == 1, mid, jnp.where(r8 == 2, lo, 0.0)))
            prods[g, i] = (outer, _dot_tn(l3.astype(BF16), ones8))
    h16 = {}
    for g in range(G_C):
        ws = slice(gw * g, gw * (g + 1))
        for i in range(nb):
            outer, dacol = prods[g, i]
            h_new = dacol * h_ref[0, i, ws, :] + outer
            ho_ref[0, i, ws, :] = h_new
            h16[g, i] = h_new.astype(BF16)
    ycols = []
    for g in range(G_C):
        ns = slice(N_C * g, N_C * (g + 1))
        yacc = jnp.zeros((nb, gw), F32)
        for i in range(nb):
            c8 = jnp.where(rn == i, cmat[:, ns], 0.0).astype(BF16)
            yacc = yacc + _dot_nt(c8, h16[g, i])
        ycols.append(yacc)
    y_ref[...] = jnp.concatenate(ycols, axis=1)
    xc_ref[...] = xc


def _ss_call(zc, conv_state, hstate, cw, cb, dtb_pad, a_pad, layer, prev):
    n = zc.shape[0]
    nb = ROWS_PER_SAMPLE_STEP
    e = np.zeros((128, D_MODEL), np.float32)
    for hh in range(HC):
        e[hh, HD_C * hh:HD_C * (hh + 1)] = 1.0
    e3 = jnp.asarray(np.concatenate([e, e, e], axis=0), dtype=BF16)
    cwid = (CONV_W - 1) * CONV_DIM
    full = lambda i: (0, 0)
    hspec = pl.BlockSpec((1, nb, HC * HD_C, N_C), lambda i: (layer, i, 0, 0))
    extra, alias = _carry_outputs(prev, 8)
    return pl.pallas_call(
        _ss_kernel,
        grid=(n // nb,),
        in_specs=[pl.BlockSpec((nb, SEG_C), lambda i: (i, 0)),
                  pl.BlockSpec((1, nb, cwid), lambda i: (layer, i, 0)),
                  pl.BlockSpec((CONV_W, CONV_DIM), full),
                  pl.BlockSpec((1, CONV_DIM), full),
                  pl.BlockSpec((1, 128), full),
                  pl.BlockSpec((1, 128), full),
                  pl.BlockSpec((3 * 128, D_MODEL), full),
                  hspec] + extra,
        out_specs=[hspec,
                   pl.BlockSpec((nb, cwid), lambda i: (i, 0)),
                   pl.BlockSpec((nb, D_MODEL), lambda i: (i, 0)),
                   pl.BlockSpec((nb, D_MODEL), lambda i: (i, 0))],
        out_shape=[jax.ShapeDtypeStruct((DEPTH, n, HC * HD_C, N_C), F32),
                   jax.ShapeDtypeStruct((n, cwid), F32),
                   jax.ShapeDtypeStruct((n, D_MODEL), F32),
                   jax.ShapeDtypeStruct((n, D_MODEL), F32)],
        input_output_aliases=alias,
        compiler_params=_cparams(1),
        name="sample_ssd",
    )(zc, conv_state, cw, cb, dtb_pad, a_pad, e3, hstate, *prev)


def _sm_kernel(x_ref, g1_ref, oa_ref, oret_ref, y_ref, xc_ref, za_ref, zb_ref, zc_ref,
               dsk_ref, snw_ref, wo_ref, o_ref):
    ob = jnp.concatenate([_rms_lanes(oret_ref[:, DV_B * hh:DV_B * (hh + 1)]) for hh in range(HB)], axis=1)
    ob = _silu(zb_ref[:, 2048:3072]) * ob
    yc = (y_ref[...] + dsk_ref[...] * xc_ref[...]) * _silu(zc_ref[:, 0:1024])
    gw = D_MODEL // G_C
    oc = jnp.concatenate([_rms_lanes(yc[:, gw * g:gw * (g + 1)]) for g in range(G_C)], axis=1) * snw_ref[...]
    mix = (_sigmoid(za_ref[:, 1536:2560]) * oa_ref[...] + _sigmoid(zb_ref[:, 3072:4096]) * ob
           + _sigmoid(zc_ref[:, C_GC:C_GC + 1024]) * oc)
    o_ref[...] = x_ref[...] + g1_ref[...] * _dot(mix.astype(BF16), wo_ref[...])


def _sm_call(xs, mod_s, oa, oret, y, xc, za, zb, zc, dsk_full, snw, wo):
    n = xs.shape[0]
    full = lambda i: (0, 0)
    row = lambda w: pl.BlockSpec((n, w), full)
    return pl.pallas_call(
        _sm_kernel,
        grid=(1,),
        in_specs=[row(D_MODEL),
                  pl.BlockSpec((n, D_MODEL), lambda i: (0, 2)),
                  row(D_MODEL), row(D_MODEL), row(D_MODEL), row(D_MODEL),
                  row(SEG_A), row(SEG_B), row(SEG_C),
                  pl.BlockSpec((1, D_MODEL), full),
                  pl.BlockSpec((1, D_MODEL), full),
                  pl.BlockSpec((D_MODEL, D_MODEL), full)],
        out_specs=row(D_MODEL),
        out_shape=jax.ShapeDtypeStruct((n, D_MODEL), F32),
        compiler_params=_cparams(1),
        name="sample_merge_out",
    )(xs, mod_s, oa, oret, y, xc, za, zb, zc, dsk_full, snw, wo)


PROMPT_TILE = 512
MLP_ROWS = 1024
MLP_FF = 512


def _prep_w_in(w):
    wa = jnp.concatenate([w[:, O_AQ:O_BQ], w[:, O_GTS:O_GTS + 1024]], axis=1)
    wb = jnp.concatenate([w[:, O_BQ:O_CZ], w[:, O_GTS + 1024:O_GTS + 2048]], axis=1)
    wc = jnp.concatenate([w[:, O_CZ:O_CDT], jnp.pad(w[:, O_CDT:O_GTS], ((0, 0), (0, 128 - HC))),
                          w[:, O_GTS + 2048:O_GTS + 3072]], axis=1)
    return wa.astype(BF16), wb.astype(BF16), wc.astype(BF16)


def _forward(x_prompt, x_sample, cache_win_k, cache_win_v, state_ret, state_ssm, state_conv,
             c_prompt, c_sample, rel_bias_table, attn_sinks, norm1_w, norm2_w, ada_w, ada_b,
             w_in, conv_w, conv_b, dt_bias, A_log, D_skip, ssm_norm_w, w_out, w_up, w_down,
             final_norm_w, *, prompt_tile, mlp_rows, mlp_ff):
    B, T, _ = x_prompt.shape
    DB = x_sample.shape[0]
    kw = HA_KV * HD_A

    mod_all = _ada_call(jnp.concatenate([c_prompt, c_sample], axis=0), ada_w, ada_b)

    bias_t, bias_s = _bias_tables(rel_bias_table)

    wu16 = w_up.astype(BF16)
    wd16 = w_down.astype(BF16)
    wo16 = w_out.astype(BF16)
    fn = final_norm_w.reshape(1, D_MODEL)
    ck = cache_win_k.reshape(DEPTH, DB, WINDOW, kw)
    cv = cache_win_v.reshape(DEPTH, DB, WINDOW, kw)
    sret = state_ret.reshape(DEPTH, DB, HB * DK_B, DV_B)
    sssm = state_ssm.reshape(DEPTH, DB, HC * HD_C, N_C)
    sconv = state_conv.reshape(DEPTH, DB, (CONV_W - 1) * CONV_DIM)

    xp = x_prompt
    xs = x_sample.reshape(DB, D_MODEL)
    outs_p = [[] for _ in range(5)]
    conv_s = []
    win_kv, ret_all, ssm_all = (), (), ()
    for l in range(DEPTH):
        wl = w_in[l]
        wa, wb, wc = _prep_w_in(wl)
        n1 = norm1_w[l].reshape(1, D_MODEL)
        n2 = norm2_w[l].reshape(1, D_MODEL)
        cw = conv_w[l]
        cb = conv_b[l].reshape(1, CONV_DIM)
        dtb_pad = jnp.pad(dt_bias[l], (0, 128 - HC)).reshape(1, 128)
        a_pad = jnp.pad(-jnp.exp(A_log[l].astype(F32)), (0, 128 - HC)).reshape(1, 128)
        dsk_full = jnp.repeat(D_skip[l], HD_C).reshape(1, D_MODEL)
        snw = ssm_norm_w[l].reshape(1, D_MODEL)
        final = l == DEPTH - 1
        mod_p = mod_all[l, :B].reshape(B, 1, 6 * D_MODEL)
        mod_s = mod_all[l, B:]

        wqgt = jnp.concatenate([wl[:, O_AQ:O_AK] * (HD_A ** -0.5), wl[:, O_GTS:O_GTS + D_MODEL]], axis=1).T.astype(BF16)
        wkv = wl[:, O_AK:O_BQ].astype(BF16)
        sink_rows = jnp.repeat(attn_sinks[l], WINDOW).reshape(HA_KV, 1, (HA_Q // HA_KV) * WINDOW)
        mixa, kbuf, vbuf = _pa_call(xp, mod_p, n1, wqgt, wkv, bias_t, sink_rows, prompt_tile)
        wbt = jnp.concatenate([_deinterleave_pairs(wl[:, O_BQ:O_BK]),
                               _deinterleave_pairs(wl[:, O_BK:O_BV]) * (DK_B ** -0.5), wl[:, O_BV:O_CZ],
                               wl[:, O_GTS + D_MODEL:O_GTS + 2 * D_MODEL]], axis=1).T.astype(BF16)
        mixab, s_perm = _pb_call(xp, mod_p, n1, wbt, mixa, prompt_tile)
        s_ret = s_perm.reshape(B, HB, 2, DK_B // 2, DV_B).transpose(0, 1, 3, 2, 4)
        x1, h_ssm, conv_new = _pc_call(xp, mod_p, n1, wc, cw, cb, dtb_pad, a_pad, dsk_full, snw,
                                       mixab, wo16[l], prompt_tile)
        xp = _mlp_call(x1.reshape(B * T, D_MODEL), mod_p, n2, wu16, wd16, fn, l,
                       mlp_rows, mlp_ff, T, final).reshape(B, T, D_MODEL)
        for lst, v in zip(outs_p, (kbuf.reshape(B, WINDOW, HA_KV, HD_A), vbuf.reshape(B, WINDOW, HA_KV, HD_A),
                                   s_ret.reshape(B, HB, DK_B, DV_B), h_ssm.reshape(B, HC, HD_C, N_C), conv_new)):
            lst.append(v)

        za = _sproj_call(xs, mod_s, n1, wa)
        zb = _sproj_call(xs, mod_s, n1, wb)
        zc = _sproj_call(xs, mod_s, n1, wc)
        q3 = za[:, 0:HA_Q * HD_A].reshape(DB, HA_Q, HD_A)
        kv = za[:, HA_Q * HD_A:HA_Q * HD_A + 2 * kw]
        ck_new, cv_new, oa3 = _sa_call(q3, kv, ck, cv, bias_s, attn_sinks[l].reshape(HA_Q, 1), l, win_kv)
        win_kv = (ck_new, cv_new)
        s_new, o3 = _sr_call(zb, sret, l, ret_all)
        ret_all = (s_new,)
        h_new, cs_new, y_s, xc_s = _ss_call(zc, sconv, sssm, cw, cb, dtb_pad, a_pad, l, ssm_all)
        ssm_all = (h_new,)
        xs1 = _sm_call(xs, mod_s, oa3.reshape(DB, D_MODEL), o3.reshape(DB, D_MODEL), y_s, xc_s,
                       za, zb, zc, dsk_full, snw, wo16[l])
        xs = _mlp_call(xs1, mod_s, n2, wu16, wd16, fn, l, DB, mlp_ff, 1, final)
        conv_s.append(cs_new.reshape(DB, CONV_W - 1, CONV_DIM))

    stk = lambda lst: jnp.stack(lst, axis=0)
    return (xp, xs.reshape(DB, 1, D_MODEL),
            stk(outs_p[0]), stk(outs_p[1]), stk(outs_p[2]), stk(outs_p[3]), stk(outs_p[4]),
            win_kv[0].reshape(DEPTH, DB, WINDOW, HA_KV, HD_A), win_kv[1].reshape(DEPTH, DB, WINDOW, HA_KV, HD_A),
            ret_all[0].reshape(DEPTH, DB, HB, DK_B, DV_B), ssm_all[0].reshape(DEPTH, DB, HC, HD_C, N_C),
            stk(conv_s))


def kernel(x_prompt, x_sample, cache_win_k, cache_win_v, state_ret, state_ssm, state_conv, c_prompt, c_sample,
           rel_bias_table, attn_sinks, norm1_w, norm2_w, ada_w, ada_b, w_in, conv_w, conv_b, dt_bias, A_log,
           D_skip, ssm_norm_w, w_out, w_up, w_down, final_norm_w):
    return _forward(x_prompt, x_sample, cache_win_k, cache_win_v, state_ret, state_ssm, state_conv,
                    c_prompt, c_sample, rel_bias_table, attn_sinks, norm1_w, norm2_w, ada_w, ada_b,
                    w_in, conv_w, conv_b, dt_bias, A_log, D_skip, ssm_norm_w, w_out, w_up, w_down,
                    final_norm_w, prompt_tile=PROMPT_TILE, mlp_rows=MLP_ROWS, mlp_ff=MLP_FF)
```

```python
import functools
import math

import numpy as np
import jax
import jax.numpy as jnp
from jax import lax
from jax.experimental import pallas as pl
from jax.experimental.pallas import tpu as pltpu

F32 = jnp.float32
BF16 = jnp.bfloat16

D_MODEL = 1024
DEPTH = 2
PAST_LEN = 16384
WINDOW = 128
HA_Q = 16
HA_KV = 4
HD_A = 64
NUM_BUCKETS = 32
MAX_DISTANCE = WINDOW
HB = 8
DK_B = 64
DV_B = 128
HC = 16
HD_C = 64
N_C = 128
G_C = 2
CONV_W = 4
CONV_DIM = D_MODEL + 2 * G_C * N_C
D_FF = 4 * D_MODEL
EPS = 1e-6
NEG = -1e30

O_AQ, O_AK, O_AV = 0, 1024, 1280
O_BQ, O_BK, O_BV, O_BG = 1536, 2048, 2560, 3584
O_CZ, O_CXBC, O_CDT, O_GTS = 4608, 5632, 7168, 7184
SEG_A = 2560
SEG_B = 4096
SEG_C = 3712
C_DT = 2560
C_GC = 2688

VMEM_LIMIT_V7X = 56 * 1024 * 1024
ROWS_PER_SAMPLE_STEP = 8


def _cparams(n_axes):
    return pltpu.CompilerParams(dimension_semantics=("arbitrary",) * n_axes,
                                vmem_limit_bytes=VMEM_LIMIT_V7X)


def _dot(a, b):
    return jnp.dot(a, b, preferred_element_type=F32)


def _dot_nt(a, b):
    return lax.dot_general(a, b, (((1,), (1,)), ((), ())), preferred_element_type=F32)


def _dot_tn(a, b):
    return lax.dot_general(a, b, (((0,), (0,)), ((), ())), preferred_element_type=F32)


def _sigmoid(x):
    return 0.5 * (jnp.tanh(0.5 * x) + 1.0)


def _silu(x):
    return x * _sigmoid(x)


def _softplus(x):
    return jnp.maximum(x, 0.0) + jnp.log1p(jnp.exp(-jnp.abs(x)))


def _modnorm(x, nw, sc, sh):
    ms = jnp.mean(x * x, axis=-1, keepdims=True)
    return (x * lax.rsqrt(ms + EPS) * nw) * (1.0 + sc) + sh


def _rms_lanes(x):
    ms = jnp.mean(x * x, axis=-1, keepdims=True)
    return x * lax.rsqrt(ms + EPS)


def _pairswap(x):
    ax = x.ndim - 1
    n = x.shape[ax]
    lane = lax.broadcasted_iota(jnp.int32, x.shape, ax)
    nxt = pltpu.roll(x, n - 1, ax)
    prv = pltpu.roll(x, 1, ax)
    return jnp.where((lane & 1) == 0, nxt, prv)


def _split3(x):
    hi = x.astype(BF16)
    r1 = x - hi.astype(F32)
    mid = r1.astype(BF16)
    lo = (r1 - mid.astype(F32)).astype(BF16)
    return hi, mid, lo


def _gammas():
    return 1.0 - 2.0 ** (-5.0 - np.arange(HB, dtype=np.float64))


def _rot_tables(pos):
    theta = 1.0 / (10000.0 ** np.linspace(0.0, 1.0, DK_B // 2))
    ang = np.asarray(pos, np.float64)[:, None] * theta[None, :]
    cos = np.repeat(np.cos(ang), 2, axis=1)
    sin = np.repeat(np.sin(ang), 2, axis=1)
    sin[:, 0::2] *= -1.0
    return (np.tile(cos, (1, HB)).astype(np.float32), np.tile(sin, (1, HB)).astype(np.float32))


def _ret_tables():
    g = _gammas()
    L = WINDOW
    i = np.arange(L, dtype=np.float64)
    diff = i[:, None] - i[None, :]
    dm = np.where(diff >= 0, g[:, None, None] ** np.maximum(diff, 0.0), 0.0)
    qdec = np.repeat(g[None, :] ** (i[:, None] + 1.0), DK_B, axis=1)
    kdec = np.repeat(g[None, :] ** (L - 1.0 - i[:, None]), DK_B, axis=1)
    return dm.astype(np.float32), qdec.astype(np.float32), kdec.astype(np.float32)


def _t5_bucket_np(dist):
    max_exact = NUM_BUCKETS // 2
    n = np.maximum(dist, 0)
    nf = np.maximum(n, 1).astype(np.float32)
    large = max_exact + (np.log(nf / np.float32(max_exact)) / np.float32(math.log(MAX_DISTANCE / max_exact))
                         * np.float32(NUM_BUCKETS - max_exact)).astype(np.int32)
    large = np.minimum(large, NUM_BUCKETS - 1)
    return np.where(n < max_exact, n, large)


def _bias_tables(rel_table):
    gsz = HA_Q // HA_KV
    qi = np.arange(WINDOW)[None, :]
    kj = np.arange(WINDOW)[:, None]
    dist = np.where(kj > qi, qi + WINDOW - kj, qi - kj)
    onehot = _t5_bucket_np(dist)[..., None] == np.arange(NUM_BUCKETS)
    tab = rel_table.astype(F32)
    bias_kq = jnp.einsum('kqb,bh->hkq', jnp.asarray(onehot, F32), tab, precision=lax.Precision.HIGHEST)
    bias_t = bias_kq.reshape(HA_KV, gsz, WINDOW, WINDOW).transpose(0, 2, 1, 3).reshape(HA_KV, WINDOW, gsz * WINDOW)
    oh_s = _t5_bucket_np(WINDOW - 1 - np.arange(WINDOW))[:, None] == np.arange(NUM_BUCKETS)
    bias_s = jnp.einsum('jb,bh->hj', jnp.asarray(oh_s, F32), tab, precision=lax.Precision.HIGHEST)
    return bias_t, bias_s


def _ada_kernel(c_ref, w_ref, b_ref, o_ref):
    s = _silu(c_ref[...])
    o_ref[0] = _dot(s.astype(BF16), w_ref[0].astype(BF16)) + b_ref[0]


def _ada_call(c_all, ada_w, ada_b):
    n = c_all.shape[0]
    nb = 6
    return pl.pallas_call(
        _ada_kernel,
        grid=(DEPTH, nb),
        in_specs=[pl.BlockSpec((n, D_MODEL), lambda l, j: (0, 0)),
                  pl.BlockSpec((1, D_MODEL, D_MODEL), lambda l, j: (l, 0, j)),
                  pl.BlockSpec((1, 1, D_MODEL), lambda l, j: (l, 0, j))],
        out_specs=pl.BlockSpec((1, n, D_MODEL), lambda l, j: (l, 0, j)),
        out_shape=jax.ShapeDtypeStruct((DEPTH, n, 6 * D_MODEL), F32),
        compiler_params=_cparams(2),
        name="ada_mod",
    )(c_all, ada_w, ada_b.reshape(DEPTH, 1, 6 * D_MODEL))


def _pa_kernel(x_ref, mod_ref, n1_ref, wt_ref, wkv_ref, bias_ref, sink_ref, lowm_ref,
               mix_ref, ko_ref, vo_ref, zt_ref, kv_ref, kprev_ref, vtprev_ref, pen_ref, *, tt):
    t = pl.program_id(1)
    nchunk = tt // WINDOW
    kw = HA_KV * HD_A
    gsz = HA_Q // HA_KV

    @pl.when(t == 0)
    def _():
        kprev_ref[...] = jnp.zeros_like(kprev_ref)
        vtprev_ref[...] = jnp.zeros_like(vtprev_ref)
        pen_ref[...] = jnp.full(pen_ref.shape, NEG, F32)

    mod = mod_ref[0]
    h = _modnorm(x_ref[0], n1_ref[...], mod[:, D_MODEL:2 * D_MODEL], mod[:, 0:D_MODEL]).astype(BF16)
    zt = _dot_nt(wt_ref[...], h)
    for c in range(nchunk):
        zt_ref[c] = zt[:, WINDOW * c:WINDOW * (c + 1)]
    kv_ref[...] = _dot(h, wkv_ref[...])
    qw = gsz * WINDOW
    lower = (lax.broadcasted_iota(jnp.int32, (WINDOW, qw), 0)
             > (lax.broadcasted_iota(jnp.int32, (WINDOW, qw), 1) & (WINDOW - 1)))

    def chunk(c, carry):
        r0 = pl.multiple_of(c * WINDOW, WINDOW)
        rows = pl.ds(r0, WINDOW)
        kc = kv_ref[rows, 0:kw]
        vc = kv_ref[rows, kw:2 * kw]
        vt = vc.T
        kk = jnp.concatenate([kprev_ref[...], kc], axis=0).astype(BF16)
        vvt = jnp.concatenate([vtprev_ref[...], vt], axis=1).astype(BF16)
        qt = zt_ref[c, 0:D_MODEL, :].astype(BF16)
        pen = pen_ref[0:1, :]
        s_all = []
        for g in range(HA_KV):
            gs = slice(HD_A * g, HD_A * (g + 1))
            qcat = jnp.concatenate([qt[HD_A * (gsz * g + j):HD_A * (gsz * g + j + 1), :] for j in range(gsz)], axis=1)
            s_all.append(_dot(kk[:, gs], qcat))
        p_all = []
        for g in range(HA_KV):
            sg = jnp.where(lower, s_all[g][0:WINDOW, :] + pen, s_all[g][WINDOW:2 * WINDOW, :]) + bias_ref[g]
            sink = sink_ref[g]
            m = jnp.maximum(jnp.max(sg, axis=0, keepdims=True), sink)
            pw = jnp.exp(sg - m)
            den = jnp.sum(pw, axis=0, keepdims=True) + jnp.exp(sink - m)
            pb = pw.astype(BF16)
            p_prev = pb * lowm_ref[...]
            p_all.append((jnp.concatenate([p_prev, pb - p_prev], axis=0), 1.0 / den))
        pieces = []
        for g in range(HA_KV):
            gs = slice(HD_A * g, HD_A * (g + 1))
            p, rden = p_all[g]
            ot = _dot(vvt[gs, :], p) * rden
            pieces += [ot[:, WINDOW * j:WINDOW * (j + 1)] for j in range(gsz)]
        oat = jnp.concatenate(pieces, axis=0)
        mixt = _sigmoid(zt_ref[c, D_MODEL:2 * D_MODEL, :]) * oat
        mix_ref[0, rows, :] = mixt.T
        kprev_ref[...] = kc
        vtprev_ref[...] = vt
        pen_ref[...] = jnp.zeros_like(pen_ref)
        ko_ref[0] = kc
        vo_ref[0] = vc
        return carry

    lax.fori_loop(0, nchunk, chunk, 0)


def _pa_call(x, mod3, n1, wqgt, wkv, bias_t, sink_rows, tt):
    B, T, _ = x.shape
    kw = HA_KV * HD_A
    qw = (HA_Q // HA_KV) * WINDOW
    kern = functools.partial(_pa_kernel, tt=tt)
    lowm = (np.arange(WINDOW)[:, None] > (np.arange(qw)[None, :] % WINDOW)).astype(np.float32)
    return pl.pallas_call(
        kern,
        grid=(B, T // tt),
        in_specs=[pl.BlockSpec((1, tt, D_MODEL), lambda b, t: (b, t, 0)),
                  pl.BlockSpec((1, 1, 6 * D_MODEL), lambda b, t: (b, 0, 0)),
                  pl.BlockSpec((1, D_MODEL), lambda b, t: (0, 0)),
                  pl.BlockSpec((2 * D_MODEL, D_MODEL), lambda b, t: (0, 0)),
                  pl.BlockSpec((D_MODEL, 2 * kw), lambda b, t: (0, 0)),
                  pl.BlockSpec((HA_KV, WINDOW, qw), lambda b, t: (0, 0, 0)),
                  pl.BlockSpec((HA_KV, 1, qw), lambda b, t: (0, 0, 0)),
                  pl.BlockSpec((WINDOW, qw), lambda b, t: (0, 0))],
        out_specs=[pl.BlockSpec((1, tt, D_MODEL), lambda b, t: (b, t, 0)),
                   pl.BlockSpec((1, WINDOW, kw), lambda b, t: (b, 0, 0)),
                   pl.BlockSpec((1, WINDOW, kw), lambda b, t: (b, 0, 0))],
        out_shape=[jax.ShapeDtypeStruct((B, T, D_MODEL), F32),
                   jax.ShapeDtypeStruct((B, WINDOW, kw), F32),
                   jax.ShapeDtypeStruct((B, WINDOW, kw), F32)],
        scratch_shapes=[pltpu.VMEM((tt // WINDOW, 2 * D_MODEL, WINDOW), F32),
                        pltpu.VMEM((tt, 2 * kw), F32),
                        pltpu.VMEM((WINDOW, kw), F32),
                        pltpu.VMEM((kw, WINDOW), F32),
                        pltpu.VMEM((8, qw), F32)],
        compiler_params=_cparams(2),
        name="prompt_attn",
    )(x, mod3, n1, wqgt, wkv, bias_t, sink_rows, jnp.asarray(lowm, BF16))


def _pb_kernel(x_ref, mod_ref, n1_ref, wt_ref, cos_ref, sin_ref, qdec_ref, kdec_ref, dm_ref, mixa_ref,
               mix_ref, so_ref, zt_ref, s_ref, *, tt, glast):
    t = pl.program_id(1)
    nchunk = tt // WINDOW
    hw = HB * DK_B // 2
    hp = DK_B // 2

    @pl.when(t == 0)
    def _():
        s_ref[...] = jnp.zeros_like(s_ref)

    mod = mod_ref[0]
    h = _modnorm(x_ref[0], n1_ref[...], mod[:, D_MODEL:2 * D_MODEL], mod[:, 0:D_MODEL]).astype(BF16)
    for cc in range(nchunk // 2):
        z2 = _dot_nt(wt_ref[...], h[2 * WINDOW * cc:2 * WINDOW * (cc + 1), :])
        zt_ref[2 * cc] = z2[:, 0:WINDOW]
        zt_ref[2 * cc + 1] = z2[:, WINDOW:2 * WINDOW]

    def head_rows(pair, hh):
        return jnp.concatenate([pair[0][hp * hh:hp * (hh + 1), :], pair[1][hp * hh:hp * (hh + 1), :]], axis=0)

    def chunk(c, carry):
        r0 = pl.multiple_of(c * WINDOW, WINDOW)
        rows = pl.ds(r0, WINDOW)
        cos = cos_ref[c]
        sin = sin_ref[c]
        q1 = zt_ref[c, 0:hw, :]
        q2 = zt_ref[c, hw:2 * hw, :]
        k1 = zt_ref[c, 2 * hw:3 * hw, :]
        k2 = zt_ref[c, 3 * hw:4 * hw, :]
        rq = (q1 * cos - q2 * sin, q1 * sin + q2 * cos)
        rk = (k1 * cos - k2 * sin, k1 * sin + k2 * cos)
        qdec = qdec_ref[...]
        kdec = kdec_ref[...]
        qb = tuple(a.astype(BF16) for a in rq)
        kb = tuple(a.astype(BF16) for a in rk)
        qd = tuple((a * qdec).astype(BF16) for a in rq)
        kd = tuple((a * kdec).astype(BF16) for a in rk)
        vt = zt_ref[c, 4 * hw:4 * hw + HB * DV_B, :].astype(BF16)
        inner, cross, supd = [], [], []
        for hh in range(HB):
            s_old = s_ref[DK_B * hh:DK_B * (hh + 1), :]
            inner.append(_dot_tn(head_rows(qb, hh), head_rows(kb, hh)))
            cross.append(_dot_tn(s_old.astype(BF16), head_rows(qd, hh)))
            supd.append(glast[hh] * s_old + _dot_nt(head_rows(kd, hh), vt[DV_B * hh:DV_B * (hh + 1), :]))
        innd = [(inner[hh] * dm_ref[hh]).astype(BF16) for hh in range(HB)]
        outs = []
        for hh in range(HB):
            ot = _dot_nt(vt[DV_B * hh:DV_B * (hh + 1), :], innd[hh]) + cross[hh]
            s_ref[DK_B * hh:DK_B * (hh + 1), :] = supd[hh]
            ms = jnp.mean(ot * ot, axis=0, keepdims=True)
            outs.append(ot * lax.rsqrt(ms + EPS))
        obt = jnp.concatenate(outs, axis=0)
        bgt = zt_ref[c, 4 * hw + D_MODEL:4 * hw + 2 * D_MODEL, :]
        gbt = zt_ref[c, 4 * hw + 2 * D_MODEL:4 * hw + 3 * D_MODEL, :]
        mixt = _sigmoid(gbt) * (_silu(bgt) * obt)
        mix_ref[0, rows, :] = mixa_ref[0, rows, :] + mixt.T
        return carry

    lax.fori_loop(0, nchunk, chunk, 0)
    so_ref[0] = s_ref[...]


def _deinterleave_pairs(w):
    return w.reshape(w.shape[0], HB, DK_B // 2, 2).transpose(0, 3, 1, 2).reshape(w.shape[0], HB * DK_B)


def _rot_tables_t(T):
    theta = 1.0 / (10000.0 ** np.linspace(0.0, 1.0, DK_B // 2))
    ang = theta[:, None] * np.arange(T, dtype=np.float64)[None, :]

    def lay(a):
        a = np.tile(a, (HB, 1))
        return np.ascontiguousarray(a.reshape(a.shape[0], T // WINDOW, WINDOW).transpose(1, 0, 2)).astype(np.float32)

    return lay(np.cos(ang)), lay(np.sin(ang))


def _pb_call(x, mod3, n1, wbt, mixa, tt):
    B, T, _ = x.shape
    hw = HB * DK_B // 2
    cos, sin = _rot_tables_t(T)
    dm, qdec, kdec = _ret_tables()
    qdec_t = np.ascontiguousarray(qdec[:, ::2].T)
    kdec_t = np.ascontiguousarray(kdec[:, ::2].T)
    glast = tuple(float(v) for v in (_gammas() ** WINDOW))
    kern = functools.partial(_pb_kernel, tt=tt, glast=glast)
    full2 = lambda b, t: (0, 0)
    nct = tt // WINDOW
    return pl.pallas_call(
        kern,
        grid=(B, T // tt),
        in_specs=[pl.BlockSpec((1, tt, D_MODEL), lambda b, t: (b, t, 0)),
                  pl.BlockSpec((1, 1, 6 * D_MODEL), lambda b, t: (b, 0, 0)),
                  pl.BlockSpec((1, D_MODEL), full2),
                  pl.BlockSpec((SEG_B, D_MODEL), full2),
                  pl.BlockSpec((nct, hw, WINDOW), lambda b, t: (t, 0, 0)),
                  pl.BlockSpec((nct, hw, WINDOW), lambda b, t: (t, 0, 0)),
                  pl.BlockSpec((hw, WINDOW), full2),
                  pl.BlockSpec((hw, WINDOW), full2),
                  pl.BlockSpec((HB, WINDOW, WINDOW), lambda b, t: (0, 0, 0)),
                  pl.BlockSpec((1, tt, D_MODEL), lambda b, t: (b, t, 0))],
        out_specs=[pl.BlockSpec((1, tt, D_MODEL), lambda b, t: (b, t, 0)),
                   pl.BlockSpec((1, HB * DK_B, DV_B), lambda b, t: (b, 0, 0))],
        out_shape=[jax.ShapeDtypeStruct((B, T, D_MODEL), F32),
                   jax.ShapeDtypeStruct((B, HB * DK_B, DV_B), F32)],
        scratch_shapes=[pltpu.VMEM((nct, SEG_B, WINDOW), F32),
                        pltpu.VMEM((HB * DK_B, DV_B), F32)],
        compiler_params=_cparams(2),
        name="prompt_retention",
    )(x, mod3, n1, wbt, jnp.asarray(cos), jnp.asarray(sin), jnp.asarray(qdec_t), jnp.asarray(kdec_t),
      jnp.asarray(dm), mixa)


def _pc_kernel(x_ref, mod_ref, n1_ref, w_ref, cw_ref, cb_ref, dtb_ref, a_ref, dsk_ref, snw_ref, tri_ref,
               mixab_ref, wo_ref,
               xo_ref, ho_ref, co_ref,
               z_ref, xbuf_ref, xbc_ref, dt_ref, hst_ref, mixs_ref, *, tt):
    t = pl.program_id(1)

    @pl.when(t == 0)
    def _():
        xbuf_ref[0:8, :] = jnp.zeros((8, CONV_DIM), F32)
        hst_ref[...] = jnp.zeros_like(hst_ref)

    x = x_ref[0]
    mod = mod_ref[0]
    h = _modnorm(x, n1_ref[...], mod[:, D_MODEL:2 * D_MODEL], mod[:, 0:D_MODEL])
    z_ref[...] = _dot(h.astype(BF16), w_ref[...])

    xbuf_ref[8:8 + tt, :] = z_ref[:, 1024:2560]
    acc = cb_ref[...]
    for i in range(CONV_W):
        acc = acc + xbuf_ref[5 + i:5 + i + tt, :] * cw_ref[i:i + 1, :]
    xbc_ref[...] = _silu(acc)
    co_ref[0] = xbuf_ref[tt + 5:tt + 8, :]
    xbuf_ref[0:8, :] = xbuf_ref[tt:tt + 8, :]
    dt_ref[...] = _softplus(z_ref[:, C_DT:C_DT + 128] + dtb_ref[...])

    ii = lax.broadcasted_iota(jnp.int32, (WINDOW, WINDOW), 0)
    jj = lax.broadcasted_iota(jnp.int32, (WINDOW, WINDOW), 1)
    causal = ii >= jj
    hpg = HC // G_C

    def chunk(c, carry):
        r0 = pl.multiple_of(c * WINDOW, WINDOW)
        rows = pl.ds(r0, WINDOW)
        xc = xbc_ref[rows, 0:1024]
        bmat = xbc_ref[rows, 1024:1280]
        cmat = xbc_ref[rows, 1280:1536]
        dtc = dt_ref[rows, :]
        acum = jnp.dot(tri_ref[...], dtc * a_ref[...], precision=lax.Precision.HIGHEST,
                       preferred_element_type=F32)
        acum_t = acum.T
        dt_t = dtc.T
        x_t = xc.T
        xb = xc.astype(BF16)
        bb = bmat.astype(BF16)
        cb16 = cmat.astype(BF16)
        ys = []
        for g in range(G_C):
            ns = slice(N_C * g, N_C * (g + 1))
            cbg = _dot_nt(cb16[:, ns], bb[:, ns])
            for hh in range(hpg * g, hpg * (g + 1)):
                ps = slice(HD_C * hh, HD_C * (hh + 1))
                colb = jnp.broadcast_to(acum[:, hh:hh + 1], (WINDOW, WINDOW))
                row = acum_t[hh:hh + 1, :]
                dtrow = dt_t[hh:hh + 1, :]
                lmat = jnp.exp(jnp.where(causal, colb - row, NEG))
                m = cbg * lmat * dtrow
                hs = hst_ref[ps, :]
                cs = cmat[:, ns] * jnp.exp(colb)
                y = _dot(m.astype(BF16), xb[:, ps]) + _dot_nt(cs.astype(BF16), hs.astype(BF16))
                last = colb[WINDOW - 1:WINDOW, :]
                wrow = jnp.exp(last - row) * dtrow
                xw = (x_t[ps, :] * wrow).astype(BF16)
                hst_ref[ps, :] = jnp.exp(last) * hs + _dot(xw, bb[:, ns])
                ys.append(y)
        y = jnp.concatenate(ys, axis=1) + dsk_ref[...] * xc
        yc = y * _silu(z_ref[rows, 0:1024])
        gw = D_MODEL // G_C
        oc = jnp.concatenate([_rms_lanes(yc[:, gw * g:gw * (g + 1)]) for g in range(G_C)], axis=1) * snw_ref[...]
        gc = _sigmoid(z_ref[rows, C_GC:C_GC + 1024])
        mixs_ref[rows, :] = (mixab_ref[0, rows, :] + gc * oc).astype(BF16)
        return carry

    lax.fori_loop(0, tt // WINDOW, chunk, 0)
    ho_ref[0] = hst_ref[...]
    g1 = mod[:, 2 * D_MODEL:3 * D_MODEL]
    xo_ref[0] = x + g1 * _dot(mixs_ref[...], wo_ref[...])


def _pc_call(x, mod3, n1, wc, cw, cb, dtb_pad, a_pad, dsk_full, snw, mixab, wo, tt):
    B, T, _ = x.shape
    tri = jnp.asarray(np.tril(np.ones((WINDOW, WINDOW), np.float32)))
    kern = functools.partial(_pc_kernel, tt=tt)
    full2 = lambda b, t: (0, 0)
    return pl.pallas_call(
        kern,
        grid=(B, T // tt),
        in_specs=[pl.BlockSpec((1, tt, D_MODEL), lambda b, t: (b, t, 0)),
                  pl.BlockSpec((1, 1, 6 * D_MODEL), lambda b, t: (b, 0, 0)),
                  pl.BlockSpec((1, D_MODEL), full2),
                  pl.BlockSpec((D_MODEL, SEG_C), full2),
                  pl.BlockSpec((CONV_W, CONV_DIM), full2),
                  pl.BlockSpec((1, CONV_DIM), full2),
                  pl.BlockSpec((1, 128), full2),
                  pl.BlockSpec((1, 128), full2),
                  pl.BlockSpec((1, D_MODEL), full2),
                  pl.BlockSpec((1, D_MODEL), full2),
                  pl.BlockSpec((WINDOW, WINDOW), full2),
                  pl.BlockSpec((1, tt, D_MODEL), lambda b, t: (b, t, 0)),
                  pl.BlockSpec((D_MODEL, D_MODEL), full2)],
        out_specs=[pl.BlockSpec((1, tt, D_MODEL), lambda b, t: (b, t, 0)),
                   pl.BlockSpec((1, HC * HD_C, N_C), lambda b, t: (b, 0, 0)),
                   pl.BlockSpec((1, CONV_W - 1, CONV_DIM), lambda b, t: (b, 0, 0))],
        out_shape=[jax.ShapeDtypeStruct((B, T, D_MODEL), F32),
                   jax.ShapeDtypeStruct((B, HC * HD_C, N_C), F32),
                   jax.ShapeDtypeStruct((B, CONV_W - 1, CONV_DIM), F32)],
        scratch_shapes=[pltpu.VMEM((tt, SEG_C), F32),
                        pltpu.VMEM((tt + 8, CONV_DIM), F32),
                        pltpu.VMEM((tt, CONV_DIM), F32),
                        pltpu.VMEM((tt, 128), F32),
                        pltpu.VMEM((HC * HD_C, N_C), F32),
                        pltpu.VMEM((tt, D_MODEL), BF16)],
        compiler_params=_cparams(2),
        name="prompt_ssd_out",
    )(x, mod3, n1, wc, cw, cb, dtb_pad, a_pad, dsk_full, snw, tri, mixab, wo)


def _mlp_kernel(x_ref, sh_ref, sc_ref, g_ref, n2_ref, wu_ref, wd_ref, fn_ref, o_ref, h_ref, acc_ref,
                *, nf, final, per_row):
    f = pl.program_id(1)
    rd = (lambda r: r[...]) if per_row else (lambda r: r[0])

    @pl.when(f == 0)
    def _():
        h_ref[...] = _modnorm(x_ref[...], n2_ref[...], rd(sc_ref), rd(sh_ref)).astype(BF16)
        acc_ref[...] = jnp.zeros_like(acc_ref)

    u = _dot(h_ref[...], wu_ref[0])
    u = jnp.square(jnp.maximum(u, 0.0)).astype(BF16)
    acc_ref[...] += _dot(u, wd_ref[0])

    @pl.when(f == nf - 1)
    def _():
        y = x_ref[...] + rd(g_ref) * acc_ref[...]
        if final:
            y = _rms_lanes(y) * fn_ref[...]
        o_ref[...] = y


def _mlp_call(x2, mod, n2, wu, wd, fn, layer, tm, tf, rows_per_mod, final):
    M = x2.shape[0]
    nf = D_FF // tf
    per_row = rows_per_mod == 1
    if per_row:
        mspec = lambda j: pl.BlockSpec((tm, D_MODEL), lambda m, f: (m, j))
    else:
        mspec = lambda j: pl.BlockSpec((1, 1, D_MODEL), lambda m, f: ((m * tm) // rows_per_mod, 0, j))
    kern = functools.partial(_mlp_kernel, nf=nf, final=final, per_row=per_row)
    return pl.pallas_call(
        kern,
        grid=(M // tm, nf),
        in_specs=[pl.BlockSpec((tm, D_MODEL), lambda m, f: (m, 0)),
                  mspec(3), mspec(4), mspec(5),
                  pl.BlockSpec((1, D_MODEL), lambda m, f: (0, 0)),
                  pl.BlockSpec((1, D_MODEL, tf), lambda m, f: (layer, 0, f)),
                  pl.BlockSpec((1, tf, D_MODEL), lambda m, f: (layer, f, 0)),
                  pl.BlockSpec((1, D_MODEL), lambda m, f: (0, 0))],
        out_specs=pl.BlockSpec((tm, D_MODEL), lambda m, f: (m, 0)),
        out_shape=jax.ShapeDtypeStruct((M, D_MODEL), F32),
        scratch_shapes=[pltpu.VMEM((tm, D_MODEL), BF16),
                        pltpu.VMEM((tm, D_MODEL), F32)],
        compiler_params=_cparams(2),
        name="mlp",
    )(x2, mod, mod, mod, n2, wu, wd, fn)


def _sproj_kernel(x_ref, sh_ref, sc_ref, n1_ref, w_ref, o_ref):
    h = _modnorm(x_ref[...], n1_ref[...], sc_ref[...], sh_ref[...])
    o_ref[...] = _dot(h.astype(BF16), w_ref[...])


def _sproj_call(xs, mod_s, n1, w):
    n, width = xs.shape[0], w.shape[1]
    return pl.pallas_call(
        _sproj_kernel,
        grid=(1,),
        in_specs=[pl.BlockSpec((n, D_MODEL), lambda i: (0, 0)),
                  pl.BlockSpec((n, D_MODEL), lambda i: (0, 0)),
                  pl.BlockSpec((n, D_MODEL), lambda i: (0, 1)),
                  pl.BlockSpec((1, D_MODEL), lambda i: (0, 0)),
                  pl.BlockSpec((D_MODEL, width), lambda i: (0, 0))],
        out_specs=pl.BlockSpec((n, width), lambda i: (0, 0)),
        out_shape=jax.ShapeDtypeStruct((n, width), F32),
        compiler_params=_cparams(1),
        name="sample_proj",
    )(xs, mod_s, mod_s, n1, w)


def _carry_outputs(prev, n_in, nsteps):
    nslab = 1 if prev else DEPTH
    extra = [pl.BlockSpec(memory_space=pl.ANY) for _ in prev]
    alias = {n_in + k: k for k in range(len(prev))}
    row = lambda p, i: jnp.where(p == 0, i, nsteps - 1)
    return nslab, extra, alias, row


def _with_fill(body, state_outs):
    def kern(*refs):
        p = pl.program_id(0)

        @pl.when(p == 0)
        def _():
            body(*refs)

        @pl.when(p != 0)
        def _():
            for k in state_outs:
                refs[k][...] = jnp.zeros_like(refs[k])

    return kern


def _sa_kernel(q_ref, kv_ref, ck_ref, cv_ref, bias_ref, sink_ref, *rest):
    ko_ref, vo_ref, oa_ref = rest[-3:]
    nb = ROWS_PER_SAMPLE_STEP
    gsz = HA_Q // HA_KV
    rg = lax.broadcasted_iota(jnp.int32, (HA_Q, HD_A), 0) // gsz
    sink = sink_ref[...]
    kw = HA_KV * HD_A
    scores, vmats = [], []
    for i in range(nb):
        ko_ref[0, i, 0:WINDOW - 1, :] = ck_ref[0, i, 1:WINDOW, :]
        ko_ref[0, i, WINDOW - 1:WINDOW, :] = kv_ref[i:i + 1, 0:kw]
        vo_ref[0, i, 0:WINDOW - 1, :] = cv_ref[0, i, 1:WINDOW, :]
        vo_ref[0, i, WINDOW - 1:WINDOW, :] = kv_ref[i:i + 1, kw:2 * kw]
        kmat = ko_ref[0, i].astype(BF16)
        vmats.append(vo_ref[0, i].astype(BF16))
        q = q_ref[i] * (HD_A ** -0.5)
        qe = jnp.concatenate([jnp.where(rg == g, q, 0.0) for g in range(HA_KV)], axis=1).astype(BF16)
        scores.append(_dot_nt(qe, kmat))
    probs = []
    for i in range(nb):
        sc = scores[i] + bias_ref[...]
        m = jnp.maximum(jnp.max(sc, axis=-1, keepdims=True), sink)
        p = jnp.exp(sc - m)
        den = jnp.sum(p, axis=-1, keepdims=True) + jnp.exp(sink - m)
        probs.append((p.astype(BF16), den))
    outs = [_dot(probs[i][0], vmats[i]) for i in range(nb)]
    for i in range(nb):
        o = outs[i] / probs[i][1]
        o16 = jnp.zeros((HA_Q, HD_A), F32)
        for g in range(HA_KV):
            o16 = o16 + jnp.where(rg == g, o[:, HD_A * g:HD_A * (g + 1)], 0.0)
        oa_ref[i] = o16


def _sa_call(q3, kv, cache_k, cache_v, bias_s, sink_col, layer, prev):
    n = q3.shape[0]
    nb = ROWS_PER_SAMPLE_STEP
    kw = HA_KV * HD_A
    nslab, extra, alias, row = _carry_outputs(prev, 6, n // nb)
    cspec = pl.BlockSpec((1, nb, WINDOW, kw), lambda p, i: (layer, row(p, i), 0, 0))
    ospec = pl.BlockSpec((1, nb, WINDOW, kw), lambda p, i: ((layer + p) % DEPTH, i, 0, 0))
    return pl.pallas_call(
        _with_fill(_sa_kernel, (-3, -2)),
        grid=(nslab, n // nb),
        in_specs=[pl.BlockSpec((nb, HA_Q, HD_A), lambda p, i: (row(p, i), 0, 0)),
                  pl.BlockSpec((nb, 2 * kw), lambda p, i: (row(p, i), 0)),
                  cspec, cspec,
                  pl.BlockSpec((HA_Q, WINDOW), lambda p, i: (0, 0)),
                  pl.BlockSpec((HA_Q, 1), lambda p, i: (0, 0))] + extra,
        out_specs=[ospec, ospec, pl.BlockSpec((nb, HA_Q, HD_A), lambda p, i: (row(p, i), 0, 0))],
        out_shape=[jax.ShapeDtypeStruct((DEPTH, n, WINDOW, kw), F32),
                   jax.ShapeDtypeStruct((DEPTH, n, WINDOW, kw), F32),
                   jax.ShapeDtypeStruct((n, HA_Q, HD_A), F32)],
        input_output_aliases=alias,
        compiler_params=_cparams(2),
        name="sample_attn",
    )(q3, kv, cache_k, cache_v, bias_s, sink_col, *prev)


def _sr_kernel(zb_ref, cos_ref, sin_ref, gcol_ref, s_ref, *rest):
    so_ref, o_ref = rest[-2:]
    nb = ROWS_PER_SAMPLE_STEP
    wq = HB * DK_B
    cos = cos_ref[...]
    sin = sin_ref[...]
    qf = zb_ref[:, 0:wq]
    kf = zb_ref[:, wq:2 * wq]
    qr = qf * cos + _pairswap(qf) * sin
    kr = (kf * cos + _pairswap(kf) * sin) * (DK_B ** -0.5)
    v = zb_ref[:, 2 * wq:2 * wq + HB * DV_B]
    r8 = lax.broadcasted_iota(jnp.int32, (HB, wq), 0)
    hl = lax.broadcasted_iota(jnp.int32, (HB, wq), 1) // DK_B
    rv = lax.broadcasted_iota(jnp.int32, (HB, DV_B), 0)
    outers, q8s = [], []
    for i in range(nb):
        k8 = jnp.where(hl == r8, jnp.broadcast_to(kr[i:i + 1, :], (HB, wq)), 0.0).astype(BF16)
        q8s.append(jnp.where(hl == r8, jnp.broadcast_to(qr[i:i + 1, :], (HB, wq)), 0.0).astype(BF16))
        v8 = jnp.zeros((HB, DV_B), F32)
        for r in range(HB):
            v8 = jnp.where(rv == r, jnp.broadcast_to(v[i:i + 1, DV_B * r:DV_B * (r + 1)], (HB, DV_B)), v8)
        outers.append(_dot_tn(k8, v8.astype(BF16)))
    s_news = []
    for i in range(nb):
        s_new = gcol_ref[...] * s_ref[0, i] + outers[i]
        so_ref[0, i] = s_new
        s_news.append(s_new.astype(BF16))
    for i in range(nb):
        o_ref[i] = _dot(q8s[i], s_news[i])


def _sr_call(zb, state, layer, prev):
    n = zb.shape[0]
    nb = ROWS_PER_SAMPLE_STEP
    wq = HB * DK_B
    cos, sin = _rot_tables(np.array([PAST_LEN]))
    gcol = np.repeat(_gammas(), DK_B)[:, None] * np.ones((1, DV_B))
    nslab, extra, alias, row = _carry_outputs(prev, 5, n // nb)
    full = lambda p, i: (0, 0)
    return pl.pallas_call(
        _with_fill(_sr_kernel, (-2,)),
        grid=(nslab, n // nb),
        in_specs=[pl.BlockSpec((nb, SEG_B), lambda p, i: (row(p, i), 0)),
                  pl.BlockSpec((1, wq), full),
                  pl.BlockSpec((1, wq), full),
                  pl.BlockSpec((wq, DV_B), full),
                  pl.BlockSpec((1, nb, wq, DV_B), lambda p, i: (layer, row(p, i), 0, 0))] + extra,
        out_specs=[pl.BlockSpec((1, nb, wq, DV_B), lambda p, i: ((layer + p) % DEPTH, i, 0, 0)),
                   pl.BlockSpec((nb, HB, DV_B), lambda p, i: (row(p, i), 0, 0))],
        out_shape=[jax.ShapeDtypeStruct((DEPTH, n, wq, DV_B), F32),
                   jax.ShapeDtypeStruct((n, HB, DV_B), F32)],
        input_output_aliases=alias,
        compiler_params=_cparams(2),
        name="sample_retention",
    )(zb, jnp.asarray(cos), jnp.asarray(sin), jnp.asarray(gcol.astype(np.float32)), state, *prev)


def _ss_kernel(zc_ref, cs_ref, cw_ref, cb_ref, dtb_ref, a_ref, e3_ref, h_ref, *rest):
    ho_ref, cso_ref, y_ref, xc_ref = rest[-4:]
    nb = ROWS_PER_SAMPLE_STEP
    cx = zc_ref[:, 1024:2560]
    taps = [cs_ref[0, :, CONV_DIM * i:CONV_DIM * (i + 1)] for i in range(CONV_W - 1)] + [cx]
    acc = cb_ref[...]
    for i in range(CONV_W):
        acc = acc + taps[i] * cw_ref[i:i + 1, :]
    xbc = _silu(acc)
    cso_ref[...] = jnp.concatenate(taps[1:], axis=1)
    xc = xbc[:, 0:1024]
    bmat = xbc[:, 1024:1280]
    cmat = xbc[:, 1280:1536]
    dt = _softplus(zc_ref[:, C_DT:C_DT + 128] + dtb_ref[...])
    da = jnp.exp(dt * a_ref[...])
    dt_e = _dot(jnp.concatenate(_split3(dt), axis=1), e3_ref[...])
    da_e = _dot(jnp.concatenate(_split3(da), axis=1), e3_ref[...])
    dtx = dt_e * xc
    gw = (HC // G_C) * HD_C
    r8 = lax.broadcasted_iota(jnp.int32, (nb, gw), 0)
    rn = lax.broadcasted_iota(jnp.int32, (nb, N_C), 0)
    ones8 = jnp.ones((nb, N_C), BF16)
    prods = {}
    for g in range(G_C):
        ws = slice(gw * g, gw * (g + 1))
        bg16 = bmat[:, N_C * g:N_C * (g + 1)].astype(BF16)
        for i in range(nb):
            x8 = jnp.where(r8 == i, dtx[:, ws], 0.0).astype(BF16)
            outer = _dot_tn(x8, bg16)
            hi, mid, lo = (p.astype(F32) for p in _split3(jnp.broadcast_to(da_e[i:i + 1, ws], (nb, gw))))
            l3 = jnp.where(r8 == 0, hi, jnp.where(r8 == 1, mid, jnp.where(r8 == 2, lo, 0.0)))
            prods[g, i] = (outer, _dot_tn(l3.astype(BF16), ones8))
    h16 = {}
    for g in range(G_C):
        ws = slice(gw * g, gw * (g + 1))
        for i in range(nb):
            outer, dacol = prods[g, i]
            h_new = dacol * h_ref[0, i, ws, :] + outer
            ho_ref[0, i, ws, :] = h_new
            h16[g, i] = h_new.astype(BF16)
    ycols = []
    for g in range(G_C):
        ns = slice(N_C * g, N_C * (g + 1))
        yacc = jnp.zeros((nb, gw), F32)
        for i in range(nb):
            c8 = jnp.where(rn == i, cmat[:, ns], 0.0).astype(BF16)
            yacc = yacc + _dot_nt(c8, h16[g, i])
        ycols.append(yacc)
    y_ref[...] = jnp.concatenate(ycols, axis=1)
    xc_ref[...] = xc


def _ss_call(zc, conv_state, hstate, cw, cb, dtb_pad, a_pad, layer, prev):
    n = zc.shape[0]
    nb = ROWS_PER_SAMPLE_STEP
    e = np.zeros((128, D_MODEL), np.float32)
    for hh in range(HC):
        e[hh, HD_C * hh:HD_C * (hh + 1)] = 1.0
    e3 = jnp.asarray(np.concatenate([e, e, e], axis=0), dtype=BF16)
    cwid = (CONV_W - 1) * CONV_DIM
    full = lambda p, i: (0, 0)
    nslab, extra, alias, row = _carry_outputs(prev, 8, n // nb)
    return pl.pallas_call(
        _with_fill(_ss_kernel, (-4,)),
        grid=(nslab, n // nb),
        in_specs=[pl.BlockSpec((nb, SEG_C), lambda p, i: (row(p, i), 0)),
                  pl.BlockSpec((1, nb, cwid), lambda p, i: (layer, row(p, i), 0)),
                  pl.BlockSpec((CONV_W, CONV_DIM), full),
                  pl.BlockSpec((1, CONV_DIM), full),
                  pl.BlockSpec((1, 128), full),
                  pl.BlockSpec((1, 128), full),
                  pl.BlockSpec((3 * 128, D_MODEL), full),
                  pl.BlockSpec((1, nb, HC * HD_C, N_C), lambda p, i: (layer, row(p, i), 0, 0))] + extra,
        out_specs=[pl.BlockSpec((1, nb, HC * HD_C, N_C), lambda p, i: ((layer + p) % DEPTH, i, 0, 0)),
                   pl.BlockSpec((nb, cwid), lambda p, i: (row(p, i), 0)),
                   pl.BlockSpec((nb, D_MODEL), lambda p, i: (row(p, i), 0)),
                   pl.BlockSpec((nb, D_MODEL), lambda p, i: (row(p, i), 0))],
        out_shape=[jax.ShapeDtypeStruct((DEPTH, n, HC * HD_C, N_C), F32),
                   jax.ShapeDtypeStruct((n, cwid), F32),
                   jax.ShapeDtypeStruct((n, D_MODEL), F32),
                   jax.ShapeDtypeStruct((n, D_MODEL), F32)],
        input_output_aliases=alias,
        compiler_params=_cparams(2),
        name="sample_ssd",
    )(zc, conv_state, cw, cb, dtb_pad, a_pad, e3, hstate, *prev)


def _sm_kernel(x_ref, g1_ref, oa_ref, oret_ref, y_ref, xc_ref, za_ref, zb_ref, zc_ref,
               dsk_ref, snw_ref, wo_ref, o_ref):
    ob = jnp.concatenate([_rms_lanes(oret_ref[:, DV_B * hh:DV_B * (hh + 1)]) for hh in range(HB)], axis=1)
    ob = _silu(zb_ref[:, 2048:3072]) * ob
    yc = (y_ref[...] + dsk_ref[...] * xc_ref[...]) * _silu(zc_ref[:, 0:1024])
    gw = D_MODEL // G_C
    oc = jnp.concatenate([_rms_lanes(yc[:, gw * g:gw * (g + 1)]) for g in range(G_C)], axis=1) * snw_ref[...]
    mix = (_sigmoid(za_ref[:, 1536:2560]) * oa_ref[...] + _sigmoid(zb_ref[:, 3072:4096]) * ob
           + _sigmoid(zc_ref[:, C_GC:C_GC + 1024]) * oc)
    o_ref[...] = x_ref[...] + g1_ref[...] * _dot(mix.astype(BF16), wo_ref[...])


def _sm_call(xs, mod_s, oa, oret, y, xc, za, zb, zc, dsk_full, snw, wo):
    n = xs.shape[0]
    full = lambda i: (0, 0)
    row = lambda w: pl.BlockSpec((n, w), full)
    return pl.pallas_call(
        _sm_kernel,
        grid=(1,),
        in_specs=[row(D_MODEL),
                  pl.BlockSpec((n, D_MODEL), lambda i: (0, 2)),
                  row(D_MODEL), row(D_MODEL), row(D_MODEL), row(D_MODEL),
                  row(SEG_A), row(SEG_B), row(SEG_C),
                  pl.BlockSpec((1, D_MODEL), full),
                  pl.BlockSpec((1, D_MODEL), full),
                  pl.BlockSpec((D_MODEL, D_MODEL), full)],
        out_specs=row(D_MODEL),
        out_shape=jax.ShapeDtypeStruct((n, D_MODEL), F32),
        compiler_params=_cparams(1),
        name="sample_merge_out",
    )(xs, mod_s, oa, oret, y, xc, za, zb, zc, dsk_full, snw, wo)


PROMPT_TILE = 512
MLP_ROWS = 1024
MLP_FF = 1024


def _prep_w_in(w):
    wa = jnp.concatenate([w[:, O_AQ:O_BQ], w[:, O_GTS:O_GTS + 1024]], axis=1)
    wb = jnp.concatenate([w[:, O_BQ:O_CZ], w[:, O_GTS + 1024:O_GTS + 2048]], axis=1)
    wc = jnp.concatenate([w[:, O_CZ:O_CDT], jnp.pad(w[:, O_CDT:O_GTS], ((0, 0), (0, 128 - HC))),
                          w[:, O_GTS + 2048:O_GTS + 3072]], axis=1)
    return wa.astype(BF16), wb.astype(BF16), wc.astype(BF16)


def _forward(x_prompt, x_sample, cache_win_k, cache_win_v, state_ret, state_ssm, state_conv,
             c_prompt, c_sample, rel_bias_table, attn_sinks, norm1_w, norm2_w, ada_w, ada_b,
             w_in, conv_w, conv_b, dt_bias, A_log, D_skip, ssm_norm_w, w_out, w_up, w_down,
             final_norm_w, *, prompt_tile, mlp_rows, mlp_ff):
    B, T, _ = x_prompt.shape
    DB = x_sample.shape[0]
    kw = HA_KV * HD_A

    mod_all = _ada_call(jnp.concatenate([c_prompt, c_sample], axis=0), ada_w, ada_b)

    bias_t, bias_s = _bias_tables(rel_bias_table)

    wu16 = w_up.astype(BF16)
    wd16 = w_down.astype(BF16)
    wo16 = w_out.astype(BF16)
    fn = final_norm_w.reshape(1, D_MODEL)
    ck = cache_win_k.reshape(DEPTH, DB, WINDOW, kw)
    cv = cache_win_v.reshape(DEPTH, DB, WINDOW, kw)
    sret = state_ret.reshape(DEPTH, DB, HB * DK_B, DV_B)
    sssm = state_ssm.reshape(DEPTH, DB, HC * HD_C, N_C)
    sconv = state_conv.reshape(DEPTH, DB, (CONV_W - 1) * CONV_DIM)

    xp = x_prompt
    xs = x_sample.reshape(DB, D_MODEL)
    outs_p = [[] for _ in range(5)]
    conv_s = []
    win_kv, ret_all, ssm_all = (), (), ()
    for l in range(DEPTH):
        wl = w_in[l]
        wa, wb, wc = _prep_w_in(wl)
        n1 = norm1_w[l].reshape(1, D_MODEL)
        n2 = norm2_w[l].reshape(1, D_MODEL)
        cw = conv_w[l]
        cb = conv_b[l].reshape(1, CONV_DIM)
        dtb_pad = jnp.pad(dt_bias[l], (0, 128 - HC)).reshape(1, 128)
        a_pad = jnp.pad(-jnp.exp(A_log[l].astype(F32)), (0, 128 - HC)).reshape(1, 128)
        dsk_full = jnp.repeat(D_skip[l], HD_C).reshape(1, D_MODEL)
        snw = ssm_norm_w[l].reshape(1, D_MODEL)
        final = l == DEPTH - 1
        mod_p = mod_all[l, :B].reshape(B, 1, 6 * D_MODEL)
        mod_s = mod_all[l, B:]

        wqgt = jnp.concatenate([wl[:, O_AQ:O_AK] * (HD_A ** -0.5), wl[:, O_GTS:O_GTS + D_MODEL]], axis=1).T.astype(BF16)
        wkv = wl[:, O_AK:O_BQ].astype(BF16)
        sink_rows = jnp.repeat(attn_sinks[l], WINDOW).reshape(HA_KV, 1, (HA_Q // HA_KV) * WINDOW)
        mixa, kbuf, vbuf = _pa_call(xp, mod_p, n1, wqgt, wkv, bias_t, sink_rows, prompt_tile)
        wbt = jnp.concatenate([_deinterleave_pairs(wl[:, O_BQ:O_BK]),
                               _deinterleave_pairs(wl[:, O_BK:O_BV]) * (DK_B ** -0.5), wl[:, O_BV:O_CZ],
                               wl[:, O_GTS + D_MODEL:O_GTS + 2 * D_MODEL]], axis=1).T.astype(BF16)
        mixab, s_perm = _pb_call(xp, mod_p, n1, wbt, mixa, prompt_tile)
        s_ret = s_perm.reshape(B, HB, 2, DK_B // 2, DV_B).transpose(0, 1, 3, 2, 4)
        x1, h_ssm, conv_new = _pc_call(xp, mod_p, n1, wc, cw, cb, dtb_pad, a_pad, dsk_full, snw,
                                       mixab, wo16[l], prompt_tile)
        xp = _mlp_call(x1.reshape(B * T, D_MODEL), mod_p, n2, wu16, wd16, fn, l,
                       mlp_rows, mlp_ff, T, final).reshape(B, T, D_MODEL)
        for lst, v in zip(outs_p, (kbuf.reshape(B, WINDOW, HA_KV, HD_A), vbuf.reshape(B, WINDOW, HA_KV, HD_A),
                                   s_ret.reshape(B, HB, DK_B, DV_B), h_ssm.reshape(B, HC, HD_C, N_C), conv_new)):
            lst.append(v)

        za = _sproj_call(xs, mod_s, n1, wa)
        zb = _sproj_call(xs, mod_s, n1, wb)
        zc = _sproj_call(xs, mod_s, n1, wc)
        q3 = za[:, 0:HA_Q * HD_A].reshape(DB, HA_Q, HD_A)
        kv = za[:, HA_Q * HD_A:HA_Q * HD_A + 2 * kw]
        ck_new, cv_new, oa3 = _sa_call(q3, kv, ck, cv, bias_s, attn_sinks[l].reshape(HA_Q, 1), l, win_kv)
        win_kv = (ck_new, cv_new)
        s_new, o3 = _sr_call(zb, sret, l, ret_all)
        ret_all = (s_new,)
        h_new, cs_new, y_s, xc_s = _ss_call(zc, sconv, sssm, cw, cb, dtb_pad, a_pad, l, ssm_all)
        ssm_all = (h_new,)
        xs1 = _sm_call(xs, mod_s, oa3.reshape(DB, D_MODEL), o3.reshape(DB, D_MODEL), y_s, xc_s,
                       za, zb, zc, dsk_full, snw, wo16[l])
        xs = _mlp_call(xs1, mod_s, n2, wu16, wd16, fn, l, DB, mlp_ff, 1, final)
        conv_s.append(cs_new.reshape(DB, CONV_W - 1, CONV_DIM))

    stk = lambda lst: jnp.stack(lst, axis=0)
    return (xp, xs.reshape(DB, 1, D_MODEL),
            stk(outs_p[0]), stk(outs_p[1]), stk(outs_p[2]), stk(outs_p[3]), stk(outs_p[4]),
            win_kv[0].reshape(DEPTH, DB, WINDOW, HA_KV, HD_A), win_kv[1].reshape(DEPTH, DB, WINDOW, HA_KV, HD_A),
            ret_all[0].reshape(DEPTH, DB, HB, DK_B, DV_B), ssm_all[0].reshape(DEPTH, DB, HC, HD_C, N_C),
            stk(conv_s))


def kernel(x_prompt, x_sample, cache_win_k, cache_win_v, state_ret, state_ssm, state_conv, c_prompt, c_sample,
           rel_bias_table, attn_sinks, norm1_w, norm2_w, ada_w, ada_b, w_in, conv_w, conv_b, dt_bias, A_log,
           D_skip, ssm_norm_w, w_out, w_up, w_down, final_norm_w):
    return _forward(x_prompt, x_sample, cache_win_k, cache_win_v, state_ret, state_ssm, state_conv,
                    c_prompt, c_sample, rel_bias_table, attn_sinks, norm1_w, norm2_w, ada_w, ada_b,
                    w_in, conv_w, conv_b, dt_bias, A_log, D_skip, ssm_norm_w, w_out, w_up, w_down,
                    final_norm_w, prompt_tile=PROMPT_TILE, mlp_rows=MLP_ROWS, mlp_ff=MLP_FF)
```

```python
import functools
import math

import numpy as np
import jax
import jax.numpy as jnp
from jax import lax
from jax.experimental import pallas as pl
from jax.experimental.pallas import tpu as pltpu

F32 = jnp.float32
BF16 = jnp.bfloat16

D_MODEL = 1024
DEPTH = 2
PAST_LEN = 16384
WINDOW = 128
HA_Q = 16
HA_KV = 4
HD_A = 64
NUM_BUCKETS = 32
MAX_DISTANCE = WINDOW
HB = 8
DK_B = 64
DV_B = 128
HC = 16
HD_C = 64
N_C = 128
G_C = 2
CONV_W = 4
CONV_DIM = D_MODEL + 2 * G_C * N_C
D_FF = 4 * D_MODEL
EPS = 1e-6
NEG = -1e30

O_AQ, O_AK, O_AV = 0, 1024, 1280
O_BQ, O_BK, O_BV, O_BG = 1536, 2048, 2560, 3584
O_CZ, O_CXBC, O_CDT, O_GTS = 4608, 5632, 7168, 7184
SEG_A = 2560
SEG_B = 4096
SEG_C = 3712
C_DT = 2560
C_GC = 2688

VMEM_LIMIT_V7X = 56 * 1024 * 1024
ROWS_PER_SAMPLE_STEP = 8


def _cparams(n_axes):
    return pltpu.CompilerParams(dimension_semantics=("arbitrary",) * n_axes,
                                vmem_limit_bytes=VMEM_LIMIT_V7X)


def _dot(a, b):
    return jnp.dot(a, b, preferred_element_type=F32)


def _dot_nt(a, b):
    return lax.dot_general(a, b, (((1,), (1,)), ((), ())), preferred_element_type=F32)


def _dot_tn(a, b):
    return lax.dot_general(a, b, (((0,), (0,)), ((), ())), preferred_element_type=F32)


def _sigmoid(x):
    return 0.5 * (jnp.tanh(0.5 * x) + 1.0)


def _silu(x):
    return x * _sigmoid(x)


def _softplus(x):
    return jnp.maximum(x, 0.0) + jnp.log1p(jnp.exp(-jnp.abs(x)))


def _modnorm(x, nw, sc, sh):
    ms = jnp.mean(x * x, axis=-1, keepdims=True)
    return (x * lax.rsqrt(ms + EPS) * nw) * (1.0 + sc) + sh


def _rms_lanes(x):
    ms = jnp.mean(x * x, axis=-1, keepdims=True)
    return x * lax.rsqrt(ms + EPS)


def _pairswap(x):
    ax = x.ndim - 1
    n = x.shape[ax]
    lane = lax.broadcasted_iota(jnp.int32, x.shape, ax)
    nxt = pltpu.roll(x, n - 1, ax)
    prv = pltpu.roll(x, 1, ax)
    return jnp.where((lane & 1) == 0, nxt, prv)


def _split3(x):
    hi = x.astype(BF16)
    r1 = x - hi.astype(F32)
    mid = r1.astype(BF16)
    lo = (r1 - mid.astype(F32)).astype(BF16)
    return hi, mid, lo


def _gammas():
    return 1.0 - 2.0 ** (-5.0 - np.arange(HB, dtype=np.float64))


def _rot_tables(pos):
    theta = 1.0 / (10000.0 ** np.linspace(0.0, 1.0, DK_B // 2))
    ang = np.asarray(pos, np.float64)[:, None] * theta[None, :]
    cos = np.repeat(np.cos(ang), 2, axis=1)
    sin = np.repeat(np.sin(ang), 2, axis=1)
    sin[:, 0::2] *= -1.0
    return (np.tile(cos, (1, HB)).astype(np.float32), np.tile(sin, (1, HB)).astype(np.float32))


def _ret_tables():
    g = _gammas()
    L = WINDOW
    i = np.arange(L, dtype=np.float64)
    diff = i[:, None] - i[None, :]
    dm = np.where(diff >= 0, g[:, None, None] ** np.maximum(diff, 0.0), 0.0)
    qdec = np.repeat(g[None, :] ** (i[:, None] + 1.0), DK_B, axis=1)
    kdec = np.repeat(g[None, :] ** (L - 1.0 - i[:, None]), DK_B, axis=1)
    return dm.astype(np.float32), qdec.astype(np.float32), kdec.astype(np.float32)


def _t5_bucket_np(dist):
    max_exact = NUM_BUCKETS // 2
    n = np.maximum(dist, 0)
    nf = np.maximum(n, 1).astype(np.float32)
    large = max_exact + (np.log(nf / np.float32(max_exact)) / np.float32(math.log(MAX_DISTANCE / max_exact))
                         * np.float32(NUM_BUCKETS - max_exact)).astype(np.int32)
    large = np.minimum(large, NUM_BUCKETS - 1)
    return np.where(n < max_exact, n, large)


def _bias_tables(rel_table):
    gsz = HA_Q // HA_KV
    qi = np.arange(WINDOW)[None, :]
    kj = np.arange(WINDOW)[:, None]
    dist = np.where(kj > qi, qi + WINDOW - kj, qi - kj)
    onehot = _t5_bucket_np(dist)[..., None] == np.arange(NUM_BUCKETS)
    tab = rel_table.astype(F32)
    bias_kq = jnp.einsum('kqb,bh->hkq', jnp.asarray(onehot, F32), tab, precision=lax.Precision.HIGHEST)
    bias_t = bias_kq.reshape(HA_KV, gsz, WINDOW, WINDOW).transpose(0, 2, 1, 3).reshape(HA_KV, WINDOW, gsz * WINDOW)
    oh_s = _t5_bucket_np(WINDOW - 1 - np.arange(WINDOW))[:, None] == np.arange(NUM_BUCKETS)
    bias_s = jnp.einsum('jb,bh->hj', jnp.asarray(oh_s, F32), tab, precision=lax.Precision.HIGHEST)
    return bias_t, bias_s


def _ada_kernel(c_ref, w_ref, b_ref, o_ref):
    s = _silu(c_ref[...])
    o_ref[0] = _dot(s.astype(BF16), w_ref[0].astype(BF16)) + b_ref[0]


def _ada_call(c_all, ada_w, ada_b):
    n = c_all.shape[0]
    nb = 6
    return pl.pallas_call(
        _ada_kernel,
        grid=(DEPTH, nb),
        in_specs=[pl.BlockSpec((n, D_MODEL), lambda l, j: (0, 0)),
                  pl.BlockSpec((1, D_MODEL, D_MODEL), lambda l, j: (l, 0, j)),
                  pl.BlockSpec((1, 1, D_MODEL), lambda l, j: (l, 0, j))],
        out_specs=pl.BlockSpec((1, n, D_MODEL), lambda l, j: (l, 0, j)),
        out_shape=jax.ShapeDtypeStruct((DEPTH, n, 6 * D_MODEL), F32),
        compiler_params=_cparams(2),
        name="ada_mod",
    )(c_all, ada_w, ada_b.reshape(DEPTH, 1, 6 * D_MODEL))


def _pa_kernel(x_ref, mod_ref, n1_ref, wt_ref, wkv_ref, bias_ref, sink_ref, lowm_ref,
               mix_ref, ko_ref, vo_ref, zt_ref, kv_ref, kprev_ref, vtprev_ref, pen_ref, *, tt):
    t = pl.program_id(1)
    nchunk = tt // WINDOW
    kw = HA_KV * HD_A
    gsz = HA_Q // HA_KV

    @pl.when(t == 0)
    def _():
        kprev_ref[...] = jnp.zeros_like(kprev_ref)
        vtprev_ref[...] = jnp.zeros_like(vtprev_ref)
        pen_ref[...] = jnp.full(pen_ref.shape, NEG, F32)

    mod = mod_ref[0]
    h = _modnorm(x_ref[0], n1_ref[...], mod[:, D_MODEL:2 * D_MODEL], mod[:, 0:D_MODEL]).astype(BF16)
    zt = _dot_nt(wt_ref[...], h)
    for c in range(nchunk):
        zt_ref[c] = zt[:, WINDOW * c:WINDOW * (c + 1)]
    kv_ref[...] = _dot(h, wkv_ref[...])
    qw = gsz * WINDOW
    lower = (lax.broadcasted_iota(jnp.int32, (WINDOW, qw), 0)
             > (lax.broadcasted_iota(jnp.int32, (WINDOW, qw), 1) & (WINDOW - 1)))

    def chunk(c, carry):
        r0 = pl.multiple_of(c * WINDOW, WINDOW)
        rows = pl.ds(r0, WINDOW)
        kc = kv_ref[rows, 0:kw]
        vc = kv_ref[rows, kw:2 * kw]
        vt = vc.T
        kk = jnp.concatenate([kprev_ref[...], kc], axis=0).astype(BF16)
        vvt = jnp.concatenate([vtprev_ref[...], vt], axis=1).astype(BF16)
        qt = zt_ref[c, 0:D_MODEL, :].astype(BF16)
        pen = pen_ref[0:1, :]
        s_all = []
        for g in range(HA_KV):
            gs = slice(HD_A * g, HD_A * (g + 1))
            qcat = jnp.concatenate([qt[HD_A * (gsz * g + j):HD_A * (gsz * g + j + 1), :] for j in range(gsz)], axis=1)
            s_all.append(_dot(kk[:, gs], qcat))
        p_all = []
        for g in range(HA_KV):
            sg = jnp.where(lower, s_all[g][0:WINDOW, :] + pen, s_all[g][WINDOW:2 * WINDOW, :]) + bias_ref[g]
            sink = sink_ref[g]
            m = jnp.maximum(jnp.max(sg, axis=0, keepdims=True), sink)
            pw = jnp.exp(sg - m)
            den = jnp.sum(pw, axis=0, keepdims=True) + jnp.exp(sink - m)
            pb = pw.astype(BF16)
            p_prev = pb * lowm_ref[...]
            p_all.append((jnp.concatenate([p_prev, pb - p_prev], axis=0), 1.0 / den))
        pieces = []
        for g in range(HA_KV):
            gs = slice(HD_A * g, HD_A * (g + 1))
            p, rden = p_all[g]
            ot = _dot(vvt[gs, :], p) * rden
            pieces += [ot[:, WINDOW * j:WINDOW * (j + 1)] for j in range(gsz)]
        oat = jnp.concatenate(pieces, axis=0)
        mixt = _sigmoid(zt_ref[c, D_MODEL:2 * D_MODEL, :]) * oat
        mix_ref[0, rows, :] = mixt.T
        kprev_ref[...] = kc
        vtprev_ref[...] = vt
        pen_ref[...] = jnp.zeros_like(pen_ref)
        ko_ref[0] = kc
        vo_ref[0] = vc
        return carry

    lax.fori_loop(0, nchunk, chunk, 0)


def _pa_call(x, mod3, n1, wqgt, wkv, bias_t, sink_rows, tt):
    B, T, _ = x.shape
    kw = HA_KV * HD_A
    qw = (HA_Q // HA_KV) * WINDOW
    kern = functools.partial(_pa_kernel, tt=tt)
    lowm = (np.arange(WINDOW)[:, None] > (np.arange(qw)[None, :] % WINDOW)).astype(np.float32)
    return pl.pallas_call(
        kern,
        grid=(B, T // tt),
        in_specs=[pl.BlockSpec((1, tt, D_MODEL), lambda b, t: (b, t, 0)),
                  pl.BlockSpec((1, 1, 6 * D_MODEL), lambda b, t: (b, 0, 0)),
                  pl.BlockSpec((1, D_MODEL), lambda b, t: (0, 0)),
                  pl.BlockSpec((2 * D_MODEL, D_MODEL), lambda b, t: (0, 0)),
                  pl.BlockSpec((D_MODEL, 2 * kw), lambda b, t: (0, 0)),
                  pl.BlockSpec((HA_KV, WINDOW, qw), lambda b, t: (0, 0, 0)),
                  pl.BlockSpec((HA_KV, 1, qw), lambda b, t: (0, 0, 0)),
                  pl.BlockSpec((WINDOW, qw), lambda b, t: (0, 0))],
        out_specs=[pl.BlockSpec((1, tt, D_MODEL), lambda b, t: (b, t, 0)),
                   pl.BlockSpec((1, WINDOW, kw), lambda b, t: (b, 0, 0)),
                   pl.BlockSpec((1, WINDOW, kw), lambda b, t: (b, 0, 0))],
        out_shape=[jax.ShapeDtypeStruct((B, T, D_MODEL), F32),
                   jax.ShapeDtypeStruct((B, WINDOW, kw), F32),
                   jax.ShapeDtypeStruct((B, WINDOW, kw), F32)],
        scratch_shapes=[pltpu.VMEM((tt // WINDOW, 2 * D_MODEL, WINDOW), F32),
                        pltpu.VMEM((tt, 2 * kw), F32),
                        pltpu.VMEM((WINDOW, kw), F32),
                        pltpu.VMEM((kw, WINDOW), F32),
                        pltpu.VMEM((8, qw), F32)],
        compiler_params=_cparams(2),
        name="prompt_attn",
    )(x, mod3, n1, wqgt, wkv, bias_t, sink_rows, jnp.asarray(lowm, BF16))


def _pb_kernel(x_ref, mod_ref, n1_ref, wt_ref, cos_ref, sin_ref, qdec_ref, kdec_ref, dm_ref, mixa_ref,
               mix_ref, so_ref, zt_ref, s_ref, *, tt, glast):
    t = pl.program_id(1)
    nchunk = tt // WINDOW
    hw = HB * DK_B // 2
    hp = DK_B // 2

    @pl.when(t == 0)
    def _():
        s_ref[...] = jnp.zeros_like(s_ref)

    mod = mod_ref[0]
    h = _modnorm(x_ref[0], n1_ref[...], mod[:, D_MODEL:2 * D_MODEL], mod[:, 0:D_MODEL]).astype(BF16)
    for cc in range(nchunk // 2):
        z2 = _dot_nt(wt_ref[...], h[2 * WINDOW * cc:2 * WINDOW * (cc + 1), :])
        zt_ref[2 * cc] = z2[:, 0:WINDOW]
        zt_ref[2 * cc + 1] = z2[:, WINDOW:2 * WINDOW]

    def head_rows(pair, hh):
        return jnp.concatenate([pair[0][hp * hh:hp * (hh + 1), :], pair[1][hp * hh:hp * (hh + 1), :]], axis=0)

    def chunk(c, carry):
        r0 = pl.multiple_of(c * WINDOW, WINDOW)
        rows = pl.ds(r0, WINDOW)
        cos = cos_ref[c]
        sin = sin_ref[c]
        q1 = zt_ref[c, 0:hw, :]
        q2 = zt_ref[c, hw:2 * hw, :]
        k1 = zt_ref[c, 2 * hw:3 * hw, :]
        k2 = zt_ref[c, 3 * hw:4 * hw, :]
        rq = (q1 * cos - q2 * sin, q1 * sin + q2 * cos)
        rk = (k1 * cos - k2 * sin, k1 * sin + k2 * cos)
        qdec = qdec_ref[...]
        kdec = kdec_ref[...]
        qb = tuple(a.astype(BF16) for a in rq)
        kb = tuple(a.astype(BF16) for a in rk)
        qd = tuple((a * qdec).astype(BF16) for a in rq)
        kd = tuple((a * kdec).astype(BF16) for a in rk)
        vt = zt_ref[c, 4 * hw:4 * hw + HB * DV_B, :].astype(BF16)
        inner, cross, supd = [], [], []
        for hh in range(HB):
            s_old = s_ref[DK_B * hh:DK_B * (hh + 1), :]
            inner.append(_dot_tn(head_rows(qb, hh), head_rows(kb, hh)))
            cross.append(_dot_tn(s_old.astype(BF16), head_rows(qd, hh)))
            supd.append(glast[hh] * s_old + _dot_nt(head_rows(kd, hh), vt[DV_B * hh:DV_B * (hh + 1), :]))
        innd = [(inner[hh] * dm_ref[hh]).astype(BF16) for hh in range(HB)]
        outs = []
        for hh in range(HB):
            ot = _dot_nt(vt[DV_B * hh:DV_B * (hh + 1), :], innd[hh]) + cross[hh]
            s_ref[DK_B * hh:DK_B * (hh + 1), :] = supd[hh]
            ms = jnp.mean(ot * ot, axis=0, keepdims=True)
            outs.append(ot * lax.rsqrt(ms + EPS))
        obt = jnp.concatenate(outs, axis=0)
        bgt = zt_ref[c, 4 * hw + D_MODEL:4 * hw + 2 * D_MODEL, :]
        gbt = zt_ref[c, 4 * hw + 2 * D_MODEL:4 * hw + 3 * D_MODEL, :]
        mixt = _sigmoid(gbt) * (_silu(bgt) * obt)
        mix_ref[0, rows, :] = mixa_ref[0, rows, :] + mixt.T
        return carry

    lax.fori_loop(0, nchunk, chunk, 0)
    so_ref[0] = s_ref[...]


def _deinterleave_pairs(w):
    return w.reshape(w.shape[0], HB, DK_B // 2, 2).transpose(0, 3, 1, 2).reshape(w.shape[0], HB * DK_B)


def _rot_tables_t(T):
    theta = 1.0 / (10000.0 ** np.linspace(0.0, 1.0, DK_B // 2))
    ang = theta[:, None] * np.arange(T, dtype=np.float64)[None, :]

    def lay(a):
        a = np.tile(a, (HB, 1))
        return np.ascontiguousarray(a.reshape(a.shape[0], T // WINDOW, WINDOW).transpose(1, 0, 2)).astype(np.float32)

    return lay(np.cos(ang)), lay(np.sin(ang))


def _pb_call(x, mod3, n1, wbt, mixa, tt):
    B, T, _ = x.shape
    hw = HB * DK_B // 2
    cos, sin = _rot_tables_t(T)
    dm, qdec, kdec = _ret_tables()
    qdec_t = np.ascontiguousarray(qdec[:, ::2].T)
    kdec_t = np.ascontiguousarray(kdec[:, ::2].T)
    glast = tuple(float(v) for v in (_gammas() ** WINDOW))
    kern = functools.partial(_pb_kernel, tt=tt, glast=glast)
    full2 = lambda b, t: (0, 0)
    nct = tt // WINDOW
    return pl.pallas_call(
        kern,
        grid=(B, T // tt),
        in_specs=[pl.BlockSpec((1, tt, D_MODEL), lambda b, t: (b, t, 0)),
                  pl.BlockSpec((1, 1, 6 * D_MODEL), lambda b, t: (b, 0, 0)),
                  pl.BlockSpec((1, D_MODEL), full2),
                  pl.BlockSpec((SEG_B, D_MODEL), full2),
                  pl.BlockSpec((nct, hw, WINDOW), lambda b, t: (t, 0, 0)),
                  pl.BlockSpec((nct, hw, WINDOW), lambda b, t: (t, 0, 0)),
                  pl.BlockSpec((hw, WINDOW), full2),
                  pl.BlockSpec((hw, WINDOW), full2),
                  pl.BlockSpec((HB, WINDOW, WINDOW), lambda b, t: (0, 0, 0)),
                  pl.BlockSpec((1, tt, D_MODEL), lambda b, t: (b, t, 0))],
        out_specs=[pl.BlockSpec((1, tt, D_MODEL), lambda b, t: (b, t, 0)),
                   pl.BlockSpec((1, HB * DK_B, DV_B), lambda b, t: (b, 0, 0))],
        out_shape=[jax.ShapeDtypeStruct((B, T, D_MODEL), F32),
                   jax.ShapeDtypeStruct((B, HB * DK_B, DV_B), F32)],
        scratch_shapes=[pltpu.VMEM((nct, SEG_B, WINDOW), F32),
                        pltpu.VMEM((HB * DK_B, DV_B), F32)],
        compiler_params=_cparams(2),
        name="prompt_retention",
    )(x, mod3, n1, wbt, jnp.asarray(cos), jnp.asarray(sin), jnp.asarray(qdec_t), jnp.asarray(kdec_t),
      jnp.asarray(dm), mixa)


def _pc_kernel(x_ref, mod_ref, n1_ref, w_ref, cw_ref, cb_ref, dtb_ref, a_ref, dsk_ref, snw_ref, tri_ref,
               mixab_ref, wo_ref,
               xo_ref, ho_ref, co_ref,
               z_ref, xbuf_ref, xbc_ref, dt_ref, hst_ref, mixs_ref, *, tt):
    t = pl.program_id(1)

    @pl.when(t == 0)
    def _():
        xbuf_ref[0:8, :] = jnp.zeros((8, CONV_DIM), F32)
        hst_ref[...] = jnp.zeros_like(hst_ref)

    x = x_ref[0]
    mod = mod_ref[0]
    h = _modnorm(x, n1_ref[...], mod[:, D_MODEL:2 * D_MODEL], mod[:, 0:D_MODEL])
    z_ref[...] = _dot(h.astype(BF16), w_ref[...])

    xbuf_ref[8:8 + tt, :] = z_ref[:, 1024:2560]
    acc = cb_ref[...]
    for i in range(CONV_W):
        acc = acc + xbuf_ref[5 + i:5 + i + tt, :] * cw_ref[i:i + 1, :]
    xbc_ref[...] = _silu(acc)
    co_ref[0] = xbuf_ref[tt + 5:tt + 8, :]
    xbuf_ref[0:8, :] = xbuf_ref[tt:tt + 8, :]
    dt_ref[...] = _softplus(z_ref[:, C_DT:C_DT + 128] + dtb_ref[...])

    ii = lax.broadcasted_iota(jnp.int32, (WINDOW, WINDOW), 0)
    jj = lax.broadcasted_iota(jnp.int32, (WINDOW, WINDOW), 1)
    causal = ii >= jj
    hpg = HC // G_C

    def chunk(c, carry):
        r0 = pl.multiple_of(c * WINDOW, WINDOW)
        rows = pl.ds(r0, WINDOW)
        xc = xbc_ref[rows, 0:1024]
        bmat = xbc_ref[rows, 1024:1280]
        cmat = xbc_ref[rows, 1280:1536]
        dtc = dt_ref[rows, :]
        acum = jnp.dot(tri_ref[...], dtc * a_ref[...], precision=lax.Precision.HIGHEST,
                       preferred_element_type=F32)
        acum_t = acum.T
        dt_t = dtc.T
        x_t = xc.T
        xb = xc.astype(BF16)
        bb = bmat.astype(BF16)
        cb16 = cmat.astype(BF16)
        ys = []
        for g in range(G_C):
            ns = slice(N_C * g, N_C * (g + 1))
            cbg = _dot_nt(cb16[:, ns], bb[:, ns])
            for hh in range(hpg * g, hpg * (g + 1)):
                ps = slice(HD_C * hh, HD_C * (hh + 1))
                colb = jnp.broadcast_to(acum[:, hh:hh + 1], (WINDOW, WINDOW))
                row = acum_t[hh:hh + 1, :]
                dtrow = dt_t[hh:hh + 1, :]
                lmat = jnp.exp(jnp.where(causal, colb - row, NEG))
                m = cbg * lmat * dtrow
                hs = hst_ref[ps, :]
                cs = cmat[:, ns] * jnp.exp(colb)
                y = _dot(m.astype(BF16), xb[:, ps]) + _dot_nt(cs.astype(BF16), hs.astype(BF16))
                last = colb[WINDOW - 1:WINDOW, :]
                wrow = jnp.exp(last - row) * dtrow
                xw = (x_t[ps, :] * wrow).astype(BF16)
                hst_ref[ps, :] = jnp.exp(last) * hs + _dot(xw, bb[:, ns])
                ys.append(y)
        y = jnp.concatenate(ys, axis=1) + dsk_ref[...] * xc
        yc = y * _silu(z_ref[rows, 0:1024])
        gw = D_MODEL // G_C
        oc = jnp.concatenate([_rms_lanes(yc[:, gw * g:gw * (g + 1)]) for g in range(G_C)], axis=1) * snw_ref[...]
        gc = _sigmoid(z_ref[rows, C_GC:C_GC + 1024])
        mixs_ref[rows, :] = (mixab_ref[0, rows, :] + gc * oc).astype(BF16)
        return carry

    lax.fori_loop(0, tt // WINDOW, chunk, 0)
    ho_ref[0] = hst_ref[...]
    g1 = mod[:, 2 * D_MODEL:3 * D_MODEL]
    xo_ref[0] = x + g1 * _dot(mixs_ref[...], wo_ref[...])


def _pc_call(x, mod3, n1, wc, cw, cb, dtb_pad, a_pad, dsk_full, snw, mixab, wo, tt):
    B, T, _ = x.shape
    tri = jnp.asarray(np.tril(np.ones((WINDOW, WINDOW), np.float32)))
    kern = functools.partial(_pc_kernel, tt=tt)
    full2 = lambda b, t: (0, 0)
    return pl.pallas_call(
        kern,
        grid=(B, T // tt),
        in_specs=[pl.BlockSpec((1, tt, D_MODEL), lambda b, t: (b, t, 0)),
                  pl.BlockSpec((1, 1, 6 * D_MODEL), lambda b, t: (b, 0, 0)),
                  pl.BlockSpec((1, D_MODEL), full2),
                  pl.BlockSpec((D_MODEL, SEG_C), full2),
                  pl.BlockSpec((CONV_W, CONV_DIM), full2),
                  pl.BlockSpec((1, CONV_DIM), full2),
                  pl.BlockSpec((1, 128), full2),
                  pl.BlockSpec((1, 128), full2),
                  pl.BlockSpec((1, D_MODEL), full2),
                  pl.BlockSpec((1, D_MODEL), full2),
                  pl.BlockSpec((WINDOW, WINDOW), full2),
                  pl.BlockSpec((1, tt, D_MODEL), lambda b, t: (b, t, 0)),
                  pl.BlockSpec((D_MODEL, D_MODEL), full2)],
        out_specs=[pl.BlockSpec((1, tt, D_MODEL), lambda b, t: (b, t, 0)),
                   pl.BlockSpec((1, HC * HD_C, N_C), lambda b, t: (b, 0, 0)),
                   pl.BlockSpec((1, CONV_W - 1, CONV_DIM), lambda b, t: (b, 0, 0))],
        out_shape=[jax.ShapeDtypeStruct((B, T, D_MODEL), F32),
                   jax.ShapeDtypeStruct((B, HC * HD_C, N_C), F32),
                   jax.ShapeDtypeStruct((B, CONV_W - 1, CONV_DIM), F32)],
        scratch_shapes=[pltpu.VMEM((tt, SEG_C), F32),
                        pltpu.VMEM((tt + 8, CONV_DIM), F32),
                        pltpu.VMEM((tt, CONV_DIM), F32),
                        pltpu.VMEM((tt, 128), F32),
                        pltpu.VMEM((HC * HD_C, N_C), F32),
                        pltpu.VMEM((tt, D_MODEL), BF16)],
        compiler_params=_cparams(2),
        name="prompt_ssd_out",
    )(x, mod3, n1, wc, cw, cb, dtb_pad, a_pad, dsk_full, snw, tri, mixab, wo)


def _mlp_kernel(x_ref, sh_ref, sc_ref, g_ref, n2_ref, wu_ref, wd_ref, fn_ref, o_ref, h_ref, acc_ref,
                *, nf, final, per_row):
    f = pl.program_id(1)
    rd = (lambda r: r[...]) if per_row else (lambda r: r[0])

    @pl.when(f == 0)
    def _():
        h_ref[...] = _modnorm(x_ref[...], n2_ref[...], rd(sc_ref), rd(sh_ref)).astype(BF16)
        acc_ref[...] = jnp.zeros_like(acc_ref)

    u = _dot(h_ref[...], wu_ref[0])
    u = jnp.square(jnp.maximum(u, 0.0)).astype(BF16)
    acc_ref[...] += _dot(u, wd_ref[0])

    @pl.when(f == nf - 1)
    def _():
        y = x_ref[...] + rd(g_ref) * acc_ref[...]
        if final:
            y = _rms_lanes(y) * fn_ref[...]
        o_ref[...] = y


def _mlp_call(x2, mod, n2, wu, wd, fn, layer, tm, tf, rows_per_mod, final):
    M = x2.shape[0]
    nf = D_FF // tf
    per_row = rows_per_mod == 1
    if per_row:
        mspec = lambda j: pl.BlockSpec((tm, D_MODEL), lambda m, f: (m, j))
    else:
        mspec = lambda j: pl.BlockSpec((1, 1, D_MODEL), lambda m, f: ((m * tm) // rows_per_mod, 0, j))
    kern = functools.partial(_mlp_kernel, nf=nf, final=final, per_row=per_row)
    return pl.pallas_call(
        kern,
        grid=(M // tm, nf),
        in_specs=[pl.BlockSpec((tm, D_MODEL), lambda m, f: (m, 0)),
                  mspec(3), mspec(4), mspec(5),
                  pl.BlockSpec((1, D_MODEL), lambda m, f: (0, 0)),
                  pl.BlockSpec((1, D_MODEL, tf), lambda m, f: (layer, 0, f)),
                  pl.BlockSpec((1, tf, D_MODEL), lambda m, f: (layer, f, 0)),
                  pl.BlockSpec((1, D_MODEL), lambda m, f: (0, 0))],
        out_specs=pl.BlockSpec((tm, D_MODEL), lambda m, f: (m, 0)),
        out_shape=jax.ShapeDtypeStruct((M, D_MODEL), F32),
        scratch_shapes=[pltpu.VMEM((tm, D_MODEL), BF16),
                        pltpu.VMEM((tm, D_MODEL), F32)],
        compiler_params=_cparams(2),
        name="mlp",
    )(x2, mod, mod, mod, n2, wu, wd, fn)


def _sproj_kernel(x_ref, sh_ref, sc_ref, n1_ref, w_ref, o_ref):
    h = _modnorm(x_ref[...], n1_ref[...], sc_ref[...], sh_ref[...])
    o_ref[...] = _dot(h.astype(BF16), w_ref[...])


def _sproj_call(xs, mod_s, n1, w):
    n, width = xs.shape[0], w.shape[1]
    return pl.pallas_call(
        _sproj_kernel,
        grid=(1,),
        in_specs=[pl.BlockSpec((n, D_MODEL), lambda i: (0, 0)),
                  pl.BlockSpec((n, D_MODEL), lambda i: (0, 0)),
                  pl.BlockSpec((n, D_MODEL), lambda i: (0, 1)),
                  pl.BlockSpec((1, D_MODEL), lambda i: (0, 0)),
                  pl.BlockSpec((D_MODEL, width), lambda i: (0, 0))],
        out_specs=pl.BlockSpec((n, width), lambda i: (0, 0)),
        out_shape=jax.ShapeDtypeStruct((n, width), F32),
        compiler_params=_cparams(1),
        name="sample_proj",
    )(xs, mod_s, mod_s, n1, w)


def _carry_outputs(prev, n_in, nsteps):
    nslab = 1 if prev else DEPTH
    extra = [pl.BlockSpec(memory_space=pl.ANY) for _ in prev]
    alias = {n_in + k: k for k in range(len(prev))}
    row = lambda p, i: jnp.where(p == 0, i, nsteps - 1)
    return nslab, extra, alias, row


def _with_fill(body, state_outs):
    def kern(*refs):
        p = pl.program_id(0)

        @pl.when(p == 0)
        def _():
            body(*refs)

        @pl.when(p != 0)
        def _():
            for k in state_outs:
                refs[k][...] = jnp.zeros_like(refs[k])

    return kern


def _sa_kernel(q_ref, kv_ref, ck_ref, cv_ref, bias_ref, sink_ref, *rest):
    ko_ref, vo_ref, oa_ref = rest[-3:]
    nb = ROWS_PER_SAMPLE_STEP
    gsz = HA_Q // HA_KV
    rg = lax.broadcasted_iota(jnp.int32, (HA_Q, HD_A), 0) // gsz
    sink = sink_ref[...]
    kw = HA_KV * HD_A
    scores, vmats = [], []
    for i in range(nb):
        ko_ref[0, i, 0:WINDOW - 1, :] = ck_ref[0, i, 1:WINDOW, :]
        ko_ref[0, i, WINDOW - 1:WINDOW, :] = kv_ref[i:i + 1, 0:kw]
        vo_ref[0, i, 0:WINDOW - 1, :] = cv_ref[0, i, 1:WINDOW, :]
        vo_ref[0, i, WINDOW - 1:WINDOW, :] = kv_ref[i:i + 1, kw:2 * kw]
        kmat = ko_ref[0, i].astype(BF16)
        vmats.append(vo_ref[0, i].astype(BF16))
        q = q_ref[i] * (HD_A ** -0.5)
        qe = jnp.concatenate([jnp.where(rg == g, q, 0.0) for g in range(HA_KV)], axis=1).astype(BF16)
        scores.append(_dot_nt(qe, kmat))
    probs = []
    for i in range(nb):
        sc = scores[i] + bias_ref[...]
        m = jnp.maximum(jnp.max(sc, axis=-1, keepdims=True), sink)
        p = jnp.exp(sc - m)
        den = jnp.sum(p, axis=-1, keepdims=True) + jnp.exp(sink - m)
        probs.append((p.astype(BF16), den))
    outs = [_dot(probs[i][0], vmats[i]) for i in range(nb)]
    for i in range(nb):
        o = outs[i] / probs[i][1]
        o16 = jnp.zeros((HA_Q, HD_A), F32)
        for g in range(HA_KV):
            o16 = o16 + jnp.where(rg == g, o[:, HD_A * g:HD_A * (g + 1)], 0.0)
        oa_ref[i] = o16


def _sa_call(q3, kv, cache_k, cache_v, bias_s, sink_col, layer, prev):
    n = q3.shape[0]
    nb = ROWS_PER_SAMPLE_STEP
    kw = HA_KV * HD_A
    nslab, extra, alias, row = _carry_outputs(prev, 6, n // nb)
    cspec = pl.BlockSpec((1, nb, WINDOW, kw), lambda p, i: (layer, row(p, i), 0, 0))
    ospec = pl.BlockSpec((1, nb, WINDOW, kw), lambda p, i: ((layer + p) % DEPTH, i, 0, 0))
    return pl.pallas_call(
        _with_fill(_sa_kernel, (-3, -2)),
        grid=(nslab, n // nb),
        in_specs=[pl.BlockSpec((nb, HA_Q, HD_A), lambda p, i: (row(p, i), 0, 0)),
                  pl.BlockSpec((nb, 2 * kw), lambda p, i: (row(p, i), 0)),
                  cspec, cspec,
                  pl.BlockSpec((HA_Q, WINDOW), lambda p, i: (0, 0)),
                  pl.BlockSpec((HA_Q, 1), lambda p, i: (0, 0))] + extra,
        out_specs=[ospec, ospec, pl.BlockSpec((nb, HA_Q, HD_A), lambda p, i: (row(p, i), 0, 0))],
        out_shape=[jax.ShapeDtypeStruct((DEPTH, n, WINDOW, kw), F32),
                   jax.ShapeDtypeStruct((DEPTH, n, WINDOW, kw), F32),
                   jax.ShapeDtypeStruct((n, HA_Q, HD_A), F32)],
        input_output_aliases=alias,
        compiler_params=_cparams(2),
        name="sample_attn",
    )(q3, kv, cache_k, cache_v, bias_s, sink_col, *prev)


def _sr_kernel(zb_ref, cos_ref, sin_ref, gcol_ref, s_ref, *rest):
    so_ref, o_ref = rest[-2:]
    nb = ROWS_PER_SAMPLE_STEP
    wq = HB * DK_B
    cos = cos_ref[...]
    sin = sin_ref[...]
    qf = zb_ref[:, 0:wq]
    kf = zb_ref[:, wq:2 * wq]
    qr = qf * cos + _pairswap(qf) * sin
    kr = (kf * cos + _pairswap(kf) * sin) * (DK_B ** -0.5)
    v = zb_ref[:, 2 * wq:2 * wq + HB * DV_B]
    r8 = lax.broadcasted_iota(jnp.int32, (HB, wq), 0)
    hl = lax.broadcasted_iota(jnp.int32, (HB, wq), 1) // DK_B
    rv = lax.broadcasted_iota(jnp.int32, (HB, DV_B), 0)
    outers, q8s = [], []
    for i in range(nb):
        k8 = jnp.where(hl == r8, jnp.broadcast_to(kr[i:i + 1, :], (HB, wq)), 0.0).astype(BF16)
        q8s.append(jnp.where(hl == r8, jnp.broadcast_to(qr[i:i + 1, :], (HB, wq)), 0.0).astype(BF16))
        v8 = jnp.zeros((HB, DV_B), F32)
        for r in range(HB):
            v8 = jnp.where(rv == r, jnp.broadcast_to(v[i:i + 1, DV_B * r:DV_B * (r + 1)], (HB, DV_B)), v8)
        outers.append(_dot_tn(k8, v8.astype(BF16)))
    s_news = []
    for i in range(nb):
        s_new = gcol_ref[...] * s_ref[0, i] + outers[i]
        so_ref[0, i] = s_new
        s_news.append(s_new.astype(BF16))
    for i in range(nb):
        o_ref[i] = _dot(q8s[i], s_news[i])


def _sr_call(zb, state, layer, prev):
    n = zb.shape[0]
    nb = ROWS_PER_SAMPLE_STEP
    wq = HB * DK_B
    cos, sin = _rot_tables(np.array([PAST_LEN]))
    gcol = np.repeat(_gammas(), DK_B)[:, None] * np.ones((1, DV_B))
    nslab, extra, alias, row = _carry_outputs(prev, 5, n // nb)
    full = lambda p, i: (0, 0)
    return pl.pallas_call(
        _with_fill(_sr_kernel, (-2,)),
        grid=(nslab, n // nb),
        in_specs=[pl.BlockSpec((nb, SEG_B), lambda p, i: (row(p, i), 0)),
                  pl.BlockSpec((1, wq), full),
                  pl.BlockSpec((1, wq), full),
                  pl.BlockSpec((wq, DV_B), full),
                  pl.BlockSpec((1, nb, wq, DV_B), lambda p, i: (layer, row(p, i), 0, 0))] + extra,
        out_specs=[pl.BlockSpec((1, nb, wq, DV_B), lambda p, i: ((layer + p) % DEPTH, i, 0, 0)),
                   pl.BlockSpec((nb, HB, DV_B), lambda p, i: (row(p, i), 0, 0))],
        out_shape=[jax.ShapeDtypeStruct((DEPTH, n, wq, DV_B), F32),
                   jax.ShapeDtypeStruct((n, HB, DV_B), F32)],
        input_output_aliases=alias,
        compiler_params=_cparams(2),
        name="sample_retention",
    )(zb, jnp.asarray(cos), jnp.asarray(sin), jnp.asarray(gcol.astype(np.float32)), state, *prev)


def _ss_kernel(zc_ref, cs_ref, cw_ref, cb_ref, dtb_ref, a_ref, e3_ref, h_ref, *rest):
    ho_ref, cso_ref, y_ref, xc_ref = rest[-4:]
    nb = ROWS_PER_SAMPLE_STEP
    cx = zc_ref[:, 1024:2560]
    taps = [cs_ref[0, :, CONV_DIM * i:CONV_DIM * (i + 1)] for i in range(CONV_W - 1)] + [cx]
    acc = cb_ref[...]
    for i in range(CONV_W):
        acc = acc + taps[i] * cw_ref[i:i + 1, :]
    xbc = _silu(acc)
    cso_ref[...] = jnp.concatenate(taps[1:], axis=1)
    xc = xbc[:, 0:1024]
    bmat = xbc[:, 1024:1280]
    cmat = xbc[:, 1280:1536]
    dt = _softplus(zc_ref[:, C_DT:C_DT + 128] + dtb_ref[...])
    da = jnp.exp(dt * a_ref[...])
    dt_e = _dot(jnp.concatenate(_split3(dt), axis=1), e3_ref[...])
    da_e = _dot(jnp.concatenate(_split3(da), axis=1), e3_ref[...])
    dtx = dt_e * xc
    gw = (HC // G_C) * HD_C
    r8 = lax.broadcasted_iota(jnp.int32, (nb, gw), 0)
    rn = lax.broadcasted_iota(jnp.int32, (nb, N_C), 0)
    ones8 = jnp.ones((nb, N_C), BF16)
    prods = {}
    for g in range(G_C):
        ws = slice(gw * g, gw * (g + 1))
        bg16 = bmat[:, N_C * g:N_C * (g + 1)].astype(BF16)
        for i in range(nb):
            x8 = jnp.where(r8 == i, dtx[:, ws], 0.0).astype(BF16)
            outer = _dot_tn(x8, bg16)
            hi, mid, lo = (p.astype(F32) for p in _split3(jnp.broadcast_to(da_e[i:i + 1, ws], (nb, gw))))
            l3 = jnp.where(r8 == 0, hi, jnp.where(r8 == 1, mid, jnp.where(r8 == 2, lo, 0.0)))
            prods[g, i] = (outer, _dot_tn(l3.astype(BF16), ones8))
    h16 = {}
    for g in range(G_C):
        ws = slice(gw * g, gw * (g + 1))
        for i in range(nb):
            outer, dacol = prods[g, i]
            h_new = dacol * h_ref[0, i, ws, :] + outer
            ho_ref[0, i, ws, :] = h_new
            h16[g, i] = h_new.astype(BF16)
    ycols = []
    for g in range(G_C):
        ns = slice(N_C * g, N_C * (g + 1))
        yacc = jnp.zeros((nb, gw), F32)
        for i in range(nb):
            c8 = jnp.where(rn == i, cmat[:, ns], 0.0).astype(BF16)
            yacc = yacc + _dot_nt(c8, h16[g, i])
        ycols.append(yacc)
    y_ref[...] = jnp.concatenate(ycols, axis=1)
    xc_ref[...] = xc


def _ss_call(zc, conv_state, hstate, cw, cb, dtb_pad, a_pad, layer, prev):
    n = zc.shape[0]
    nb = ROWS_PER_SAMPLE_STEP
    e = np.zeros((128, D_MODEL), np.float32)
    for hh in range(HC):
        e[hh, HD_C * hh:HD_C * (hh + 1)] = 1.0
    e3 = jnp.asarray(np.concatenate([e, e, e], axis=0), dtype=BF16)
    cwid = (CONV_W - 1) * CONV_DIM
    full = lambda p, i: (0, 0)
    nslab, extra, alias, row = _carry_outputs(prev, 8, n // nb)
    return pl.pallas_call(
        _with_fill(_ss_kernel, (-4,)),
        grid=(nslab, n // nb),
        in_specs=[pl.BlockSpec((nb, SEG_C), lambda p, i: (row(p, i), 0)),
                  pl.BlockSpec((1, nb, cwid), lambda p, i: (layer, row(p, i), 0)),
                  pl.BlockSpec((CONV_W, CONV_DIM), full),
                  pl.BlockSpec((1, CONV_DIM), full),
                  pl.BlockSpec((1, 128), full),
                  pl.BlockSpec((1, 128), full),
                  pl.BlockSpec((3 * 128, D_MODEL), full),
                  pl.BlockSpec((1, nb, HC * HD_C, N_C), lambda p, i: (layer, row(p, i), 0, 0))] + extra,
        out_specs=[pl.BlockSpec((1, nb, HC * HD_C, N_C), lambda p, i: ((layer + p) % DEPTH, i, 0, 0)),
                   pl.BlockSpec((nb, cwid), lambda p, i: (row(p, i), 0)),
                   pl.BlockSpec((nb, D_MODEL), lambda p, i: (row(p, i), 0)),
                   pl.BlockSpec((nb, D_MODEL), lambda p, i: (row(p, i), 0))],
        out_shape=[jax.ShapeDtypeStruct((DEPTH, n, HC * HD_C, N_C), F32),
                   jax.ShapeDtypeStruct((n, cwid), F32),
                   jax.ShapeDtypeStruct((n, D_MODEL), F32),
                   jax.ShapeDtypeStruct((n, D_MODEL), F32)],
        input_output_aliases=alias,
        compiler_params=_cparams(2),
        name="sample_ssd",
    )(zc, conv_state, cw, cb, dtb_pad, a_pad, e3, hstate, *prev)


def _sm_kernel(x_ref, g1_ref, oa_ref, oret_ref, y_ref, xc_ref, za_ref, zb_ref, zc_ref,
               dsk_ref, snw_ref, wo_ref, o_ref):
    ob = jnp.concatenate([_rms_lanes(oret_ref[:, DV_B * hh:DV_B * (hh + 1)]) for hh in range(HB)], axis=1)
    ob = _silu(zb_ref[:, 2048:3072]) * ob
    yc = (y_ref[...] + dsk_ref[...] * xc_ref[...]) * _silu(zc_ref[:, 0:1024])
    gw = D_MODEL // G_C
    oc = jnp.concatenate([_rms_lanes(yc[:, gw * g:gw * (g + 1)]) for g in range(G_C)], axis=1) * snw_ref[...]
    mix = (_sigmoid(za_ref[:, 1536:2560]) * oa_ref[...] + _sigmoid(zb_ref[:, 3072:4096]) * ob
           + _sigmoid(zc_ref[:, C_GC:C_GC + 1024]) * oc)
    o_ref[...] = x_ref[...] + g1_ref[...] * _dot(mix.astype(BF16), wo_ref[...])


def _sm_call(xs, mod_s, oa, oret, y, xc, za, zb, zc, dsk_full, snw, wo):
    n = xs.shape[0]
    full = lambda i: (0, 0)
    row = lambda w: pl.BlockSpec((n, w), full)
    return pl.pallas_call(
        _sm_kernel,
        grid=(1,),
        in_specs=[row(D_MODEL),
                  pl.BlockSpec((n, D_MODEL), lambda i: (0, 2)),
                  row(D_MODEL), row(D_MODEL), row(D_MODEL), row(D_MODEL),
                  row(SEG_A), row(SEG_B), row(SEG_C),
                  pl.BlockSpec((1, D_MODEL), full),
                  pl.BlockSpec((1, D_MODEL), full),
                  pl.BlockSpec((D_MODEL, D_MODEL), full)],
        out_specs=row(D_MODEL),
        out_shape=jax.ShapeDtypeStruct((n, D_MODEL), F32),
        compiler_params=_cparams(1),
        name="sample_merge_out",
    )(xs, mod_s, oa, oret, y, xc, za, zb, zc, dsk_full, snw, wo)


PROMPT_TILE = 512
MLP_ROWS = 1024
MLP_FF = 1024


def _prep_w_in(w):
    wa = jnp.concatenate([w[:, O_AQ:O_BQ], w[:, O_GTS:O_GTS + 1024]], axis=1)
    wb = jnp.concatenate([w[:, O_BQ:O_CZ], w[:, O_GTS + 1024:O_GTS + 2048]], axis=1)
    wc = jnp.concatenate([w[:, O_CZ:O_CDT], jnp.pad(w[:, O_CDT:O_GTS], ((0, 0), (0, 128 - HC))),
                          w[:, O_GTS + 2048:O_GTS + 3072]], axis=1)
    return wa, wb, wc


def _forward(x_prompt, x_sample, cache_win_k, cache_win_v, state_ret, state_ssm, state_conv,
             c_prompt, c_sample, rel_bias_table, attn_sinks, norm1_w, norm2_w, ada_w, ada_b,
             w_in, conv_w, conv_b, dt_bias, A_log, D_skip, ssm_norm_w, w_out, w_up, w_down,
             final_norm_w, *, prompt_tile, mlp_rows, mlp_ff):
    B, T, _ = x_prompt.shape
    DB = x_sample.shape[0]
    kw = HA_KV * HD_A

    mod_all = _ada_call(jnp.concatenate([c_prompt, c_sample], axis=0), ada_w, ada_b)

    bias_t, bias_s = _bias_tables(rel_bias_table)

    w16 = w_in.astype(BF16)
    wu16 = w_up.astype(BF16)
    wd16 = w_down.astype(BF16)
    wo16 = w_out.astype(BF16)
    fn = final_norm_w.reshape(1, D_MODEL)
    ck = cache_win_k.reshape(DEPTH, DB, WINDOW, kw)
    cv = cache_win_v.reshape(DEPTH, DB, WINDOW, kw)
    sret = state_ret.reshape(DEPTH, DB, HB * DK_B, DV_B)
    sssm = state_ssm.reshape(DEPTH, DB, HC * HD_C, N_C)
    sconv = state_conv.reshape(DEPTH, DB, (CONV_W - 1) * CONV_DIM)

    xp = x_prompt
    xs = x_sample.reshape(DB, D_MODEL)
    outs_p = [[] for _ in range(5)]
    conv_s = []
    win_kv, ret_all, ssm_all = (), (), ()
    for l in range(DEPTH):
        wl = w16[l]
        wa, wb, wc = _prep_w_in(wl)
        n1 = norm1_w[l].reshape(1, D_MODEL)
        n2 = norm2_w[l].reshape(1, D_MODEL)
        cw = conv_w[l]
        cb = conv_b[l].reshape(1, CONV_DIM)
        dtb_pad = jnp.pad(dt_bias[l], (0, 128 - HC)).reshape(1, 128)
        a_pad = jnp.pad(-jnp.exp(A_log[l].astype(F32)), (0, 128 - HC)).reshape(1, 128)
        dsk_full = jnp.repeat(D_skip[l], HD_C).reshape(1, D_MODEL)
        snw = ssm_norm_w[l].reshape(1, D_MODEL)
        final = l == DEPTH - 1
        mod_p = mod_all[l, :B].reshape(B, 1, 6 * D_MODEL)
        mod_s = mod_all[l, B:]

        wqgt = jnp.concatenate([wl[:, O_AQ:O_AK] * (HD_A ** -0.5), wl[:, O_GTS:O_GTS + D_MODEL]], axis=1).T
        wkv = wl[:, O_AK:O_BQ]
        sink_rows = jnp.repeat(attn_sinks[l], WINDOW).reshape(HA_KV, 1, (HA_Q // HA_KV) * WINDOW)
        mixa, kbuf, vbuf = _pa_call(xp, mod_p, n1, wqgt, wkv, bias_t, sink_rows, prompt_tile)
        wbt = jnp.concatenate([_deinterleave_pairs(wl[:, O_BQ:O_BK]),
                               _deinterleave_pairs(wl[:, O_BK:O_BV]) * (DK_B ** -0.5), wl[:, O_BV:O_CZ],
                               wl[:, O_GTS + D_MODEL:O_GTS + 2 * D_MODEL]], axis=1).T
        mixab, s_perm = _pb_call(xp, mod_p, n1, wbt, mixa, prompt_tile)
        s_ret = s_perm.reshape(B, HB, 2, DK_B // 2, DV_B).transpose(0, 1, 3, 2, 4)
        x1, h_ssm, conv_new = _pc_call(xp, mod_p, n1, wc, cw, cb, dtb_pad, a_pad, dsk_full, snw,
                                       mixab, wo16[l], prompt_tile)
        xp = _mlp_call(x1.reshape(B * T, D_MODEL), mod_p, n2, wu16, wd16, fn, l,
                       mlp_rows, mlp_ff, T, final).reshape(B, T, D_MODEL)
        for lst, v in zip(outs_p, (kbuf.reshape(B, WINDOW, HA_KV, HD_A), vbuf.reshape(B, WINDOW, HA_KV, HD_A),
                                   s_ret.reshape(B, HB, DK_B, DV_B), h_ssm.reshape(B, HC, HD_C, N_C), conv_new)):
            lst.append(v)

        za = _sproj_call(xs, mod_s, n1, wa)
        zb = _sproj_call(xs, mod_s, n1, wb)
        zc = _sproj_call(xs, mod_s, n1, wc)
        q3 = za[:, 0:HA_Q * HD_A].reshape(DB, HA_Q, HD_A)
        kv = za[:, HA_Q * HD_A:HA_Q * HD_A + 2 * kw]
        ck_new, cv_new, oa3 = _sa_call(q3, kv, ck, cv, bias_s, attn_sinks[l].reshape(HA_Q, 1), l, win_kv)
        win_kv = (ck_new, cv_new)
        s_new, o3 = _sr_call(zb, sret, l, ret_all)
        ret_all = (s_new,)
        h_new, cs_new, y_s, xc_s = _ss_call(zc, sconv, sssm, cw, cb, dtb_pad, a_pad, l, ssm_all)
        ssm_all = (h_new,)
        xs1 = _sm_call(xs, mod_s, oa3.reshape(DB, D_MODEL), o3.reshape(DB, D_MODEL), y_s, xc_s,
                       za, zb, zc, dsk_full, snw, wo16[l])
        xs = _mlp_call(xs1, mod_s, n2, wu16, wd16, fn, l, DB, mlp_ff, 1, final)
        conv_s.append(cs_new.reshape(DB, CONV_W - 1, CONV_DIM))

    stk = lambda lst: jnp.stack(lst, axis=0)
    return (xp, xs.reshape(DB, 1, D_MODEL),
            stk(outs_p[0]), stk(outs_p[1]), stk(outs_p[2]), stk(outs_p[3]), stk(outs_p[4]),
            win_kv[0].reshape(DEPTH, DB, WINDOW, HA_KV, HD_A), win_kv[1].reshape(DEPTH, DB, WINDOW, HA_KV, HD_A),
            ret_all[0].reshape(DEPTH, DB, HB, DK_B, DV_B), ssm_all[0].reshape(DEPTH, DB, HC, HD_C, N_C),
            stk(conv_s))


def kernel(x_prompt, x_sample, cache_win_k, cache_win_v, state_ret, state_ssm, state_conv, c_prompt, c_sample,
           rel_bias_table, attn_sinks, norm1_w, norm2_w, ada_w, ada_b, w_in, conv_w, conv_b, dt_bias, A_log,
           D_skip, ssm_norm_w, w_out, w_up, w_down, final_norm_w):
    return _forward(x_prompt, x_sample, cache_win_k, cache_win_v, state_ret, state_ssm, state_conv,
                    c_prompt, c_sample, rel_bias_table, attn_sinks, norm1_w, norm2_w, ada_w, ada_b,
                    w_in, conv_w, conv_b, dt_bias, A_log, D_skip, ssm_norm_w, w_out, w_up, w_down,
                    final_norm_w, prompt_tile=PROMPT_TILE, mlp_rows=MLP_ROWS, mlp_ff=MLP_FF)
```

```python
import functools
import math

import numpy as np
import jax
import jax.numpy as jnp
from jax import lax
from jax.experimental import pallas as pl
from jax.experimental.pallas import tpu as pltpu

F32 = jnp.float32
BF16 = jnp.bfloat16

D_MODEL = 1024
DEPTH = 2
PAST_LEN = 16384
WINDOW = 128
HA_Q = 16
HA_KV = 4
HD_A = 64
NUM_BUCKETS = 32
MAX_DISTANCE = WINDOW
HB = 8
DK_B = 64
DV_B = 128
HC = 16
HD_C = 64
N_C = 128
G_C = 2
CONV_W = 4
CONV_DIM = D_MODEL + 2 * G_C * N_C
D_FF = 4 * D_MODEL
EPS = 1e-6
NEG = -1e30

O_AQ, O_AK, O_AV = 0, 1024, 1280
O_BQ, O_BK, O_BV, O_BG = 1536, 2048, 2560, 3584
O_CZ, O_CXBC, O_CDT, O_GTS = 4608, 5632, 7168, 7184
SEG_A = 2560
SEG_B = 4096
SEG_C = 3712
C_DT = 2560
C_GC = 2688

VMEM_LIMIT_V7X = 56 * 1024 * 1024
ROWS_PER_SAMPLE_STEP = 8


def _cparams(n_axes):
    return pltpu.CompilerParams(dimension_semantics=("arbitrary",) * n_axes,
                                vmem_limit_bytes=VMEM_LIMIT_V7X)


def _dot(a, b):
    return jnp.dot(a, b, preferred_element_type=F32)


def _dot_nt(a, b):
    return lax.dot_general(a, b, (((1,), (1,)), ((), ())), preferred_element_type=F32)


def _dot_tn(a, b):
    return lax.dot_general(a, b, (((0,), (0,)), ((), ())), preferred_element_type=F32)


def _sigmoid(x):
    return 0.5 * (jnp.tanh(0.5 * x) + 1.0)


def _silu(x):
    return x * _sigmoid(x)


def _softplus(x):
    return jnp.maximum(x, 0.0) + jnp.log1p(jnp.exp(-jnp.abs(x)))


def _modnorm(x, nw, sc, sh):
    ms = jnp.mean(x * x, axis=-1, keepdims=True)
    return (x * lax.rsqrt(ms + EPS) * nw) * (1.0 + sc) + sh


def _rms_lanes(x):
    ms = jnp.mean(x * x, axis=-1, keepdims=True)
    return x * lax.rsqrt(ms + EPS)


def _pairswap(x):
    ax = x.ndim - 1
    n = x.shape[ax]
    lane = lax.broadcasted_iota(jnp.int32, x.shape, ax)
    nxt = pltpu.roll(x, n - 1, ax)
    prv = pltpu.roll(x, 1, ax)
    return jnp.where((lane & 1) == 0, nxt, prv)


def _split3(x):
    hi = x.astype(BF16)
    r1 = x - hi.astype(F32)
    mid = r1.astype(BF16)
    lo = (r1 - mid.astype(F32)).astype(BF16)
    return hi, mid, lo


def _gammas():
    return 1.0 - 2.0 ** (-5.0 - np.arange(HB, dtype=np.float64))


def _rot_tables(pos):
    theta = 1.0 / (10000.0 ** np.linspace(0.0, 1.0, DK_B // 2))
    ang = np.asarray(pos, np.float64)[:, None] * theta[None, :]
    cos = np.repeat(np.cos(ang), 2, axis=1)
    sin = np.repeat(np.sin(ang), 2, axis=1)
    sin[:, 0::2] *= -1.0
    return (np.tile(cos, (1, HB)).astype(np.float32), np.tile(sin, (1, HB)).astype(np.float32))


def _ret_tables():
    g = _gammas()
    L = WINDOW
    i = np.arange(L, dtype=np.float64)
    diff = i[:, None] - i[None, :]
    dm = np.where(diff >= 0, g[:, None, None] ** np.maximum(diff, 0.0), 0.0)
    qdec = np.repeat(g[None, :] ** (i[:, None] + 1.0), DK_B, axis=1)
    kdec = np.repeat(g[None, :] ** (L - 1.0 - i[:, None]), DK_B, axis=1)
    return dm.astype(np.float32), qdec.astype(np.float32), kdec.astype(np.float32)


def _t5_bucket_np(dist):
    max_exact = NUM_BUCKETS // 2
    n = np.maximum(dist, 0)
    nf = np.maximum(n, 1).astype(np.float32)
    large = max_exact + (np.log(nf / np.float32(max_exact)) / np.float32(math.log(MAX_DISTANCE / max_exact))
                         * np.float32(NUM_BUCKETS - max_exact)).astype(np.int32)
    large = np.minimum(large, NUM_BUCKETS - 1)
    return np.where(n < max_exact, n, large)


def _bias_tables(rel_table):
    gsz = HA_Q // HA_KV
    qi = np.arange(WINDOW)[None, :]
    kj = np.arange(WINDOW)[:, None]
    dist = np.where(kj > qi, qi + WINDOW - kj, qi - kj)
    onehot = _t5_bucket_np(dist)[..., None] == np.arange(NUM_BUCKETS)
    tab = rel_table.astype(F32)
    bias_kq = jnp.einsum('kqb,bh->hkq', jnp.asarray(onehot, F32), tab, precision=lax.Precision.HIGHEST)
    bias_t = bias_kq.reshape(HA_KV, gsz, WINDOW, WINDOW).transpose(0, 2, 1, 3).reshape(HA_KV, WINDOW, gsz * WINDOW)
    oh_s = _t5_bucket_np(WINDOW - 1 - np.arange(WINDOW))[:, None] == np.arange(NUM_BUCKETS)
    bias_s = jnp.einsum('jb,bh->hj', jnp.asarray(oh_s, F32), tab, precision=lax.Precision.HIGHEST)
    return bias_t, bias_s


def _ada_kernel(c_ref, w_ref, b_ref, o_ref):
    s = _silu(c_ref[...])
    o_ref[0] = _dot(s.astype(BF16), w_ref[0].astype(BF16)) + b_ref[0]


def _ada_call(c_all, ada_w, ada_b):
    n = c_all.shape[0]
    nb = 6
    return pl.pallas_call(
        _ada_kernel,
        grid=(DEPTH, nb),
        in_specs=[pl.BlockSpec((n, D_MODEL), lambda l, j: (0, 0)),
                  pl.BlockSpec((1, D_MODEL, D_MODEL), lambda l, j: (l, 0, j)),
                  pl.BlockSpec((1, 1, D_MODEL), lambda l, j: (l, 0, j))],
        out_specs=pl.BlockSpec((1, n, D_MODEL), lambda l, j: (l, 0, j)),
        out_shape=jax.ShapeDtypeStruct((DEPTH, n, 6 * D_MODEL), F32),
        compiler_params=_cparams(2),
        name="ada_mod",
    )(c_all, ada_w, ada_b.reshape(DEPTH, 1, 6 * D_MODEL))


def _pa_kernel(x_ref, mod_ref, n1_ref, wt_ref, wkv_ref, bias_ref, sink_ref, lowm_ref,
               mix_ref, ko_ref, vo_ref, zt_ref, kv_ref, kprev_ref, vtprev_ref, pen_ref, *, tt):
    t = pl.program_id(1)
    nchunk = tt // WINDOW
    kw = HA_KV * HD_A
    gsz = HA_Q // HA_KV

    @pl.when(t == 0)
    def _():
        kprev_ref[...] = jnp.zeros_like(kprev_ref)
        vtprev_ref[...] = jnp.zeros_like(vtprev_ref)
        pen_ref[...] = jnp.full(pen_ref.shape, NEG, F32)

    mod = mod_ref[0]
    h = _modnorm(x_ref[0], n1_ref[...], mod[:, D_MODEL:2 * D_MODEL], mod[:, 0:D_MODEL]).astype(BF16)
    zt = _dot_nt(wt_ref[...], h)
    for c in range(nchunk):
        zt_ref[c] = zt[:, WINDOW * c:WINDOW * (c + 1)]
    kv_ref[...] = _dot(h, wkv_ref[...])
    qw = gsz * WINDOW
    lower = (lax.broadcasted_iota(jnp.int32, (WINDOW, qw), 0)
             > (lax.broadcasted_iota(jnp.int32, (WINDOW, qw), 1) & (WINDOW - 1)))

    def chunk(c, carry):
        r0 = pl.multiple_of(c * WINDOW, WINDOW)
        rows = pl.ds(r0, WINDOW)
        kc = kv_ref[rows, 0:kw]
        vc = kv_ref[rows, kw:2 * kw]
        vt = vc.T
        kk = jnp.concatenate([kprev_ref[...], kc], axis=0).astype(BF16)
        vvt = jnp.concatenate([vtprev_ref[...], vt], axis=1).astype(BF16)
        qt = zt_ref[c, 0:D_MODEL, :].astype(BF16)
        pen = pen_ref[0:1, :]
        s_all = []
        for g in range(HA_KV):
            gs = slice(HD_A * g, HD_A * (g + 1))
            qcat = jnp.concatenate([qt[HD_A * (gsz * g + j):HD_A * (gsz * g + j + 1), :] for j in range(gsz)], axis=1)
            s_all.append(_dot(kk[:, gs], qcat))
        p_all = []
        for g in range(HA_KV):
            sg = jnp.where(lower, s_all[g][0:WINDOW, :] + pen, s_all[g][WINDOW:2 * WINDOW, :]) + bias_ref[g]
            sink = sink_ref[g]
            m = jnp.maximum(jnp.max(sg, axis=0, keepdims=True), sink)
            pw = jnp.exp(sg - m)
            den = jnp.sum(pw, axis=0, keepdims=True) + jnp.exp(sink - m)
            pb = pw.astype(BF16)
            p_prev = pb * lowm_ref[...]
            p_all.append((jnp.concatenate([p_prev, pb - p_prev], axis=0), 1.0 / den))
        pieces = []
        for g in range(HA_KV):
            gs = slice(HD_A * g, HD_A * (g + 1))
            p, rden = p_all[g]
            ot = _dot(vvt[gs, :], p) * rden
            pieces += [ot[:, WINDOW * j:WINDOW * (j + 1)] for j in range(gsz)]
        oat = jnp.concatenate(pieces, axis=0)
        mixt = _sigmoid(zt_ref[c, D_MODEL:2 * D_MODEL, :]) * oat
        mix_ref[0, rows, :] = mixt.T
        kprev_ref[...] = kc
        vtprev_ref[...] = vt
        pen_ref[...] = jnp.zeros_like(pen_ref)
        ko_ref[0] = kc
        vo_ref[0] = vc
        return carry

    lax.fori_loop(0, nchunk, chunk, 0, unroll=True)


def _pa_call(x, mod3, n1, wqgt, wkv, bias_t, sink_rows, tt):
    B, T, _ = x.shape
    kw = HA_KV * HD_A
    qw = (HA_Q // HA_KV) * WINDOW
    kern = functools.partial(_pa_kernel, tt=tt)
    lowm = (np.arange(WINDOW)[:, None] > (np.arange(qw)[None, :] % WINDOW)).astype(np.float32)
    return pl.pallas_call(
        kern,
        grid=(B, T // tt),
        in_specs=[pl.BlockSpec((1, tt, D_MODEL), lambda b, t: (b, t, 0)),
                  pl.BlockSpec((1, 1, 6 * D_MODEL), lambda b, t: (b, 0, 0)),
                  pl.BlockSpec((1, D_MODEL), lambda b, t: (0, 0)),
                  pl.BlockSpec((2 * D_MODEL, D_MODEL), lambda b, t: (0, 0)),
                  pl.BlockSpec((D_MODEL, 2 * kw), lambda b, t: (0, 0)),
                  pl.BlockSpec((HA_KV, WINDOW, qw), lambda b, t: (0, 0, 0)),
                  pl.BlockSpec((HA_KV, 1, qw), lambda b, t: (0, 0, 0)),
                  pl.BlockSpec((WINDOW, qw), lambda b, t: (0, 0))],
        out_specs=[pl.BlockSpec((1, tt, D_MODEL), lambda b, t: (b, t, 0)),
                   pl.BlockSpec((1, WINDOW, kw), lambda b, t: (b, 0, 0)),
                   pl.BlockSpec((1, WINDOW, kw), lambda b, t: (b, 0, 0))],
        out_shape=[jax.ShapeDtypeStruct((B, T, D_MODEL), F32),
                   jax.ShapeDtypeStruct((B, WINDOW, kw), F32),
                   jax.ShapeDtypeStruct((B, WINDOW, kw), F32)],
        scratch_shapes=[pltpu.VMEM((tt // WINDOW, 2 * D_MODEL, WINDOW), F32),
                        pltpu.VMEM((tt, 2 * kw), F32),
                        pltpu.VMEM((WINDOW, kw), F32),
                        pltpu.VMEM((kw, WINDOW), F32),
                        pltpu.VMEM((8, qw), F32)],
        compiler_params=_cparams(2),
        name="prompt_attn",
    )(x, mod3, n1, wqgt, wkv, bias_t, sink_rows, jnp.asarray(lowm, BF16))


def _pb_kernel(x_ref, mod_ref, n1_ref, wt_ref, cos_ref, sin_ref, qdec_ref, kdec_ref, dm_ref, mixa_ref,
               mix_ref, so_ref, zt_ref, s_ref, *, tt, glast):
    t = pl.program_id(1)
    nchunk = tt // WINDOW
    hw = HB * DK_B // 2
    hp = DK_B // 2

    @pl.when(t == 0)
    def _():
        s_ref[...] = jnp.zeros_like(s_ref)

    mod = mod_ref[0]
    h = _modnorm(x_ref[0], n1_ref[...], mod[:, D_MODEL:2 * D_MODEL], mod[:, 0:D_MODEL]).astype(BF16)
    for cc in range(nchunk // 2):
        z2 = _dot_nt(wt_ref[...], h[2 * WINDOW * cc:2 * WINDOW * (cc + 1), :])
        zt_ref[2 * cc] = z2[:, 0:WINDOW]
        zt_ref[2 * cc + 1] = z2[:, WINDOW:2 * WINDOW]

    def head_rows(pair, hh):
        return jnp.concatenate([pair[0][hp * hh:hp * (hh + 1), :], pair[1][hp * hh:hp * (hh + 1), :]], axis=0)

    def chunk(c, carry):
        r0 = pl.multiple_of(c * WINDOW, WINDOW)
        rows = pl.ds(r0, WINDOW)
        cos = cos_ref[c]
        sin = sin_ref[c]
        q1 = zt_ref[c, 0:hw, :]
        q2 = zt_ref[c, hw:2 * hw, :]
        k1 = zt_ref[c, 2 * hw:3 * hw, :]
        k2 = zt_ref[c, 3 * hw:4 * hw, :]
        rq = (q1 * cos - q2 * sin, q1 * sin + q2 * cos)
        rk = (k1 * cos - k2 * sin, k1 * sin + k2 * cos)
        qdec = qdec_ref[...]
        kdec = kdec_ref[...]
        qb = tuple(a.astype(BF16) for a in rq)
        kb = tuple(a.astype(BF16) for a in rk)
        qd = tuple((a * qdec).astype(BF16) for a in rq)
        kd = tuple((a * kdec).astype(BF16) for a in rk)
        vt = zt_ref[c, 4 * hw:4 * hw + HB * DV_B, :].astype(BF16)
        inner, cross, supd = [], [], []
        for hh in range(HB):
            s_old = s_ref[DK_B * hh:DK_B * (hh + 1), :]
            inner.append(_dot_tn(head_rows(qb, hh), head_rows(kb, hh)))
            cross.append(_dot_tn(s_old.astype(BF16), head_rows(qd, hh)))
            supd.append(glast[hh] * s_old + _dot_nt(head_rows(kd, hh), vt[DV_B * hh:DV_B * (hh + 1), :]))
        innd = [(inner[hh] * dm_ref[hh]).astype(BF16) for hh in range(HB)]
        outs = []
        for hh in range(HB):
            ot = _dot_nt(vt[DV_B * hh:DV_B * (hh + 1), :], innd[hh]) + cross[hh]
            s_ref[DK_B * hh:DK_B * (hh + 1), :] = supd[hh]
            ms = jnp.mean(ot * ot, axis=0, keepdims=True)
            outs.append(ot * lax.rsqrt(ms + EPS))
        obt = jnp.concatenate(outs, axis=0)
        bgt = zt_ref[c, 4 * hw + D_MODEL:4 * hw + 2 * D_MODEL, :]
        gbt = zt_ref[c, 4 * hw + 2 * D_MODEL:4 * hw + 3 * D_MODEL, :]
        mixt = _sigmoid(gbt) * (_silu(bgt) * obt)
        mix_ref[0, rows, :] = mixa_ref[0, rows, :] + mixt.T
        return carry

    lax.fori_loop(0, nchunk, chunk, 0, unroll=True)
    so_ref[0] = s_ref[...]


def _deinterleave_pairs(w):
    return w.reshape(w.shape[0], HB, DK_B // 2, 2).transpose(0, 3, 1, 2).reshape(w.shape[0], HB * DK_B)


def _rot_tables_t(T):
    theta = 1.0 / (10000.0 ** np.linspace(0.0, 1.0, DK_B // 2))
    ang = theta[:, None] * np.arange(T, dtype=np.float64)[None, :]

    def lay(a):
        a = np.tile(a, (HB, 1))
        return np.ascontiguousarray(a.reshape(a.shape[0], T // WINDOW, WINDOW).transpose(1, 0, 2)).astype(np.float32)

    return lay(np.cos(ang)), lay(np.sin(ang))


def _pb_call(x, mod3, n1, wbt, mixa, tt):
    B, T, _ = x.shape
    hw = HB * DK_B // 2
    cos, sin = _rot_tables_t(T)
    dm, qdec, kdec = _ret_tables()
    qdec_t = np.ascontiguousarray(qdec[:, ::2].T)
    kdec_t = np.ascontiguousarray(kdec[:, ::2].T)
    glast = tuple(float(v) for v in (_gammas() ** WINDOW))
    kern = functools.partial(_pb_kernel, tt=tt, glast=glast)
    full2 = lambda b, t: (0, 0)
    nct = tt // WINDOW
    return pl.pallas_call(
        kern,
        grid=(B, T // tt),
        in_specs=[pl.BlockSpec((1, tt, D_MODEL), lambda b, t: (b, t, 0)),
                  pl.BlockSpec((1, 1, 6 * D_MODEL), lambda b, t: (b, 0, 0)),
                  pl.BlockSpec((1, D_MODEL), full2),
                  pl.BlockSpec((SEG_B, D_MODEL), full2),
                  pl.BlockSpec((nct, hw, WINDOW), lambda b, t: (t, 0, 0)),
                  pl.BlockSpec((nct, hw, WINDOW), lambda b, t: (t, 0, 0)),
                  pl.BlockSpec((hw, WINDOW), full2),
                  pl.BlockSpec((hw, WINDOW), full2),
                  pl.BlockSpec((HB, WINDOW, WINDOW), lambda b, t: (0, 0, 0)),
                  pl.BlockSpec((1, tt, D_MODEL), lambda b, t: (b, t, 0))],
        out_specs=[pl.BlockSpec((1, tt, D_MODEL), lambda b, t: (b, t, 0)),
                   pl.BlockSpec((1, HB * DK_B, DV_B), lambda b, t: (b, 0, 0))],
        out_shape=[jax.ShapeDtypeStruct((B, T, D_MODEL), F32),
                   jax.ShapeDtypeStruct((B, HB * DK_B, DV_B), F32)],
        scratch_shapes=[pltpu.VMEM((nct, SEG_B, WINDOW), F32),
                        pltpu.VMEM((HB * DK_B, DV_B), F32)],
        compiler_params=_cparams(2),
        name="prompt_retention",
    )(x, mod3, n1, wbt, jnp.asarray(cos), jnp.asarray(sin), jnp.asarray(qdec_t), jnp.asarray(kdec_t),
      jnp.asarray(dm), mixa)


def _pc_kernel(x_ref, mod_ref, n1_ref, w_ref, cw_ref, cb_ref, dtb_ref, a_ref, dsk_ref, snw_ref, tri_ref,
               mixab_ref, wo_ref,
               xo_ref, ho_ref, co_ref,
               z_ref, xbuf_ref, xbc_ref, dt_ref, hst_ref, mixs_ref, *, tt):
    t = pl.program_id(1)

    @pl.when(t == 0)
    def _():
        xbuf_ref[0:8, :] = jnp.zeros((8, CONV_DIM), F32)
        hst_ref[...] = jnp.zeros_like(hst_ref)

    x = x_ref[0]
    mod = mod_ref[0]
    h = _modnorm(x, n1_ref[...], mod[:, D_MODEL:2 * D_MODEL], mod[:, 0:D_MODEL])
    z_ref[...] = _dot(h.astype(BF16), w_ref[...])

    xbuf_ref[8:8 + tt, :] = z_ref[:, 1024:2560]
    acc = cb_ref[...]
    for i in range(CONV_W):
        acc = acc + xbuf_ref[5 + i:5 + i + tt, :] * cw_ref[i:i + 1, :]
    xbc_ref[...] = _silu(acc)
    co_ref[0] = xbuf_ref[tt + 5:tt + 8, :]
    xbuf_ref[0:8, :] = xbuf_ref[tt:tt + 8, :]
    dt_ref[...] = _softplus(z_ref[:, C_DT:C_DT + 128] + dtb_ref[...])

    ii = lax.broadcasted_iota(jnp.int32, (WINDOW, WINDOW), 0)
    jj = lax.broadcasted_iota(jnp.int32, (WINDOW, WINDOW), 1)
    causal = ii >= jj
    hpg = HC // G_C

    def chunk(c, carry):
        r0 = pl.multiple_of(c * WINDOW, WINDOW)
        rows = pl.ds(r0, WINDOW)
        xc = xbc_ref[rows, 0:1024]
        bmat = xbc_ref[rows, 1024:1280]
        cmat = xbc_ref[rows, 1280:1536]
        dtc = dt_ref[rows, :]
        acum = jnp.dot(tri_ref[...], dtc * a_ref[...], precision=lax.Precision.HIGHEST,
                       preferred_element_type=F32)
        acum_t = acum.T
        dt_t = dtc.T
        x_t = xc.T
        xb = xc.astype(BF16)
        bb = bmat.astype(BF16)
        cb16 = cmat.astype(BF16)
        ys = []
        for g in range(G_C):
            ns = slice(N_C * g, N_C * (g + 1))
            cbg = _dot_nt(cb16[:, ns], bb[:, ns])
            for hh in range(hpg * g, hpg * (g + 1)):
                ps = slice(HD_C * hh, HD_C * (hh + 1))
                colb = jnp.broadcast_to(acum[:, hh:hh + 1], (WINDOW, WINDOW))
                row = acum_t[hh:hh + 1, :]
                dtrow = dt_t[hh:hh + 1, :]
                lmat = jnp.exp(jnp.where(causal, colb - row, NEG))
                m = cbg * lmat * dtrow
                hs = hst_ref[ps, :]
                cs = cmat[:, ns] * jnp.exp(colb)
                y = _dot(m.astype(BF16), xb[:, ps]) + _dot_nt(cs.astype(BF16), hs.astype(BF16))
                last = colb[WINDOW - 1:WINDOW, :]
                wrow = jnp.exp(last - row) * dtrow
                xw = (x_t[ps, :] * wrow).astype(BF16)
                hst_ref[ps, :] = jnp.exp(last) * hs + _dot(xw, bb[:, ns])
                ys.append(y)
        y = jnp.concatenate(ys, axis=1) + dsk_ref[...] * xc
        yc = y * _silu(z_ref[rows, 0:1024])
        gw = D_MODEL // G_C
        oc = jnp.concatenate([_rms_lanes(yc[:, gw * g:gw * (g + 1)]) for g in range(G_C)], axis=1) * snw_ref[...]
        gc = _sigmoid(z_ref[rows, C_GC:C_GC + 1024])
        mixs_ref[rows, :] = (mixab_ref[0, rows, :] + gc * oc).astype(BF16)
        return carry

    lax.fori_loop(0, tt // WINDOW, chunk, 0, unroll=True)
    ho_ref[0] = hst_ref[...]
    g1 = mod[:, 2 * D_MODEL:3 * D_MODEL]
    xo_ref[0] = x + g1 * _dot(mixs_ref[...], wo_ref[...])


def _pc_call(x, mod3, n1, wc, cw, cb, dtb_pad, a_pad, dsk_full, snw, mixab, wo, tt):
    B, T, _ = x.shape
    tri = jnp.asarray(np.tril(np.ones((WINDOW, WINDOW), np.float32)))
    kern = functools.partial(_pc_kernel, tt=tt)
    full2 = lambda b, t: (0, 0)
    return pl.pallas_call(
        kern,
        grid=(B, T // tt),
        in_specs=[pl.BlockSpec((1, tt, D_MODEL), lambda b, t: (b, t, 0)),
                  pl.BlockSpec((1, 1, 6 * D_MODEL), lambda b, t: (b, 0, 0)),
                  pl.BlockSpec((1, D_MODEL), full2),
                  pl.BlockSpec((D_MODEL, SEG_C), full2),
                  pl.BlockSpec((CONV_W, CONV_DIM), full2),
                  pl.BlockSpec((1, CONV_DIM), full2),
                  pl.BlockSpec((1, 128), full2),
                  pl.BlockSpec((1, 128), full2),
                  pl.BlockSpec((1, D_MODEL), full2),
                  pl.BlockSpec((1, D_MODEL), full2),
                  pl.BlockSpec((WINDOW, WINDOW), full2),
                  pl.BlockSpec((1, tt, D_MODEL), lambda b, t: (b, t, 0)),
                  pl.BlockSpec((D_MODEL, D_MODEL), full2)],
        out_specs=[pl.BlockSpec((1, tt, D_MODEL), lambda b, t: (b, t, 0)),
                   pl.BlockSpec((1, HC * HD_C, N_C), lambda b, t: (b, 0, 0)),
                   pl.BlockSpec((1, CONV_W - 1, CONV_DIM), lambda b, t: (b, 0, 0))],
        out_shape=[jax.ShapeDtypeStruct((B, T, D_MODEL), F32),
                   jax.ShapeDtypeStruct((B, HC * HD_C, N_C), F32),
                   jax.ShapeDtypeStruct((B, CONV_W - 1, CONV_DIM), F32)],
        scratch_shapes=[pltpu.VMEM((tt, SEG_C), F32),
                        pltpu.VMEM((tt + 8, CONV_DIM), F32),
                        pltpu.VMEM((tt, CONV_DIM), F32),
                        pltpu.VMEM((tt, 128), F32),
                        pltpu.VMEM((HC * HD_C, N_C), F32),
                        pltpu.VMEM((tt, D_MODEL), BF16)],
        compiler_params=_cparams(2),
        name="prompt_ssd_out",
    )(x, mod3, n1, wc, cw, cb, dtb_pad, a_pad, dsk_full, snw, tri, mixab, wo)


def _mlp_kernel(x_ref, sh_ref, sc_ref, g_ref, n2_ref, wu_ref, wd_ref, fn_ref, o_ref, h_ref, acc_ref,
                *, nf, final, per_row):
    f = pl.program_id(1)
    rd = (lambda r: r[...]) if per_row else (lambda r: r[0])

    @pl.when(f == 0)
    def _():
        h_ref[...] = _modnorm(x_ref[...], n2_ref[...], rd(sc_ref), rd(sh_ref)).astype(BF16)
        acc_ref[...] = jnp.zeros_like(acc_ref)

    u = _dot(h_ref[...], wu_ref[0])
    u = jnp.square(jnp.maximum(u, 0.0)).astype(BF16)
    acc_ref[...] += _dot(u, wd_ref[0])

    @pl.when(f == nf - 1)
    def _():
        y = x_ref[...] + rd(g_ref) * acc_ref[...]
        if final:
            y = _rms_lanes(y) * fn_ref[...]
        o_ref[...] = y


def _mlp_call(x2, mod, n2, wu, wd, fn, layer, tm, tf, rows_per_mod, final):
    M = x2.shape[0]
    nf = D_FF // tf
    per_row = rows_per_mod == 1
    if per_row:
        mspec = lambda j: pl.BlockSpec((tm, D_MODEL), lambda m, f: (m, j))
    else:
        mspec = lambda j: pl.BlockSpec((1, 1, D_MODEL), lambda m, f: ((m * tm) // rows_per_mod, 0, j))
    kern = functools.partial(_mlp_kernel, nf=nf, final=final, per_row=per_row)
    return pl.pallas_call(
        kern,
        grid=(M // tm, nf),
        in_specs=[pl.BlockSpec((tm, D_MODEL), lambda m, f: (m, 0)),
                  mspec(3), mspec(4), mspec(5),
                  pl.BlockSpec((1, D_MODEL), lambda m, f: (0, 0)),
                  pl.BlockSpec((1, D_MODEL, tf), lambda m, f: (layer, 0, f)),
                  pl.BlockSpec((1, tf, D_MODEL), lambda m, f: (layer, f, 0)),
                  pl.BlockSpec((1, D_MODEL), lambda m, f: (0, 0))],
        out_specs=pl.BlockSpec((tm, D_MODEL), lambda m, f: (m, 0)),
        out_shape=jax.ShapeDtypeStruct((M, D_MODEL), F32),
        scratch_shapes=[pltpu.VMEM((tm, D_MODEL), BF16),
                        pltpu.VMEM((tm, D_MODEL), F32)],
        compiler_params=_cparams(2),
        name="mlp",
    )(x2, mod, mod, mod, n2, wu, wd, fn)


def _sproj_kernel(x_ref, sh_ref, sc_ref, n1_ref, w_ref, o_ref):
    h = _modnorm(x_ref[...], n1_ref[...], sc_ref[...], sh_ref[...])
    o_ref[...] = _dot(h.astype(BF16), w_ref[...])


def _sproj_call(xs, mod_s, n1, w):
    n, width = xs.shape[0], w.shape[1]
    return pl.pallas_call(
        _sproj_kernel,
        grid=(1,),
        in_specs=[pl.BlockSpec((n, D_MODEL), lambda i: (0, 0)),
                  pl.BlockSpec((n, D_MODEL), lambda i: (0, 0)),
                  pl.BlockSpec((n, D_MODEL), lambda i: (0, 1)),
                  pl.BlockSpec((1, D_MODEL), lambda i: (0, 0)),
                  pl.BlockSpec((D_MODEL, width), lambda i: (0, 0))],
        out_specs=pl.BlockSpec((n, width), lambda i: (0, 0)),
        out_shape=jax.ShapeDtypeStruct((n, width), F32),
        compiler_params=_cparams(1),
        name="sample_proj",
    )(xs, mod_s, mod_s, n1, w)


def _carry_outputs(prev, n_in, nsteps):
    nslab = 1 if prev else DEPTH
    extra = [pl.BlockSpec(memory_space=pl.ANY) for _ in prev]
    alias = {n_in + k: k for k in range(len(prev))}
    row = lambda p, i: jnp.where(p == 0, i, nsteps - 1)
    return nslab, extra, alias, row


def _with_fill(body, state_outs):
    def kern(*refs):
        p = pl.program_id(0)

        @pl.when(p == 0)
        def _():
            body(*refs)

        @pl.when(p != 0)
        def _():
            for k in state_outs:
                refs[k][...] = jnp.zeros_like(refs[k])

    return kern


def _sa_kernel(q_ref, kv_ref, ck_ref, cv_ref, bias_ref, sink_ref, *rest):
    ko_ref, vo_ref, oa_ref = rest[-3:]
    nb = ROWS_PER_SAMPLE_STEP
    gsz = HA_Q // HA_KV
    rg = lax.broadcasted_iota(jnp.int32, (HA_Q, HD_A), 0) // gsz
    sink = sink_ref[...]
    kw = HA_KV * HD_A
    scores, vmats = [], []
    for i in range(nb):
        ko_ref[0, i, 0:WINDOW - 1, :] = ck_ref[0, i, 1:WINDOW, :]
        ko_ref[0, i, WINDOW - 1:WINDOW, :] = kv_ref[i:i + 1, 0:kw]
        vo_ref[0, i, 0:WINDOW - 1, :] = cv_ref[0, i, 1:WINDOW, :]
        vo_ref[0, i, WINDOW - 1:WINDOW, :] = kv_ref[i:i + 1, kw:2 * kw]
        kmat = ko_ref[0, i].astype(BF16)
        vmats.append(vo_ref[0, i].astype(BF16))
        q = q_ref[i] * (HD_A ** -0.5)
        qe = jnp.concatenate([jnp.where(rg == g, q, 0.0) for g in range(HA_KV)], axis=1).astype(BF16)
        scores.append(_dot_nt(qe, kmat))
    probs = []
    for i in range(nb):
        sc = scores[i] + bias_ref[...]
        m = jnp.maximum(jnp.max(sc, axis=-1, keepdims=True), sink)
        p = jnp.exp(sc - m)
        den = jnp.sum(p, axis=-1, keepdims=True) + jnp.exp(sink - m)
        probs.append((p.astype(BF16), den))
    outs = [_dot(probs[i][0], vmats[i]) for i in range(nb)]
    for i in range(nb):
        o = outs[i] / probs[i][1]
        o16 = jnp.zeros((HA_Q, HD_A), F32)
        for g in range(HA_KV):
            o16 = o16 + jnp.where(rg == g, o[:, HD_A * g:HD_A * (g + 1)], 0.0)
        oa_ref[i] = o16


def _sa_call(q3, kv, cache_k, cache_v, bias_s, sink_col, layer, prev):
    n = q3.shape[0]
    nb = ROWS_PER_SAMPLE_STEP
    kw = HA_KV * HD_A
    nslab, extra, alias, row = _carry_outputs(prev, 6, n // nb)
    cspec = pl.BlockSpec((1, nb, WINDOW, kw), lambda p, i: (layer, row(p, i), 0, 0))
    ospec = pl.BlockSpec((1, nb, WINDOW, kw), lambda p, i: ((layer + p) % DEPTH, i, 0, 0))
    return pl.pallas_call(
        _with_fill(_sa_kernel, (-3, -2)),
        grid=(nslab, n // nb),
        in_specs=[pl.BlockSpec((nb, HA_Q, HD_A), lambda p, i: (row(p, i), 0, 0)),
                  pl.BlockSpec((nb, 2 * kw), lambda p, i: (row(p, i), 0)),
                  cspec, cspec,
                  pl.BlockSpec((HA_Q, WINDOW), lambda p, i: (0, 0)),
                  pl.BlockSpec((HA_Q, 1), lambda p, i: (0, 0))] + extra,
        out_specs=[ospec, ospec, pl.BlockSpec((nb, HA_Q, HD_A), lambda p, i: (row(p, i), 0, 0))],
        out_shape=[jax.ShapeDtypeStruct((DEPTH, n, WINDOW, kw), F32),
                   jax.ShapeDtypeStruct((DEPTH, n, WINDOW, kw), F32),
                   jax.ShapeDtypeStruct((n, HA_Q, HD_A), F32)],
        input_output_aliases=alias,
        compiler_params=_cparams(2),
        name="sample_attn",
    )(q3, kv, cache_k, cache_v, bias_s, sink_col, *prev)


def _sr_kernel(zb_ref, cos_ref, sin_ref, gcol_ref, s_ref, *rest):
    so_ref, o_ref = rest[-2:]
    nb = ROWS_PER_SAMPLE_STEP
    wq = HB * DK_B
    cos = cos_ref[...]
    sin = sin_ref[...]
    qf = zb_ref[:, 0:wq]
    kf = zb_ref[:, wq:2 * wq]
    qr = qf * cos + _pairswap(qf) * sin
    kr = (kf * cos + _pairswap(kf) * sin) * (DK_B ** -0.5)
    v = zb_ref[:, 2 * wq:2 * wq + HB * DV_B]
    r8 = lax.broadcasted_iota(jnp.int32, (HB, wq), 0)
    hl = lax.broadcasted_iota(jnp.int32, (HB, wq), 1) // DK_B
    rv = lax.broadcasted_iota(jnp.int32, (HB, DV_B), 0)
    outers, q8s = [], []
    for i in range(nb):
        k8 = jnp.where(hl == r8, jnp.broadcast_to(kr[i:i + 1, :], (HB, wq)), 0.0).astype(BF16)
        q8s.append(jnp.where(hl == r8, jnp.broadcast_to(qr[i:i + 1, :], (HB, wq)), 0.0).astype(BF16))
        v8 = jnp.zeros((HB, DV_B), F32)
        for r in range(HB):
            v8 = jnp.where(rv == r, jnp.broadcast_to(v[i:i + 1, DV_B * r:DV_B * (r + 1)], (HB, DV_B)), v8)
        outers.append(_dot_tn(k8, v8.astype(BF16)))
    s_news = []
    for i in range(nb):
        s_new = gcol_ref[...] * s_ref[0, i] + outers[i]
        so_ref[0, i] = s_new
        s_news.append(s_new.astype(BF16))
    for i in range(nb):
        o_ref[i] = _dot(q8s[i], s_news[i])


def _sr_call(zb, state, layer, prev):
    n = zb.shape[0]
    nb = ROWS_PER_SAMPLE_STEP
    wq = HB * DK_B
    cos, sin = _rot_tables(np.array([PAST_LEN]))
    gcol = np.repeat(_gammas(), DK_B)[:, None] * np.ones((1, DV_B))
    nslab, extra, alias, row = _carry_outputs(prev, 5, n // nb)
    full = lambda p, i: (0, 0)
    return pl.pallas_call(
        _with_fill(_sr_kernel, (-2,)),
        grid=(nslab, n // nb),
        in_specs=[pl.BlockSpec((nb, SEG_B), lambda p, i: (row(p, i), 0)),
                  pl.BlockSpec((1, wq), full),
                  pl.BlockSpec((1, wq), full),
                  pl.BlockSpec((wq, DV_B), full),
                  pl.BlockSpec((1, nb, wq, DV_B), lambda p, i: (layer, row(p, i), 0, 0))] + extra,
        out_specs=[pl.BlockSpec((1, nb, wq, DV_B), lambda p, i: ((layer + p) % DEPTH, i, 0, 0)),
                   pl.BlockSpec((nb, HB, DV_B), lambda p, i: (row(p, i), 0, 0))],
        out_shape=[jax.ShapeDtypeStruct((DEPTH, n, wq, DV_B), F32),
                   jax.ShapeDtypeStruct((n, HB, DV_B), F32)],
        input_output_aliases=alias,
        compiler_params=_cparams(2),
        name="sample_retention",
    )(zb, jnp.asarray(cos), jnp.asarray(sin), jnp.asarray(gcol.astype(np.float32)), state, *prev)


def _ss_kernel(zc_ref, cs_ref, cw_ref, cb_ref, dtb_ref, a_ref, e3_ref, h_ref, *rest):
    ho_ref, cso_ref, y_ref, xc_ref = rest[-4:]
    nb = ROWS_PER_SAMPLE_STEP
    cx = zc_ref[:, 1024:2560]
    taps = [cs_ref[0, :, CONV_DIM * i:CONV_DIM * (i + 1)] for i in range(CONV_W - 1)] + [cx]
    acc = cb_ref[...]
    for i in range(CONV_W):
        acc = acc + taps[i] * cw_ref[i:i + 1, :]
    xbc = _silu(acc)
    cso_ref[...] = jnp.concatenate(taps[1:], axis=1)
    xc = xbc[:, 0:1024]
    bmat = xbc[:, 1024:1280]
    cmat = xbc[:, 1280:1536]
    dt = _softplus(zc_ref[:, C_DT:C_DT + 128] + dtb_ref[...])
    da = jnp.exp(dt * a_ref[...])
    dt_e = _dot(jnp.concatenate(_split3(dt), axis=1), e3_ref[...])
    da_e = _dot(jnp.concatenate(_split3(da), axis=1), e3_ref[...])
    dtx = dt_e * xc
    gw = (HC // G_C) * HD_C
    r8 = lax.broadcasted_iota(jnp.int32, (nb, gw), 0)
    rn = lax.broadcasted_iota(jnp.int32, (nb, N_C), 0)
    ones8 = jnp.ones((nb, N_C), BF16)
    prods = {}
    for g in range(G_C):
        ws = slice(gw * g, gw * (g + 1))
        bg16 = bmat[:, N_C * g:N_C * (g + 1)].astype(BF16)
        for i in range(nb):
            x8 = jnp.where(r8 == i, dtx[:, ws], 0.0).astype(BF16)
            outer = _dot_tn(x8, bg16)
            hi, mid, lo = (p.astype(F32) for p in _split3(jnp.broadcast_to(da_e[i:i + 1, ws], (nb, gw))))
            l3 = jnp.where(r8 == 0, hi, jnp.where(r8 == 1, mid, jnp.where(r8 == 2, lo, 0.0)))
            prods[g, i] = (outer, _dot_tn(l3.astype(BF16), ones8))
    h16 = {}
    for g in range(G_C):
        ws = slice(gw * g, gw * (g + 1))
        for i in range(nb):
            outer, dacol = prods[g, i]
            h_new = dacol * h_ref[0, i, ws, :] + outer
            ho_ref[0, i, ws, :] = h_new
            h16[g, i] = h_new.astype(BF16)
    ycols = []
    for g in range(G_C):
        ns = slice(N_C * g, N_C * (g + 1))
        yacc = jnp.zeros((nb, gw), F32)
        for i in range(nb):
            c8 = jnp.where(rn == i, cmat[:, ns], 0.0).astype(BF16)
            yacc = yacc + _dot_nt(c8, h16[g, i])
        ycols.append(yacc)
    y_ref[...] = jnp.concatenate(ycols, axis=1)
    xc_ref[...] = xc


def _ss_call(zc, conv_state, hstate, cw, cb, dtb_pad, a_pad, layer, prev):
    n = zc.shape[0]
    nb = ROWS_PER_SAMPLE_STEP
    e = np.zeros((128, D_MODEL), np.float32)
    for hh in range(HC):
        e[hh, HD_C * hh:HD_C * (hh + 1)] = 1.0
    e3 = jnp.asarray(np.concatenate([e, e, e], axis=0), dtype=BF16)
    cwid = (CONV_W - 1) * CONV_DIM
    full = lambda p, i: (0, 0)
    nslab, extra, alias, row = _carry_outputs(prev, 8, n // nb)
    return pl.pallas_call(
        _with_fill(_ss_kernel, (-4,)),
        grid=(nslab, n // nb),
        in_specs=[pl.BlockSpec((nb, SEG_C), lambda p, i: (row(p, i), 0)),
                  pl.BlockSpec((1, nb, cwid), lambda p, i: (layer, row(p, i), 0)),
                  pl.BlockSpec((CONV_W, CONV_DIM), full),
                  pl.BlockSpec((1, CONV_DIM), full),
                  pl.BlockSpec((1, 128), full),
                  pl.BlockSpec((1, 128), full),
                  pl.BlockSpec((3 * 128, D_MODEL), full),
                  pl.BlockSpec((1, nb, HC * HD_C, N_C), lambda p, i: (layer, row(p, i), 0, 0))] + extra,
        out_specs=[pl.BlockSpec((1, nb, HC * HD_C, N_C), lambda p, i: ((layer + p) % DEPTH, i, 0, 0)),
                   pl.BlockSpec((nb, cwid), lambda p, i: (row(p, i), 0)),
                   pl.BlockSpec((nb, D_MODEL), lambda p, i: (row(p, i), 0)),
                   pl.BlockSpec((nb, D_MODEL), lambda p, i: (row(p, i), 0))],
        out_shape=[jax.ShapeDtypeStruct((DEPTH, n, HC * HD_C, N_C), F32),
                   jax.ShapeDtypeStruct((n, cwid), F32),
                   jax.ShapeDtypeStruct((n, D_MODEL), F32),
                   jax.ShapeDtypeStruct((n, D_MODEL), F32)],
        input_output_aliases=alias,
        compiler_params=_cparams(2),
        name="sample_ssd",
    )(zc, conv_state, cw, cb, dtb_pad, a_pad, e3, hstate, *prev)


def _sm_kernel(x_ref, g1_ref, oa_ref, oret_ref, y_ref, xc_ref, za_ref, zb_ref, zc_ref,
               dsk_ref, snw_ref, wo_ref, o_ref):
    ob = jnp.concatenate([_rms_lanes(oret_ref[:, DV_B * hh:DV_B * (hh + 1)]) for hh in range(HB)], axis=1)
    ob = _silu(zb_ref[:, 2048:3072]) * ob
    yc = (y_ref[...] + dsk_ref[...] * xc_ref[...]) * _silu(zc_ref[:, 0:1024])
    gw = D_MODEL // G_C
    oc = jnp.concatenate([_rms_lanes(yc[:, gw * g:gw * (g + 1)]) for g in range(G_C)], axis=1) * snw_ref[...]
    mix = (_sigmoid(za_ref[:, 1536:2560]) * oa_ref[...] + _sigmoid(zb_ref[:, 3072:4096]) * ob
           + _sigmoid(zc_ref[:, C_GC:C_GC + 1024]) * oc)
    o_ref[...] = x_ref[...] + g1_ref[...] * _dot(mix.astype(BF16), wo_ref[...])


def _sm_call(xs, mod_s, oa, oret, y, xc, za, zb, zc, dsk_full, snw, wo):
    n = xs.shape[0]
    full = lambda i: (0, 0)
    row = lambda w: pl.BlockSpec((n, w), full)
    return pl.pallas_call(
        _sm_kernel,
        grid=(1,),
        in_specs=[row(D_MODEL),
                  pl.BlockSpec((n, D_MODEL), lambda i: (0, 2)),
                  row(D_MODEL), row(D_MODEL), row(D_MODEL), row(D_MODEL),
                  row(SEG_A), row(SEG_B), row(SEG_C),
                  pl.BlockSpec((1, D_MODEL), full),
                  pl.BlockSpec((1, D_MODEL), full),
                  pl.BlockSpec((D_MODEL, D_MODEL), full)],
        out_specs=row(D_MODEL),
        out_shape=jax.ShapeDtypeStruct((n, D_MODEL), F32),
        compiler_params=_cparams(1),
        name="sample_merge_out",
    )(xs, mod_s, oa, oret, y, xc, za, zb, zc, dsk_full, snw, wo)


PROMPT_TILE = 512
MLP_ROWS = 1024
MLP_FF = 1024


def _prep_w_in(w):
    wa = jnp.concatenate([w[:, O_AQ:O_BQ], w[:, O_GTS:O_GTS + 1024]], axis=1)
    wb = jnp.concatenate([w[:, O_BQ:O_CZ], w[:, O_GTS + 1024:O_GTS + 2048]], axis=1)
    wc = jnp.concatenate([w[:, O_CZ:O_CDT], jnp.pad(w[:, O_CDT:O_GTS], ((0, 0), (0, 128 - HC))),
                          w[:, O_GTS + 2048:O_GTS + 3072]], axis=1)
    return wa, wb, wc


def _forward(x_prompt, x_sample, cache_win_k, cache_win_v, state_ret, state_ssm, state_conv,
             c_prompt, c_sample, rel_bias_table, attn_sinks, norm1_w, norm2_w, ada_w, ada_b,
             w_in, conv_w, conv_b, dt_bias, A_log, D_skip, ssm_norm_w, w_out, w_up, w_down,
             final_norm_w, *, prompt_tile, mlp_rows, mlp_ff):
    B, T, _ = x_prompt.shape
    DB = x_sample.shape[0]
    kw = HA_KV * HD_A

    mod_all = _ada_call(jnp.concatenate([c_prompt, c_sample], axis=0), ada_w, ada_b)

    bias_t, bias_s = _bias_tables(rel_bias_table)

    w16 = w_in.astype(BF16)
    wu16 = w_up.astype(BF16)
    wd16 = w_down.astype(BF16)
    wo16 = w_out.astype(BF16)
    fn = final_norm_w.reshape(1, D_MODEL)
    ck = cache_win_k.reshape(DEPTH, DB, WINDOW, kw)
    cv = cache_win_v.reshape(DEPTH, DB, WINDOW, kw)
    sret = state_ret.reshape(DEPTH, DB, HB * DK_B, DV_B)
    sssm = state_ssm.reshape(DEPTH, DB, HC * HD_C, N_C)
    sconv = state_conv.reshape(DEPTH, DB, (CONV_W - 1) * CONV_DIM)

    xp = x_prompt
    xs = x_sample.reshape(DB, D_MODEL)
    outs_p = [[] for _ in range(5)]
    conv_s = []
    win_kv, ret_all, ssm_all = (), (), ()
    for l in range(DEPTH):
        wl = w16[l]
        wa, wb, wc = _prep_w_in(wl)
        n1 = norm1_w[l].reshape(1, D_MODEL)
        n2 = norm2_w[l].reshape(1, D_MODEL)
        cw = conv_w[l]
        cb = conv_b[l].reshape(1, CONV_DIM)
        dtb_pad = jnp.pad(dt_bias[l], (0, 128 - HC)).reshape(1, 128)
        a_pad = jnp.pad(-jnp.exp(A_log[l].astype(F32)), (0, 128 - HC)).reshape(1, 128)
        dsk_full = jnp.repeat(D_skip[l], HD_C).reshape(1, D_MODEL)
        snw = ssm_norm_w[l].reshape(1, D_MODEL)
        final = l == DEPTH - 1
        mod_p = mod_all[l, :B].reshape(B, 1, 6 * D_MODEL)
        mod_s = mod_all[l, B:]

        wqgt = jnp.concatenate([wl[:, O_AQ:O_AK] * (HD_A ** -0.5), wl[:, O_GTS:O_GTS + D_MODEL]], axis=1).T
        wkv = wl[:, O_AK:O_BQ]
        sink_rows = jnp.repeat(attn_sinks[l], WINDOW).reshape(HA_KV, 1, (HA_Q // HA_KV) * WINDOW)
        mixa, kbuf, vbuf = _pa_call(xp, mod_p, n1, wqgt, wkv, bias_t, sink_rows, prompt_tile)
        wbt = jnp.concatenate([_deinterleave_pairs(wl[:, O_BQ:O_BK]),
                               _deinterleave_pairs(wl[:, O_BK:O_BV]) * (DK_B ** -0.5), wl[:, O_BV:O_CZ],
                               wl[:, O_GTS + D_MODEL:O_GTS + 2 * D_MODEL]], axis=1).T
        mixab, s_perm = _pb_call(xp, mod_p, n1, wbt, mixa, prompt_tile)
        s_ret = s_perm.reshape(B, HB, 2, DK_B // 2, DV_B).transpose(0, 1, 3, 2, 4)
        x1, h_ssm, conv_new = _pc_call(xp, mod_p, n1, wc, cw, cb, dtb_pad, a_pad, dsk_full, snw,
                                       mixab, wo16[l], prompt_tile)
        xp = _mlp_call(x1.reshape(B * T, D_MODEL), mod_p, n2, wu16, wd16, fn, l,
                       mlp_rows, mlp_ff, T, final).reshape(B, T, D_MODEL)
        for lst, v in zip(outs_p, (kbuf.reshape(B, WINDOW, HA_KV, HD_A), vbuf.reshape(B, WINDOW, HA_KV, HD_A),
                                   s_ret.reshape(B, HB, DK_B, DV_B), h_ssm.reshape(B, HC, HD_C, N_C), conv_new)):
            lst.append(v)

        za = _sproj_call(xs, mod_s, n1, wa)
        zb = _sproj_call(xs, mod_s, n1, wb)
        zc = _sproj_call(xs, mod_s, n1, wc)
        q3 = za[:, 0:HA_Q * HD_A].reshape(DB, HA_Q, HD_A)
        kv = za[:, HA_Q * HD_A:HA_Q * HD_A + 2 * kw]
        ck_new, cv_new, oa3 = _sa_call(q3, kv, ck, cv, bias_s, attn_sinks[l].reshape(HA_Q, 1), l, win_kv)
        win_kv = (ck_new, cv_new)
        s_new, o3 = _sr_call(zb, sret, l, ret_all)
        ret_all = (s_new,)
        h_new, cs_new, y_s, xc_s = _ss_call(zc, sconv, sssm, cw, cb, dtb_pad, a_pad, l, ssm_all)
        ssm_all = (h_new,)
        xs1 = _sm_call(xs, mod_s, oa3.reshape(DB, D_MODEL), o3.reshape(DB, D_MODEL), y_s, xc_s,
                       za, zb, zc, dsk_full, snw, wo16[l])
        xs = _mlp_call(xs1, mod_s, n2, wu16, wd16, fn, l, DB, mlp_ff, 1, final)
        conv_s.append(cs_new.reshape(DB, CONV_W - 1, CONV_DIM))

    stk = lambda lst: jnp.stack(lst, axis=0)
    return (xp, xs.reshape(DB, 1, D_MODEL),
            stk(outs_p[0]), stk(outs_p[1]), stk(outs_p[2]), stk(outs_p[3]), stk(outs_p[4]),
            win_kv[0].reshape(DEPTH, DB, WINDOW, HA_KV, HD_A), win_kv[1].reshape(DEPTH, DB, WINDOW, HA_KV, HD_A),
            ret_all[0].reshape(DEPTH, DB, HB, DK_B, DV_B), ssm_all[0].reshape(DEPTH, DB, HC, HD_C, N_C),
            stk(conv_s))


def kernel(x_prompt, x_sample, cache_win_k, cache_win_v, state_ret, state_ssm, state_conv, c_prompt, c_sample,
           rel_bias_table, attn_sinks, norm1_w, norm2_w, ada_w, ada_b, w_in, conv_w, conv_b, dt_bias, A_log,
           D_skip, ssm_norm_w, w_out, w_up, w_down, final_norm_w):
    return _forward(x_prompt, x_sample, cache_win_k, cache_win_v, state_ret, state_ssm, state_conv,
                    c_prompt, c_sample, rel_bias_table, attn_sinks, norm1_w, norm2_w, ada_w, ada_b,
                    w_in, conv_w, conv_b, dt_bias, A_log, D_skip, ssm_norm_w, w_out, w_up, w_down,
                    final_norm_w, prompt_tile=PROMPT_TILE, mlp_rows=MLP_ROWS, mlp_ff=MLP_FF)
```

```python
import functools
import math

import numpy as np
import jax
import jax.numpy as jnp
from jax import lax
from jax.experimental import pallas as pl
from jax.experimental.pallas import tpu as pltpu

F32 = jnp.float32
BF16 = jnp.bfloat16

D_MODEL = 1024
DEPTH = 2
PAST_LEN = 16384
WINDOW = 128
HA_Q = 16
HA_KV = 4
HD_A = 64
NUM_BUCKETS = 32
MAX_DISTANCE = WINDOW
HB = 8
DK_B = 64
DV_B = 128
HC = 16
HD_C = 64
N_C = 128
G_C = 2
CONV_W = 4
CONV_DIM = D_MODEL + 2 * G_C * N_C
D_FF = 4 * D_MODEL
EPS = 1e-6
NEG = -1e30
LOG2E = 1.4426950408889634

O_AQ, O_AK, O_AV = 0, 1024, 1280
O_BQ, O_BK, O_BV, O_BG = 1536, 2048, 2560, 3584
O_CZ, O_CXBC, O_CDT, O_GTS = 4608, 5632, 7168, 7184
SEG_A = 2560
SEG_B = 4096
SEG_C = 3712
C_DT = 2560
C_GC = 2688

VMEM_LIMIT_V7X = 56 * 1024 * 1024
ROWS_PER_SAMPLE_STEP = 8


def _cparams(n_axes):
    return pltpu.CompilerParams(dimension_semantics=("arbitrary",) * n_axes,
                                vmem_limit_bytes=VMEM_LIMIT_V7X)


def _dot(a, b):
    return jnp.dot(a, b, preferred_element_type=F32)


def _dot_nt(a, b):
    return lax.dot_general(a, b, (((1,), (1,)), ((), ())), preferred_element_type=F32)


def _dot_tn(a, b):
    return lax.dot_general(a, b, (((0,), (0,)), ((), ())), preferred_element_type=F32)


def _sigmoid(x):
    return 0.5 * (jnp.tanh(0.5 * x) + 1.0)


def _silu(x):
    return x * _sigmoid(x)


def _softplus(x):
    return jnp.maximum(x, 0.0) + jnp.log1p(jnp.exp(-jnp.abs(x)))


def _modnorm(x, nw, sc, sh):
    ms = jnp.mean(x * x, axis=-1, keepdims=True)
    return (x * lax.rsqrt(ms + EPS) * nw) * (1.0 + sc) + sh


def _rms_lanes(x):
    ms = jnp.mean(x * x, axis=-1, keepdims=True)
    return x * lax.rsqrt(ms + EPS)


def _pairswap(x):
    ax = x.ndim - 1
    n = x.shape[ax]
    lane = lax.broadcasted_iota(jnp.int32, x.shape, ax)
    nxt = pltpu.roll(x, n - 1, ax)
    prv = pltpu.roll(x, 1, ax)
    return jnp.where((lane & 1) == 0, nxt, prv)


def _split3(x):
    hi = x.astype(BF16)
    r1 = x - hi.astype(F32)
    mid = r1.astype(BF16)
    lo = (r1 - mid.astype(F32)).astype(BF16)
    return hi, mid, lo


def _gammas():
    return 1.0 - 2.0 ** (-5.0 - np.arange(HB, dtype=np.float64))


def _rot_tables(pos):
    theta = 1.0 / (10000.0 ** np.linspace(0.0, 1.0, DK_B // 2))
    ang = np.asarray(pos, np.float64)[:, None] * theta[None, :]
    cos = np.repeat(np.cos(ang), 2, axis=1)
    sin = np.repeat(np.sin(ang), 2, axis=1)
    sin[:, 0::2] *= -1.0
    return (np.tile(cos, (1, HB)).astype(np.float32), np.tile(sin, (1, HB)).astype(np.float32))


def _ret_tables():
    g = _gammas()
    L = WINDOW
    i = np.arange(L, dtype=np.float64)
    diff = i[:, None] - i[None, :]
    dm = np.where(diff >= 0, g[:, None, None] ** np.maximum(diff, 0.0), 0.0)
    qdec = np.repeat(g[None, :] ** (i[:, None] + 1.0), DK_B, axis=1)
    kdec = np.repeat(g[None, :] ** (L - 1.0 - i[:, None]), DK_B, axis=1)
    return dm.astype(np.float32), qdec.astype(np.float32), kdec.astype(np.float32)


def _t5_bucket_np(dist):
    max_exact = NUM_BUCKETS // 2
    n = np.maximum(dist, 0)
    nf = np.maximum(n, 1).astype(np.float32)
    large = max_exact + (np.log(nf / np.float32(max_exact)) / np.float32(math.log(MAX_DISTANCE / max_exact))
                         * np.float32(NUM_BUCKETS - max_exact)).astype(np.int32)
    large = np.minimum(large, NUM_BUCKETS - 1)
    return np.where(n < max_exact, n, large)


def _bias_tables(rel_table):
    gsz = HA_Q // HA_KV
    qi = np.arange(WINDOW)[None, :]
    kj = np.arange(WINDOW)[:, None]
    dist = np.where(kj > qi, qi + WINDOW - kj, qi - kj)
    onehot = _t5_bucket_np(dist)[..., None] == np.arange(NUM_BUCKETS)
    tab = rel_table.astype(F32)
    bias_kq = jnp.einsum('kqb,bh->hkq', jnp.asarray(onehot, F32), tab, precision=lax.Precision.HIGHEST)
    bias_t = bias_kq.reshape(HA_KV, gsz, WINDOW, WINDOW).transpose(0, 2, 1, 3).reshape(HA_KV, WINDOW, gsz * WINDOW)
    oh_s = _t5_bucket_np(WINDOW - 1 - np.arange(WINDOW))[:, None] == np.arange(NUM_BUCKETS)
    bias_s = jnp.einsum('jb,bh->hj', jnp.asarray(oh_s, F32), tab, precision=lax.Precision.HIGHEST)
    return bias_t, bias_s


def _ada_kernel(c_ref, w_ref, b_ref, o_ref):
    s = _silu(c_ref[...])
    o_ref[0] = _dot(s.astype(BF16), w_ref[0].astype(BF16)) + b_ref[0]


def _ada_call(c_all, ada_w, ada_b):
    n = c_all.shape[0]
    nb = 6
    return pl.pallas_call(
        _ada_kernel,
        grid=(DEPTH, nb),
        in_specs=[pl.BlockSpec((n, D_MODEL), lambda l, j: (0, 0)),
                  pl.BlockSpec((1, D_MODEL, D_MODEL), lambda l, j: (l, 0, j)),
                  pl.BlockSpec((1, 1, D_MODEL), lambda l, j: (l, 0, j))],
        out_specs=pl.BlockSpec((1, n, D_MODEL), lambda l, j: (l, 0, j)),
        out_shape=jax.ShapeDtypeStruct((DEPTH, n, 6 * D_MODEL), F32),
        compiler_params=_cparams(2),
        name="ada_mod",
    )(c_all, ada_w, ada_b.reshape(DEPTH, 1, 6 * D_MODEL))


def _pa_kernel(x_ref, mod_ref, n1_ref, wt_ref, wkv_ref, bias_ref, sink_ref, lowm_ref,
               mix_ref, ko_ref, vo_ref, zt_ref, kv_ref, kprev_ref, vtprev_ref, pen_ref, *, tt):
    t = pl.program_id(1)
    nchunk = tt // WINDOW
    kw = HA_KV * HD_A
    gsz = HA_Q // HA_KV

    @pl.when(t == 0)
    def _():
        kprev_ref[...] = jnp.zeros_like(kprev_ref)
        vtprev_ref[...] = jnp.zeros_like(vtprev_ref)
        pen_ref[...] = jnp.full(pen_ref.shape, NEG, F32)

    mod = mod_ref[0]
    h = _modnorm(x_ref[0], n1_ref[...], mod[:, D_MODEL:2 * D_MODEL], mod[:, 0:D_MODEL]).astype(BF16)
    zt = _dot_nt(wt_ref[...], h)
    for c in range(nchunk):
        zt_ref[c] = zt[:, WINDOW * c:WINDOW * (c + 1)]
    kv_ref[...] = _dot(h, wkv_ref[...])
    qw = gsz * WINDOW
    lower = (lax.broadcasted_iota(jnp.int32, (WINDOW, qw), 0)
             > (lax.broadcasted_iota(jnp.int32, (WINDOW, qw), 1) & (WINDOW - 1)))

    def chunk(c, carry):
        r0 = pl.multiple_of(c * WINDOW, WINDOW)
        rows = pl.ds(r0, WINDOW)
        kc = kv_ref[rows, 0:kw]
        vc = kv_ref[rows, kw:2 * kw]
        vt = vc.T
        kk = jnp.concatenate([kprev_ref[...], kc], axis=0).astype(BF16)
        vvt = jnp.concatenate([vtprev_ref[...], vt], axis=1).astype(BF16)
        qt = zt_ref[c, 0:D_MODEL, :].astype(BF16)
        pen = pen_ref[0:1, :]
        s_all = []
        for g in range(HA_KV):
            gs = slice(HD_A * g, HD_A * (g + 1))
            qcat = jnp.concatenate([qt[HD_A * (gsz * g + j):HD_A * (gsz * g + j + 1), :] for j in range(gsz)], axis=1)
            s_all.append(_dot(kk[:, gs], qcat))
        p_all = []
        for g in range(HA_KV):
            sg = jnp.where(lower, s_all[g][0:WINDOW, :] + pen, s_all[g][WINDOW:2 * WINDOW, :]) + bias_ref[g]
            sink = sink_ref[g]
            m = jnp.maximum(jnp.max(sg, axis=0, keepdims=True), sink)
            pw = jnp.exp(sg - m)
            den = jnp.sum(pw, axis=0, keepdims=True) + jnp.exp(sink - m)
            pb = pw.astype(BF16)
            p_prev = pb * lowm_ref[...]
            p_all.append((jnp.concatenate([p_prev, pb - p_prev], axis=0), 1.0 / den))
        pieces = []
        for g in range(HA_KV):
            gs = slice(HD_A * g, HD_A * (g + 1))
            p, rden = p_all[g]
            ot = _dot(vvt[gs, :], p) * rden
            pieces += [ot[:, WINDOW * j:WINDOW * (j + 1)] for j in range(gsz)]
        oat = jnp.concatenate(pieces, axis=0)
        mixt = _sigmoid(zt_ref[c, D_MODEL:2 * D_MODEL, :]) * oat
        mix_ref[0, rows, :] = mixt.T
        kprev_ref[...] = kc
        vtprev_ref[...] = vt
        pen_ref[...] = jnp.zeros_like(pen_ref)
        ko_ref[0] = kc
        vo_ref[0] = vc
        return carry

    lax.fori_loop(0, nchunk, chunk, 0, unroll=True)


def _pa_call(x, mod3, n1, wqgt, wkv, bias_t, sink_rows, tt):
    B, T, _ = x.shape
    kw = HA_KV * HD_A
    qw = (HA_Q // HA_KV) * WINDOW
    kern = functools.partial(_pa_kernel, tt=tt)
    lowm = (np.arange(WINDOW)[:, None] > (np.arange(qw)[None, :] % WINDOW)).astype(np.float32)
    return pl.pallas_call(
        kern,
        grid=(B, T // tt),
        in_specs=[pl.BlockSpec((1, tt, D_MODEL), lambda b, t: (b, t, 0)),
                  pl.BlockSpec((1, 1, 6 * D_MODEL), lambda b, t: (b, 0, 0)),
                  pl.BlockSpec((1, D_MODEL), lambda b, t: (0, 0)),
                  pl.BlockSpec((2 * D_MODEL, D_MODEL), lambda b, t: (0, 0)),
                  pl.BlockSpec((D_MODEL, 2 * kw), lambda b, t: (0, 0)),
                  pl.BlockSpec((HA_KV, WINDOW, qw), lambda b, t: (0, 0, 0)),
                  pl.BlockSpec((HA_KV, 1, qw), lambda b, t: (0, 0, 0)),
                  pl.BlockSpec((WINDOW, qw), lambda b, t: (0, 0))],
        out_specs=[pl.BlockSpec((1, tt, D_MODEL), lambda b, t: (b, t, 0)),
                   pl.BlockSpec((1, WINDOW, kw), lambda b, t: (b, 0, 0)),
                   pl.BlockSpec((1, WINDOW, kw), lambda b, t: (b, 0, 0))],
        out_shape=[jax.ShapeDtypeStruct((B, T, D_MODEL), F32),
                   jax.ShapeDtypeStruct((B, WINDOW, kw), F32),
                   jax.ShapeDtypeStruct((B, WINDOW, kw), F32)],
        scratch_shapes=[pltpu.VMEM((tt // WINDOW, 2 * D_MODEL, WINDOW), F32),
                        pltpu.VMEM((tt, 2 * kw), F32),
                        pltpu.VMEM((WINDOW, kw), F32),
                        pltpu.VMEM((kw, WINDOW), F32),
                        pltpu.VMEM((8, qw), F32)],
        compiler_params=_cparams(2),
        name="prompt_attn",
    )(x, mod3, n1, wqgt, wkv, bias_t, sink_rows, jnp.asarray(lowm, BF16))


def _pb_kernel(x_ref, mod_ref, n1_ref, wt_ref, cos_ref, sin_ref, qdec_ref, kdec_ref, dm_ref, mixa_ref,
               mix_ref, so_ref, zt_ref, s_ref, *, tt, glast):
    t = pl.program_id(1)
    nchunk = tt // WINDOW
    hw = HB * DK_B // 2
    hp = DK_B // 2

    @pl.when(t == 0)
    def _():
        s_ref[...] = jnp.zeros_like(s_ref)

    mod = mod_ref[0]
    h = _modnorm(x_ref[0], n1_ref[...], mod[:, D_MODEL:2 * D_MODEL], mod[:, 0:D_MODEL]).astype(BF16)
    for cc in range(nchunk // 2):
        z2 = _dot_nt(wt_ref[...], h[2 * WINDOW * cc:2 * WINDOW * (cc + 1), :])
        zt_ref[2 * cc] = z2[:, 0:WINDOW]
        zt_ref[2 * cc + 1] = z2[:, WINDOW:2 * WINDOW]

    def head_rows(pair, hh):
        return jnp.concatenate([pair[0][hp * hh:hp * (hh + 1), :], pair[1][hp * hh:hp * (hh + 1), :]], axis=0)

    def chunk(c, carry):
        r0 = pl.multiple_of(c * WINDOW, WINDOW)
        rows = pl.ds(r0, WINDOW)
        cos = cos_ref[c]
        sin = sin_ref[c]
        q1 = zt_ref[c, 0:hw, :]
        q2 = zt_ref[c, hw:2 * hw, :]
        k1 = zt_ref[c, 2 * hw:3 * hw, :]
        k2 = zt_ref[c, 3 * hw:4 * hw, :]
        rq = (q1 * cos - q2 * sin, q1 * sin + q2 * cos)
        rk = (k1 * cos - k2 * sin, k1 * sin + k2 * cos)
        qdec = qdec_ref[...]
        kdec = kdec_ref[...]
        qb = tuple(a.astype(BF16) for a in rq)
        kb = tuple(a.astype(BF16) for a in rk)
        qd = tuple((a * qdec).astype(BF16) for a in rq)
        kd = tuple((a * kdec).astype(BF16) for a in rk)
        vt = zt_ref[c, 4 * hw:4 * hw + HB * DV_B, :].astype(BF16)
        inner, cross, supd = [], [], []
        for hh in range(HB):
            s_old = s_ref[DK_B * hh:DK_B * (hh + 1), :]
            inner.append(_dot_tn(head_rows(qb, hh), head_rows(kb, hh)))
            cross.append(_dot_tn(s_old.astype(BF16), head_rows(qd, hh)))
            supd.append(glast[hh] * s_old + _dot_nt(head_rows(kd, hh), vt[DV_B * hh:DV_B * (hh + 1), :]))
        innd = [(inner[hh] * dm_ref[hh]).astype(BF16) for hh in range(HB)]
        outs = []
        for hh in range(HB):
            ot = _dot_nt(vt[DV_B * hh:DV_B * (hh + 1), :], innd[hh]) + cross[hh]
            s_ref[DK_B * hh:DK_B * (hh + 1), :] = supd[hh]
            ms = jnp.mean(ot * ot, axis=0, keepdims=True)
            outs.append(ot * lax.rsqrt(ms + EPS))
        obt = jnp.concatenate(outs, axis=0)
        bgt = zt_ref[c, 4 * hw + D_MODEL:4 * hw + 2 * D_MODEL, :]
        gbt = zt_ref[c, 4 * hw + 2 * D_MODEL:4 * hw + 3 * D_MODEL, :]
        mixt = _sigmoid(gbt) * (_silu(bgt) * obt)
        mix_ref[0, rows, :] = mixa_ref[0, rows, :] + mixt.T
        return carry

    lax.fori_loop(0, nchunk, chunk, 0, unroll=True)
    so_ref[0] = s_ref[...]


def _deinterleave_pairs(w):
    return w.reshape(w.shape[0], HB, DK_B // 2, 2).transpose(0, 3, 1, 2).reshape(w.shape[0], HB * DK_B)


def _rot_tables_t(T):
    theta = 1.0 / (10000.0 ** np.linspace(0.0, 1.0, DK_B // 2))
    ang = theta[:, None] * np.arange(T, dtype=np.float64)[None, :]

    def lay(a):
        a = np.tile(a, (HB, 1))
        return np.ascontiguousarray(a.reshape(a.shape[0], T // WINDOW, WINDOW).transpose(1, 0, 2)).astype(np.float32)

    return lay(np.cos(ang)), lay(np.sin(ang))


def _pb_call(x, mod3, n1, wbt, mixa, tt):
    B, T, _ = x.shape
    hw = HB * DK_B // 2
    cos, sin = _rot_tables_t(T)
    dm, qdec, kdec = _ret_tables()
    qdec_t = np.ascontiguousarray(qdec[:, ::2].T)
    kdec_t = np.ascontiguousarray(kdec[:, ::2].T)
    glast = tuple(float(v) for v in (_gammas() ** WINDOW))
    kern = functools.partial(_pb_kernel, tt=tt, glast=glast)
    full2 = lambda b, t: (0, 0)
    nct = tt // WINDOW
    return pl.pallas_call(
        kern,
        grid=(B, T // tt),
        in_specs=[pl.BlockSpec((1, tt, D_MODEL), lambda b, t: (b, t, 0)),
                  pl.BlockSpec((1, 1, 6 * D_MODEL), lambda b, t: (b, 0, 0)),
                  pl.BlockSpec((1, D_MODEL), full2),
                  pl.BlockSpec((SEG_B, D_MODEL), full2),
                  pl.BlockSpec((nct, hw, WINDOW), lambda b, t: (t, 0, 0)),
                  pl.BlockSpec((nct, hw, WINDOW), lambda b, t: (t, 0, 0)),
                  pl.BlockSpec((hw, WINDOW), full2),
                  pl.BlockSpec((hw, WINDOW), full2),
                  pl.BlockSpec((HB, WINDOW, WINDOW), lambda b, t: (0, 0, 0)),
                  pl.BlockSpec((1, tt, D_MODEL), lambda b, t: (b, t, 0))],
        out_specs=[pl.BlockSpec((1, tt, D_MODEL), lambda b, t: (b, t, 0)),
                   pl.BlockSpec((1, HB * DK_B, DV_B), lambda b, t: (b, 0, 0))],
        out_shape=[jax.ShapeDtypeStruct((B, T, D_MODEL), F32),
                   jax.ShapeDtypeStruct((B, HB * DK_B, DV_B), F32)],
        scratch_shapes=[pltpu.VMEM((nct, SEG_B, WINDOW), F32),
                        pltpu.VMEM((HB * DK_B, DV_B), F32)],
        compiler_params=_cparams(2),
        name="prompt_retention",
    )(x, mod3, n1, wbt, jnp.asarray(cos), jnp.asarray(sin), jnp.asarray(qdec_t), jnp.asarray(kdec_t),
      jnp.asarray(dm), mixa)


def _pc_kernel(x_ref, mod_ref, n1_ref, w_ref, cw_ref, cb_ref, dtb_ref, a_ref, dsk_ref, snw_ref, tri_ref,
               mixab_ref, wo_ref,
               xo_ref, ho_ref, co_ref,
               z_ref, xbuf_ref, xbc_ref, dt_ref, hst_ref, mixs_ref, *, tt):
    t = pl.program_id(1)

    @pl.when(t == 0)
    def _():
        xbuf_ref[0:8, :] = jnp.zeros((8, CONV_DIM), F32)
        hst_ref[...] = jnp.zeros_like(hst_ref)

    x = x_ref[0]
    mod = mod_ref[0]
    h = _modnorm(x, n1_ref[...], mod[:, D_MODEL:2 * D_MODEL], mod[:, 0:D_MODEL])
    z_ref[...] = _dot(h.astype(BF16), w_ref[...])

    xbuf_ref[8:8 + tt, :] = z_ref[:, 1024:2560]
    acc = cb_ref[...]
    for i in range(CONV_W):
        acc = acc + xbuf_ref[5 + i:5 + i + tt, :] * cw_ref[i:i + 1, :]
    xbc_ref[...] = _silu(acc)
    co_ref[0] = xbuf_ref[tt + 5:tt + 8, :]
    xbuf_ref[0:8, :] = xbuf_ref[tt:tt + 8, :]
    dt_ref[...] = _softplus(z_ref[:, C_DT:C_DT + 128] + dtb_ref[...])

    ii = lax.broadcasted_iota(jnp.int32, (WINDOW, WINDOW), 0)
    jj = lax.broadcasted_iota(jnp.int32, (WINDOW, WINDOW), 1)
    causal = ii >= jj
    hpg = HC // G_C

    def chunk(c, carry):
        r0 = pl.multiple_of(c * WINDOW, WINDOW)
        rows = pl.ds(r0, WINDOW)
        xc = xbc_ref[rows, 0:1024]
        bmat = xbc_ref[rows, 1024:1280]
        cmat = xbc_ref[rows, 1280:1536]
        dtc = dt_ref[rows, :]
        acum = jnp.dot(tri_ref[...], dtc * a_ref[...], precision=lax.Precision.HIGHEST,
                       preferred_element_type=F32)
        acum = acum * LOG2E
        acum_t = acum.T
        rowp_t = acum_t - jnp.log2(dtc.T)
        x_t = xc.T
        xb = xc.astype(BF16)
        bb = bmat.astype(BF16)
        cb16 = cmat.astype(BF16)
        ys = []
        for g in range(G_C):
            ns = slice(N_C * g, N_C * (g + 1))
            cbg = _dot_nt(cb16[:, ns], bb[:, ns])
            for hh in range(hpg * g, hpg * (g + 1)):
                ps = slice(HD_C * hh, HD_C * (hh + 1))
                colb = jnp.broadcast_to(acum[:, hh:hh + 1], (WINDOW, WINDOW))
                rowp = rowp_t[hh:hh + 1, :]
                m = cbg * jnp.exp2(jnp.where(causal, colb - rowp, NEG))
                hs = hst_ref[ps, :]
                ecolb = jnp.exp2(colb)
                cs = cmat[:, ns] * ecolb
                y = _dot(m.astype(BF16), xb[:, ps]) + _dot_nt(cs.astype(BF16), hs.astype(BF16))
                wrow = jnp.exp2(colb[WINDOW - 1:WINDOW, :] - rowp)
                xw = (x_t[ps, :] * wrow).astype(BF16)
                hst_ref[ps, :] = ecolb[WINDOW - 1:WINDOW, :] * hs + _dot(xw, bb[:, ns])
                ys.append(y)
        y = jnp.concatenate(ys, axis=1) + dsk_ref[...] * xc
        yc = y * _silu(z_ref[rows, 0:1024])
        gw = D_MODEL // G_C
        oc = jnp.concatenate([_rms_lanes(yc[:, gw * g:gw * (g + 1)]) for g in range(G_C)], axis=1) * snw_ref[...]
        gc = _sigmoid(z_ref[rows, C_GC:C_GC + 1024])
        mixs_ref[rows, :] = (mixab_ref[0, rows, :] + gc * oc).astype(BF16)
        return carry

    lax.fori_loop(0, tt // WINDOW, chunk, 0, unroll=True)
    ho_ref[0] = hst_ref[...]
    g1 = mod[:, 2 * D_MODEL:3 * D_MODEL]
    xo_ref[0] = x + g1 * _dot(mixs_ref[...], wo_ref[...])


def _pc_call(x, mod3, n1, wc, cw, cb, dtb_pad, a_pad, dsk_full, snw, mixab, wo, tt):
    B, T, _ = x.shape
    tri = jnp.asarray(np.tril(np.ones((WINDOW, WINDOW), np.float32)))
    kern = functools.partial(_pc_kernel, tt=tt)
    full2 = lambda b, t: (0, 0)
    return pl.pallas_call(
        kern,
        grid=(B, T // tt),
        in_specs=[pl.BlockSpec((1, tt, D_MODEL), lambda b, t: (b, t, 0)),
                  pl.BlockSpec((1, 1, 6 * D_MODEL), lambda b, t: (b, 0, 0)),
                  pl.BlockSpec((1, D_MODEL), full2),
                  pl.BlockSpec((D_MODEL, SEG_C), full2),
                  pl.BlockSpec((CONV_W, CONV_DIM), full2),
                  pl.BlockSpec((1, CONV_DIM), full2),
                  pl.BlockSpec((1, 128), full2),
                  pl.BlockSpec((1, 128), full2),
                  pl.BlockSpec((1, D_MODEL), full2),
                  pl.BlockSpec((1, D_MODEL), full2),
                  pl.BlockSpec((WINDOW, WINDOW), full2),
                  pl.BlockSpec((1, tt, D_MODEL), lambda b, t: (b, t, 0)),
                  pl.BlockSpec((D_MODEL, D_MODEL), full2)],
        out_specs=[pl.BlockSpec((1, tt, D_MODEL), lambda b, t: (b, t, 0)),
                   pl.BlockSpec((1, HC * HD_C, N_C), lambda b, t: (b, 0, 0)),
                   pl.BlockSpec((1, CONV_W - 1, CONV_DIM), lambda b, t: (b, 0, 0))],
        out_shape=[jax.ShapeDtypeStruct((B, T, D_MODEL), F32),
                   jax.ShapeDtypeStruct((B, HC * HD_C, N_C), F32),
                   jax.ShapeDtypeStruct((B, CONV_W - 1, CONV_DIM), F32)],
        scratch_shapes=[pltpu.VMEM((tt, SEG_C), F32),
                        pltpu.VMEM((tt + 8, CONV_DIM), F32),
                        pltpu.VMEM((tt, CONV_DIM), F32),
                        pltpu.VMEM((tt, 128), F32),
                        pltpu.VMEM((HC * HD_C, N_C), F32),
                        pltpu.VMEM((tt, D_MODEL), BF16)],
        compiler_params=_cparams(2),
        name="prompt_ssd_out",
    )(x, mod3, n1, wc, cw, cb, dtb_pad, a_pad, dsk_full, snw, tri, mixab, wo)


def _mlp_kernel(x_ref, sh_ref, sc_ref, g_ref, n2_ref, wu_ref, wd_ref, fn_ref, o_ref, h_ref, acc_ref,
                *, nf, final, per_row):
    f = pl.program_id(1)
    rd = (lambda r: r[...]) if per_row else (lambda r: r[0])

    @pl.when(f == 0)
    def _():
        h_ref[...] = _modnorm(x_ref[...], n2_ref[...], rd(sc_ref), rd(sh_ref)).astype(BF16)
        acc_ref[...] = jnp.zeros_like(acc_ref)

    u = _dot(h_ref[...], wu_ref[0])
    u = jnp.square(jnp.maximum(u, 0.0)).astype(BF16)
    acc_ref[...] += _dot(u, wd_ref[0])

    @pl.when(f == nf - 1)
    def _():
        y = x_ref[...] + rd(g_ref) * acc_ref[...]
        if final:
            y = _rms_lanes(y) * fn_ref[...]
        o_ref[...] = y


def _mlp_call(x2, mod, n2, wu, wd, fn, layer, tm, tf, rows_per_mod, final):
    M = x2.shape[0]
    nf = D_FF // tf
    per_row = rows_per_mod == 1
    if per_row:
        mspec = lambda j: pl.BlockSpec((tm, D_MODEL), lambda m, f: (m, j))
    else:
        mspec = lambda j: pl.BlockSpec((1, 1, D_MODEL), lambda m, f: ((m * tm) // rows_per_mod, 0, j))
    kern = functools.partial(_mlp_kernel, nf=nf, final=final, per_row=per_row)
    return pl.pallas_call(
        kern,
        grid=(M // tm, nf),
        in_specs=[pl.BlockSpec((tm, D_MODEL), lambda m, f: (m, 0)),
                  mspec(3), mspec(4), mspec(5),
                  pl.BlockSpec((1, D_MODEL), lambda m, f: (0, 0)),
                  pl.BlockSpec((1, D_MODEL, tf), lambda m, f: (layer, 0, f)),
                  pl.BlockSpec((1, tf, D_MODEL), lambda m, f: (layer, f, 0)),
                  pl.BlockSpec((1, D_MODEL), lambda m, f: (0, 0))],
        out_specs=pl.BlockSpec((tm, D_MODEL), lambda m, f: (m, 0)),
        out_shape=jax.ShapeDtypeStruct((M, D_MODEL), F32),
        scratch_shapes=[pltpu.VMEM((tm, D_MODEL), BF16),
                        pltpu.VMEM((tm, D_MODEL), F32)],
        compiler_params=_cparams(2),
        name="mlp",
    )(x2, mod, mod, mod, n2, wu, wd, fn)


def _sproj_kernel(x_ref, sh_ref, sc_ref, n1_ref, w_ref, o_ref):
    h = _modnorm(x_ref[...], n1_ref[...], sc_ref[...], sh_ref[...])
    o_ref[...] = _dot(h.astype(BF16), w_ref[...])


def _sproj_call(xs, mod_s, n1, w):
    n, width = xs.shape[0], w.shape[1]
    return pl.pallas_call(
        _sproj_kernel,
        grid=(1,),
        in_specs=[pl.BlockSpec((n, D_MODEL), lambda i: (0, 0)),
                  pl.BlockSpec((n, D_MODEL), lambda i: (0, 0)),
                  pl.BlockSpec((n, D_MODEL), lambda i: (0, 1)),
                  pl.BlockSpec((1, D_MODEL), lambda i: (0, 0)),
                  pl.BlockSpec((D_MODEL, width), lambda i: (0, 0))],
        out_specs=pl.BlockSpec((n, width), lambda i: (0, 0)),
        out_shape=jax.ShapeDtypeStruct((n, width), F32),
        compiler_params=_cparams(1),
        name="sample_proj",
    )(xs, mod_s, mod_s, n1, w)


def _carry_outputs(prev, n_in, nsteps):
    nslab = 1 if prev else DEPTH
    extra = [pl.BlockSpec(memory_space=pl.ANY) for _ in prev]
    alias = {n_in + k: k for k in range(len(prev))}
    row = lambda p, i: jnp.where(p == 0, i, nsteps - 1)
    return nslab, extra, alias, row


def _with_fill(body, state_outs):
    def kern(*refs):
        p = pl.program_id(0)

        @pl.when(p == 0)
        def _():
            body(*refs)

        @pl.when(p != 0)
        def _():
            for k in state_outs:
                refs[k][...] = jnp.zeros_like(refs[k])

    return kern


def _sa_kernel(q_ref, kv_ref, ck_ref, cv_ref, bias_ref, sink_ref, *rest):
    ko_ref, vo_ref, oa_ref = rest[-3:]
    nb = ROWS_PER_SAMPLE_STEP
    gsz = HA_Q // HA_KV
    rg = lax.broadcasted_iota(jnp.int32, (HA_Q, HD_A), 0) // gsz
    sink = sink_ref[...]
    kw = HA_KV * HD_A
    scores, vmats = [], []
    for i in range(nb):
        ko_ref[0, i, 0:WINDOW - 1, :] = ck_ref[0, i, 1:WINDOW, :]
        ko_ref[0, i, WINDOW - 1:WINDOW, :] = kv_ref[i:i + 1, 0:kw]
        vo_ref[0, i, 0:WINDOW - 1, :] = cv_ref[0, i, 1:WINDOW, :]
        vo_ref[0, i, WINDOW - 1:WINDOW, :] = kv_ref[i:i + 1, kw:2 * kw]
        kmat = ko_ref[0, i].astype(BF16)
        vmats.append(vo_ref[0, i].astype(BF16))
        q = q_ref[i] * (HD_A ** -0.5)
        qe = jnp.concatenate([jnp.where(rg == g, q, 0.0) for g in range(HA_KV)], axis=1).astype(BF16)
        scores.append(_dot_nt(qe, kmat))
    probs = []
    for i in range(nb):
        sc = scores[i] + bias_ref[...]
        m = jnp.maximum(jnp.max(sc, axis=-1, keepdims=True), sink)
        p = jnp.exp(sc - m)
        den = jnp.sum(p, axis=-1, keepdims=True) + jnp.exp(sink - m)
        probs.append((p.astype(BF16), den))
    outs = [_dot(probs[i][0], vmats[i]) for i in range(nb)]
    for i in range(nb):
        o = outs[i] / probs[i][1]
        o16 = jnp.zeros((HA_Q, HD_A), F32)
        for g in range(HA_KV):
            o16 = o16 + jnp.where(rg == g, o[:, HD_A * g:HD_A * (g + 1)], 0.0)
        oa_ref[i] = o16


def _sa_call(q3, kv, cache_k, cache_v, bias_s, sink_col, layer, prev):
    n = q3.shape[0]
    nb = ROWS_PER_SAMPLE_STEP
    kw = HA_KV * HD_A
    nslab, extra, alias, row = _carry_outputs(prev, 6, n // nb)
    cspec = pl.BlockSpec((1, nb, WINDOW, kw), lambda p, i: (layer, row(p, i), 0, 0))
    ospec = pl.BlockSpec((1, nb, WINDOW, kw), lambda p, i: ((layer + p) % DEPTH, i, 0, 0))
    return pl.pallas_call(
        _with_fill(_sa_kernel, (-3, -2)),
        grid=(nslab, n // nb),
        in_specs=[pl.BlockSpec((nb, HA_Q, HD_A), lambda p, i: (row(p, i), 0, 0)),
                  pl.BlockSpec((nb, 2 * kw), lambda p, i: (row(p, i), 0)),
                  cspec, cspec,
                  pl.BlockSpec((HA_Q, WINDOW), lambda p, i: (0, 0)),
                  pl.BlockSpec((HA_Q, 1), lambda p, i: (0, 0))] + extra,
        out_specs=[ospec, ospec, pl.BlockSpec((nb, HA_Q, HD_A), lambda p, i: (row(p, i), 0, 0))],
        out_shape=[jax.ShapeDtypeStruct((DEPTH, n, WINDOW, kw), F32),
                   jax.ShapeDtypeStruct((DEPTH, n, WINDOW, kw), F32),
                   jax.ShapeDtypeStruct((n, HA_Q, HD_A), F32)],
        input_output_aliases=alias,
        compiler_params=_cparams(2),
        name="sample_attn",
    )(q3, kv, cache_k, cache_v, bias_s, sink_col, *prev)


def _sr_kernel(zb_ref, cos_ref, sin_ref, gcol_ref, s_ref, *rest):
    so_ref, o_ref = rest[-2:]
    nb = ROWS_PER_SAMPLE_STEP
    wq = HB * DK_B
    cos = cos_ref[...]
    sin = sin_ref[...]
    qf = zb_ref[:, 0:wq]
    kf = zb_ref[:, wq:2 * wq]
    qr = qf * cos + _pairswap(qf) * sin
    kr = (kf * cos + _pairswap(kf) * sin) * (DK_B ** -0.5)
    v = zb_ref[:, 2 * wq:2 * wq + HB * DV_B]
    r8 = lax.broadcasted_iota(jnp.int32, (HB, wq), 0)
    hl = lax.broadcasted_iota(jnp.int32, (HB, wq), 1) // DK_B
    rv = lax.broadcasted_iota(jnp.int32, (HB, DV_B), 0)
    outers, q8s = [], []
    for i in range(nb):
        k8 = jnp.where(hl == r8, jnp.broadcast_to(kr[i:i + 1, :], (HB, wq)), 0.0).astype(BF16)
        q8s.append(jnp.where(hl == r8, jnp.broadcast_to(qr[i:i + 1, :], (HB, wq)), 0.0).astype(BF16))
        v8 = jnp.zeros((HB, DV_B), F32)
        for r in range(HB):
            v8 = jnp.where(rv == r, jnp.broadcast_to(v[i:i + 1, DV_B * r:DV_B * (r + 1)], (HB, DV_B)), v8)
        outers.append(_dot_tn(k8, v8.astype(BF16)))
    s_news = []
    for i in range(nb):
        s_new = gcol_ref[...] * s_ref[0, i] + outers[i]
        so_ref[0, i] = s_new
        s_news.append(s_new.astype(BF16))
    for i in range(nb):
        o_ref[i] = _dot(q8s[i], s_news[i])


def _sr_call(zb, state, layer, prev):
    n = zb.shape[0]
    nb = ROWS_PER_SAMPLE_STEP
    wq = HB * DK_B
    cos, sin = _rot_tables(np.array([PAST_LEN]))
    gcol = np.repeat(_gammas(), DK_B)[:, None] * np.ones((1, DV_B))
    nslab, extra, alias, row = _carry_outputs(prev, 5, n // nb)
    full = lambda p, i: (0, 0)
    return pl.pallas_call(
        _with_fill(_sr_kernel, (-2,)),
        grid=(nslab, n // nb),
        in_specs=[pl.BlockSpec((nb, SEG_B), lambda p, i: (row(p, i), 0)),
                  pl.BlockSpec((1, wq), full),
                  pl.BlockSpec((1, wq), full),
                  pl.BlockSpec((wq, DV_B), full),
                  pl.BlockSpec((1, nb, wq, DV_B), lambda p, i: (layer, row(p, i), 0, 0))] + extra,
        out_specs=[pl.BlockSpec((1, nb, wq, DV_B), lambda p, i: ((layer + p) % DEPTH, i, 0, 0)),
                   pl.BlockSpec((nb, HB, DV_B), lambda p, i: (row(p, i), 0, 0))],
        out_shape=[jax.ShapeDtypeStruct((DEPTH, n, wq, DV_B), F32),
                   jax.ShapeDtypeStruct((n, HB, DV_B), F32)],
        input_output_aliases=alias,
        compiler_params=_cparams(2),
        name="sample_retention",
    )(zb, jnp.asarray(cos), jnp.asarray(sin), jnp.asarray(gcol.astype(np.float32)), state, *prev)


def _ss_kernel(zc_ref, cs_ref, cw_ref, cb_ref, dtb_ref, a_ref, e3_ref, h_ref, *rest):
    ho_ref, cso_ref, y_ref, xc_ref = rest[-4:]
    nb = ROWS_PER_SAMPLE_STEP
    cx = zc_ref[:, 1024:2560]
    taps = [cs_ref[0, :, CONV_DIM * i:CONV_DIM * (i + 1)] for i in range(CONV_W - 1)] + [cx]
    acc = cb_ref[...]
    for i in range(CONV_W):
        acc = acc + taps[i] * cw_ref[i:i + 1, :]
    xbc = _silu(acc)
    cso_ref[...] = jnp.concatenate(taps[1:], axis=1)
    xc = xbc[:, 0:1024]
    bmat = xbc[:, 1024:1280]
    cmat = xbc[:, 1280:1536]
    dt = _softplus(zc_ref[:, C_DT:C_DT + 128] + dtb_ref[...])
    da = jnp.exp(dt * a_ref[...])
    dt_e = _dot(jnp.concatenate(_split3(dt), axis=1), e3_ref[...])
    da_e = _dot(jnp.concatenate(_split3(da), axis=1), e3_ref[...])
    dtx = dt_e * xc
    gw = (HC // G_C) * HD_C
    r8 = lax.broadcasted_iota(jnp.int32, (nb, gw), 0)
    rn = lax.broadcasted_iota(jnp.int32, (nb, N_C), 0)
    ones8 = jnp.ones((nb, N_C), BF16)
    prods = {}
    for g in range(G_C):
        ws = slice(gw * g, gw * (g + 1))
        bg16 = bmat[:, N_C * g:N_C * (g + 1)].astype(BF16)
        for i in range(nb):
            x8 = jnp.where(r8 == i, dtx[:, ws], 0.0).astype(BF16)
            outer = _dot_tn(x8, bg16)
            hi, mid, lo = (p.astype(F32) for p in _split3(jnp.broadcast_to(da_e[i:i + 1, ws], (nb, gw))))
            l3 = jnp.where(r8 == 0, hi, jnp.where(r8 == 1, mid, jnp.where(r8 == 2, lo, 0.0)))
            prods[g, i] = (outer, _dot_tn(l3.astype(BF16), ones8))
    h16 = {}
    for g in range(G_C):
        ws = slice(gw * g, gw * (g + 1))
        for i in range(nb):
            outer, dacol = prods[g, i]
            h_new = dacol * h_ref[0, i, ws, :] + outer
            ho_ref[0, i, ws, :] = h_new
            h16[g, i] = h_new.astype(BF16)
    ycols = []
    for g in range(G_C):
        ns = slice(N_C * g, N_C * (g + 1))
        yacc = jnp.zeros((nb, gw), F32)
        for i in range(nb):
            c8 = jnp.where(rn == i, cmat[:, ns], 0.0).astype(BF16)
            yacc = yacc + _dot_nt(c8, h16[g, i])
        ycols.append(yacc)
    y_ref[...] = jnp.concatenate(ycols, axis=1)
    xc_ref[...] = xc


def _ss_call(zc, conv_state, hstate, cw, cb, dtb_pad, a_pad, layer, prev):
    n = zc.shape[0]
    nb = ROWS_PER_SAMPLE_STEP
    e = np.zeros((128, D_MODEL), np.float32)
    for hh in range(HC):
        e[hh, HD_C * hh:HD_C * (hh + 1)] = 1.0
    e3 = jnp.asarray(np.concatenate([e, e, e], axis=0), dtype=BF16)
    cwid = (CONV_W - 1) * CONV_DIM
    full = lambda p, i: (0, 0)
    nslab, extra, alias, row = _carry_outputs(prev, 8, n // nb)
    return pl.pallas_call(
        _with_fill(_ss_kernel, (-4,)),
        grid=(nslab, n // nb),
        in_specs=[pl.BlockSpec((nb, SEG_C), lambda p, i: (row(p, i), 0)),
                  pl.BlockSpec((1, nb, cwid), lambda p, i: (layer, row(p, i), 0)),
                  pl.BlockSpec((CONV_W, CONV_DIM), full),
                  pl.BlockSpec((1, CONV_DIM), full),
                  pl.BlockSpec((1, 128), full),
                  pl.BlockSpec((1, 128), full),
                  pl.BlockSpec((3 * 128, D_MODEL), full),
                  pl.BlockSpec((1, nb, HC * HD_C, N_C), lambda p, i: (layer, row(p, i), 0, 0))] + extra,
        out_specs=[pl.BlockSpec((1, nb, HC * HD_C, N_C), lambda p, i: ((layer + p) % DEPTH, i, 0, 0)),
                   pl.BlockSpec((nb, cwid), lambda p, i: (row(p, i), 0)),
                   pl.BlockSpec((nb, D_MODEL), lambda p, i: (row(p, i), 0)),
                   pl.BlockSpec((nb, D_MODEL), lambda p, i: (row(p, i), 0))],
        out_shape=[jax.ShapeDtypeStruct((DEPTH, n, HC * HD_C, N_C), F32),
                   jax.ShapeDtypeStruct((n, cwid), F32),
                   jax.ShapeDtypeStruct((n, D_MODEL), F32),
                   jax.ShapeDtypeStruct((n, D_MODEL), F32)],
        input_output_aliases=alias,
        compiler_params=_cparams(2),
        name="sample_ssd",
    )(zc, conv_state, cw, cb, dtb_pad, a_pad, e3, hstate, *prev)


def _sm_kernel(x_ref, g1_ref, oa_ref, oret_ref, y_ref, xc_ref, za_ref, zb_ref, zc_ref,
               dsk_ref, snw_ref, wo_ref, o_ref):
    ob = jnp.concatenate([_rms_lanes(oret_ref[:, DV_B * hh:DV_B * (hh + 1)]) for hh in range(HB)], axis=1)
    ob = _silu(zb_ref[:, 2048:3072]) * ob
    yc = (y_ref[...] + dsk_ref[...] * xc_ref[...]) * _silu(zc_ref[:, 0:1024])
    gw = D_MODEL // G_C
    oc = jnp.concatenate([_rms_lanes(yc[:, gw * g:gw * (g + 1)]) for g in range(G_C)], axis=1) * snw_ref[...]
    mix = (_sigmoid(za_ref[:, 1536:2560]) * oa_ref[...] + _sigmoid(zb_ref[:, 3072:4096]) * ob
           + _sigmoid(zc_ref[:, C_GC:C_GC + 1024]) * oc)
    o_ref[...] = x_ref[...] + g1_ref[...] * _dot(mix.astype(BF16), wo_ref[...])


def _sm_call(xs, mod_s, oa, oret, y, xc, za, zb, zc, dsk_full, snw, wo):
    n = xs.shape[0]
    full = lambda i: (0, 0)
    row = lambda w: pl.BlockSpec((n, w), full)
    return pl.pallas_call(
        _sm_kernel,
        grid=(1,),
        in_specs=[row(D_MODEL),
                  pl.BlockSpec((n, D_MODEL), lambda i: (0, 2)),
                  row(D_MODEL), row(D_MODEL), row(D_MODEL), row(D_MODEL),
                  row(SEG_A), row(SEG_B), row(SEG_C),
                  pl.BlockSpec((1, D_MODEL), full),
                  pl.BlockSpec((1, D_MODEL), full),
                  pl.BlockSpec((D_MODEL, D_MODEL), full)],
        out_specs=row(D_MODEL),
        out_shape=jax.ShapeDtypeStruct((n, D_MODEL), F32),
        compiler_params=_cparams(1),
        name="sample_merge_out",
    )(xs, mod_s, oa, oret, y, xc, za, zb, zc, dsk_full, snw, wo)


PROMPT_TILE = 512
MLP_ROWS = 1024
MLP_FF = 1024


def _prep_w_in(w):
    wa = jnp.concatenate([w[:, O_AQ:O_BQ], w[:, O_GTS:O_GTS + 1024]], axis=1)
    wb = jnp.concatenate([w[:, O_BQ:O_CZ], w[:, O_GTS + 1024:O_GTS + 2048]], axis=1)
    wc = jnp.concatenate([w[:, O_CZ:O_CDT], jnp.pad(w[:, O_CDT:O_GTS], ((0, 0), (0, 128 - HC))),
                          w[:, O_GTS + 2048:O_GTS + 3072]], axis=1)
    return wa, wb, wc


def _forward(x_prompt, x_sample, cache_win_k, cache_win_v, state_ret, state_ssm, state_conv,
             c_prompt, c_sample, rel_bias_table, attn_sinks, norm1_w, norm2_w, ada_w, ada_b,
             w_in, conv_w, conv_b, dt_bias, A_log, D_skip, ssm_norm_w, w_out, w_up, w_down,
             final_norm_w, *, prompt_tile, mlp_rows, mlp_ff):
    B, T, _ = x_prompt.shape
    DB = x_sample.shape[0]
    kw = HA_KV * HD_A

    mod_all = _ada_call(jnp.concatenate([c_prompt, c_sample], axis=0), ada_w, ada_b)

    bias_t, bias_s = _bias_tables(rel_bias_table)

    w16 = w_in.astype(BF16)
    wu16 = w_up.astype(BF16)
    wd16 = w_down.astype(BF16)
    wo16 = w_out.astype(BF16)
    fn = final_norm_w.reshape(1, D_MODEL)
    ck = cache_win_k.reshape(DEPTH, DB, WINDOW, kw)
    cv = cache_win_v.reshape(DEPTH, DB, WINDOW, kw)
    sret = state_ret.reshape(DEPTH, DB, HB * DK_B, DV_B)
    sssm = state_ssm.reshape(DEPTH, DB, HC * HD_C, N_C)
    sconv = state_conv.reshape(DEPTH, DB, (CONV_W - 1) * CONV_DIM)

    xp = x_prompt
    xs = x_sample.reshape(DB, D_MODEL)
    outs_p = [[] for _ in range(5)]
    conv_s = []
    win_kv, ret_all, ssm_all = (), (), ()
    for l in range(DEPTH):
        wl = w16[l]
        wa, wb, wc = _prep_w_in(wl)
        n1 = norm1_w[l].reshape(1, D_MODEL)
        n2 = norm2_w[l].reshape(1, D_MODEL)
        cw = conv_w[l]
        cb = conv_b[l].reshape(1, CONV_DIM)
        dtb_pad = jnp.pad(dt_bias[l], (0, 128 - HC)).reshape(1, 128)
        a_pad = jnp.pad(-jnp.exp(A_log[l].astype(F32)), (0, 128 - HC)).reshape(1, 128)
        dsk_full = jnp.repeat(D_skip[l], HD_C).reshape(1, D_MODEL)
        snw = ssm_norm_w[l].reshape(1, D_MODEL)
        final = l == DEPTH - 1
        mod_p = mod_all[l, :B].reshape(B, 1, 6 * D_MODEL)
        mod_s = mod_all[l, B:]

        wqgt = jnp.concatenate([wl[:, O_AQ:O_AK] * (HD_A ** -0.5), wl[:, O_GTS:O_GTS + D_MODEL]], axis=1).T
        wkv = wl[:, O_AK:O_BQ]
        sink_rows = jnp.repeat(attn_sinks[l], WINDOW).reshape(HA_KV, 1, (HA_Q // HA_KV) * WINDOW)
        mixa, kbuf, vbuf = _pa_call(xp, mod_p, n1, wqgt, wkv, bias_t, sink_rows, prompt_tile)
        wbt = jnp.concatenate([_deinterleave_pairs(wl[:, O_BQ:O_BK]),
                               _deinterleave_pairs(wl[:, O_BK:O_BV]) * (DK_B ** -0.5), wl[:, O_BV:O_CZ],
                               wl[:, O_GTS + D_MODEL:O_GTS + 2 * D_MODEL]], axis=1).T
        mixab, s_perm = _pb_call(xp, mod_p, n1, wbt, mixa, prompt_tile)
        s_ret = s_perm.reshape(B, HB, 2, DK_B // 2, DV_B).transpose(0, 1, 3, 2, 4)
        x1, h_ssm, conv_new = _pc_call(xp, mod_p, n1, wc, cw, cb, dtb_pad, a_pad, dsk_full, snw,
                                       mixab, wo16[l], prompt_tile)
        xp = _mlp_call(x1.reshape(B * T, D_MODEL), mod_p, n2, wu16, wd16, fn, l,
                       mlp_rows, mlp_ff, T, final).reshape(B, T, D_MODEL)
        for lst, v in zip(outs_p, (kbuf.reshape(B, WINDOW, HA_KV, HD_A), vbuf.reshape(B, WINDOW, HA_KV, HD_A),
                                   s_ret.reshape(B, HB, DK_B, DV_B), h_ssm.reshape(B, HC, HD_C, N_C), conv_new)):
            lst.append(v)

        za = _sproj_call(xs, mod_s, n1, wa)
        zb = _sproj_call(xs, mod_s, n1, wb)
        zc = _sproj_call(xs, mod_s, n1, wc)
        q3 = za[:, 0:HA_Q * HD_A].reshape(DB, HA_Q, HD_A)
        kv = za[:, HA_Q * HD_A:HA_Q * HD_A + 2 * kw]
        ck_new, cv_new, oa3 = _sa_call(q3, kv, ck, cv, bias_s, attn_sinks[l].reshape(HA_Q, 1), l, win_kv)
        win_kv = (ck_new, cv_new)
        s_new, o3 = _sr_call(zb, sret, l, ret_all)
        ret_all = (s_new,)
        h_new, cs_new, y_s, xc_s = _ss_call(zc, sconv, sssm, cw, cb, dtb_pad, a_pad, l, ssm_all)
        ssm_all = (h_new,)
        xs1 = _sm_call(xs, mod_s, oa3.reshape(DB, D_MODEL), o3.reshape(DB, D_MODEL), y_s, xc_s,
                       za, zb, zc, dsk_full, snw, wo16[l])
        xs = _mlp_call(xs1, mod_s, n2, wu16, wd16, fn, l, DB, mlp_ff, 1, final)
        conv_s.append(cs_new.reshape(DB, CONV_W - 1, CONV_DIM))

    stk = lambda lst: jnp.stack(lst, axis=0)
    return (xp, xs.reshape(DB, 1, D_MODEL),
            stk(outs_p[0]), stk(outs_p[1]), stk(outs_p[2]), stk(outs_p[3]), stk(outs_p[4]),
            win_kv[0].reshape(DEPTH, DB, WINDOW, HA_KV, HD_A), win_kv[1].reshape(DEPTH, DB, WINDOW, HA_KV, HD_A),
            ret_all[0].reshape(DEPTH, DB, HB, DK_B, DV_B), ssm_all[0].reshape(DEPTH, DB, HC, HD_C, N_C),
            stk(conv_s))


def kernel(x_prompt, x_sample, cache_win_k, cache_win_v, state_ret, state_ssm, state_conv, c_prompt, c_sample,
           rel_bias_table, attn_sinks, norm1_w, norm2_w, ada_w, ada_b, w_in, conv_w, conv_b, dt_bias, A_log,
           D_skip, ssm_norm_w, w_out, w_up, w_down, final_norm_w):
    return _forward(x_prompt, x_sample, cache_win_k, cache_win_v, state_ret, state_ssm, state_conv,
                    c_prompt, c_sample, rel_bias_table, attn_sinks, norm1_w, norm2_w, ada_w, ada_b,
                    w_in, conv_w, conv_b, dt_bias, A_log, D_skip, ssm_norm_w, w_out, w_up, w_down,
                    final_norm_w, prompt_tile=PROMPT_TILE, mlp_rows=MLP_ROWS, mlp_ff=MLP_FF)
```

```python
import functools
import math

import numpy as np
import jax
import jax.numpy as jnp
from jax import lax
from jax.experimental import pallas as pl
from jax.experimental.pallas import tpu as pltpu

F32 = jnp.float32
BF16 = jnp.bfloat16

D_MODEL = 1024
DEPTH = 2
PAST_LEN = 16384
WINDOW = 128
HA_Q = 16
HA_KV = 4
HD_A = 64
NUM_BUCKETS = 32
MAX_DISTANCE = WINDOW
HB = 8
DK_B = 64
DV_B = 128
HC = 16
HD_C = 64
N_C = 128
G_C = 2
CONV_W = 4
CONV_DIM = D_MODEL + 2 * G_C * N_C
D_FF = 4 * D_MODEL
EPS = 1e-6
NEG = -1e30
LOG2E = 1.4426950408889634

O_AQ, O_AK, O_AV = 0, 1024, 1280
O_BQ, O_BK, O_BV, O_BG = 1536, 2048, 2560, 3584
O_CZ, O_CXBC, O_CDT, O_GTS = 4608, 5632, 7168, 7184
SEG_A = 2560
SEG_B = 4096
SEG_C = 3712
C_DT = 2560
C_GC = 2688

VMEM_LIMIT_V7X = 56 * 1024 * 1024
ROWS_PER_SAMPLE_STEP = 8


def _cparams(n_axes):
    return pltpu.CompilerParams(dimension_semantics=("arbitrary",) * n_axes,
                                vmem_limit_bytes=VMEM_LIMIT_V7X)


def _dot(a, b):
    return jnp.dot(a, b, preferred_element_type=F32)


def _dot_nt(a, b):
    return lax.dot_general(a, b, (((1,), (1,)), ((), ())), preferred_element_type=F32)


def _dot_tn(a, b):
    return lax.dot_general(a, b, (((0,), (0,)), ((), ())), preferred_element_type=F32)


def _sigmoid(x):
    return 0.5 * (jnp.tanh(0.5 * x) + 1.0)


def _silu(x):
    return x * _sigmoid(x)


def _softplus(x):
    return jnp.maximum(x, 0.0) + jnp.log1p(jnp.exp(-jnp.abs(x)))


def _modnorm(x, nw, sc, sh):
    ms = jnp.mean(x * x, axis=-1, keepdims=True)
    return (x * lax.rsqrt(ms + EPS) * nw) * (1.0 + sc) + sh


def _rms_lanes(x):
    ms = jnp.mean(x * x, axis=-1, keepdims=True)
    return x * lax.rsqrt(ms + EPS)


def _pairswap(x):
    ax = x.ndim - 1
    n = x.shape[ax]
    lane = lax.broadcasted_iota(jnp.int32, x.shape, ax)
    nxt = pltpu.roll(x, n - 1, ax)
    prv = pltpu.roll(x, 1, ax)
    return jnp.where((lane & 1) == 0, nxt, prv)


def _split3(x):
    hi = x.astype(BF16)
    r1 = x - hi.astype(F32)
    mid = r1.astype(BF16)
    lo = (r1 - mid.astype(F32)).astype(BF16)
    return hi, mid, lo


def _gammas():
    return 1.0 - 2.0 ** (-5.0 - np.arange(HB, dtype=np.float64))


def _rot_tables(pos):
    theta = 1.0 / (10000.0 ** np.linspace(0.0, 1.0, DK_B // 2))
    ang = np.asarray(pos, np.float64)[:, None] * theta[None, :]
    cos = np.repeat(np.cos(ang), 2, axis=1)
    sin = np.repeat(np.sin(ang), 2, axis=1)
    sin[:, 0::2] *= -1.0
    return (np.tile(cos, (1, HB)).astype(np.float32), np.tile(sin, (1, HB)).astype(np.float32))


def _ret_tables():
    g = _gammas()
    L = WINDOW
    i = np.arange(L, dtype=np.float64)
    diff = i[:, None] - i[None, :]
    dm = np.where(diff >= 0, g[:, None, None] ** np.maximum(diff, 0.0), 0.0)
    qdec = np.repeat(g[None, :] ** (i[:, None] + 1.0), DK_B, axis=1)
    kdec = np.repeat(g[None, :] ** (L - 1.0 - i[:, None]), DK_B, axis=1)
    return dm.astype(np.float32), qdec.astype(np.float32), kdec.astype(np.float32)


def _t5_bucket_np(dist):
    max_exact = NUM_BUCKETS // 2
    n = np.maximum(dist, 0)
    nf = np.maximum(n, 1).astype(np.float32)
    large = max_exact + (np.log(nf / np.float32(max_exact)) / np.float32(math.log(MAX_DISTANCE / max_exact))
                         * np.float32(NUM_BUCKETS - max_exact)).astype(np.int32)
    large = np.minimum(large, NUM_BUCKETS - 1)
    return np.where(n < max_exact, n, large)


def _bias_tables(rel_table):
    gsz = HA_Q // HA_KV
    qi = np.arange(WINDOW)[None, :]
    kj = np.arange(WINDOW)[:, None]
    dist = np.where(kj > qi, qi + WINDOW - kj, qi - kj)
    onehot = _t5_bucket_np(dist)[..., None] == np.arange(NUM_BUCKETS)
    tab = rel_table.astype(F32)
    bias_kq = jnp.einsum('kqb,bh->hkq', jnp.asarray(onehot, F32), tab, precision=lax.Precision.HIGHEST)
    bias_t = bias_kq.reshape(HA_KV, gsz, WINDOW, WINDOW).transpose(0, 2, 1, 3).reshape(HA_KV, WINDOW, gsz * WINDOW)
    oh_s = _t5_bucket_np(WINDOW - 1 - np.arange(WINDOW))[:, None] == np.arange(NUM_BUCKETS)
    bias_s = jnp.einsum('jb,bh->hj', jnp.asarray(oh_s, F32), tab, precision=lax.Precision.HIGHEST)
    return bias_t, bias_s


def _ada_kernel(c_ref, w_ref, b_ref, o_ref):
    s = _silu(c_ref[...])
    o_ref[0] = _dot(s.astype(BF16), w_ref[0].astype(BF16)) + b_ref[0]


def _ada_call(c_all, ada_w, ada_b):
    n = c_all.shape[0]
    nb = 6
    return pl.pallas_call(
        _ada_kernel,
        grid=(DEPTH, nb),
        in_specs=[pl.BlockSpec((n, D_MODEL), lambda l, j: (0, 0)),
                  pl.BlockSpec((1, D_MODEL, D_MODEL), lambda l, j: (l, 0, j)),
                  pl.BlockSpec((1, 1, D_MODEL), lambda l, j: (l, 0, j))],
        out_specs=pl.BlockSpec((1, n, D_MODEL), lambda l, j: (l, 0, j)),
        out_shape=jax.ShapeDtypeStruct((DEPTH, n, 6 * D_MODEL), F32),
        compiler_params=_cparams(2),
        name="ada_mod",
    )(c_all, ada_w, ada_b.reshape(DEPTH, 1, 6 * D_MODEL))


def _pa_kernel(x_ref, mod_ref, n1_ref, wt_ref, wkv_ref, bias_ref, sink_ref, lowm_ref,
               mix_ref, ko_ref, vo_ref, zt_ref, kv_ref, kprev_ref, vtprev_ref, pen_ref, *, tt):
    t = pl.program_id(1)
    nchunk = tt // WINDOW
    kw = HA_KV * HD_A
    gsz = HA_Q // HA_KV

    @pl.when(t == 0)
    def _():
        kprev_ref[...] = jnp.zeros_like(kprev_ref)
        vtprev_ref[...] = jnp.zeros_like(vtprev_ref)
        pen_ref[...] = jnp.full(pen_ref.shape, NEG, F32)

    mod = mod_ref[0]
    h = _modnorm(x_ref[0], n1_ref[...], mod[:, D_MODEL:2 * D_MODEL], mod[:, 0:D_MODEL]).astype(BF16)
    kv_ref[...] = _dot(h, wkv_ref[...])
    for cc in range(nchunk // 2):
        z2 = _dot_nt(wt_ref[...], h[2 * WINDOW * cc:2 * WINDOW * (cc + 1), :])
        zt_ref[2 * cc] = z2[:, 0:WINDOW]
        zt_ref[2 * cc + 1] = z2[:, WINDOW:2 * WINDOW]
    qw = gsz * WINDOW
    lower = (lax.broadcasted_iota(jnp.int32, (WINDOW, qw), 0)
             > (lax.broadcasted_iota(jnp.int32, (WINDOW, qw), 1) & (WINDOW - 1)))

    def chunk(c, carry):
        r0 = pl.multiple_of(c * WINDOW, WINDOW)
        rows = pl.ds(r0, WINDOW)
        kc = kv_ref[rows, 0:kw]
        vc = kv_ref[rows, kw:2 * kw]
        vt = vc.T
        kk = jnp.concatenate([kprev_ref[...], kc], axis=0).astype(BF16)
        vvt = jnp.concatenate([vtprev_ref[...], vt], axis=1).astype(BF16)
        qt = zt_ref[c, 0:D_MODEL, :].astype(BF16)
        pen = pen_ref[0:1, :]
        s_all = []
        for g in range(HA_KV):
            gs = slice(HD_A * g, HD_A * (g + 1))
            qcat = jnp.concatenate([qt[HD_A * (gsz * g + j):HD_A * (gsz * g + j + 1), :] for j in range(gsz)], axis=1)
            s_all.append(_dot(kk[:, gs], qcat))
        p_all = []
        for g in range(HA_KV):
            sg = jnp.where(lower, s_all[g][0:WINDOW, :] + pen, s_all[g][WINDOW:2 * WINDOW, :]) + bias_ref[g]
            sink = sink_ref[g]
            m = jnp.maximum(jnp.max(sg, axis=0, keepdims=True), sink)
            pw = jnp.exp(sg - m)
            den = jnp.sum(pw, axis=0, keepdims=True) + jnp.exp(sink - m)
            pb = pw.astype(BF16)
            p_prev = pb * lowm_ref[...]
            p_all.append((jnp.concatenate([p_prev, pb - p_prev], axis=0), 1.0 / den))
        pieces = []
        for g in range(HA_KV):
            gs = slice(HD_A * g, HD_A * (g + 1))
            p, rden = p_all[g]
            ot = _dot(vvt[gs, :], p) * rden
            pieces += [ot[:, WINDOW * j:WINDOW * (j + 1)] for j in range(gsz)]
        oat = jnp.concatenate(pieces, axis=0)
        mixt = _sigmoid(zt_ref[c, D_MODEL:2 * D_MODEL, :]) * oat
        mix_ref[0, rows, :] = mixt.T
        kprev_ref[...] = kc
        vtprev_ref[...] = vt
        pen_ref[...] = jnp.zeros_like(pen_ref)
        ko_ref[0] = kc
        vo_ref[0] = vc
        return carry

    lax.fori_loop(0, nchunk, chunk, 0, unroll=True)


def _pa_call(x, mod3, n1, wqgt, wkv, bias_t, sink_rows, tt):
    B, T, _ = x.shape
    kw = HA_KV * HD_A
    qw = (HA_Q // HA_KV) * WINDOW
    kern = functools.partial(_pa_kernel, tt=tt)
    lowm = (np.arange(WINDOW)[:, None] > (np.arange(qw)[None, :] % WINDOW)).astype(np.float32)
    return pl.pallas_call(
        kern,
        grid=(B, T // tt),
        in_specs=[pl.BlockSpec((1, tt, D_MODEL), lambda b, t: (b, t, 0)),
                  pl.BlockSpec((1, 1, 6 * D_MODEL), lambda b, t: (b, 0, 0)),
                  pl.BlockSpec((1, D_MODEL), lambda b, t: (0, 0)),
                  pl.BlockSpec((2 * D_MODEL, D_MODEL), lambda b, t: (0, 0)),
                  pl.BlockSpec((D_MODEL, 2 * kw), lambda b, t: (0, 0)),
                  pl.BlockSpec((HA_KV, WINDOW, qw), lambda b, t: (0, 0, 0)),
                  pl.BlockSpec((HA_KV, 1, qw), lambda b, t: (0, 0, 0)),
                  pl.BlockSpec((WINDOW, qw), lambda b, t: (0, 0))],
        out_specs=[pl.BlockSpec((1, tt, D_MODEL), lambda b, t: (b, t, 0)),
                   pl.BlockSpec((1, WINDOW, kw), lambda b, t: (b, 0, 0)),
                   pl.BlockSpec((1, WINDOW, kw), lambda b, t: (b, 0, 0))],
        out_shape=[jax.ShapeDtypeStruct((B, T, D_MODEL), F32),
                   jax.ShapeDtypeStruct((B, WINDOW, kw), F32),
                   jax.ShapeDtypeStruct((B, WINDOW, kw), F32)],
        scratch_shapes=[pltpu.VMEM((tt // WINDOW, 2 * D_MODEL, WINDOW), F32),
                        pltpu.VMEM((tt, 2 * kw), F32),
                        pltpu.VMEM((WINDOW, kw), F32),
                        pltpu.VMEM((kw, WINDOW), F32),
                        pltpu.VMEM((8, qw), F32)],
        compiler_params=_cparams(2),
        name="prompt_attn",
    )(x, mod3, n1, wqgt, wkv, bias_t, sink_rows, jnp.asarray(lowm, BF16))


def _pb_kernel(x_ref, mod_ref, n1_ref, wt_ref, cos_ref, sin_ref, qdec_ref, kdec_ref, dm_ref, mixa_ref,
               mix_ref, so_ref, zt_ref, s_ref, *, tt, glast):
    t = pl.program_id(1)
    nchunk = tt // WINDOW
    hw = HB * DK_B // 2
    hp = DK_B // 2

    @pl.when(t == 0)
    def _():
        s_ref[...] = jnp.zeros_like(s_ref)

    mod = mod_ref[0]
    h = _modnorm(x_ref[0], n1_ref[...], mod[:, D_MODEL:2 * D_MODEL], mod[:, 0:D_MODEL]).astype(BF16)
    for cc in range(nchunk // 2):
        z2 = _dot_nt(wt_ref[...], h[2 * WINDOW * cc:2 * WINDOW * (cc + 1), :])
        zt_ref[2 * cc] = z2[:, 0:WINDOW]
        zt_ref[2 * cc + 1] = z2[:, WINDOW:2 * WINDOW]

    def head_rows(pair, hh):
        return jnp.concatenate([pair[0][hp * hh:hp * (hh + 1), :], pair[1][hp * hh:hp * (hh + 1), :]], axis=0)

    def chunk(c, carry):
        r0 = pl.multiple_of(c * WINDOW, WINDOW)
        rows = pl.ds(r0, WINDOW)
        cos = cos_ref[c]
        sin = sin_ref[c]
        q1 = zt_ref[c, 0:hw, :]
        q2 = zt_ref[c, hw:2 * hw, :]
        k1 = zt_ref[c, 2 * hw:3 * hw, :]
        k2 = zt_ref[c, 3 * hw:4 * hw, :]
        rq = (q1 * cos - q2 * sin, q1 * sin + q2 * cos)
        rk = (k1 * cos - k2 * sin, k1 * sin + k2 * cos)
        qdec = qdec_ref[...]
        kdec = kdec_ref[...]
        qb = tuple(a.astype(BF16) for a in rq)
        kb = tuple(a.astype(BF16) for a in rk)
        qd = tuple((a * qdec).astype(BF16) for a in rq)
        kd = tuple((a * kdec).astype(BF16) for a in rk)
        vt = zt_ref[c, 4 * hw:4 * hw + HB * DV_B, :].astype(BF16)
        inner, cross, supd = [], [], []
        for hh in range(HB):
            s_old = s_ref[DK_B * hh:DK_B * (hh + 1), :]
            inner.append(_dot_tn(head_rows(qb, hh), head_rows(kb, hh)))
            cross.append(_dot_tn(s_old.astype(BF16), head_rows(qd, hh)))
            supd.append(glast[hh] * s_old + _dot_nt(head_rows(kd, hh), vt[DV_B * hh:DV_B * (hh + 1), :]))
        innd = [(inner[hh] * dm_ref[hh]).astype(BF16) for hh in range(HB)]
        outs = []
        for hh in range(HB):
            ot = _dot_nt(vt[DV_B * hh:DV_B * (hh + 1), :], innd[hh]) + cross[hh]
            s_ref[DK_B * hh:DK_B * (hh + 1), :] = supd[hh]
            ms = jnp.mean(ot * ot, axis=0, keepdims=True)
            outs.append(ot * lax.rsqrt(ms + EPS))
        obt = jnp.concatenate(outs, axis=0)
        bgt = zt_ref[c, 4 * hw + D_MODEL:4 * hw + 2 * D_MODEL, :]
        gbt = zt_ref[c, 4 * hw + 2 * D_MODEL:4 * hw + 3 * D_MODEL, :]
        mixt = _sigmoid(gbt) * (_silu(bgt) * obt)
        mix_ref[0, rows, :] = mixa_ref[0, rows, :] + mixt.T
        return carry

    lax.fori_loop(0, nchunk, chunk, 0, unroll=True)
    so_ref[0] = s_ref[...]


def _deinterleave_pairs(w):
    return w.reshape(w.shape[0], HB, DK_B // 2, 2).transpose(0, 3, 1, 2).reshape(w.shape[0], HB * DK_B)


def _rot_tables_t(T):
    theta = 1.0 / (10000.0 ** np.linspace(0.0, 1.0, DK_B // 2))
    ang = theta[:, None] * np.arange(T, dtype=np.float64)[None, :]

    def lay(a):
        a = np.tile(a, (HB, 1))
        return np.ascontiguousarray(a.reshape(a.shape[0], T // WINDOW, WINDOW).transpose(1, 0, 2)).astype(np.float32)

    return lay(np.cos(ang)), lay(np.sin(ang))


def _pb_call(x, mod3, n1, wbt, mixa, tt):
    B, T, _ = x.shape
    hw = HB * DK_B // 2
    cos, sin = _rot_tables_t(T)
    dm, qdec, kdec = _ret_tables()
    qdec_t = np.ascontiguousarray(qdec[:, ::2].T)
    kdec_t = np.ascontiguousarray(kdec[:, ::2].T)
    glast = tuple(float(v) for v in (_gammas() ** WINDOW))
    kern = functools.partial(_pb_kernel, tt=tt, glast=glast)
    full2 = lambda b, t: (0, 0)
    nct = tt // WINDOW
    return pl.pallas_call(
        kern,
        grid=(B, T // tt),
        in_specs=[pl.BlockSpec((1, tt, D_MODEL), lambda b, t: (b, t, 0)),
                  pl.BlockSpec((1, 1, 6 * D_MODEL), lambda b, t: (b, 0, 0)),
                  pl.BlockSpec((1, D_MODEL), full2),
                  pl.BlockSpec((SEG_B, D_MODEL), full2),
                  pl.BlockSpec((nct, hw, WINDOW), lambda b, t: (t, 0, 0)),
                  pl.BlockSpec((nct, hw, WINDOW), lambda b, t: (t, 0, 0)),
                  pl.BlockSpec((hw, WINDOW), full2),
                  pl.BlockSpec((hw, WINDOW), full2),
                  pl.BlockSpec((HB, WINDOW, WINDOW), lambda b, t: (0, 0, 0)),
                  pl.BlockSpec((1, tt, D_MODEL), lambda b, t: (b, t, 0))],
        out_specs=[pl.BlockSpec((1, tt, D_MODEL), lambda b, t: (b, t, 0)),
                   pl.BlockSpec((1, HB * DK_B, DV_B), lambda b, t: (b, 0, 0))],
        out_shape=[jax.ShapeDtypeStruct((B, T, D_MODEL), F32),
                   jax.ShapeDtypeStruct((B, HB * DK_B, DV_B), F32)],
        scratch_shapes=[pltpu.VMEM((nct, SEG_B, WINDOW), F32),
                        pltpu.VMEM((HB * DK_B, DV_B), F32)],
        compiler_params=_cparams(2),
        name="prompt_retention",
    )(x, mod3, n1, wbt, jnp.asarray(cos), jnp.asarray(sin), jnp.asarray(qdec_t), jnp.asarray(kdec_t),
      jnp.asarray(dm), mixa)


def _pc_kernel(x_ref, mod_ref, n1_ref, w_ref, cw_ref, cb_ref, dtb_ref, a_ref, dsk_ref, snw_ref, tri_ref,
               mixab_ref, wo_ref,
               xo_ref, ho_ref, co_ref,
               z_ref, xbuf_ref, xbc_ref, dt_ref, hst_ref, mixs_ref, *, tt):
    t = pl.program_id(1)

    @pl.when(t == 0)
    def _():
        xbuf_ref[0:8, :] = jnp.zeros((8, CONV_DIM), F32)
        hst_ref[...] = jnp.zeros_like(hst_ref)

    x = x_ref[0]
    mod = mod_ref[0]
    h = _modnorm(x, n1_ref[...], mod[:, D_MODEL:2 * D_MODEL], mod[:, 0:D_MODEL])
    z_ref[...] = _dot(h.astype(BF16), w_ref[...])

    xbuf_ref[8:8 + tt, :] = z_ref[:, 1024:2560]
    acc = cb_ref[...]
    for i in range(CONV_W):
        acc = acc + xbuf_ref[5 + i:5 + i + tt, :] * cw_ref[i:i + 1, :]
    xbc_ref[...] = _silu(acc)
    co_ref[0] = xbuf_ref[tt + 5:tt + 8, :]
    xbuf_ref[0:8, :] = xbuf_ref[tt:tt + 8, :]
    dt_ref[...] = _softplus(z_ref[:, C_DT:C_DT + 128] + dtb_ref[...])

    ii = lax.broadcasted_iota(jnp.int32, (WINDOW, WINDOW), 0)
    jj = lax.broadcasted_iota(jnp.int32, (WINDOW, WINDOW), 1)
    causal = ii >= jj
    hpg = HC // G_C

    def chunk(c):
        rows = slice(WINDOW * c, WINDOW * (c + 1))
        xc = xbc_ref[rows, 0:1024]
        bmat = xbc_ref[rows, 1024:1280]
        cmat = xbc_ref[rows, 1280:1536]
        dtc = dt_ref[rows, :]
        acum = jnp.dot(tri_ref[...], dtc * a_ref[...], precision=lax.Precision.HIGHEST,
                       preferred_element_type=F32)
        acum = acum * LOG2E
        acum_t = acum.T
        rowp_t = acum_t - jnp.log2(dtc.T)
        x_t = xc.T
        xb = xc.astype(BF16)
        bb = bmat.astype(BF16)
        cb16 = cmat.astype(BF16)
        ys = []
        for g in range(G_C):
            ns = slice(N_C * g, N_C * (g + 1))
            cbg = _dot_nt(cb16[:, ns], bb[:, ns])
            for hh in range(hpg * g, hpg * (g + 1)):
                ps = slice(HD_C * hh, HD_C * (hh + 1))
                colb = jnp.broadcast_to(acum[:, hh:hh + 1], (WINDOW, WINDOW))
                rowp = rowp_t[hh:hh + 1, :]
                m = cbg * jnp.exp2(jnp.where(causal, colb - rowp, NEG))
                hs = hst_ref[ps, :]
                ecolb = jnp.exp2(colb)
                cs = cmat[:, ns] * ecolb
                y = _dot(m.astype(BF16), xb[:, ps]) + _dot_nt(cs.astype(BF16), hs.astype(BF16))
                wrow = jnp.exp2(colb[WINDOW - 1:WINDOW, :] - rowp)
                xw = (x_t[ps, :] * wrow).astype(BF16)
                hst_ref[ps, :] = ecolb[WINDOW - 1:WINDOW, :] * hs + _dot(xw, bb[:, ns])
                ys.append(y)
        y = jnp.concatenate(ys, axis=1) + dsk_ref[...] * xc
        yc = y * _silu(z_ref[rows, 0:1024])
        gw = D_MODEL // G_C
        oc = jnp.concatenate([_rms_lanes(yc[:, gw * g:gw * (g + 1)]) for g in range(G_C)], axis=1) * snw_ref[...]
        gc = _sigmoid(z_ref[rows, C_GC:C_GC + 1024])
        mixs_ref[rows, :] = (mixab_ref[0, rows, :] + gc * oc).astype(BF16)

    g1 = mod[:, 2 * D_MODEL:3 * D_MODEL]
    for c in range(tt // WINDOW):
        chunk(c)
        if c % 2 == 1:
            pr = slice(WINDOW * (c - 1), WINDOW * (c + 1))
            xo_ref[0, pr, :] = x_ref[0, pr, :] + g1 * _dot(mixs_ref[pr, :], wo_ref[...])
    ho_ref[0] = hst_ref[...]


def _pc_call(x, mod3, n1, wc, cw, cb, dtb_pad, a_pad, dsk_full, snw, mixab, wo, tt):
    B, T, _ = x.shape
    tri = jnp.asarray(np.tril(np.ones((WINDOW, WINDOW), np.float32)))
    kern = functools.partial(_pc_kernel, tt=tt)
    full2 = lambda b, t: (0, 0)
    return pl.pallas_call(
        kern,
        grid=(B, T // tt),
        in_specs=[pl.BlockSpec((1, tt, D_MODEL), lambda b, t: (b, t, 0)),
                  pl.BlockSpec((1, 1, 6 * D_MODEL), lambda b, t: (b, 0, 0)),
                  pl.BlockSpec((1, D_MODEL), full2),
                  pl.BlockSpec((D_MODEL, SEG_C), full2),
                  pl.BlockSpec((CONV_W, CONV_DIM), full2),
                  pl.BlockSpec((1, CONV_DIM), full2),
                  pl.BlockSpec((1, 128), full2),
                  pl.BlockSpec((1, 128), full2),
                  pl.BlockSpec((1, D_MODEL), full2),
                  pl.BlockSpec((1, D_MODEL), full2),
                  pl.BlockSpec((WINDOW, WINDOW), full2),
                  pl.BlockSpec((1, tt, D_MODEL), lambda b, t: (b, t, 0)),
                  pl.BlockSpec((D_MODEL, D_MODEL), full2)],
        out_specs=[pl.BlockSpec((1, tt, D_MODEL), lambda b, t: (b, t, 0)),
                   pl.BlockSpec((1, HC * HD_C, N_C), lambda b, t: (b, 0, 0)),
                   pl.BlockSpec((1, CONV_W - 1, CONV_DIM), lambda b, t: (b, 0, 0))],
        out_shape=[jax.ShapeDtypeStruct((B, T, D_MODEL), F32),
                   jax.ShapeDtypeStruct((B, HC * HD_C, N_C), F32),
                   jax.ShapeDtypeStruct((B, CONV_W - 1, CONV_DIM), F32)],
        scratch_shapes=[pltpu.VMEM((tt, SEG_C), F32),
                        pltpu.VMEM((tt + 8, CONV_DIM), F32),
                        pltpu.VMEM((tt, CONV_DIM), F32),
                        pltpu.VMEM((tt, 128), F32),
                        pltpu.VMEM((HC * HD_C, N_C), F32),
                        pltpu.VMEM((tt, D_MODEL), BF16)],
        compiler_params=_cparams(2),
        name="prompt_ssd_out",
    )(x, mod3, n1, wc, cw, cb, dtb_pad, a_pad, dsk_full, snw, tri, mixab, wo)


def _mlp_kernel(x_ref, sh_ref, sc_ref, g_ref, n2_ref, wu_ref, wd_ref, fn_ref, o_ref, h_ref, acc_ref,
                *, nf, final, per_row):
    f = pl.program_id(1)
    rd = (lambda r: r[...]) if per_row else (lambda r: r[0])

    @pl.when(f == 0)
    def _():
        h_ref[...] = _modnorm(x_ref[...], n2_ref[...], rd(sc_ref), rd(sh_ref)).astype(BF16)
        acc_ref[...] = jnp.zeros_like(acc_ref)

    u = _dot(h_ref[...], wu_ref[0])
    u = jnp.square(jnp.maximum(u, 0.0)).astype(BF16)
    acc_ref[...] += _dot(u, wd_ref[0])

    @pl.when(f == nf - 1)
    def _():
        y = x_ref[...] + rd(g_ref) * acc_ref[...]
        if final:
            y = _rms_lanes(y) * fn_ref[...]
        o_ref[...] = y


def _mlp_call(x2, mod, n2, wu, wd, fn, layer, tm, tf, rows_per_mod, final):
    M = x2.shape[0]
    nf = D_FF // tf
    per_row = rows_per_mod == 1
    if per_row:
        mspec = lambda j: pl.BlockSpec((tm, D_MODEL), lambda m, f: (m, j))
    else:
        mspec = lambda j: pl.BlockSpec((1, 1, D_MODEL), lambda m, f: ((m * tm) // rows_per_mod, 0, j))
    kern = functools.partial(_mlp_kernel, nf=nf, final=final, per_row=per_row)
    return pl.pallas_call(
        kern,
        grid=(M // tm, nf),
        in_specs=[pl.BlockSpec((tm, D_MODEL), lambda m, f: (m, 0)),
                  mspec(3), mspec(4), mspec(5),
                  pl.BlockSpec((1, D_MODEL), lambda m, f: (0, 0)),
                  pl.BlockSpec((1, D_MODEL, tf), lambda m, f: (layer, 0, f)),
                  pl.BlockSpec((1, tf, D_MODEL), lambda m, f: (layer, f, 0)),
                  pl.BlockSpec((1, D_MODEL), lambda m, f: (0, 0))],
        out_specs=pl.BlockSpec((tm, D_MODEL), lambda m, f: (m, 0)),
        out_shape=jax.ShapeDtypeStruct((M, D_MODEL), F32),
        scratch_shapes=[pltpu.VMEM((tm, D_MODEL), BF16),
                        pltpu.VMEM((tm, D_MODEL), F32)],
        compiler_params=_cparams(2),
        name="mlp",
    )(x2, mod, mod, mod, n2, wu, wd, fn)


def _sproj_kernel(x_ref, sh_ref, sc_ref, n1_ref, w_ref, o_ref):
    h = _modnorm(x_ref[...], n1_ref[...], sc_ref[...], sh_ref[...])
    o_ref[...] = _dot(h.astype(BF16), w_ref[...])


def _sproj_call(xs, mod_s, n1, w):
    n, width = xs.shape[0], w.shape[1]
    return pl.pallas_call(
        _sproj_kernel,
        grid=(1,),
        in_specs=[pl.BlockSpec((n, D_MODEL), lambda i: (0, 0)),
                  pl.BlockSpec((n, D_MODEL), lambda i: (0, 0)),
                  pl.BlockSpec((n, D_MODEL), lambda i: (0, 1)),
                  pl.BlockSpec((1, D_MODEL), lambda i: (0, 0)),
                  pl.BlockSpec((D_MODEL, width), lambda i: (0, 0))],
        out_specs=pl.BlockSpec((n, width), lambda i: (0, 0)),
        out_shape=jax.ShapeDtypeStruct((n, width), F32),
        compiler_params=_cparams(1),
        name="sample_proj",
    )(xs, mod_s, mod_s, n1, w)


def _carry_outputs(prev, n_in, nsteps):
    nslab = 1 if prev else DEPTH
    extra = [pl.BlockSpec(memory_space=pl.ANY) for _ in prev]
    alias = {n_in + k: k for k in range(len(prev))}
    row = lambda p, i: jnp.where(p == 0, i, nsteps - 1)
    return nslab, extra, alias, row


def _with_fill(body, state_outs):
    def kern(*refs):
        p = pl.program_id(0)

        @pl.when(p == 0)
        def _():
            body(*refs)

        @pl.when(p != 0)
        def _():
            for k in state_outs:
                refs[k][...] = jnp.zeros_like(refs[k])

    return kern


def _sa_kernel(q_ref, kv_ref, ck_ref, cv_ref, bias_ref, sink_ref, *rest):
    ko_ref, vo_ref, oa_ref = rest[-3:]
    nb = ROWS_PER_SAMPLE_STEP
    gsz = HA_Q // HA_KV
    rg = lax.broadcasted_iota(jnp.int32, (HA_Q, HD_A), 0) // gsz
    sink = sink_ref[...]
    kw = HA_KV * HD_A
    scores, vmats = [], []
    for i in range(nb):
        ko_ref[0, i, 0:WINDOW - 1, :] = ck_ref[0, i, 1:WINDOW, :]
        ko_ref[0, i, WINDOW - 1:WINDOW, :] = kv_ref[i:i + 1, 0:kw]
        vo_ref[0, i, 0:WINDOW - 1, :] = cv_ref[0, i, 1:WINDOW, :]
        vo_ref[0, i, WINDOW - 1:WINDOW, :] = kv_ref[i:i + 1, kw:2 * kw]
        kmat = ko_ref[0, i].astype(BF16)
        vmats.append(vo_ref[0, i].astype(BF16))
        q = q_ref[i] * (HD_A ** -0.5)
        qe = jnp.concatenate([jnp.where(rg == g, q, 0.0) for g in range(HA_KV)], axis=1).astype(BF16)
        scores.append(_dot_nt(qe, kmat))
    probs = []
    for i in range(nb):
        sc = scores[i] + bias_ref[...]
        m = jnp.maximum(jnp.max(sc, axis=-1, keepdims=True), sink)
        p = jnp.exp(sc - m)
        den = jnp.sum(p, axis=-1, keepdims=True) + jnp.exp(sink - m)
        probs.append((p.astype(BF16), den))
    outs = [_dot(probs[i][0], vmats[i]) for i in range(nb)]
    for i in range(nb):
        o = outs[i] / probs[i][1]
        o16 = jnp.zeros((HA_Q, HD_A), F32)
        for g in range(HA_KV):
            o16 = o16 + jnp.where(rg == g, o[:, HD_A * g:HD_A * (g + 1)], 0.0)
        oa_ref[i] = o16


def _sa_call(q3, kv, cache_k, cache_v, bias_s, sink_col, layer, prev):
    n = q3.shape[0]
    nb = ROWS_PER_SAMPLE_STEP
    kw = HA_KV * HD_A
    nslab, extra, alias, row = _carry_outputs(prev, 6, n // nb)
    cspec = pl.BlockSpec((1, nb, WINDOW, kw), lambda p, i: (layer, row(p, i), 0, 0))
    ospec = pl.BlockSpec((1, nb, WINDOW, kw), lambda p, i: ((layer + p) % DEPTH, i, 0, 0))
    return pl.pallas_call(
        _with_fill(_sa_kernel, (-3, -2)),
        grid=(nslab, n // nb),
        in_specs=[pl.BlockSpec((nb, HA_Q, HD_A), lambda p, i: (row(p, i), 0, 0)),
                  pl.BlockSpec((nb, 2 * kw), lambda p, i: (row(p, i), 0)),
                  cspec, cspec,
                  pl.BlockSpec((HA_Q, WINDOW), lambda p, i: (0, 0)),
                  pl.BlockSpec((HA_Q, 1), lambda p, i: (0, 0))] + extra,
        out_specs=[ospec, ospec, pl.BlockSpec((nb, HA_Q, HD_A), lambda p, i: (row(p, i), 0, 0))],
        out_shape=[jax.ShapeDtypeStruct((DEPTH, n, WINDOW, kw), F32),
                   jax.ShapeDtypeStruct((DEPTH, n, WINDOW, kw), F32),
                   jax.ShapeDtypeStruct((n, HA_Q, HD_A), F32)],
        input_output_aliases=alias,
        compiler_params=_cparams(2),
        name="sample_attn",
    )(q3, kv, cache_k, cache_v, bias_s, sink_col, *prev)


def _sr_kernel(zb_ref, cos_ref, sin_ref, gcol_ref, s_ref, *rest):
    so_ref, o_ref = rest[-2:]
    nb = ROWS_PER_SAMPLE_STEP
    wq = HB * DK_B
    cos = cos_ref[...]
    sin = sin_ref[...]
    qf = zb_ref[:, 0:wq]
    kf = zb_ref[:, wq:2 * wq]
    qr = qf * cos + _pairswap(qf) * sin
    kr = (kf * cos + _pairswap(kf) * sin) * (DK_B ** -0.5)
    v = zb_ref[:, 2 * wq:2 * wq + HB * DV_B]
    r8 = lax.broadcasted_iota(jnp.int32, (HB, wq), 0)
    hl = lax.broadcasted_iota(jnp.int32, (HB, wq), 1) // DK_B
    rv = lax.broadcasted_iota(jnp.int32, (HB, DV_B), 0)
    outers, q8s = [], []
    for i in range(nb):
        k8 = jnp.where(hl == r8, jnp.broadcast_to(kr[i:i + 1, :], (HB, wq)), 0.0).astype(BF16)
        q8s.append(jnp.where(hl == r8, jnp.broadcast_to(qr[i:i + 1, :], (HB, wq)), 0.0).astype(BF16))
        v8 = jnp.zeros((HB, DV_B), F32)
        for r in range(HB):
            v8 = jnp.where(rv == r, jnp.broadcast_to(v[i:i + 1, DV_B * r:DV_B * (r + 1)], (HB, DV_B)), v8)
        outers.append(_dot_tn(k8, v8.astype(BF16)))
    s_news = []
    for i in range(nb):
        s_new = gcol_ref[...] * s_ref[0, i] + outers[i]
        so_ref[0, i] = s_new
        s_news.append(s_new.astype(BF16))
    for i in range(nb):
        o_ref[i] = _dot(q8s[i], s_news[i])


def _sr_call(zb, state, layer, prev):
    n = zb.shape[0]
    nb = ROWS_PER_SAMPLE_STEP
    wq = HB * DK_B
    cos, sin = _rot_tables(np.array([PAST_LEN]))
    gcol = np.repeat(_gammas(), DK_B)[:, None] * np.ones((1, DV_B))
    nslab, extra, alias, row = _carry_outputs(prev, 5, n // nb)
    full = lambda p, i: (0, 0)
    return pl.pallas_call(
        _with_fill(_sr_kernel, (-2,)),
        grid=(nslab, n // nb),
        in_specs=[pl.BlockSpec((nb, SEG_B), lambda p, i: (row(p, i), 0)),
                  pl.BlockSpec((1, wq), full),
                  pl.BlockSpec((1, wq), full),
                  pl.BlockSpec((wq, DV_B), full),
                  pl.BlockSpec((1, nb, wq, DV_B), lambda p, i: (layer, row(p, i), 0, 0))] + extra,
        out_specs=[pl.BlockSpec((1, nb, wq, DV_B), lambda p, i: ((layer + p) % DEPTH, i, 0, 0)),
                   pl.BlockSpec((nb, HB, DV_B), lambda p, i: (row(p, i), 0, 0))],
        out_shape=[jax.ShapeDtypeStruct((DEPTH, n, wq, DV_B), F32),
                   jax.ShapeDtypeStruct((n, HB, DV_B), F32)],
        input_output_aliases=alias,
        compiler_params=_cparams(2),
        name="sample_retention",
    )(zb, jnp.asarray(cos), jnp.asarray(sin), jnp.asarray(gcol.astype(np.float32)), state, *prev)


def _ss_kernel(zc_ref, cs_ref, cw_ref, cb_ref, dtb_ref, a_ref, e3_ref, h_ref, *rest):
    ho_ref, cso_ref, y_ref, xc_ref = rest[-4:]
    nb = ROWS_PER_SAMPLE_STEP
    cx = zc_ref[:, 1024:2560]
    taps = [cs_ref[0, :, CONV_DIM * i:CONV_DIM * (i + 1)] for i in range(CONV_W - 1)] + [cx]
    acc = cb_ref[...]
    for i in range(CONV_W):
        acc = acc + taps[i] * cw_ref[i:i + 1, :]
    xbc = _silu(acc)
    cso_ref[...] = jnp.concatenate(taps[1:], axis=1)
    xc = xbc[:, 0:1024]
    bmat = xbc[:, 1024:1280]
    cmat = xbc[:, 1280:1536]
    dt = _softplus(zc_ref[:, C_DT:C_DT + 128] + dtb_ref[...])
    da = jnp.exp(dt * a_ref[...])
    dt_e = _dot(jnp.concatenate(_split3(dt), axis=1), e3_ref[...])
    da_e = _dot(jnp.concatenate(_split3(da), axis=1), e3_ref[...])
    dtx = dt_e * xc
    gw = (HC // G_C) * HD_C
    r8 = lax.broadcasted_iota(jnp.int32, (nb, gw), 0)
    rn = lax.broadcasted_iota(jnp.int32, (nb, N_C), 0)
    ones8 = jnp.ones((nb, N_C), BF16)
    prods = {}
    for g in range(G_C):
        ws = slice(gw * g, gw * (g + 1))
        bg16 = bmat[:, N_C * g:N_C * (g + 1)].astype(BF16)
        for i in range(nb):
            x8 = jnp.where(r8 == i, dtx[:, ws], 0.0).astype(BF16)
            outer = _dot_tn(x8, bg16)
            hi, mid, lo = (p.astype(F32) for p in _split3(jnp.broadcast_to(da_e[i:i + 1, ws], (nb, gw))))
            l3 = jnp.where(r8 == 0, hi, jnp.where(r8 == 1, mid, jnp.where(r8 == 2, lo, 0.0)))
            prods[g, i] = (outer, _dot_tn(l3.astype(BF16), ones8))
    h16 = {}
    for g in range(G_C):
        ws = slice(gw * g, gw * (g + 1))
        for i in range(nb):
            outer, dacol = prods[g, i]
            h_new = dacol * h_ref[0, i, ws, :] + outer
            ho_ref[0, i, ws, :] = h_new
            h16[g, i] = h_new.astype(BF16)
    ycols = []
    for g in range(G_C):
        ns = slice(N_C * g, N_C * (g + 1))
        yacc = jnp.zeros((nb, gw), F32)
        for i in range(nb):
            c8 = jnp.where(rn == i, cmat[:, ns], 0.0).astype(BF16)
            yacc = yacc + _dot_nt(c8, h16[g, i])
        ycols.append(yacc)
    y_ref[...] = jnp.concatenate(ycols, axis=1)
    xc_ref[...] = xc


def _ss_call(zc, conv_state, hstate, cw, cb, dtb_pad, a_pad, layer, prev):
    n = zc.shape[0]
    nb = ROWS_PER_SAMPLE_STEP
    e = np.zeros((128, D_MODEL), np.float32)
    for hh in range(HC):
        e[hh, HD_C * hh:HD_C * (hh + 1)] = 1.0
    e3 = jnp.asarray(np.concatenate([e, e, e], axis=0), dtype=BF16)
    cwid = (CONV_W - 1) * CONV_DIM
    full = lambda p, i: (0, 0)
    nslab, extra, alias, row = _carry_outputs(prev, 8, n // nb)
    return pl.pallas_call(
        _with_fill(_ss_kernel, (-4,)),
        grid=(nslab, n // nb),
        in_specs=[pl.BlockSpec((nb, SEG_C), lambda p, i: (row(p, i), 0)),
                  pl.BlockSpec((1, nb, cwid), lambda p, i: (layer, row(p, i), 0)),
                  pl.BlockSpec((CONV_W, CONV_DIM), full),
                  pl.BlockSpec((1, CONV_DIM), full),
                  pl.BlockSpec((1, 128), full),
                  pl.BlockSpec((1, 128), full),
                  pl.BlockSpec((3 * 128, D_MODEL), full),
                  pl.BlockSpec((1, nb, HC * HD_C, N_C), lambda p, i: (layer, row(p, i), 0, 0))] + extra,
        out_specs=[pl.BlockSpec((1, nb, HC * HD_C, N_C), lambda p, i: ((layer + p) % DEPTH, i, 0, 0)),
                   pl.BlockSpec((nb, cwid), lambda p, i: (row(p, i), 0)),
                   pl.BlockSpec((nb, D_MODEL), lambda p, i: (row(p, i), 0)),
                   pl.BlockSpec((nb, D_MODEL), lambda p, i: (row(p, i), 0))],
        out_shape=[jax.ShapeDtypeStruct((DEPTH, n, HC * HD_C, N_C), F32),
                   jax.ShapeDtypeStruct((n, cwid), F32),
                   jax.ShapeDtypeStruct((n, D_MODEL), F32),
                   jax.ShapeDtypeStruct((n, D_MODEL), F32)],
        input_output_aliases=alias,
        compiler_params=_cparams(2),
        name="sample_ssd",
    )(zc, conv_state, cw, cb, dtb_pad, a_pad, e3, hstate, *prev)


def _sm_kernel(x_ref, g1_ref, oa_ref, oret_ref, y_ref, xc_ref, za_ref, zb_ref, zc_ref,
               dsk_ref, snw_ref, wo_ref, o_ref):
    ob = jnp.concatenate([_rms_lanes(oret_ref[:, DV_B * hh:DV_B * (hh + 1)]) for hh in range(HB)], axis=1)
    ob = _silu(zb_ref[:, 2048:3072]) * ob
    yc = (y_ref[...] + dsk_ref[...] * xc_ref[...]) * _silu(zc_ref[:, 0:1024])
    gw = D_MODEL // G_C
    oc = jnp.concatenate([_rms_lanes(yc[:, gw * g:gw * (g + 1)]) for g in range(G_C)], axis=1) * snw_ref[...]
    mix = (_sigmoid(za_ref[:, 1536:2560]) * oa_ref[...] + _sigmoid(zb_ref[:, 3072:4096]) * ob
           + _sigmoid(zc_ref[:, C_GC:C_GC + 1024]) * oc)
    o_ref[...] = x_ref[...] + g1_ref[...] * _dot(mix.astype(BF16), wo_ref[...])


def _sm_call(xs, mod_s, oa, oret, y, xc, za, zb, zc, dsk_full, snw, wo):
    n = xs.shape[0]
    full = lambda i: (0, 0)
    row = lambda w: pl.BlockSpec((n, w), full)
    return pl.pallas_call(
        _sm_kernel,
        grid=(1,),
        in_specs=[row(D_MODEL),
                  pl.BlockSpec((n, D_MODEL), lambda i: (0, 2)),
                  row(D_MODEL), row(D_MODEL), row(D_MODEL), row(D_MODEL),
                  row(SEG_A), row(SEG_B), row(SEG_C),
                  pl.BlockSpec((1, D_MODEL), full),
                  pl.BlockSpec((1, D_MODEL), full),
                  pl.BlockSpec((D_MODEL, D_MODEL), full)],
        out_specs=row(D_MODEL),
        out_shape=jax.ShapeDtypeStruct((n, D_MODEL), F32),
        compiler_params=_cparams(1),
        name="sample_merge_out",
    )(xs, mod_s, oa, oret, y, xc, za, zb, zc, dsk_full, snw, wo)


PROMPT_TILE = 512
MLP_ROWS = 1024
MLP_FF = 1024


def _prep_w_in(w):
    wa = jnp.concatenate([w[:, O_AQ:O_BQ], w[:, O_GTS:O_GTS + 1024]], axis=1)
    wb = jnp.concatenate([w[:, O_BQ:O_CZ], w[:, O_GTS + 1024:O_GTS + 2048]], axis=1)
    wc = jnp.concatenate([w[:, O_CZ:O_CDT], jnp.pad(w[:, O_CDT:O_GTS], ((0, 0), (0, 128 - HC))),
                          w[:, O_GTS + 2048:O_GTS + 3072]], axis=1)
    return wa, wb, wc


def _forward(x_prompt, x_sample, cache_win_k, cache_win_v, state_ret, state_ssm, state_conv,
             c_prompt, c_sample, rel_bias_table, attn_sinks, norm1_w, norm2_w, ada_w, ada_b,
             w_in, conv_w, conv_b, dt_bias, A_log, D_skip, ssm_norm_w, w_out, w_up, w_down,
             final_norm_w, *, prompt_tile, mlp_rows, mlp_ff):
    B, T, _ = x_prompt.shape
    DB = x_sample.shape[0]
    kw = HA_KV * HD_A

    mod_all = _ada_call(jnp.concatenate([c_prompt, c_sample], axis=0), ada_w, ada_b)

    bias_t, bias_s = _bias_tables(rel_bias_table)

    w16 = w_in.astype(BF16)
    wu16 = w_up.astype(BF16)
    wd16 = w_down.astype(BF16)
    wo16 = w_out.astype(BF16)
    fn = final_norm_w.reshape(1, D_MODEL)
    ck = cache_win_k.reshape(DEPTH, DB, WINDOW, kw)
    cv = cache_win_v.reshape(DEPTH, DB, WINDOW, kw)
    sret = state_ret.reshape(DEPTH, DB, HB * DK_B, DV_B)
    sssm = state_ssm.reshape(DEPTH, DB, HC * HD_C, N_C)
    sconv = state_conv.reshape(DEPTH, DB, (CONV_W - 1) * CONV_DIM)

    xp = x_prompt
    xs = x_sample.reshape(DB, D_MODEL)
    outs_p = [[] for _ in range(5)]
    conv_s = []
    win_kv, ret_all, ssm_all = (), (), ()
    for l in range(DEPTH):
        wl = w16[l]
        wa, wb, wc = _prep_w_in(wl)
        n1 = norm1_w[l].reshape(1, D_MODEL)
        n2 = norm2_w[l].reshape(1, D_MODEL)
        cw = conv_w[l]
        cb = conv_b[l].reshape(1, CONV_DIM)
        dtb_pad = jnp.pad(dt_bias[l], (0, 128 - HC)).reshape(1, 128)
        a_pad = jnp.pad(-jnp.exp(A_log[l].astype(F32)), (0, 128 - HC)).reshape(1, 128)
        dsk_full = jnp.repeat(D_skip[l], HD_C).reshape(1, D_MODEL)
        snw = ssm_norm_w[l].reshape(1, D_MODEL)
        final = l == DEPTH - 1
        mod_p = mod_all[l, :B].reshape(B, 1, 6 * D_MODEL)
        mod_s = mod_all[l, B:]

        wqgt = jnp.concatenate([wl[:, O_AQ:O_AK] * (HD_A ** -0.5), wl[:, O_GTS:O_GTS + D_MODEL]], axis=1).T
        wkv = wl[:, O_AK:O_BQ]
        sink_rows = jnp.repeat(attn_sinks[l], WINDOW).reshape(HA_KV, 1, (HA_Q // HA_KV) * WINDOW)
        mixa, kbuf, vbuf = _pa_call(xp, mod_p, n1, wqgt, wkv, bias_t, sink_rows, prompt_tile)
        wbt = jnp.concatenate([_deinterleave_pairs(wl[:, O_BQ:O_BK]),
                               _deinterleave_pairs(wl[:, O_BK:O_BV]) * (DK_B ** -0.5), wl[:, O_BV:O_CZ],
                               wl[:, O_GTS + D_MODEL:O_GTS + 2 * D_MODEL]], axis=1).T
        mixab, s_perm = _pb_call(xp, mod_p, n1, wbt, mixa, prompt_tile)
        s_ret = s_perm.reshape(B, HB, 2, DK_B // 2, DV_B).transpose(0, 1, 3, 2, 4)
        x1, h_ssm, conv_new = _pc_call(xp, mod_p, n1, wc, cw, cb, dtb_pad, a_pad, dsk_full, snw,
                                       mixab, wo16[l], prompt_tile)
        xp = _mlp_call(x1.reshape(B * T, D_MODEL), mod_p, n2, wu16, wd16, fn, l,
                       mlp_rows, mlp_ff, T, final).reshape(B, T, D_MODEL)
        for lst, v in zip(outs_p, (kbuf.reshape(B, WINDOW, HA_KV, HD_A), vbuf.reshape(B, WINDOW, HA_KV, HD_A),
                                   s_ret.reshape(B, HB, DK_B, DV_B), h_ssm.reshape(B, HC, HD_C, N_C), conv_new)):
            lst.append(v)

        za = _sproj_call(xs, mod_s, n1, wa)
        zb = _sproj_call(xs, mod_s, n1, wb)
        zc = _sproj_call(xs, mod_s, n1, wc)
        q3 = za[:, 0:HA_Q * HD_A].reshape(DB, HA_Q, HD_A)
        kv = za[:, HA_Q * HD_A:HA_Q * HD_A + 2 * kw]
        ck_new, cv_new, oa3 = _sa_call(q3, kv, ck, cv, bias_s, attn_sinks[l].reshape(HA_Q, 1), l, win_kv)
        win_kv = (ck_new, cv_new)
        s_new, o3 = _sr_call(zb, sret, l, ret_all)
        ret_all = (s_new,)
        h_new, cs_new, y_s, xc_s = _ss_call(zc, sconv, sssm, cw, cb, dtb_pad, a_pad, l, ssm_all)
        ssm_all = (h_new,)
        xs1 = _sm_call(xs, mod_s, oa3.reshape(DB, D_MODEL), o3.reshape(DB, D_MODEL), y_s, xc_s,
                       za, zb, zc, dsk_full, snw, wo16[l])
        xs = _mlp_call(xs1, mod_s, n2, wu16, wd16, fn, l, DB, mlp_ff, 1, final)
        conv_s.append(cs_new.reshape(DB, CONV_W - 1, CONV_DIM))

    stk = lambda lst: jnp.stack(lst, axis=0)
    return (xp, xs.reshape(DB, 1, D_MODEL),
            stk(outs_p[0]), stk(outs_p[1]), stk(outs_p[2]), stk(outs_p[3]), stk(outs_p[4]),
            win_kv[0].reshape(DEPTH, DB, WINDOW, HA_KV, HD_A), win_kv[1].reshape(DEPTH, DB, WINDOW, HA_KV, HD_A),
            ret_all[0].reshape(DEPTH, DB, HB, DK_B, DV_B), ssm_all[0].reshape(DEPTH, DB, HC, HD_C, N_C),
            stk(conv_s))


def kernel(x_prompt, x_sample, cache_win_k, cache_win_v, state_ret, state_ssm, state_conv, c_prompt, c_sample,
           rel_bias_table, attn_sinks, norm1_w, norm2_w, ada_w, ada_b, w_in, conv_w, conv_b, dt_bias, A_log,
           D_skip, ssm_norm_w, w_out, w_up, w_down, final_norm_w):
    return _forward(x_prompt, x_sample, cache_win_k, cache_win_v, state_ret, state_ssm, state_conv,
                    c_prompt, c_sample, rel_bias_table, attn_sinks, norm1_w, norm2_w, ada_w, ada_b,
                    w_in, conv_w, conv_b, dt_bias, A_log, D_skip, ssm_norm_w, w_out, w_up, w_down,
                    final_norm_w, prompt_tile=PROMPT_TILE, mlp_rows=MLP_ROWS, mlp_ff=MLP_FF)
```

```python
import functools
import math

import numpy as np
import jax
import jax.numpy as jnp
from jax import lax
from jax.experimental import pallas as pl
from jax.experimental.pallas import tpu as pltpu

F32 = jnp.float32
BF16 = jnp.bfloat16

D_MODEL = 1024
DEPTH = 2
PAST_LEN = 16384
WINDOW = 128
HA_Q = 16
HA_KV = 4
HD_A = 64
NUM_BUCKETS = 32
MAX_DISTANCE = WINDOW
HB = 8
DK_B = 64
DV_B = 128
HC = 16
HD_C = 64
N_C = 128
G_C = 2
CONV_W = 4
CONV_DIM = D_MODEL + 2 * G_C * N_C
D_FF = 4 * D_MODEL
EPS = 1e-6
NEG = -1e30
LOG2E = 1.4426950408889634

O_AQ, O_AK, O_AV = 0, 1024, 1280
O_BQ, O_BK, O_BV, O_BG = 1536, 2048, 2560, 3584
O_CZ, O_CXBC, O_CDT, O_GTS = 4608, 5632, 7168, 7184
SEG_A = 2560
SEG_B = 4096
SEG_C = 3712
C_DT = 2560
C_GC = 2688

VMEM_LIMIT_V7X = 56 * 1024 * 1024
ROWS_PER_SAMPLE_STEP = 8


def _cparams(n_axes):
    return pltpu.CompilerParams(dimension_semantics=("arbitrary",) * n_axes,
                                vmem_limit_bytes=VMEM_LIMIT_V7X)


def _dot(a, b):
    return jnp.dot(a, b, preferred_element_type=F32)


def _dot_nt(a, b):
    return lax.dot_general(a, b, (((1,), (1,)), ((), ())), preferred_element_type=F32)


def _dot_tn(a, b):
    return lax.dot_general(a, b, (((0,), (0,)), ((), ())), preferred_element_type=F32)


def _sigmoid(x):
    return 0.5 * (jnp.tanh(0.5 * x) + 1.0)


def _silu(x):
    return x * _sigmoid(x)


def _softplus(x):
    return jnp.maximum(x, 0.0) + jnp.log1p(jnp.exp(-jnp.abs(x)))


def _modnorm(x, nw, sc, sh):
    ms = jnp.mean(x * x, axis=-1, keepdims=True)
    return (x * lax.rsqrt(ms + EPS) * nw) * (1.0 + sc) + sh


def _rms_lanes(x):
    ms = jnp.mean(x * x, axis=-1, keepdims=True)
    return x * lax.rsqrt(ms + EPS)


def _pairswap(x):
    ax = x.ndim - 1
    n = x.shape[ax]
    lane = lax.broadcasted_iota(jnp.int32, x.shape, ax)
    nxt = pltpu.roll(x, n - 1, ax)
    prv = pltpu.roll(x, 1, ax)
    return jnp.where((lane & 1) == 0, nxt, prv)


def _split3(x):
    hi = x.astype(BF16)
    r1 = x - hi.astype(F32)
    mid = r1.astype(BF16)
    lo = (r1 - mid.astype(F32)).astype(BF16)
    return hi, mid, lo


def _gammas():
    return 1.0 - 2.0 ** (-5.0 - np.arange(HB, dtype=np.float64))


def _rot_tables(pos):
    theta = 1.0 / (10000.0 ** np.linspace(0.0, 1.0, DK_B // 2))
    ang = np.asarray(pos, np.float64)[:, None] * theta[None, :]
    cos = np.repeat(np.cos(ang), 2, axis=1)
    sin = np.repeat(np.sin(ang), 2, axis=1)
    sin[:, 0::2] *= -1.0
    return (np.tile(cos, (1, HB)).astype(np.float32), np.tile(sin, (1, HB)).astype(np.float32))


def _ret_tables():
    g = _gammas()
    L = WINDOW
    i = np.arange(L, dtype=np.float64)
    diff = i[:, None] - i[None, :]
    dm = np.where(diff >= 0, g[:, None, None] ** np.maximum(diff, 0.0), 0.0)
    qdec = np.repeat(g[None, :] ** (i[:, None] + 1.0), DK_B, axis=1)
    kdec = np.repeat(g[None, :] ** (L - 1.0 - i[:, None]), DK_B, axis=1)
    return dm.astype(np.float32), qdec.astype(np.float32), kdec.astype(np.float32)


def _t5_bucket_np(dist):
    max_exact = NUM_BUCKETS // 2
    n = np.maximum(dist, 0)
    nf = np.maximum(n, 1).astype(np.float32)
    large = max_exact + (np.log(nf / np.float32(max_exact)) / np.float32(math.log(MAX_DISTANCE / max_exact))
                         * np.float32(NUM_BUCKETS - max_exact)).astype(np.int32)
    large = np.minimum(large, NUM_BUCKETS - 1)
    return np.where(n < max_exact, n, large)


def _bias_tables(rel_table):
    gsz = HA_Q // HA_KV
    qi = np.arange(WINDOW)[None, :]
    kj = np.arange(WINDOW)[:, None]
    dist = np.where(kj > qi, qi + WINDOW - kj, qi - kj)
    onehot = _t5_bucket_np(dist)[..., None] == np.arange(NUM_BUCKETS)
    tab = rel_table.astype(F32)
    bias_kq = jnp.einsum('kqb,bh->hkq', jnp.asarray(onehot, F32), tab, precision=lax.Precision.HIGHEST)
    bias_t = bias_kq.reshape(HA_KV, gsz, WINDOW, WINDOW).transpose(0, 2, 1, 3).reshape(HA_KV, WINDOW, gsz * WINDOW)
    oh_s = _t5_bucket_np(WINDOW - 1 - np.arange(WINDOW))[:, None] == np.arange(NUM_BUCKETS)
    bias_s = jnp.einsum('jb,bh->hj', jnp.asarray(oh_s, F32), tab, precision=lax.Precision.HIGHEST)
    return bias_t, bias_s


def _ada_kernel(c_ref, w_ref, b_ref, o_ref):
    s = _silu(c_ref[...])
    o_ref[0] = _dot(s.astype(BF16), w_ref[0].astype(BF16)) + b_ref[0]


def _ada_call(c_all, ada_w, ada_b):
    n = c_all.shape[0]
    nb = 6
    return pl.pallas_call(
        _ada_kernel,
        grid=(DEPTH, nb),
        in_specs=[pl.BlockSpec((n, D_MODEL), lambda l, j: (0, 0)),
                  pl.BlockSpec((1, D_MODEL, D_MODEL), lambda l, j: (l, 0, j)),
                  pl.BlockSpec((1, 1, D_MODEL), lambda l, j: (l, 0, j))],
        out_specs=pl.BlockSpec((1, n, D_MODEL), lambda l, j: (l, 0, j)),
        out_shape=jax.ShapeDtypeStruct((DEPTH, n, 6 * D_MODEL), F32),
        compiler_params=_cparams(2),
        name="ada_mod",
    )(c_all, ada_w, ada_b.reshape(DEPTH, 1, 6 * D_MODEL))


def _pa_kernel(x_ref, mod_ref, n1_ref, wt_ref, wkv_ref, bias_ref, sink_ref, lowm_ref,
               mix_ref, ko_ref, vo_ref, zt_ref, kv_ref, kprev_ref, vtprev_ref, pen_ref, *, tt):
    t = pl.program_id(1)
    nchunk = tt // WINDOW
    kw = HA_KV * HD_A
    gsz = HA_Q // HA_KV

    @pl.when(t == 0)
    def _():
        kprev_ref[...] = jnp.zeros_like(kprev_ref)
        vtprev_ref[...] = jnp.zeros_like(vtprev_ref)
        pen_ref[...] = jnp.full(pen_ref.shape, NEG, F32)

    mod = mod_ref[0]
    h = _modnorm(x_ref[0], n1_ref[...], mod[:, D_MODEL:2 * D_MODEL], mod[:, 0:D_MODEL]).astype(BF16)
    kv_ref[...] = _dot(h, wkv_ref[...])
    for cc in range(nchunk // 2):
        z2 = _dot_nt(wt_ref[...], h[2 * WINDOW * cc:2 * WINDOW * (cc + 1), :])
        zt_ref[2 * cc] = z2[:, 0:WINDOW]
        zt_ref[2 * cc + 1] = z2[:, WINDOW:2 * WINDOW]
    qw = gsz * WINDOW
    lower = (lax.broadcasted_iota(jnp.int32, (WINDOW, qw), 0)
             > (lax.broadcasted_iota(jnp.int32, (WINDOW, qw), 1) & (WINDOW - 1)))

    def chunk(c, carry):
        r0 = pl.multiple_of(c * WINDOW, WINDOW)
        rows = pl.ds(r0, WINDOW)
        kc = kv_ref[rows, 0:kw]
        vc = kv_ref[rows, kw:2 * kw]
        vt = vc.T
        kk = jnp.concatenate([kprev_ref[...], kc], axis=0).astype(BF16)
        vvt = jnp.concatenate([vtprev_ref[...], vt], axis=1).astype(BF16)
        qt = zt_ref[c, 0:D_MODEL, :].astype(BF16)
        pen = pen_ref[0:1, :]
        s_all = []
        for g in range(HA_KV):
            gs = slice(HD_A * g, HD_A * (g + 1))
            qcat = jnp.concatenate([qt[HD_A * (gsz * g + j):HD_A * (gsz * g + j + 1), :] for j in range(gsz)], axis=1)
            s_all.append(_dot(kk[:, gs], qcat))
        p_all = []
        for g in range(HA_KV):
            sg = jnp.where(lower, s_all[g][0:WINDOW, :] + pen, s_all[g][WINDOW:2 * WINDOW, :]) + bias_ref[g]
            sink = sink_ref[g]
            m = jnp.maximum(jnp.max(sg, axis=0, keepdims=True), sink)
            pw = jnp.exp(sg - m)
            den = jnp.sum(pw, axis=0, keepdims=True) + jnp.exp(sink - m)
            pb = pw.astype(BF16)
            p_prev = pb * lowm_ref[...]
            p_all.append((jnp.concatenate([p_prev, pb - p_prev], axis=0), 1.0 / den))
        pieces = []
        for g in range(HA_KV):
            gs = slice(HD_A * g, HD_A * (g + 1))
            p, rden = p_all[g]
            ot = _dot(vvt[gs, :], p) * rden
            pieces += [ot[:, WINDOW * j:WINDOW * (j + 1)] for j in range(gsz)]
        oat = jnp.concatenate(pieces, axis=0)
        mixt = _sigmoid(zt_ref[c, D_MODEL:2 * D_MODEL, :]) * oat
        mix_ref[0, rows, :] = mixt.T
        kprev_ref[...] = kc
        vtprev_ref[...] = vt
        pen_ref[...] = jnp.zeros_like(pen_ref)
        ko_ref[0] = kc
        vo_ref[0] = vc
        return carry

    lax.fori_loop(0, nchunk, chunk, 0, unroll=True)


def _pa_call(x, mod3, n1, wqgt, wkv, bias_t, sink_rows, tt):
    B, T, _ = x.shape
    kw = HA_KV * HD_A
    qw = (HA_Q // HA_KV) * WINDOW
    kern = functools.partial(_pa_kernel, tt=tt)
    lowm = (np.arange(WINDOW)[:, None] > (np.arange(qw)[None, :] % WINDOW)).astype(np.float32)
    return pl.pallas_call(
        kern,
        grid=(B, T // tt),
        in_specs=[pl.BlockSpec((1, tt, D_MODEL), lambda b, t: (b, t, 0)),
                  pl.BlockSpec((1, 1, 6 * D_MODEL), lambda b, t: (b, 0, 0)),
                  pl.BlockSpec((1, D_MODEL), lambda b, t: (0, 0)),
                  pl.BlockSpec((2 * D_MODEL, D_MODEL), lambda b, t: (0, 0)),
                  pl.BlockSpec((D_MODEL, 2 * kw), lambda b, t: (0, 0)),
                  pl.BlockSpec((HA_KV, WINDOW, qw), lambda b, t: (0, 0, 0)),
                  pl.BlockSpec((HA_KV, 1, qw), lambda b, t: (0, 0, 0)),
                  pl.BlockSpec((WINDOW, qw), lambda b, t: (0, 0))],
        out_specs=[pl.BlockSpec((1, tt, D_MODEL), lambda b, t: (b, t, 0)),
                   pl.BlockSpec((1, WINDOW, kw), lambda b, t: (b, 0, 0)),
                   pl.BlockSpec((1, WINDOW, kw), lambda b, t: (b, 0, 0))],
        out_shape=[jax.ShapeDtypeStruct((B, T, D_MODEL), F32),
                   jax.ShapeDtypeStruct((B, WINDOW, kw), F32),
                   jax.ShapeDtypeStruct((B, WINDOW, kw), F32)],
        scratch_shapes=[pltpu.VMEM((tt // WINDOW, 2 * D_MODEL, WINDOW), F32),
                        pltpu.VMEM((tt, 2 * kw), F32),
                        pltpu.VMEM((WINDOW, kw), F32),
                        pltpu.VMEM((kw, WINDOW), F32),
                        pltpu.VMEM((8, qw), F32)],
        compiler_params=_cparams(2),
        name="prompt_attn",
    )(x, mod3, n1, wqgt, wkv, bias_t, sink_rows, jnp.asarray(lowm, BF16))


def _pb_kernel(x_ref, mod_ref, n1_ref, wt_ref, cos_ref, sin_ref, qdec_ref, kdec_ref, dm_ref, mixa_ref,
               mix_ref, so_ref, zt_ref, s_ref, *, tt, glast):
    t = pl.program_id(1)
    nchunk = tt // WINDOW
    hw = HB * DK_B // 2
    hp = DK_B // 2

    @pl.when(t == 0)
    def _():
        s_ref[...] = jnp.zeros_like(s_ref)

    mod = mod_ref[0]
    h = _modnorm(x_ref[0], n1_ref[...], mod[:, D_MODEL:2 * D_MODEL], mod[:, 0:D_MODEL]).astype(BF16)
    for cc in range(nchunk // 2):
        z2 = _dot_nt(wt_ref[...], h[2 * WINDOW * cc:2 * WINDOW * (cc + 1), :])
        zt_ref[2 * cc] = z2[:, 0:WINDOW]
        zt_ref[2 * cc + 1] = z2[:, WINDOW:2 * WINDOW]

    def head_rows(pair, hh):
        return jnp.concatenate([pair[0][hp * hh:hp * (hh + 1), :], pair[1][hp * hh:hp * (hh + 1), :]], axis=0)

    def chunk(c, carry):
        r0 = pl.multiple_of(c * WINDOW, WINDOW)
        rows = pl.ds(r0, WINDOW)
        cos = cos_ref[c]
        sin = sin_ref[c]
        q1 = zt_ref[c, 0:hw, :]
        q2 = zt_ref[c, hw:2 * hw, :]
        k1 = zt_ref[c, 2 * hw:3 * hw, :]
        k2 = zt_ref[c, 3 * hw:4 * hw, :]
        rq = (q1 * cos - q2 * sin, q1 * sin + q2 * cos)
        rk = (k1 * cos - k2 * sin, k1 * sin + k2 * cos)
        qdec = qdec_ref[...]
        kdec = kdec_ref[...]
        qb = tuple(a.astype(BF16) for a in rq)
        kb = tuple(a.astype(BF16) for a in rk)
        qd = tuple((a * qdec).astype(BF16) for a in rq)
        kd = tuple((a * kdec).astype(BF16) for a in rk)
        vt = zt_ref[c, 4 * hw:4 * hw + HB * DV_B, :].astype(BF16)
        inner, cross, supd = [], [], []
        for hh in range(HB):
            s_old = s_ref[DK_B * hh:DK_B * (hh + 1), :]
            inner.append(_dot_tn(head_rows(qb, hh), head_rows(kb, hh)))
            cross.append(_dot_tn(s_old.astype(BF16), head_rows(qd, hh)))
            supd.append(glast[hh] * s_old + _dot_nt(head_rows(kd, hh), vt[DV_B * hh:DV_B * (hh + 1), :]))
        innd = [(inner[hh] * dm_ref[hh]).astype(BF16) for hh in range(HB)]
        outs = []
        for hh in range(HB):
            ot = _dot_nt(vt[DV_B * hh:DV_B * (hh + 1), :], innd[hh]) + cross[hh]
            s_ref[DK_B * hh:DK_B * (hh + 1), :] = supd[hh]
            ms = jnp.mean(ot * ot, axis=0, keepdims=True)
            outs.append(ot * lax.rsqrt(ms + EPS))
        obt = jnp.concatenate(outs, axis=0)
        bgt = zt_ref[c, 4 * hw + D_MODEL:4 * hw + 2 * D_MODEL, :]
        gbt = zt_ref[c, 4 * hw + 2 * D_MODEL:4 * hw + 3 * D_MODEL, :]
        mixt = _sigmoid(gbt) * (_silu(bgt) * obt)
        mix_ref[0, rows, :] = mixa_ref[0, rows, :] + mixt.T
        return carry

    lax.fori_loop(0, nchunk, chunk, 0, unroll=True)
    so_ref[0] = s_ref[...]


def _deinterleave_pairs(w):
    return w.reshape(w.shape[0], HB, DK_B // 2, 2).transpose(0, 3, 1, 2).reshape(w.shape[0], HB * DK_B)


def _rot_tables_t(T):
    theta = 1.0 / (10000.0 ** np.linspace(0.0, 1.0, DK_B // 2))
    ang = theta[:, None] * np.arange(T, dtype=np.float64)[None, :]

    def lay(a):
        a = np.tile(a, (HB, 1))
        return np.ascontiguousarray(a.reshape(a.shape[0], T // WINDOW, WINDOW).transpose(1, 0, 2)).astype(np.float32)

    return lay(np.cos(ang)), lay(np.sin(ang))


def _pb_call(x, mod3, n1, wbt, mixa, tt):
    B, T, _ = x.shape
    hw = HB * DK_B // 2
    cos, sin = _rot_tables_t(T)
    dm, qdec, kdec = _ret_tables()
    qdec_t = np.ascontiguousarray(qdec[:, ::2].T)
    kdec_t = np.ascontiguousarray(kdec[:, ::2].T)
    glast = tuple(float(v) for v in (_gammas() ** WINDOW))
    kern = functools.partial(_pb_kernel, tt=tt, glast=glast)
    full2 = lambda b, t: (0, 0)
    nct = tt // WINDOW
    return pl.pallas_call(
        kern,
        grid=(B, T // tt),
        in_specs=[pl.BlockSpec((1, tt, D_MODEL), lambda b, t: (b, t, 0)),
                  pl.BlockSpec((1, 1, 6 * D_MODEL), lambda b, t: (b, 0, 0)),
                  pl.BlockSpec((1, D_MODEL), full2),
                  pl.BlockSpec((SEG_B, D_MODEL), full2),
                  pl.BlockSpec((nct, hw, WINDOW), lambda b, t: (t, 0, 0)),
                  pl.BlockSpec((nct, hw, WINDOW), lambda b, t: (t, 0, 0)),
                  pl.BlockSpec((hw, WINDOW), full2),
                  pl.BlockSpec((hw, WINDOW), full2),
                  pl.BlockSpec((HB, WINDOW, WINDOW), lambda b, t: (0, 0, 0)),
                  pl.BlockSpec((1, tt, D_MODEL), lambda b, t: (b, t, 0))],
        out_specs=[pl.BlockSpec((1, tt, D_MODEL), lambda b, t: (b, t, 0)),
                   pl.BlockSpec((1, HB * DK_B, DV_B), lambda b, t: (b, 0, 0))],
        out_shape=[jax.ShapeDtypeStruct((B, T, D_MODEL), F32),
                   jax.ShapeDtypeStruct((B, HB * DK_B, DV_B), F32)],
        scratch_shapes=[pltpu.VMEM((nct, SEG_B, WINDOW), F32),
                        pltpu.VMEM((HB * DK_B, DV_B), F32)],
        compiler_params=_cparams(2),
        name="prompt_retention",
    )(x, mod3, n1, wbt, jnp.asarray(cos), jnp.asarray(sin), jnp.asarray(qdec_t), jnp.asarray(kdec_t),
      jnp.asarray(dm), mixa)


def _pc_kernel(x_ref, mod_ref, n1_ref, w_ref, cw_ref, cb_ref, dtb_ref, a_ref, dsk_ref, snw_ref, tri_ref,
               mixab_ref, wo_ref,
               xo_ref, ho_ref, co_ref,
               z_ref, xbuf_ref, xbc_ref, dt_ref, hst_ref, mixs_ref, *, tt):
    t = pl.program_id(1)

    @pl.when(t == 0)
    def _():
        xbuf_ref[0:8, :] = jnp.zeros((8, CONV_DIM), F32)
        hst_ref[...] = jnp.zeros_like(hst_ref)

    x = x_ref[0]
    mod = mod_ref[0]
    h = _modnorm(x, n1_ref[...], mod[:, D_MODEL:2 * D_MODEL], mod[:, 0:D_MODEL])
    z_ref[...] = _dot(h.astype(BF16), w_ref[...])

    xbuf_ref[8:8 + tt, :] = z_ref[:, 1024:2560]
    acc = cb_ref[...]
    for i in range(CONV_W):
        acc = acc + xbuf_ref[5 + i:5 + i + tt, :] * cw_ref[i:i + 1, :]
    xbc_ref[...] = _silu(acc)
    co_ref[0] = xbuf_ref[tt + 5:tt + 8, :]
    xbuf_ref[0:8, :] = xbuf_ref[tt:tt + 8, :]
    dt_ref[...] = _softplus(z_ref[:, C_DT:C_DT + 128] + dtb_ref[...])

    ii = lax.broadcasted_iota(jnp.int32, (WINDOW, WINDOW), 0)
    jj = lax.broadcasted_iota(jnp.int32, (WINDOW, WINDOW), 1)
    causal = ii >= jj
    hpg = HC // G_C

    def chunk(c):
        rows = slice(WINDOW * c, WINDOW * (c + 1))
        xc = xbc_ref[rows, 0:1024]
        bmat = xbc_ref[rows, 1024:1280]
        cmat = xbc_ref[rows, 1280:1536]
        dtc = dt_ref[rows, :]
        acum = jnp.dot(tri_ref[...], dtc * a_ref[...], precision=lax.Precision.HIGHEST,
                       preferred_element_type=F32)
        acum = acum * LOG2E
        acum_t = acum.T
        rowp_t = acum_t - jnp.log2(dtc.T)
        x_t = xc.T
        xb = xc.astype(BF16)
        bb = bmat.astype(BF16)
        cb16 = cmat.astype(BF16)
        ys = []
        for g in range(G_C):
            ns = slice(N_C * g, N_C * (g + 1))
            cbg = _dot_nt(cb16[:, ns], bb[:, ns])
            for hh in range(hpg * g, hpg * (g + 1)):
                ps = slice(HD_C * hh, HD_C * (hh + 1))
                colb = jnp.broadcast_to(acum[:, hh:hh + 1], (WINDOW, WINDOW))
                rowp = rowp_t[hh:hh + 1, :]
                m = cbg * jnp.exp2(jnp.where(causal, colb - rowp, NEG))
                hs = hst_ref[ps, :]
                ecolb = jnp.exp2(colb)
                cs = cmat[:, ns] * ecolb
                y = _dot(m.astype(BF16), xb[:, ps]) + _dot_nt(cs.astype(BF16), hs.astype(BF16))
                wrow = jnp.exp2(colb[WINDOW - 1:WINDOW, :] - rowp)
                xw = (x_t[ps, :] * wrow).astype(BF16)
                hst_ref[ps, :] = ecolb[WINDOW - 1:WINDOW, :] * hs + _dot(xw, bb[:, ns])
                ys.append(y)
        y = jnp.concatenate(ys, axis=1) + dsk_ref[...] * xc
        yc = y * _silu(z_ref[rows, 0:1024])
        gw = D_MODEL // G_C
        oc = jnp.concatenate([_rms_lanes(yc[:, gw * g:gw * (g + 1)]) for g in range(G_C)], axis=1) * snw_ref[...]
        gc = _sigmoid(z_ref[rows, C_GC:C_GC + 1024])
        mixs_ref[rows, :] = (mixab_ref[0, rows, :] + gc * oc).astype(BF16)

    g1 = mod[:, 2 * D_MODEL:3 * D_MODEL]
    for c in range(tt // WINDOW):
        chunk(c)
        if c % 2 == 1:
            pr = slice(WINDOW * (c - 1), WINDOW * (c + 1))
            xo_ref[0, pr, :] = x_ref[0, pr, :] + g1 * _dot(mixs_ref[pr, :], wo_ref[...])
    ho_ref[0] = hst_ref[...]


def _pc_call(x, mod3, n1, wc, cw, cb, dtb_pad, a_pad, dsk_full, snw, mixab, wo, tt):
    B, T, _ = x.shape
    tri = jnp.asarray(np.tril(np.ones((WINDOW, WINDOW), np.float32)))
    kern = functools.partial(_pc_kernel, tt=tt)
    full2 = lambda b, t: (0, 0)
    return pl.pallas_call(
        kern,
        grid=(B, T // tt),
        in_specs=[pl.BlockSpec((1, tt, D_MODEL), lambda b, t: (b, t, 0)),
                  pl.BlockSpec((1, 1, 6 * D_MODEL), lambda b, t: (b, 0, 0)),
                  pl.BlockSpec((1, D_MODEL), full2),
                  pl.BlockSpec((D_MODEL, SEG_C), full2),
                  pl.BlockSpec((CONV_W, CONV_DIM), full2),
                  pl.BlockSpec((1, CONV_DIM), full2),
                  pl.BlockSpec((1, 128), full2),
                  pl.BlockSpec((1, 128), full2),
                  pl.BlockSpec((1, D_MODEL), full2),
                  pl.BlockSpec((1, D_MODEL), full2),
                  pl.BlockSpec((WINDOW, WINDOW), full2),
                  pl.BlockSpec((1, tt, D_MODEL), lambda b, t: (b, t, 0)),
                  pl.BlockSpec((D_MODEL, D_MODEL), full2)],
        out_specs=[pl.BlockSpec((1, tt, D_MODEL), lambda b, t: (b, t, 0)),
                   pl.BlockSpec((1, HC * HD_C, N_C), lambda b, t: (b, 0, 0)),
                   pl.BlockSpec((1, CONV_W - 1, CONV_DIM), lambda b, t: (b, 0, 0))],
        out_shape=[jax.ShapeDtypeStruct((B, T, D_MODEL), F32),
                   jax.ShapeDtypeStruct((B, HC * HD_C, N_C), F32),
                   jax.ShapeDtypeStruct((B, CONV_W - 1, CONV_DIM), F32)],
        scratch_shapes=[pltpu.VMEM((tt, SEG_C), F32),
                        pltpu.VMEM((tt + 8, CONV_DIM), F32),
                        pltpu.VMEM((tt, CONV_DIM), F32),
                        pltpu.VMEM((tt, 128), F32),
                        pltpu.VMEM((HC * HD_C, N_C), F32),
                        pltpu.VMEM((tt, D_MODEL), BF16)],
        compiler_params=_cparams(2),
        name="prompt_ssd_out",
    )(x, mod3, n1, wc, cw, cb, dtb_pad, a_pad, dsk_full, snw, tri, mixab, wo)


def _mlp_kernel(x_ref, sh_ref, sc_ref, g_ref, n2_ref, wu_ref, wd_ref, fn_ref, o_ref, *, tf, final, per_row):
    rd = (lambda r: r[...]) if per_row else (lambda r: r[0])
    x = x_ref[...]
    hb = _modnorm(x, n2_ref[...], rd(sc_ref), rd(sh_ref)).astype(BF16)
    acc = None
    for f in range(D_FF // tf):
        u = _dot(hb, wu_ref[0, :, tf * f:tf * (f + 1)])
        u = jnp.square(jnp.maximum(u, 0.0)).astype(BF16)
        part = _dot(u, wd_ref[0, tf * f:tf * (f + 1), :])
        acc = part if acc is None else acc + part
    y = x + rd(g_ref) * acc
    if final:
        y = _rms_lanes(y) * fn_ref[...]
    o_ref[...] = y


def _mlp_call(x2, mod, n2, wu, wd, fn, layer, tm, tf, rows_per_mod, final):
    M = x2.shape[0]
    per_row = rows_per_mod == 1
    if per_row:
        mspec = lambda j: pl.BlockSpec((tm, D_MODEL), lambda m: (m, j))
    else:
        mspec = lambda j: pl.BlockSpec((1, 1, D_MODEL), lambda m: ((m * tm) // rows_per_mod, 0, j))
    kern = functools.partial(_mlp_kernel, tf=tf, final=final, per_row=per_row)
    resident = pl.Buffered(1)
    return pl.pallas_call(
        kern,
        grid=(M // tm,),
        in_specs=[pl.BlockSpec((tm, D_MODEL), lambda m: (m, 0)),
                  mspec(3), mspec(4), mspec(5),
                  pl.BlockSpec((1, D_MODEL), lambda m: (0, 0)),
                  pl.BlockSpec((1, D_MODEL, D_FF), lambda m: (layer, 0, 0), pipeline_mode=resident),
                  pl.BlockSpec((1, D_FF, D_MODEL), lambda m: (layer, 0, 0), pipeline_mode=resident),
                  pl.BlockSpec((1, D_MODEL), lambda m: (0, 0))],
        out_specs=pl.BlockSpec((tm, D_MODEL), lambda m: (m, 0)),
        out_shape=jax.ShapeDtypeStruct((M, D_MODEL), F32),
        compiler_params=_cparams(1),
        name="mlp",
    )(x2, mod, mod, mod, n2, wu, wd, fn)


def _sproj_kernel(x_ref, sh_ref, sc_ref, n1_ref, w_ref, o_ref):
    h = _modnorm(x_ref[...], n1_ref[...], sc_ref[...], sh_ref[...])
    o_ref[...] = _dot(h.astype(BF16), w_ref[...])


def _sproj_call(xs, mod_s, n1, w):
    n, width = xs.shape[0], w.shape[1]
    return pl.pallas_call(
        _sproj_kernel,
        grid=(1,),
        in_specs=[pl.BlockSpec((n, D_MODEL), lambda i: (0, 0)),
                  pl.BlockSpec((n, D_MODEL), lambda i: (0, 0)),
                  pl.BlockSpec((n, D_MODEL), lambda i: (0, 1)),
                  pl.BlockSpec((1, D_MODEL), lambda i: (0, 0)),
                  pl.BlockSpec((D_MODEL, width), lambda i: (0, 0))],
        out_specs=pl.BlockSpec((n, width), lambda i: (0, 0)),
        out_shape=jax.ShapeDtypeStruct((n, width), F32),
        compiler_params=_cparams(1),
        name="sample_proj",
    )(xs, mod_s, mod_s, n1, w)


def _carry_outputs(prev, n_in, nsteps):
    nslab = 1 if prev else DEPTH
    extra = [pl.BlockSpec(memory_space=pl.ANY) for _ in prev]
    alias = {n_in + k: k for k in range(len(prev))}
    row = lambda p, i: jnp.where(p == 0, i, nsteps - 1)
    return nslab, extra, alias, row


def _with_fill(body, state_outs):
    def kern(*refs):
        p = pl.program_id(0)

        @pl.when(p == 0)
        def _():
            body(*refs)

        @pl.when(p != 0)
        def _():
            for k in state_outs:
                refs[k][...] = jnp.zeros_like(refs[k])

    return kern


def _sa_kernel(q_ref, kv_ref, ck_ref, cv_ref, bias_ref, sink_ref, *rest):
    ko_ref, vo_ref, oa_ref = rest[-3:]
    nb = ROWS_PER_SAMPLE_STEP
    gsz = HA_Q // HA_KV
    rg = lax.broadcasted_iota(jnp.int32, (HA_Q, HD_A), 0) // gsz
    sink = sink_ref[...]
    kw = HA_KV * HD_A
    scores, vmats = [], []
    for i in range(nb):
        ko_ref[0, i, 0:WINDOW - 1, :] = ck_ref[0, i, 1:WINDOW, :]
        ko_ref[0, i, WINDOW - 1:WINDOW, :] = kv_ref[i:i + 1, 0:kw]
        vo_ref[0, i, 0:WINDOW - 1, :] = cv_ref[0, i, 1:WINDOW, :]
        vo_ref[0, i, WINDOW - 1:WINDOW, :] = kv_ref[i:i + 1, kw:2 * kw]
        kmat = ko_ref[0, i].astype(BF16)
        vmats.append(vo_ref[0, i].astype(BF16))
        q = q_ref[i] * (HD_A ** -0.5)
        qe = jnp.concatenate([jnp.where(rg == g, q, 0.0) for g in range(HA_KV)], axis=1).astype(BF16)
        scores.append(_dot_nt(qe, kmat))
    probs = []
    for i in range(nb):
        sc = scores[i] + bias_ref[...]
        m = jnp.maximum(jnp.max(sc, axis=-1, keepdims=True), sink)
        p = jnp.exp(sc - m)
        den = jnp.sum(p, axis=-1, keepdims=True) + jnp.exp(sink - m)
        probs.append((p.astype(BF16), den))
    outs = [_dot(probs[i][0], vmats[i]) for i in range(nb)]
    for i in range(nb):
        o = outs[i] / probs[i][1]
        o16 = jnp.zeros((HA_Q, HD_A), F32)
        for g in range(HA_KV):
            o16 = o16 + jnp.where(rg == g, o[:, HD_A * g:HD_A * (g + 1)], 0.0)
        oa_ref[i] = o16


def _sa_call(q3, kv, cache_k, cache_v, bias_s, sink_col, layer, prev):
    n = q3.shape[0]
    nb = ROWS_PER_SAMPLE_STEP
    kw = HA_KV * HD_A
    nslab, extra, alias, row = _carry_outputs(prev, 6, n // nb)
    cspec = pl.BlockSpec((1, nb, WINDOW, kw), lambda p, i: (layer, row(p, i), 0, 0))
    ospec = pl.BlockSpec((1, nb, WINDOW, kw), lambda p, i: ((layer + p) % DEPTH, i, 0, 0))
    return pl.pallas_call(
        _with_fill(_sa_kernel, (-3, -2)),
        grid=(nslab, n // nb),
        in_specs=[pl.BlockSpec((nb, HA_Q, HD_A), lambda p, i: (row(p, i), 0, 0)),
                  pl.BlockSpec((nb, 2 * kw), lambda p, i: (row(p, i), 0)),
                  cspec, cspec,
                  pl.BlockSpec((HA_Q, WINDOW), lambda p, i: (0, 0)),
                  pl.BlockSpec((HA_Q, 1), lambda p, i: (0, 0))] + extra,
        out_specs=[ospec, ospec, pl.BlockSpec((nb, HA_Q, HD_A), lambda p, i: (row(p, i), 0, 0))],
        out_shape=[jax.ShapeDtypeStruct((DEPTH, n, WINDOW, kw), F32),
                   jax.ShapeDtypeStruct((DEPTH, n, WINDOW, kw), F32),
                   jax.ShapeDtypeStruct((n, HA_Q, HD_A), F32)],
        input_output_aliases=alias,
        compiler_params=_cparams(2),
        name="sample_attn",
    )(q3, kv, cache_k, cache_v, bias_s, sink_col, *prev)


def _sr_kernel(zb_ref, cos_ref, sin_ref, gcol_ref, s_ref, *rest):
    so_ref, o_ref = rest[-2:]
    nb = ROWS_PER_SAMPLE_STEP
    wq = HB * DK_B
    cos = cos_ref[...]
    sin = sin_ref[...]
    qf = zb_ref[:, 0:wq]
    kf = zb_ref[:, wq:2 * wq]
    qr = qf * cos + _pairswap(qf) * sin
    kr = (kf * cos + _pairswap(kf) * sin) * (DK_B ** -0.5)
    v = zb_ref[:, 2 * wq:2 * wq + HB * DV_B]
    r8 = lax.broadcasted_iota(jnp.int32, (HB, wq), 0)
    hl = lax.broadcasted_iota(jnp.int32, (HB, wq), 1) // DK_B
    rv = lax.broadcasted_iota(jnp.int32, (HB, DV_B), 0)
    outers, q8s = [], []
    for i in range(nb):
        k8 = jnp.where(hl == r8, jnp.broadcast_to(kr[i:i + 1, :], (HB, wq)), 0.0).astype(BF16)
        q8s.append(jnp.where(hl == r8, jnp.broadcast_to(qr[i:i + 1, :], (HB, wq)), 0.0).astype(BF16))
        v8 = jnp.zeros((HB, DV_B), F32)
        for r in range(HB):
            v8 = jnp.where(rv == r, jnp.broadcast_to(v[i:i + 1, DV_B * r:DV_B * (r + 1)], (HB, DV_B)), v8)
        outers.append(_dot_tn(k8, v8.astype(BF16)))
    s_news = []
    for i in range(nb):
        s_new = gcol_ref[...] * s_ref[0, i] + outers[i]
        so_ref[0, i] = s_new
        s_news.append(s_new.astype(BF16))
    for i in range(nb):
        o_ref[i] = _dot(q8s[i], s_news[i])


def _sr_call(zb, state, layer, prev):
    n = zb.shape[0]
    nb = ROWS_PER_SAMPLE_STEP
    wq = HB * DK_B
    cos, sin = _rot_tables(np.array([PAST_LEN]))
    gcol = np.repeat(_gammas(), DK_B)[:, None] * np.ones((1, DV_B))
    nslab, extra, alias, row = _carry_outputs(prev, 5, n // nb)
    full = lambda p, i: (0, 0)
    return pl.pallas_call(
        _with_fill(_sr_kernel, (-2,)),
        grid=(nslab, n // nb),
        in_specs=[pl.BlockSpec((nb, SEG_B), lambda p, i: (row(p, i), 0)),
                  pl.BlockSpec((1, wq), full),
                  pl.BlockSpec((1, wq), full),
                  pl.BlockSpec((wq, DV_B), full),
                  pl.BlockSpec((1, nb, wq, DV_B), lambda p, i: (layer, row(p, i), 0, 0))] + extra,
        out_specs=[pl.BlockSpec((1, nb, wq, DV_B), lambda p, i: ((layer + p) % DEPTH, i, 0, 0)),
                   pl.BlockSpec((nb, HB, DV_B), lambda p, i: (row(p, i), 0, 0))],
        out_shape=[jax.ShapeDtypeStruct((DEPTH, n, wq, DV_B), F32),
                   jax.ShapeDtypeStruct((n, HB, DV_B), F32)],
        input_output_aliases=alias,
        compiler_params=_cparams(2),
        name="sample_retention",
    )(zb, jnp.asarray(cos), jnp.asarray(sin), jnp.asarray(gcol.astype(np.float32)), state, *prev)


def _ss_kernel(zc_ref, cs_ref, cw_ref, cb_ref, dtb_ref, a_ref, e3_ref, h_ref, *rest):
    ho_ref, cso_ref, y_ref, xc_ref = rest[-4:]
    nb = ROWS_PER_SAMPLE_STEP
    cx = zc_ref[:, 1024:2560]
    taps = [cs_ref[0, :, CONV_DIM * i:CONV_DIM * (i + 1)] for i in range(CONV_W - 1)] + [cx]
    acc = cb_ref[...]
    for i in range(CONV_W):
        acc = acc + taps[i] * cw_ref[i:i + 1, :]
    xbc = _silu(acc)
    cso_ref[...] = jnp.concatenate(taps[1:], axis=1)
    xc = xbc[:, 0:1024]
    bmat = xbc[:, 1024:1280]
    cmat = xbc[:, 1280:1536]
    dt = _softplus(zc_ref[:, C_DT:C_DT + 128] + dtb_ref[...])
    da = jnp.exp(dt * a_ref[...])
    dt_e = _dot(jnp.concatenate(_split3(dt), axis=1), e3_ref[...])
    da_e = _dot(jnp.concatenate(_split3(da), axis=1), e3_ref[...])
    dtx = dt_e * xc
    gw = (HC // G_C) * HD_C
    r8 = lax.broadcasted_iota(jnp.int32, (nb, gw), 0)
    rn = lax.broadcasted_iota(jnp.int32, (nb, N_C), 0)
    ones8 = jnp.ones((nb, N_C), BF16)
    prods = {}
    for g in range(G_C):
        ws = slice(gw * g, gw * (g + 1))
        bg16 = bmat[:, N_C * g:N_C * (g + 1)].astype(BF16)
        for i in range(nb):
            x8 = jnp.where(r8 == i, dtx[:, ws], 0.0).astype(BF16)
            outer = _dot_tn(x8, bg16)
            hi, mid, lo = (p.astype(F32) for p in _split3(jnp.broadcast_to(da_e[i:i + 1, ws], (nb, gw))))
            l3 = jnp.where(r8 == 0, hi, jnp.where(r8 == 1, mid, jnp.where(r8 == 2, lo, 0.0)))
            prods[g, i] = (outer, _dot_tn(l3.astype(BF16), ones8))
    h16 = {}
    for g in range(G_C):
        ws = slice(gw * g, gw * (g + 1))
        for i in range(nb):
            outer, dacol = prods[g, i]
            h_new = dacol * h_ref[0, i, ws, :] + outer
            ho_ref[0, i, ws, :] = h_new
            h16[g, i] = h_new.astype(BF16)
    ycols = []
    for g in range(G_C):
        ns = slice(N_C * g, N_C * (g + 1))
        yacc = jnp.zeros((nb, gw), F32)
        for i in range(nb):
            c8 = jnp.where(rn == i, cmat[:, ns], 0.0).astype(BF16)
            yacc = yacc + _dot_nt(c8, h16[g, i])
        ycols.append(yacc)
    y_ref[...] = jnp.concatenate(ycols, axis=1)
    xc_ref[...] = xc


def _ss_call(zc, conv_state, hstate, cw, cb, dtb_pad, a_pad, layer, prev):
    n = zc.shape[0]
    nb = ROWS_PER_SAMPLE_STEP
    e = np.zeros((128, D_MODEL), np.float32)
    for hh in range(HC):
        e[hh, HD_C * hh:HD_C * (hh + 1)] = 1.0
    e3 = jnp.asarray(np.concatenate([e, e, e], axis=0), dtype=BF16)
    cwid = (CONV_W - 1) * CONV_DIM
    full = lambda p, i: (0, 0)
    nslab, extra, alias, row = _carry_outputs(prev, 8, n // nb)
    return pl.pallas_call(
        _with_fill(_ss_kernel, (-4,)),
        grid=(nslab, n // nb),
        in_specs=[pl.BlockSpec((nb, SEG_C), lambda p, i: (row(p, i), 0)),
                  pl.BlockSpec((1, nb, cwid), lambda p, i: (layer, row(p, i), 0)),
                  pl.BlockSpec((CONV_W, CONV_DIM), full),
                  pl.BlockSpec((1, CONV_DIM), full),
                  pl.BlockSpec((1, 128), full),
                  pl.BlockSpec((1, 128), full),
                  pl.BlockSpec((3 * 128, D_MODEL), full),
                  pl.BlockSpec((1, nb, HC * HD_C, N_C), lambda p, i: (layer, row(p, i), 0, 0))] + extra,
        out_specs=[pl.BlockSpec((1, nb, HC * HD_C, N_C), lambda p, i: ((layer + p) % DEPTH, i, 0, 0)),
                   pl.BlockSpec((nb, cwid), lambda p, i: (row(p, i), 0)),
                   pl.BlockSpec((nb, D_MODEL), lambda p, i: (row(p, i), 0)),
                   pl.BlockSpec((nb, D_MODEL), lambda p, i: (row(p, i), 0))],
        out_shape=[jax.ShapeDtypeStruct((DEPTH, n, HC * HD_C, N_C), F32),
                   jax.ShapeDtypeStruct((n, cwid), F32),
                   jax.ShapeDtypeStruct((n, D_MODEL), F32),
                   jax.ShapeDtypeStruct((n, D_MODEL), F32)],
        input_output_aliases=alias,
        compiler_params=_cparams(2),
        name="sample_ssd",
    )(zc, conv_state, cw, cb, dtb_pad, a_pad, e3, hstate, *prev)


def _sm_kernel(x_ref, g1_ref, oa_ref, oret_ref, y_ref, xc_ref, za_ref, zb_ref, zc_ref,
               dsk_ref, snw_ref, wo_ref, o_ref):
    ob = jnp.concatenate([_rms_lanes(oret_ref[:, DV_B * hh:DV_B * (hh + 1)]) for hh in range(HB)], axis=1)
    ob = _silu(zb_ref[:, 2048:3072]) * ob
    yc = (y_ref[...] + dsk_ref[...] * xc_ref[...]) * _silu(zc_ref[:, 0:1024])
    gw = D_MODEL // G_C
    oc = jnp.concatenate([_rms_lanes(yc[:, gw * g:gw * (g + 1)]) for g in range(G_C)], axis=1) * snw_ref[...]
    mix = (_sigmoid(za_ref[:, 1536:2560]) * oa_ref[...] + _sigmoid(zb_ref[:, 3072:4096]) * ob
           + _sigmoid(zc_ref[:, C_GC:C_GC + 1024]) * oc)
    o_ref[...] = x_ref[...] + g1_ref[...] * _dot(mix.astype(BF16), wo_ref[...])


def _sm_call(xs, mod_s, oa, oret, y, xc, za, zb, zc, dsk_full, snw, wo):
    n = xs.shape[0]
    full = lambda i: (0, 0)
    row = lambda w: pl.BlockSpec((n, w), full)
    return pl.pallas_call(
        _sm_kernel,
        grid=(1,),
        in_specs=[row(D_MODEL),
                  pl.BlockSpec((n, D_MODEL), lambda i: (0, 2)),
                  row(D_MODEL), row(D_MODEL), row(D_MODEL), row(D_MODEL),
                  row(SEG_A), row(SEG_B), row(SEG_C),
                  pl.BlockSpec((1, D_MODEL), full),
                  pl.BlockSpec((1, D_MODEL), full),
                  pl.BlockSpec((D_MODEL, D_MODEL), full)],
        out_specs=row(D_MODEL),
        out_shape=jax.ShapeDtypeStruct((n, D_MODEL), F32),
        compiler_params=_cparams(1),
        name="sample_merge_out",
    )(xs, mod_s, oa, oret, y, xc, za, zb, zc, dsk_full, snw, wo)


PROMPT_TILE = 512
MLP_ROWS = 1024
MLP_FF = 1024


def _prep_w_in(w):
    wa = jnp.concatenate([w[:, O_AQ:O_BQ], w[:, O_GTS:O_GTS + 1024]], axis=1)
    wb = jnp.concatenate([w[:, O_BQ:O_CZ], w[:, O_GTS + 1024:O_GTS + 2048]], axis=1)
    wc = jnp.concatenate([w[:, O_CZ:O_CDT], jnp.pad(w[:, O_CDT:O_GTS], ((0, 0), (0, 128 - HC))),
                          w[:, O_GTS + 2048:O_GTS + 3072]], axis=1)
    return wa, wb, wc


def _forward(x_prompt, x_sample, cache_win_k, cache_win_v, state_ret, state_ssm, state_conv,
             c_prompt, c_sample, rel_bias_table, attn_sinks, norm1_w, norm2_w, ada_w, ada_b,
             w_in, conv_w, conv_b, dt_bias, A_log, D_skip, ssm_norm_w, w_out, w_up, w_down,
             final_norm_w, *, prompt_tile, mlp_rows, mlp_ff):
    B, T, _ = x_prompt.shape
    DB = x_sample.shape[0]
    kw = HA_KV * HD_A

    mod_all = _ada_call(jnp.concatenate([c_prompt, c_sample], axis=0), ada_w, ada_b)

    bias_t, bias_s = _bias_tables(rel_bias_table)

    w16 = w_in.astype(BF16)
    wu16 = w_up.astype(BF16)
    wd16 = w_down.astype(BF16)
    wo16 = w_out.astype(BF16)
    fn = final_norm_w.reshape(1, D_MODEL)
    ck = cache_win_k.reshape(DEPTH, DB, WINDOW, kw)
    cv = cache_win_v.reshape(DEPTH, DB, WINDOW, kw)
    sret = state_ret.reshape(DEPTH, DB, HB * DK_B, DV_B)
    sssm = state_ssm.reshape(DEPTH, DB, HC * HD_C, N_C)
    sconv = state_conv.reshape(DEPTH, DB, (CONV_W - 1) * CONV_DIM)

    xp = x_prompt
    xs = x_sample.reshape(DB, D_MODEL)
    outs_p = [[] for _ in range(5)]
    conv_s = []
    win_kv, ret_all, ssm_all = (), (), ()
    for l in range(DEPTH):
        wl = w16[l]
        wa, wb, wc = _prep_w_in(wl)
        n1 = norm1_w[l].reshape(1, D_MODEL)
        n2 = norm2_w[l].reshape(1, D_MODEL)
        cw = conv_w[l]
        cb = conv_b[l].reshape(1, CONV_DIM)
        dtb_pad = jnp.pad(dt_bias[l], (0, 128 - HC)).reshape(1, 128)
        a_pad = jnp.pad(-jnp.exp(A_log[l].astype(F32)), (0, 128 - HC)).reshape(1, 128)
        dsk_full = jnp.repeat(D_skip[l], HD_C).reshape(1, D_MODEL)
        snw = ssm_norm_w[l].reshape(1, D_MODEL)
        final = l == DEPTH - 1
        mod_p = mod_all[l, :B].reshape(B, 1, 6 * D_MODEL)
        mod_s = mod_all[l, B:]

        wqgt = jnp.concatenate([wl[:, O_AQ:O_AK] * (HD_A ** -0.5), wl[:, O_GTS:O_GTS + D_MODEL]], axis=1).T
        wkv = wl[:, O_AK:O_BQ]
        sink_rows = jnp.repeat(attn_sinks[l], WINDOW).reshape(HA_KV, 1, (HA_Q // HA_KV) * WINDOW)
        mixa, kbuf, vbuf = _pa_call(xp, mod_p, n1, wqgt, wkv, bias_t, sink_rows, prompt_tile)
        wbt = jnp.concatenate([_deinterleave_pairs(wl[:, O_BQ:O_BK]),
                               _deinterleave_pairs(wl[:, O_BK:O_BV]) * (DK_B ** -0.5), wl[:, O_BV:O_CZ],
                               wl[:, O_GTS + D_MODEL:O_GTS + 2 * D_MODEL]], axis=1).T
        mixab, s_perm = _pb_call(xp, mod_p, n1, wbt, mixa, prompt_tile)
        s_ret = s_perm.reshape(B, HB, 2, DK_B // 2, DV_B).transpose(0, 1, 3, 2, 4)
        x1, h_ssm, conv_new = _pc_call(xp, mod_p, n1, wc, cw, cb, dtb_pad, a_pad, dsk_full, snw,
                                       mixab, wo16[l], prompt_tile)
        xp = _mlp_call(x1.reshape(B * T, D_MODEL), mod_p, n2, wu16, wd16, fn, l,
                       mlp_rows, mlp_ff, T, final).reshape(B, T, D_MODEL)
        for lst, v in zip(outs_p, (kbuf.reshape(B, WINDOW, HA_KV, HD_A), vbuf.reshape(B, WINDOW, HA_KV, HD_A),
                                   s_ret.reshape(B, HB, DK_B, DV_B), h_ssm.reshape(B, HC, HD_C, N_C), conv_new)):
            lst.append(v)

        za = _sproj_call(xs, mod_s, n1, wa)
        zb = _sproj_call(xs, mod_s, n1, wb)
        zc = _sproj_call(xs, mod_s, n1, wc)
        q3 = za[:, 0:HA_Q * HD_A].reshape(DB, HA_Q, HD_A)
        kv = za[:, HA_Q * HD_A:HA_Q * HD_A + 2 * kw]
        ck_new, cv_new, oa3 = _sa_call(q3, kv, ck, cv, bias_s, attn_sinks[l].reshape(HA_Q, 1), l, win_kv)
        win_kv = (ck_new, cv_new)
        s_new, o3 = _sr_call(zb, sret, l, ret_all)
        ret_all = (s_new,)
        h_new, cs_new, y_s, xc_s = _ss_call(zc, sconv, sssm, cw, cb, dtb_pad, a_pad, l, ssm_all)
        ssm_all = (h_new,)
        xs1 = _sm_call(xs, mod_s, oa3.reshape(DB, D_MODEL), o3.reshape(DB, D_MODEL), y_s, xc_s,
                       za, zb, zc, dsk_full, snw, wo16[l])
        xs = _mlp_call(xs1, mod_s, n2, wu16, wd16, fn, l, DB, mlp_ff, 1, final)
        conv_s.append(cs_new.reshape(DB, CONV_W - 1, CONV_DIM))

    stk = lambda lst: jnp.stack(lst, axis=0)
    return (xp, xs.reshape(DB, 1, D_MODEL),
            stk(outs_p[0]), stk(outs_p[1]), stk(outs_p[2]), stk(outs_p[3]), stk(outs_p[4]),
            win_kv[0].reshape(DEPTH, DB, WINDOW, HA_KV, HD_A), win_kv[1].reshape(DEPTH, DB, WINDOW, HA_KV, HD_A),
            ret_all[0].reshape(DEPTH, DB, HB, DK_B, DV_B), ssm_all[0].reshape(DEPTH, DB, HC, HD_C, N_C),
            stk(conv_s))


def kernel(x_prompt, x_sample, cache_win_k, cache_win_v, state_ret, state_ssm, state_conv, c_prompt, c_sample,
           rel_bias_table, attn_sinks, norm1_w, norm2_w, ada_w, ada_b, w_in, conv_w, conv_b, dt_bias, A_log,
           D_skip, ssm_norm_w, w_out, w_up, w_down, final_norm_w):
    return _forward(x_prompt, x_sample, cache_win_k, cache_win_v, state_ret, state_ssm, state_conv,
                    c_prompt, c_sample, rel_bias_table, attn_sinks, norm1_w, norm2_w, ada_w, ada_b,
                    w_in, conv_w, conv_b, dt_bias, A_log, D_skip, ssm_norm_w, w_out, w_up, w_down,
                    final_norm_w, prompt_tile=PROMPT_TILE, mlp_rows=MLP_ROWS, mlp_ff=MLP_FF)
```

```python
import functools
import math

import numpy as np
import jax
import jax.numpy as jnp
from jax import lax
from jax.experimental import pallas as pl
from jax.experimental.pallas import tpu as pltpu

F32 = jnp.float32
BF16 = jnp.bfloat16

D_MODEL = 1024
DEPTH = 2
PAST_LEN = 16384
WINDOW = 128
HA_Q = 16
HA_KV = 4
HD_A = 64
NUM_BUCKETS = 32
MAX_DISTANCE = WINDOW
HB = 8
DK_B = 64
DV_B = 128
HC = 16
HD_C = 64
N_C = 128
G_C = 2
CONV_W = 4
CONV_DIM = D_MODEL + 2 * G_C * N_C
D_FF = 4 * D_MODEL
EPS = 1e-6
NEG = -1e30
LOG2E = 1.4426950408889634

O_AQ, O_AK, O_AV = 0, 1024, 1280
O_BQ, O_BK, O_BV, O_BG = 1536, 2048, 2560, 3584
O_CZ, O_CXBC, O_CDT, O_GTS = 4608, 5632, 7168, 7184
SEG_A = 2560
SEG_B = 4096
SEG_C = 3712
C_DT = 2560
C_GC = 2688

VMEM_LIMIT_V7X = 56 * 1024 * 1024
ROWS_PER_SAMPLE_STEP = 8


def _cparams(n_axes):
    return pltpu.CompilerParams(dimension_semantics=("arbitrary",) * n_axes,
                                vmem_limit_bytes=VMEM_LIMIT_V7X)


def _dot(a, b):
    return jnp.dot(a, b, preferred_element_type=F32)


def _dot_nt(a, b):
    return lax.dot_general(a, b, (((1,), (1,)), ((), ())), preferred_element_type=F32)


def _dot_tn(a, b):
    return lax.dot_general(a, b, (((0,), (0,)), ((), ())), preferred_element_type=F32)


def _sigmoid(x):
    return 0.5 * (jnp.tanh(0.5 * x) + 1.0)


def _silu(x):
    return x * _sigmoid(x)


def _tanh1(half):
    return jnp.tanh(half) + 1.0


def _softplus(x):
    return jnp.maximum(x, 0.0) + jnp.log1p(jnp.exp(-jnp.abs(x)))


def _modnorm(x, nw, sc, sh):
    ms = jnp.mean(x * x, axis=-1, keepdims=True)
    return (x * lax.rsqrt(ms + EPS) * nw) * (1.0 + sc) + sh


def _rms_lanes(x):
    ms = jnp.mean(x * x, axis=-1, keepdims=True)
    return x * lax.rsqrt(ms + EPS)


def _pairswap(x):
    ax = x.ndim - 1
    n = x.shape[ax]
    lane = lax.broadcasted_iota(jnp.int32, x.shape, ax)
    nxt = pltpu.roll(x, n - 1, ax)
    prv = pltpu.roll(x, 1, ax)
    return jnp.where((lane & 1) == 0, nxt, prv)


def _split3(x):
    hi = x.astype(BF16)
    r1 = x - hi.astype(F32)
    mid = r1.astype(BF16)
    lo = (r1 - mid.astype(F32)).astype(BF16)
    return hi, mid, lo


def _gammas():
    return 1.0 - 2.0 ** (-5.0 - np.arange(HB, dtype=np.float64))


def _rot_tables(pos):
    theta = 1.0 / (10000.0 ** np.linspace(0.0, 1.0, DK_B // 2))
    ang = np.asarray(pos, np.float64)[:, None] * theta[None, :]
    cos = np.repeat(np.cos(ang), 2, axis=1)
    sin = np.repeat(np.sin(ang), 2, axis=1)
    sin[:, 0::2] *= -1.0
    return (np.tile(cos, (1, HB)).astype(np.float32), np.tile(sin, (1, HB)).astype(np.float32))


def _ret_tables():
    g = _gammas()
    L = WINDOW
    i = np.arange(L, dtype=np.float64)
    diff = i[:, None] - i[None, :]
    dm = np.where(diff >= 0, g[:, None, None] ** np.maximum(diff, 0.0), 0.0)
    qdec = np.repeat(g[None, :] ** (i[:, None] + 1.0), DK_B, axis=1)
    kdec = np.repeat(g[None, :] ** (L - 1.0 - i[:, None]), DK_B, axis=1)
    return dm.astype(np.float32), qdec.astype(np.float32), kdec.astype(np.float32)


def _t5_bucket_np(dist):
    max_exact = NUM_BUCKETS // 2
    n = np.maximum(dist, 0)
    nf = np.maximum(n, 1).astype(np.float32)
    large = max_exact + (np.log(nf / np.float32(max_exact)) / np.float32(math.log(MAX_DISTANCE / max_exact))
                         * np.float32(NUM_BUCKETS - max_exact)).astype(np.int32)
    large = np.minimum(large, NUM_BUCKETS - 1)
    return np.where(n < max_exact, n, large)


def _bias_tables(rel_table):
    gsz = HA_Q // HA_KV
    qi = np.arange(WINDOW)[None, :]
    kj = np.arange(WINDOW)[:, None]
    dist = np.where(kj > qi, qi + WINDOW - kj, qi - kj)
    onehot = _t5_bucket_np(dist)[..., None] == np.arange(NUM_BUCKETS)
    tab = rel_table.astype(F32)
    bias_kq = jnp.einsum('kqb,bh->hkq', jnp.asarray(onehot, F32), tab, precision=lax.Precision.HIGHEST)
    bias_t = bias_kq.reshape(HA_KV, gsz, WINDOW, WINDOW).transpose(0, 2, 1, 3).reshape(HA_KV, WINDOW, gsz * WINDOW)
    oh_s = _t5_bucket_np(WINDOW - 1 - np.arange(WINDOW))[:, None] == np.arange(NUM_BUCKETS)
    bias_s = jnp.einsum('jb,bh->hj', jnp.asarray(oh_s, F32), tab, precision=lax.Precision.HIGHEST)
    return bias_t, bias_s


def _ada_kernel(c_ref, w_ref, b_ref, o_ref):
    s = _silu(c_ref[...])
    o_ref[0] = _dot(s.astype(BF16), w_ref[0].astype(BF16)) + b_ref[0]


def _ada_call(c_all, ada_w, ada_b):
    n = c_all.shape[0]
    nb = 6
    return pl.pallas_call(
        _ada_kernel,
        grid=(DEPTH, nb),
        in_specs=[pl.BlockSpec((n, D_MODEL), lambda l, j: (0, 0)),
                  pl.BlockSpec((1, D_MODEL, D_MODEL), lambda l, j: (l, 0, j)),
                  pl.BlockSpec((1, 1, D_MODEL), lambda l, j: (l, 0, j))],
        out_specs=pl.BlockSpec((1, n, D_MODEL), lambda l, j: (l, 0, j)),
        out_shape=jax.ShapeDtypeStruct((DEPTH, n, 6 * D_MODEL), F32),
        compiler_params=_cparams(2),
        name="ada_mod",
    )(c_all, ada_w, ada_b.reshape(DEPTH, 1, 6 * D_MODEL))


def _pa_kernel(x_ref, mod_ref, n1_ref, wt_ref, wkv_ref, bias_ref, sink_ref, lowm_ref,
               mix_ref, ko_ref, vo_ref, zt_ref, kv_ref, kprev_ref, vtprev_ref, pen_ref, *, tt):
    t = pl.program_id(1)
    nchunk = tt // WINDOW
    kw = HA_KV * HD_A
    gsz = HA_Q // HA_KV

    @pl.when(t == 0)
    def _():
        kprev_ref[...] = jnp.zeros_like(kprev_ref)
        vtprev_ref[...] = jnp.zeros_like(vtprev_ref)
        pen_ref[...] = jnp.full(pen_ref.shape, NEG, F32)

    mod = mod_ref[0]
    h = _modnorm(x_ref[0], n1_ref[...], mod[:, D_MODEL:2 * D_MODEL], mod[:, 0:D_MODEL]).astype(BF16)
    kv_ref[...] = _dot(h, wkv_ref[...])
    for cc in range(nchunk // 2):
        z2 = _dot_nt(wt_ref[...], h[2 * WINDOW * cc:2 * WINDOW * (cc + 1), :])
        zt_ref[2 * cc] = z2[:, 0:WINDOW]
        zt_ref[2 * cc + 1] = z2[:, WINDOW:2 * WINDOW]
    qw = gsz * WINDOW
    lower = (lax.broadcasted_iota(jnp.int32, (WINDOW, qw), 0)
             > (lax.broadcasted_iota(jnp.int32, (WINDOW, qw), 1) & (WINDOW - 1)))

    def chunk(c, carry):
        r0 = pl.multiple_of(c * WINDOW, WINDOW)
        rows = pl.ds(r0, WINDOW)
        kc = kv_ref[rows, 0:kw]
        vc = kv_ref[rows, kw:2 * kw]
        vt = vc.T
        kk = jnp.concatenate([kprev_ref[...], kc], axis=0).astype(BF16)
        vvt = jnp.concatenate([vtprev_ref[...], vt], axis=1).astype(BF16)
        qt = zt_ref[c, 0:D_MODEL, :].astype(BF16)
        pen = pen_ref[0:1, :]
        s_all = []
        for g in range(HA_KV):
            gs = slice(HD_A * g, HD_A * (g + 1))
            qcat = jnp.concatenate([qt[HD_A * (gsz * g + j):HD_A * (gsz * g + j + 1), :] for j in range(gsz)], axis=1)
            s_all.append(_dot(kk[:, gs], qcat))
        p_all = []
        for g in range(HA_KV):
            sg = jnp.where(lower, s_all[g][0:WINDOW, :] + pen, s_all[g][WINDOW:2 * WINDOW, :]) + bias_ref[g]
            sink = sink_ref[g]
            m = jnp.maximum(jnp.max(sg, axis=0, keepdims=True), sink)
            pw = jnp.exp(sg - m)
            den = jnp.sum(pw, axis=0, keepdims=True) + jnp.exp(sink - m)
            pb = pw.astype(BF16)
            p_prev = pb * lowm_ref[...]
            p_all.append((jnp.concatenate([p_prev, pb - p_prev], axis=0), 1.0 / den))
        pieces = []
        for g in range(HA_KV):
            gs = slice(HD_A * g, HD_A * (g + 1))
            p, rden = p_all[g]
            ot = _dot(vvt[gs, :], p) * rden
            pieces += [ot[:, WINDOW * j:WINDOW * (j + 1)] for j in range(gsz)]
        oat = jnp.concatenate(pieces, axis=0)
        mixt = _sigmoid(zt_ref[c, D_MODEL:2 * D_MODEL, :]) * oat
        mix_ref[0, rows, :] = mixt.T
        kprev_ref[...] = kc
        vtprev_ref[...] = vt
        pen_ref[...] = jnp.zeros_like(pen_ref)
        ko_ref[0] = kc
        vo_ref[0] = vc
        return carry

    lax.fori_loop(0, nchunk, chunk, 0, unroll=True)


def _pa_call(x, mod3, n1, wqgt, wkv, bias_t, sink_rows, tt):
    B, T, _ = x.shape
    kw = HA_KV * HD_A
    qw = (HA_Q // HA_KV) * WINDOW
    kern = functools.partial(_pa_kernel, tt=tt)
    lowm = (np.arange(WINDOW)[:, None] > (np.arange(qw)[None, :] % WINDOW)).astype(np.float32)
    return pl.pallas_call(
        kern,
        grid=(B, T // tt),
        in_specs=[pl.BlockSpec((1, tt, D_MODEL), lambda b, t: (b, t, 0)),
                  pl.BlockSpec((1, 1, 6 * D_MODEL), lambda b, t: (b, 0, 0)),
                  pl.BlockSpec((1, D_MODEL), lambda b, t: (0, 0)),
                  pl.BlockSpec((2 * D_MODEL, D_MODEL), lambda b, t: (0, 0)),
                  pl.BlockSpec((D_MODEL, 2 * kw), lambda b, t: (0, 0)),
                  pl.BlockSpec((HA_KV, WINDOW, qw), lambda b, t: (0, 0, 0)),
                  pl.BlockSpec((HA_KV, 1, qw), lambda b, t: (0, 0, 0)),
                  pl.BlockSpec((WINDOW, qw), lambda b, t: (0, 0))],
        out_specs=[pl.BlockSpec((1, tt, D_MODEL), lambda b, t: (b, t, 0)),
                   pl.BlockSpec((1, WINDOW, kw), lambda b, t: (b, 0, 0)),
                   pl.BlockSpec((1, WINDOW, kw), lambda b, t: (b, 0, 0))],
        out_shape=[jax.ShapeDtypeStruct((B, T, D_MODEL), F32),
                   jax.ShapeDtypeStruct((B, WINDOW, kw), F32),
                   jax.ShapeDtypeStruct((B, WINDOW, kw), F32)],
        scratch_shapes=[pltpu.VMEM((tt // WINDOW, 2 * D_MODEL, WINDOW), F32),
                        pltpu.VMEM((tt, 2 * kw), F32),
                        pltpu.VMEM((WINDOW, kw), F32),
                        pltpu.VMEM((kw, WINDOW), F32),
                        pltpu.VMEM((8, qw), F32)],
        compiler_params=_cparams(2),
        name="prompt_attn",
    )(x, mod3, n1, wqgt, wkv, bias_t, sink_rows, jnp.asarray(lowm, BF16))


def _pb_kernel(x_ref, mod_ref, n1_ref, wt_ref, cos_ref, sin_ref, qdec_ref, kdec_ref, dm_ref, mixa_ref,
               mix_ref, so_ref, zt_ref, s_ref, *, tt, glast):
    t = pl.program_id(1)
    nchunk = tt // WINDOW
    hw = HB * DK_B // 2
    hp = DK_B // 2

    @pl.when(t == 0)
    def _():
        s_ref[...] = jnp.zeros_like(s_ref)

    mod = mod_ref[0]
    h = _modnorm(x_ref[0], n1_ref[...], mod[:, D_MODEL:2 * D_MODEL], mod[:, 0:D_MODEL]).astype(BF16)
    for cc in range(nchunk // 2):
        z2 = _dot_nt(wt_ref[...], h[2 * WINDOW * cc:2 * WINDOW * (cc + 1), :])
        zt_ref[2 * cc] = z2[:, 0:WINDOW]
        zt_ref[2 * cc + 1] = z2[:, WINDOW:2 * WINDOW]

    def head_rows(pair, hh):
        return jnp.concatenate([pair[0][hp * hh:hp * (hh + 1), :], pair[1][hp * hh:hp * (hh + 1), :]], axis=0)

    def chunk(c, carry):
        r0 = pl.multiple_of(c * WINDOW, WINDOW)
        rows = pl.ds(r0, WINDOW)
        cos = cos_ref[c]
        sin = sin_ref[c]
        q1 = zt_ref[c, 0:hw, :]
        q2 = zt_ref[c, hw:2 * hw, :]
        k1 = zt_ref[c, 2 * hw:3 * hw, :]
        k2 = zt_ref[c, 3 * hw:4 * hw, :]
        rq = (q1 * cos - q2 * sin, q1 * sin + q2 * cos)
        rk = (k1 * cos - k2 * sin, k1 * sin + k2 * cos)
        qdec = qdec_ref[...]
        kdec = kdec_ref[...]
        qb = tuple(a.astype(BF16) for a in rq)
        kb = tuple(a.astype(BF16) for a in rk)
        qd = tuple((a * qdec).astype(BF16) for a in rq)
        kd = tuple((a * kdec).astype(BF16) for a in rk)
        vt = zt_ref[c, 4 * hw:4 * hw + HB * DV_B, :].astype(BF16)
        inner, cross, supd = [], [], []
        for hh in range(HB):
            s_old = s_ref[DK_B * hh:DK_B * (hh + 1), :]
            inner.append(_dot_tn(head_rows(qb, hh), head_rows(kb, hh)))
            cross.append(_dot_tn(s_old.astype(BF16), head_rows(qd, hh)))
            supd.append(glast[hh] * s_old + _dot_nt(head_rows(kd, hh), vt[DV_B * hh:DV_B * (hh + 1), :]))
        innd = [(inner[hh] * dm_ref[hh]).astype(BF16) for hh in range(HB)]
        outs = []
        for hh in range(HB):
            ot = _dot_nt(vt[DV_B * hh:DV_B * (hh + 1), :], innd[hh]) + cross[hh]
            s_ref[DK_B * hh:DK_B * (hh + 1), :] = supd[hh]
            ms = jnp.mean(ot * ot, axis=0, keepdims=True)
            outs.append(ot * (0.5 * lax.rsqrt(ms + EPS)))
        obt = jnp.concatenate(outs, axis=0)
        bgt = zt_ref[c, 4 * hw + D_MODEL:4 * hw + 2 * D_MODEL, :]
        gbt = zt_ref[c, 4 * hw + 2 * D_MODEL:4 * hw + 3 * D_MODEL, :]
        mixt = _tanh1(gbt) * (bgt * _tanh1(bgt) * obt)
        mix_ref[0, rows, :] = mixa_ref[0, rows, :] + mixt.T
        return carry

    lax.fori_loop(0, nchunk, chunk, 0, unroll=True)
    so_ref[0] = s_ref[...]


def _deinterleave_pairs(w):
    return w.reshape(w.shape[0], HB, DK_B // 2, 2).transpose(0, 3, 1, 2).reshape(w.shape[0], HB * DK_B)


def _rot_tables_t(T):
    theta = 1.0 / (10000.0 ** np.linspace(0.0, 1.0, DK_B // 2))
    ang = theta[:, None] * np.arange(T, dtype=np.float64)[None, :]

    def lay(a):
        a = np.tile(a, (HB, 1))
        return np.ascontiguousarray(a.reshape(a.shape[0], T // WINDOW, WINDOW).transpose(1, 0, 2)).astype(np.float32)

    return lay(np.cos(ang)), lay(np.sin(ang))


def _pb_call(x, mod3, n1, wbt, mixa, tt):
    B, T, _ = x.shape
    hw = HB * DK_B // 2
    cos, sin = _rot_tables_t(T)
    dm, qdec, kdec = _ret_tables()
    qdec_t = np.ascontiguousarray(qdec[:, ::2].T)
    kdec_t = np.ascontiguousarray(kdec[:, ::2].T)
    glast = tuple(float(v) for v in (_gammas() ** WINDOW))
    kern = functools.partial(_pb_kernel, tt=tt, glast=glast)
    full2 = lambda b, t: (0, 0)
    nct = tt // WINDOW
    return pl.pallas_call(
        kern,
        grid=(B, T // tt),
        in_specs=[pl.BlockSpec((1, tt, D_MODEL), lambda b, t: (b, t, 0)),
                  pl.BlockSpec((1, 1, 6 * D_MODEL), lambda b, t: (b, 0, 0)),
                  pl.BlockSpec((1, D_MODEL), full2),
                  pl.BlockSpec((SEG_B, D_MODEL), full2),
                  pl.BlockSpec((nct, hw, WINDOW), lambda b, t: (t, 0, 0)),
                  pl.BlockSpec((nct, hw, WINDOW), lambda b, t: (t, 0, 0)),
                  pl.BlockSpec((hw, WINDOW), full2),
                  pl.BlockSpec((hw, WINDOW), full2),
                  pl.BlockSpec((HB, WINDOW, WINDOW), lambda b, t: (0, 0, 0)),
                  pl.BlockSpec((1, tt, D_MODEL), lambda b, t: (b, t, 0))],
        out_specs=[pl.BlockSpec((1, tt, D_MODEL), lambda b, t: (b, t, 0)),
                   pl.BlockSpec((1, HB * DK_B, DV_B), lambda b, t: (b, 0, 0))],
        out_shape=[jax.ShapeDtypeStruct((B, T, D_MODEL), F32),
                   jax.ShapeDtypeStruct((B, HB * DK_B, DV_B), F32)],
        scratch_shapes=[pltpu.VMEM((nct, SEG_B, WINDOW), F32),
                        pltpu.VMEM((HB * DK_B, DV_B), F32)],
        compiler_params=_cparams(2),
        name="prompt_retention",
    )(x, mod3, n1, wbt, jnp.asarray(cos), jnp.asarray(sin), jnp.asarray(qdec_t), jnp.asarray(kdec_t),
      jnp.asarray(dm), mixa)


def _pc_kernel(x_ref, mod_ref, n1_ref, w_ref, cw_ref, cb_ref, dtb_ref, a_ref, dsk_ref, snw_ref, tri_ref,
               mixab_ref, wo_ref,
               xo_ref, ho_ref, co_ref,
               z_ref, xbuf_ref, xbc_ref, dt_ref, hst_ref, mixs_ref, *, tt):
    t = pl.program_id(1)

    @pl.when(t == 0)
    def _():
        xbuf_ref[0:8, :] = jnp.zeros((8, CONV_DIM), F32)
        hst_ref[...] = jnp.zeros_like(hst_ref)

    x = x_ref[0]
    mod = mod_ref[0]
    h = _modnorm(x, n1_ref[...], mod[:, D_MODEL:2 * D_MODEL], mod[:, 0:D_MODEL])
    z_ref[...] = _dot(h.astype(BF16), w_ref[...])

    xbuf_ref[8:8 + tt, :] = z_ref[:, 1024:2560]
    acc = cb_ref[...]
    for i in range(CONV_W):
        acc = acc + xbuf_ref[5 + i:5 + i + tt, :] * cw_ref[i:i + 1, :]
    xbc_ref[...] = _silu(acc)
    co_ref[0] = xbuf_ref[tt + 5:tt + 8, :]
    xbuf_ref[0:8, :] = xbuf_ref[tt:tt + 8, :]
    dt_ref[...] = _softplus(z_ref[:, C_DT:C_DT + 128] + dtb_ref[...])

    ii = lax.broadcasted_iota(jnp.int32, (WINDOW, WINDOW), 0)
    jj = lax.broadcasted_iota(jnp.int32, (WINDOW, WINDOW), 1)
    causal = ii >= jj
    hpg = HC // G_C

    def chunk(c):
        rows = slice(WINDOW * c, WINDOW * (c + 1))
        xc = xbc_ref[rows, 0:1024]
        bmat = xbc_ref[rows, 1024:1280]
        cmat = xbc_ref[rows, 1280:1536]
        dtc = dt_ref[rows, :]
        acum = jnp.dot(tri_ref[...], dtc * a_ref[...], precision=lax.Precision.HIGHEST,
                       preferred_element_type=F32)
        acum = acum * LOG2E
        acum_t = acum.T
        rowp_t = acum_t - jnp.log2(dtc.T)
        x_t = xc.T
        xb = xc.astype(BF16)
        bb = bmat.astype(BF16)
        cb16 = cmat.astype(BF16)
        ys = []
        for g in range(G_C):
            ns = slice(N_C * g, N_C * (g + 1))
            cbg = _dot_nt(cb16[:, ns], bb[:, ns])
            for hh in range(hpg * g, hpg * (g + 1)):
                ps = slice(HD_C * hh, HD_C * (hh + 1))
                colb = jnp.broadcast_to(acum[:, hh:hh + 1], (WINDOW, WINDOW))
                rowp = rowp_t[hh:hh + 1, :]
                m = cbg * jnp.exp2(jnp.where(causal, colb - rowp, NEG))
                hs = hst_ref[ps, :]
                ecolb = jnp.exp2(colb)
                cs = cmat[:, ns] * ecolb
                y = _dot(m.astype(BF16), xb[:, ps]) + _dot_nt(cs.astype(BF16), hs.astype(BF16))
                wrow = jnp.exp2(colb[WINDOW - 1:WINDOW, :] - rowp)
                xw = (x_t[ps, :] * wrow).astype(BF16)
                hst_ref[ps, :] = ecolb[WINDOW - 1:WINDOW, :] * hs + _dot(xw, bb[:, ns])
                ys.append(y)
        y = jnp.concatenate(ys, axis=1) + dsk_ref[...] * xc
        yc = y * _silu(z_ref[rows, 0:1024])
        gw = D_MODEL // G_C
        oc = jnp.concatenate([_rms_lanes(yc[:, gw * g:gw * (g + 1)]) for g in range(G_C)], axis=1) * snw_ref[...]
        gc = _sigmoid(z_ref[rows, C_GC:C_GC + 1024])
        mixs_ref[rows, :] = (mixab_ref[0, rows, :] + gc * oc).astype(BF16)

    g1 = mod[:, 2 * D_MODEL:3 * D_MODEL]
    for c in range(tt // WINDOW):
        chunk(c)
        if c % 2 == 1:
            pr = slice(WINDOW * (c - 1), WINDOW * (c + 1))
            xo_ref[0, pr, :] = x_ref[0, pr, :] + g1 * _dot(mixs_ref[pr, :], wo_ref[...])
    ho_ref[0] = hst_ref[...]


def _pc_call(x, mod3, n1, wc, cw, cb, dtb_pad, a_pad, dsk_full, snw, mixab, wo, tt):
    B, T, _ = x.shape
    tri = jnp.asarray(np.tril(np.ones((WINDOW, WINDOW), np.float32)))
    kern = functools.partial(_pc_kernel, tt=tt)
    full2 = lambda b, t: (0, 0)
    return pl.pallas_call(
        kern,
        grid=(B, T // tt),
        in_specs=[pl.BlockSpec((1, tt, D_MODEL), lambda b, t: (b, t, 0)),
                  pl.BlockSpec((1, 1, 6 * D_MODEL), lambda b, t: (b, 0, 0)),
                  pl.BlockSpec((1, D_MODEL), full2),
                  pl.BlockSpec((D_MODEL, SEG_C), full2),
                  pl.BlockSpec((CONV_W, CONV_DIM), full2),
                  pl.BlockSpec((1, CONV_DIM), full2),
                  pl.BlockSpec((1, 128), full2),
                  pl.BlockSpec((1, 128), full2),
                  pl.BlockSpec((1, D_MODEL), full2),
                  pl.BlockSpec((1, D_MODEL), full2),
                  pl.BlockSpec((WINDOW, WINDOW), full2),
                  pl.BlockSpec((1, tt, D_MODEL), lambda b, t: (b, t, 0)),
                  pl.BlockSpec((D_MODEL, D_MODEL), full2)],
        out_specs=[pl.BlockSpec((1, tt, D_MODEL), lambda b, t: (b, t, 0)),
                   pl.BlockSpec((1, HC * HD_C, N_C), lambda b, t: (b, 0, 0)),
                   pl.BlockSpec((1, CONV_W - 1, CONV_DIM), lambda b, t: (b, 0, 0))],
        out_shape=[jax.ShapeDtypeStruct((B, T, D_MODEL), F32),
                   jax.ShapeDtypeStruct((B, HC * HD_C, N_C), F32),
                   jax.ShapeDtypeStruct((B, CONV_W - 1, CONV_DIM), F32)],
        scratch_shapes=[pltpu.VMEM((tt, SEG_C), F32),
                        pltpu.VMEM((tt + 8, CONV_DIM), F32),
                        pltpu.VMEM((tt, CONV_DIM), F32),
                        pltpu.VMEM((tt, 128), F32),
                        pltpu.VMEM((HC * HD_C, N_C), F32),
                        pltpu.VMEM((tt, D_MODEL), BF16)],
        compiler_params=_cparams(2),
        name="prompt_ssd_out",
    )(x, mod3, n1, wc, cw, cb, dtb_pad, a_pad, dsk_full, snw, tri, mixab, wo)


def _mlp_kernel(x_ref, sh_ref, sc_ref, g_ref, n2_ref, wu_ref, wd_ref, fn_ref, o_ref, *, tf, final, per_row):
    rd = (lambda r: r[...]) if per_row else (lambda r: r[0])
    x = x_ref[...]
    hb = _modnorm(x, n2_ref[...], rd(sc_ref), rd(sh_ref)).astype(BF16)
    acc = None
    for f in range(D_FF // tf):
        u = _dot(hb, wu_ref[0, :, tf * f:tf * (f + 1)])
        u = jnp.square(jnp.maximum(u, 0.0)).astype(BF16)
        part = _dot(u, wd_ref[0, tf * f:tf * (f + 1), :])
        acc = part if acc is None else acc + part
    y = x + rd(g_ref) * acc
    if final:
        y = _rms_lanes(y) * fn_ref[...]
    o_ref[...] = y


def _mlp_call(x2, mod, n2, wu, wd, fn, layer, tm, tf, rows_per_mod, final):
    M = x2.shape[0]
    per_row = rows_per_mod == 1
    if per_row:
        mspec = lambda j: pl.BlockSpec((tm, D_MODEL), lambda m: (m, j))
    else:
        mspec = lambda j: pl.BlockSpec((1, 1, D_MODEL), lambda m: ((m * tm) // rows_per_mod, 0, j))
    kern = functools.partial(_mlp_kernel, tf=tf, final=final, per_row=per_row)
    resident = pl.Buffered(1)
    return pl.pallas_call(
        kern,
        grid=(M // tm,),
        in_specs=[pl.BlockSpec((tm, D_MODEL), lambda m: (m, 0)),
                  mspec(3), mspec(4), mspec(5),
                  pl.BlockSpec((1, D_MODEL), lambda m: (0, 0)),
                  pl.BlockSpec((1, D_MODEL, D_FF), lambda m: (layer, 0, 0), pipeline_mode=resident),
                  pl.BlockSpec((1, D_FF, D_MODEL), lambda m: (layer, 0, 0), pipeline_mode=resident),
                  pl.BlockSpec((1, D_MODEL), lambda m: (0, 0))],
        out_specs=pl.BlockSpec((tm, D_MODEL), lambda m: (m, 0)),
        out_shape=jax.ShapeDtypeStruct((M, D_MODEL), F32),
        compiler_params=_cparams(1),
        name="mlp",
    )(x2, mod, mod, mod, n2, wu, wd, fn)


def _sproj_kernel(x_ref, sh_ref, sc_ref, n1_ref, w_ref, o_ref):
    h = _modnorm(x_ref[...], n1_ref[...], sc_ref[...], sh_ref[...])
    o_ref[...] = _dot(h.astype(BF16), w_ref[...])


def _sproj_call(xs, mod_s, n1, w):
    n, width = xs.shape[0], w.shape[1]
    return pl.pallas_call(
        _sproj_kernel,
        grid=(1,),
        in_specs=[pl.BlockSpec((n, D_MODEL), lambda i: (0, 0)),
                  pl.BlockSpec((n, D_MODEL), lambda i: (0, 0)),
                  pl.BlockSpec((n, D_MODEL), lambda i: (0, 1)),
                  pl.BlockSpec((1, D_MODEL), lambda i: (0, 0)),
                  pl.BlockSpec((D_MODEL, width), lambda i: (0, 0))],
        out_specs=pl.BlockSpec((n, width), lambda i: (0, 0)),
        out_shape=jax.ShapeDtypeStruct((n, width), F32),
        compiler_params=_cparams(1),
        name="sample_proj",
    )(xs, mod_s, mod_s, n1, w)


def _carry_outputs(prev, n_in, nsteps):
    nslab = 1 if prev else DEPTH
    extra = [pl.BlockSpec(memory_space=pl.ANY) for _ in prev]
    alias = {n_in + k: k for k in range(len(prev))}
    row = lambda p, i: jnp.where(p == 0, i, nsteps - 1)
    return nslab, extra, alias, row


def _with_fill(body, state_outs):
    def kern(*refs):
        p = pl.program_id(0)

        @pl.when(p == 0)
        def _():
            body(*refs)

        @pl.when(p != 0)
        def _():
            for k in state_outs:
                refs[k][...] = jnp.zeros_like(refs[k])

    return kern


def _sa_kernel(q_ref, kv_ref, ck_ref, cv_ref, bias_ref, sink_ref, *rest):
    ko_ref, vo_ref, oa_ref = rest[-3:]
    nb = ROWS_PER_SAMPLE_STEP
    gsz = HA_Q // HA_KV
    rg = lax.broadcasted_iota(jnp.int32, (HA_Q, HD_A), 0) // gsz
    sink = sink_ref[...]
    kw = HA_KV * HD_A
    scores, vmats = [], []
    for i in range(nb):
        ko_ref[0, i, 0:WINDOW - 1, :] = ck_ref[0, i, 1:WINDOW, :]
        ko_ref[0, i, WINDOW - 1:WINDOW, :] = kv_ref[i:i + 1, 0:kw]
        vo_ref[0, i, 0:WINDOW - 1, :] = cv_ref[0, i, 1:WINDOW, :]
        vo_ref[0, i, WINDOW - 1:WINDOW, :] = kv_ref[i:i + 1, kw:2 * kw]
        kmat = ko_ref[0, i].astype(BF16)
        vmats.append(vo_ref[0, i].astype(BF16))
        q = q_ref[i] * (HD_A ** -0.5)
        qe = jnp.concatenate([jnp.where(rg == g, q, 0.0) for g in range(HA_KV)], axis=1).astype(BF16)
        scores.append(_dot_nt(qe, kmat))
    probs = []
    for i in range(nb):
        sc = scores[i] + bias_ref[...]
        m = jnp.maximum(jnp.max(sc, axis=-1, keepdims=True), sink)
        p = jnp.exp(sc - m)
        den = jnp.sum(p, axis=-1, keepdims=True) + jnp.exp(sink - m)
        probs.append((p.astype(BF16), den))
    outs = [_dot(probs[i][0], vmats[i]) for i in range(nb)]
    for i in range(nb):
        o = outs[i] / probs[i][1]
        o16 = jnp.zeros((HA_Q, HD_A), F32)
        for g in range(HA_KV):
            o16 = o16 + jnp.where(rg == g, o[:, HD_A * g:HD_A * (g + 1)], 0.0)
        oa_ref[i] = o16


def _sa_call(q3, kv, cache_k, cache_v, bias_s, sink_col, layer, prev):
    n = q3.shape[0]
    nb = ROWS_PER_SAMPLE_STEP
    kw = HA_KV * HD_A
    nslab, extra, alias, row = _carry_outputs(prev, 6, n // nb)
    cspec = pl.BlockSpec((1, nb, WINDOW, kw), lambda p, i: (layer, row(p, i), 0, 0))
    ospec = pl.BlockSpec((1, nb, WINDOW, kw), lambda p, i: ((layer + p) % DEPTH, i, 0, 0))
    return pl.pallas_call(
        _with_fill(_sa_kernel, (-3, -2)),
        grid=(nslab, n // nb),
        in_specs=[pl.BlockSpec((nb, HA_Q, HD_A), lambda p, i: (row(p, i), 0, 0)),
                  pl.BlockSpec((nb, 2 * kw), lambda p, i: (row(p, i), 0)),
                  cspec, cspec,
                  pl.BlockSpec((HA_Q, WINDOW), lambda p, i: (0, 0)),
                  pl.BlockSpec((HA_Q, 1), lambda p, i: (0, 0))] + extra,
        out_specs=[ospec, ospec, pl.BlockSpec((nb, HA_Q, HD_A), lambda p, i: (row(p, i), 0, 0))],
        out_shape=[jax.ShapeDtypeStruct((DEPTH, n, WINDOW, kw), F32),
                   jax.ShapeDtypeStruct((DEPTH, n, WINDOW, kw), F32),
                   jax.ShapeDtypeStruct((n, HA_Q, HD_A), F32)],
        input_output_aliases=alias,
        compiler_params=_cparams(2),
        name="sample_attn",
    )(q3, kv, cache_k, cache_v, bias_s, sink_col, *prev)


def _sr_kernel(zb_ref, cos_ref, sin_ref, gcol_ref, s_ref, *rest):
    so_ref, o_ref = rest[-2:]
    nb = ROWS_PER_SAMPLE_STEP
    wq = HB * DK_B
    cos = cos_ref[...]
    sin = sin_ref[...]
    qf = zb_ref[:, 0:wq]
    kf = zb_ref[:, wq:2 * wq]
    qr = qf * cos + _pairswap(qf) * sin
    kr = (kf * cos + _pairswap(kf) * sin) * (DK_B ** -0.5)
    v = zb_ref[:, 2 * wq:2 * wq + HB * DV_B]
    r8 = lax.broadcasted_iota(jnp.int32, (HB, wq), 0)
    hl = lax.broadcasted_iota(jnp.int32, (HB, wq), 1) // DK_B
    rv = lax.broadcasted_iota(jnp.int32, (HB, DV_B), 0)
    outers, q8s = [], []
    for i in range(nb):
        k8 = jnp.where(hl == r8, jnp.broadcast_to(kr[i:i + 1, :], (HB, wq)), 0.0).astype(BF16)
        q8s.append(jnp.where(hl == r8, jnp.broadcast_to(qr[i:i + 1, :], (HB, wq)), 0.0).astype(BF16))
        v8 = jnp.zeros((HB, DV_B), F32)
        for r in range(HB):
            v8 = jnp.where(rv == r, jnp.broadcast_to(v[i:i + 1, DV_B * r:DV_B * (r + 1)], (HB, DV_B)), v8)
        outers.append(_dot_tn(k8, v8.astype(BF16)))
    s_news = []
    for i in range(nb):
        s_new = gcol_ref[...] * s_ref[0, i] + outers[i]
        so_ref[0, i] = s_new
        s_news.append(s_new.astype(BF16))
    for i in range(nb):
        o_ref[i] = _dot(q8s[i], s_news[i])


def _sr_call(zb, state, layer, prev):
    n = zb.shape[0]
    nb = ROWS_PER_SAMPLE_STEP
    wq = HB * DK_B
    cos, sin = _rot_tables(np.array([PAST_LEN]))
    gcol = np.repeat(_gammas(), DK_B)[:, None] * np.ones((1, DV_B))
    nslab, extra, alias, row = _carry_outputs(prev, 5, n // nb)
    full = lambda p, i: (0, 0)
    return pl.pallas_call(
        _with_fill(_sr_kernel, (-2,)),
        grid=(nslab, n // nb),
        in_specs=[pl.BlockSpec((nb, SEG_B), lambda p, i: (row(p, i), 0)),
                  pl.BlockSpec((1, wq), full),
                  pl.BlockSpec((1, wq), full),
                  pl.BlockSpec((wq, DV_B), full),
                  pl.BlockSpec((1, nb, wq, DV_B), lambda p, i: (layer, row(p, i), 0, 0))] + extra,
        out_specs=[pl.BlockSpec((1, nb, wq, DV_B), lambda p, i: ((layer + p) % DEPTH, i, 0, 0)),
                   pl.BlockSpec((nb, HB, DV_B), lambda p, i: (row(p, i), 0, 0))],
        out_shape=[jax.ShapeDtypeStruct((DEPTH, n, wq, DV_B), F32),
                   jax.ShapeDtypeStruct((n, HB, DV_B), F32)],
        input_output_aliases=alias,
        compiler_params=_cparams(2),
        name="sample_retention",
    )(zb, jnp.asarray(cos), jnp.asarray(sin), jnp.asarray(gcol.astype(np.float32)), state, *prev)


def _ss_kernel(zc_ref, cs_ref, cw_ref, cb_ref, dtb_ref, a_ref, e3_ref, h_ref, *rest):
    ho_ref, cso_ref, y_ref, xc_ref = rest[-4:]
    nb = ROWS_PER_SAMPLE_STEP
    cx = zc_ref[:, 1024:2560]
    taps = [cs_ref[0, :, CONV_DIM * i:CONV_DIM * (i + 1)] for i in range(CONV_W - 1)] + [cx]
    acc = cb_ref[...]
    for i in range(CONV_W):
        acc = acc + taps[i] * cw_ref[i:i + 1, :]
    xbc = _silu(acc)
    cso_ref[...] = jnp.concatenate(taps[1:], axis=1)
    xc = xbc[:, 0:1024]
    bmat = xbc[:, 1024:1280]
    cmat = xbc[:, 1280:1536]
    dt = _softplus(zc_ref[:, C_DT:C_DT + 128] + dtb_ref[...])
    da = jnp.exp(dt * a_ref[...])
    dt_e = _dot(jnp.concatenate(_split3(dt), axis=1), e3_ref[...])
    da_e = _dot(jnp.concatenate(_split3(da), axis=1), e3_ref[...])
    dtx = dt_e * xc
    gw = (HC // G_C) * HD_C
    r8 = lax.broadcasted_iota(jnp.int32, (nb, gw), 0)
    rn = lax.broadcasted_iota(jnp.int32, (nb, N_C), 0)
    ones8 = jnp.ones((nb, N_C), BF16)
    prods = {}
    for g in range(G_C):
        ws = slice(gw * g, gw * (g + 1))
        bg16 = bmat[:, N_C * g:N_C * (g + 1)].astype(BF16)
        for i in range(nb):
            x8 = jnp.where(r8 == i, dtx[:, ws], 0.0).astype(BF16)
            outer = _dot_tn(x8, bg16)
            hi, mid, lo = (p.astype(F32) for p in _split3(jnp.broadcast_to(da_e[i:i + 1, ws], (nb, gw))))
            l3 = jnp.where(r8 == 0, hi, jnp.where(r8 == 1, mid, jnp.where(r8 == 2, lo, 0.0)))
            prods[g, i] = (outer, _dot_tn(l3.astype(BF16), ones8))
    h16 = {}
    for g in range(G_C):
        ws = slice(gw * g, gw * (g + 1))
        for i in range(nb):
            outer, dacol = prods[g, i]
            h_new = dacol * h_ref[0, i, ws, :] + outer
            ho_ref[0, i, ws, :] = h_new
            h16[g, i] = h_new.astype(BF16)
    ycols = []
    for g in range(G_C):
        ns = slice(N_C * g, N_C * (g + 1))
        yacc = jnp.zeros((nb, gw), F32)
        for i in range(nb):
            c8 = jnp.where(rn == i, cmat[:, ns], 0.0).astype(BF16)
            yacc = yacc + _dot_nt(c8, h16[g, i])
        ycols.append(yacc)
    y_ref[...] = jnp.concatenate(ycols, axis=1)
    xc_ref[...] = xc


def _ss_call(zc, conv_state, hstate, cw, cb, dtb_pad, a_pad, layer, prev):
    n = zc.shape[0]
    nb = ROWS_PER_SAMPLE_STEP
    e = np.zeros((128, D_MODEL), np.float32)
    for hh in range(HC):
        e[hh, HD_C * hh:HD_C * (hh + 1)] = 1.0
    e3 = jnp.asarray(np.concatenate([e, e, e], axis=0), dtype=BF16)
    cwid = (CONV_W - 1) * CONV_DIM
    full = lambda p, i: (0, 0)
    nslab, extra, alias, row = _carry_outputs(prev, 8, n // nb)
    return pl.pallas_call(
        _with_fill(_ss_kernel, (-4,)),
        grid=(nslab, n // nb),
        in_specs=[pl.BlockSpec((nb, SEG_C), lambda p, i: (row(p, i), 0)),
                  pl.BlockSpec((1, nb, cwid), lambda p, i: (layer, row(p, i), 0)),
                  pl.BlockSpec((CONV_W, CONV_DIM), full),
                  pl.BlockSpec((1, CONV_DIM), full),
                  pl.BlockSpec((1, 128), full),
                  pl.BlockSpec((1, 128), full),
                  pl.BlockSpec((3 * 128, D_MODEL), full),
                  pl.BlockSpec((1, nb, HC * HD_C, N_C), lambda p, i: (layer, row(p, i), 0, 0))] + extra,
        out_specs=[pl.BlockSpec((1, nb, HC * HD_C, N_C), lambda p, i: ((layer + p) % DEPTH, i, 0, 0)),
                   pl.BlockSpec((nb, cwid), lambda p, i: (row(p, i), 0)),
                   pl.BlockSpec((nb, D_MODEL), lambda p, i: (row(p, i), 0)),
                   pl.BlockSpec((nb, D_MODEL), lambda p, i: (row(p, i), 0))],
        out_shape=[jax.ShapeDtypeStruct((DEPTH, n, HC * HD_C, N_C), F32),
                   jax.ShapeDtypeStruct((n, cwid), F32),
                   jax.ShapeDtypeStruct((n, D_MODEL), F32),
                   jax.ShapeDtypeStruct((n, D_MODEL), F32)],
        input_output_aliases=alias,
        compiler_params=_cparams(2),
        name="sample_ssd",
    )(zc, conv_state, cw, cb, dtb_pad, a_pad, e3, hstate, *prev)


def _sm_kernel(x_ref, g1_ref, oa_ref, oret_ref, y_ref, xc_ref, za_ref, zb_ref, zc_ref,
               dsk_ref, snw_ref, wo_ref, o_ref):
    ob = jnp.concatenate([_rms_lanes(oret_ref[:, DV_B * hh:DV_B * (hh + 1)]) for hh in range(HB)], axis=1)
    ob = _silu(zb_ref[:, 2048:3072]) * ob
    yc = (y_ref[...] + dsk_ref[...] * xc_ref[...]) * _silu(zc_ref[:, 0:1024])
    gw = D_MODEL // G_C
    oc = jnp.concatenate([_rms_lanes(yc[:, gw * g:gw * (g + 1)]) for g in range(G_C)], axis=1) * snw_ref[...]
    mix = (_sigmoid(za_ref[:, 1536:2560]) * oa_ref[...] + _sigmoid(zb_ref[:, 3072:4096]) * ob
           + _sigmoid(zc_ref[:, C_GC:C_GC + 1024]) * oc)
    o_ref[...] = x_ref[...] + g1_ref[...] * _dot(mix.astype(BF16), wo_ref[...])


def _sm_call(xs, mod_s, oa, oret, y, xc, za, zb, zc, dsk_full, snw, wo):
    n = xs.shape[0]
    full = lambda i: (0, 0)
    row = lambda w: pl.BlockSpec((n, w), full)
    return pl.pallas_call(
        _sm_kernel,
        grid=(1,),
        in_specs=[row(D_MODEL),
                  pl.BlockSpec((n, D_MODEL), lambda i: (0, 2)),
                  row(D_MODEL), row(D_MODEL), row(D_MODEL), row(D_MODEL),
                  row(SEG_A), row(SEG_B), row(SEG_C),
                  pl.BlockSpec((1, D_MODEL), full),
                  pl.BlockSpec((1, D_MODEL), full),
                  pl.BlockSpec((D_MODEL, D_MODEL), full)],
        out_specs=row(D_MODEL),
        out_shape=jax.ShapeDtypeStruct((n, D_MODEL), F32),
        compiler_params=_cparams(1),
        name="sample_merge_out",
    )(xs, mod_s, oa, oret, y, xc, za, zb, zc, dsk_full, snw, wo)


PROMPT_TILE = 512
ATTN_TILE = 1024
MLP_ROWS = 1024
MLP_FF = 1024


def _prep_w_in(w):
    wa = jnp.concatenate([w[:, O_AQ:O_BQ], w[:, O_GTS:O_GTS + 1024]], axis=1)
    wb = jnp.concatenate([w[:, O_BQ:O_CZ], w[:, O_GTS + 1024:O_GTS + 2048]], axis=1)
    wc = jnp.concatenate([w[:, O_CZ:O_CDT], jnp.pad(w[:, O_CDT:O_GTS], ((0, 0), (0, 128 - HC))),
                          w[:, O_GTS + 2048:O_GTS + 3072]], axis=1)
    return wa, wb, wc


def _forward(x_prompt, x_sample, cache_win_k, cache_win_v, state_ret, state_ssm, state_conv,
             c_prompt, c_sample, rel_bias_table, attn_sinks, norm1_w, norm2_w, ada_w, ada_b,
             w_in, conv_w, conv_b, dt_bias, A_log, D_skip, ssm_norm_w, w_out, w_up, w_down,
             final_norm_w, *, prompt_tile, attn_tile, mlp_rows, mlp_ff):
    B, T, _ = x_prompt.shape
    DB = x_sample.shape[0]
    kw = HA_KV * HD_A

    mod_all = _ada_call(jnp.concatenate([c_prompt, c_sample], axis=0), ada_w, ada_b)

    bias_t, bias_s = _bias_tables(rel_bias_table)

    w16 = w_in.astype(BF16)
    wu16 = w_up.astype(BF16)
    wd16 = w_down.astype(BF16)
    wo16 = w_out.astype(BF16)
    fn = final_norm_w.reshape(1, D_MODEL)
    ck = cache_win_k.reshape(DEPTH, DB, WINDOW, kw)
    cv = cache_win_v.reshape(DEPTH, DB, WINDOW, kw)
    sret = state_ret.reshape(DEPTH, DB, HB * DK_B, DV_B)
    sssm = state_ssm.reshape(DEPTH, DB, HC * HD_C, N_C)
    sconv = state_conv.reshape(DEPTH, DB, (CONV_W - 1) * CONV_DIM)

    xp = x_prompt
    xs = x_sample.reshape(DB, D_MODEL)
    outs_p = [[] for _ in range(5)]
    conv_s = []
    win_kv, ret_all, ssm_all = (), (), ()
    for l in range(DEPTH):
        wl = w16[l]
        wa, wb, wc = _prep_w_in(wl)
        n1 = norm1_w[l].reshape(1, D_MODEL)
        n2 = norm2_w[l].reshape(1, D_MODEL)
        cw = conv_w[l]
        cb = conv_b[l].reshape(1, CONV_DIM)
        dtb_pad = jnp.pad(dt_bias[l], (0, 128 - HC)).reshape(1, 128)
        a_pad = jnp.pad(-jnp.exp(A_log[l].astype(F32)), (0, 128 - HC)).reshape(1, 128)
        dsk_full = jnp.repeat(D_skip[l], HD_C).reshape(1, D_MODEL)
        snw = ssm_norm_w[l].reshape(1, D_MODEL)
        final = l == DEPTH - 1
        mod_p = mod_all[l, :B].reshape(B, 1, 6 * D_MODEL)
        mod_s = mod_all[l, B:]

        wqgt = jnp.concatenate([wl[:, O_AQ:O_AK] * (HD_A ** -0.5), wl[:, O_GTS:O_GTS + D_MODEL]], axis=1).T
        wkv = wl[:, O_AK:O_BQ]
        sink_rows = jnp.repeat(attn_sinks[l], WINDOW).reshape(HA_KV, 1, (HA_Q // HA_KV) * WINDOW)
        mixa, kbuf, vbuf = _pa_call(xp, mod_p, n1, wqgt, wkv, bias_t, sink_rows, attn_tile)
        wbt = jnp.concatenate([_deinterleave_pairs(wl[:, O_BQ:O_BK]),
                               _deinterleave_pairs(wl[:, O_BK:O_BV]) * (DK_B ** -0.5), wl[:, O_BV:O_BG],
                               wl[:, O_BG:O_CZ] * 0.5, wl[:, O_GTS + D_MODEL:O_GTS + 2 * D_MODEL] * 0.5], axis=1).T
        mixab, s_perm = _pb_call(xp, mod_p, n1, wbt, mixa, prompt_tile)
        s_ret = s_perm.reshape(B, HB, 2, DK_B // 2, DV_B).transpose(0, 1, 3, 2, 4)
        x1, h_ssm, conv_new = _pc_call(xp, mod_p, n1, wc, cw, cb, dtb_pad, a_pad, dsk_full, snw,
                                       mixab, wo16[l], prompt_tile)
        xp = _mlp_call(x1.reshape(B * T, D_MODEL), mod_p, n2, wu16, wd16, fn, l,
                       mlp_rows, mlp_ff, T, final).reshape(B, T, D_MODEL)
        for lst, v in zip(outs_p, (kbuf.reshape(B, WINDOW, HA_KV, HD_A), vbuf.reshape(B, WINDOW, HA_KV, HD_A),
                                   s_ret.reshape(B, HB, DK_B, DV_B), h_ssm.reshape(B, HC, HD_C, N_C), conv_new)):
            lst.append(v)

        za = _sproj_call(xs, mod_s, n1, wa)
        zb = _sproj_call(xs, mod_s, n1, wb)
        zc = _sproj_call(xs, mod_s, n1, wc)
        q3 = za[:, 0:HA_Q * HD_A].reshape(DB, HA_Q, HD_A)
        kv = za[:, HA_Q * HD_A:HA_Q * HD_A + 2 * kw]
        ck_new, cv_new, oa3 = _sa_call(q3, kv, ck, cv, bias_s, attn_sinks[l].reshape(HA_Q, 1), l, win_kv)
        win_kv = (ck_new, cv_new)
        s_new, o3 = _sr_call(zb, sret, l, ret_all)
        ret_all = (s_new,)
        h_new, cs_new, y_s, xc_s = _ss_call(zc, sconv, sssm, cw, cb, dtb_pad, a_pad, l, ssm_all)
        ssm_all = (h_new,)
        xs1 = _sm_call(xs, mod_s, oa3.reshape(DB, D_MODEL), o3.reshape(DB, D_MODEL), y_s, xc_s,
                       za, zb, zc, dsk_full, snw, wo16[l])
        xs = _mlp_call(xs1, mod_s, n2, wu16, wd16, fn, l, DB, mlp_ff, 1, final)
        conv_s.append(cs_new.reshape(DB, CONV_W - 1, CONV_DIM))

    stk = lambda lst: jnp.stack(lst, axis=0)
    return (xp, xs.reshape(DB, 1, D_MODEL),
            stk(outs_p[0]), stk(outs_p[1]), stk(outs_p[2]), stk(outs_p[3]), stk(outs_p[4]),
            win_kv[0].reshape(DEPTH, DB, WINDOW, HA_KV, HD_A), win_kv[1].reshape(DEPTH, DB, WINDOW, HA_KV, HD_A),
            ret_all[0].reshape(DEPTH, DB, HB, DK_B, DV_B), ssm_all[0].reshape(DEPTH, DB, HC, HD_C, N_C),
            stk(conv_s))


def kernel(x_prompt, x_sample, cache_win_k, cache_win_v, state_ret, state_ssm, state_conv, c_prompt, c_sample,
           rel_bias_table, attn_sinks, norm1_w, norm2_w, ada_w, ada_b, w_in, conv_w, conv_b, dt_bias, A_log,
           D_skip, ssm_norm_w, w_out, w_up, w_down, final_norm_w):
    return _forward(x_prompt, x_sample, cache_win_k, cache_win_v, state_ret, state_ssm, state_conv,
                    c_prompt, c_sample, rel_bias_table, attn_sinks, norm1_w, norm2_w, ada_w, ada_b,
                    w_in, conv_w, conv_b, dt_bias, A_log, D_skip, ssm_norm_w, w_out, w_up, w_down,
                    final_norm_w, prompt_tile=PROMPT_TILE, attn_tile=ATTN_TILE, mlp_rows=MLP_ROWS, mlp_ff=MLP_FF)
```

```python
import functools
import math

import numpy as np
import jax
import jax.numpy as jnp
from jax import lax
from jax.experimental import pallas as pl
from jax.experimental.pallas import tpu as pltpu

F32 = jnp.float32
BF16 = jnp.bfloat16

D_MODEL = 1024
DEPTH = 2
PAST_LEN = 16384
WINDOW = 128
HA_Q = 16
HA_KV = 4
HD_A = 64
NUM_BUCKETS = 32
MAX_DISTANCE = WINDOW
HB = 8
DK_B = 64
DV_B = 128
HC = 16
HD_C = 64
N_C = 128
G_C = 2
CONV_W = 4
CONV_DIM = D_MODEL + 2 * G_C * N_C
D_FF = 4 * D_MODEL
EPS = 1e-6
NEG = -1e30
LOG2E = 1.4426950408889634

O_AQ, O_AK, O_AV = 0, 1024, 1280
O_BQ, O_BK, O_BV, O_BG = 1536, 2048, 2560, 3584
O_CZ, O_CXBC, O_CDT, O_GTS = 4608, 5632, 7168, 7184
SEG_A = 2560
SEG_B = 4096
SEG_C = 3712
C_DT = 2560
C_GC = 2688

VMEM_LIMIT_V7X = 56 * 1024 * 1024
ROWS_PER_SAMPLE_STEP = 8


def _cparams(n_axes):
    return pltpu.CompilerParams(dimension_semantics=("arbitrary",) * n_axes,
                                vmem_limit_bytes=VMEM_LIMIT_V7X)


def _dot(a, b):
    return jnp.dot(a, b, preferred_element_type=F32)


def _dot_nt(a, b):
    return lax.dot_general(a, b, (((1,), (1,)), ((), ())), preferred_element_type=F32)


def _dot_tn(a, b):
    return lax.dot_general(a, b, (((0,), (0,)), ((), ())), preferred_element_type=F32)


def _sigmoid(x):
    return 0.5 * (jnp.tanh(0.5 * x) + 1.0)


def _silu(x):
    return x * _sigmoid(x)


def _tanh1(half):
    return jnp.tanh(half) + 1.0


def _softplus(x):
    return jnp.maximum(x, 0.0) + jnp.log1p(jnp.exp(-jnp.abs(x)))


def _modnorm(x, nw, sc, sh):
    ms = jnp.mean(x * x, axis=-1, keepdims=True)
    return (x * lax.rsqrt(ms + EPS) * nw) * (1.0 + sc) + sh


def _rms_lanes(x):
    ms = jnp.mean(x * x, axis=-1, keepdims=True)
    return x * lax.rsqrt(ms + EPS)


def _pairswap(x):
    ax = x.ndim - 1
    n = x.shape[ax]
    lane = lax.broadcasted_iota(jnp.int32, x.shape, ax)
    nxt = pltpu.roll(x, n - 1, ax)
    prv = pltpu.roll(x, 1, ax)
    return jnp.where((lane & 1) == 0, nxt, prv)


def _split3(x):
    hi = x.astype(BF16)
    r1 = x - hi.astype(F32)
    mid = r1.astype(BF16)
    lo = (r1 - mid.astype(F32)).astype(BF16)
    return hi, mid, lo


def _gammas():
    return 1.0 - 2.0 ** (-5.0 - np.arange(HB, dtype=np.float64))


def _rot_tables(pos):
    theta = 1.0 / (10000.0 ** np.linspace(0.0, 1.0, DK_B // 2))
    ang = np.asarray(pos, np.float64)[:, None] * theta[None, :]
    cos = np.repeat(np.cos(ang), 2, axis=1)
    sin = np.repeat(np.sin(ang), 2, axis=1)
    sin[:, 0::2] *= -1.0
    return (np.tile(cos, (1, HB)).astype(np.float32), np.tile(sin, (1, HB)).astype(np.float32))


def _ret_tables():
    g = _gammas()
    L = WINDOW
    i = np.arange(L, dtype=np.float64)
    diff = i[:, None] - i[None, :]
    dm = np.where(diff >= 0, g[:, None, None] ** np.maximum(diff, 0.0), 0.0)
    qdec = np.repeat(g[None, :] ** (i[:, None] + 1.0), DK_B, axis=1)
    kdec = np.repeat(g[None, :] ** (L - 1.0 - i[:, None]), DK_B, axis=1)
    return dm.astype(np.float32), qdec.astype(np.float32), kdec.astype(np.float32)


def _t5_bucket_np(dist):
    max_exact = NUM_BUCKETS // 2
    n = np.maximum(dist, 0)
    nf = np.maximum(n, 1).astype(np.float32)
    large = max_exact + (np.log(nf / np.float32(max_exact)) / np.float32(math.log(MAX_DISTANCE / max_exact))
                         * np.float32(NUM_BUCKETS - max_exact)).astype(np.int32)
    large = np.minimum(large, NUM_BUCKETS - 1)
    return np.where(n < max_exact, n, large)


def _bias_tables(rel_table):
    gsz = HA_Q // HA_KV
    qi = np.arange(WINDOW)[None, :]
    kj = np.arange(WINDOW)[:, None]
    dist = np.where(kj > qi, qi + WINDOW - kj, qi - kj)
    onehot = _t5_bucket_np(dist)[..., None] == np.arange(NUM_BUCKETS)
    tab = rel_table.astype(F32)
    bias_kq = jnp.einsum('kqb,bh->hkq', jnp.asarray(onehot, F32), tab, precision=lax.Precision.HIGHEST)
    bias_t = bias_kq.reshape(HA_KV, gsz, WINDOW, WINDOW).transpose(0, 2, 1, 3).reshape(HA_KV, WINDOW, gsz * WINDOW)
    oh_s = _t5_bucket_np(WINDOW - 1 - np.arange(WINDOW))[:, None] == np.arange(NUM_BUCKETS)
    bias_s = jnp.einsum('jb,bh->hj', jnp.asarray(oh_s, F32), tab, precision=lax.Precision.HIGHEST)
    return bias_t, bias_s


def _ada_kernel(c_ref, w_ref, b_ref, o_ref):
    s = _silu(c_ref[...])
    o_ref[0] = _dot(s.astype(BF16), w_ref[0].astype(BF16)) + b_ref[0]


def _ada_call(c_all, ada_w, ada_b):
    n = c_all.shape[0]
    nb = 6
    return pl.pallas_call(
        _ada_kernel,
        grid=(DEPTH, nb),
        in_specs=[pl.BlockSpec((n, D_MODEL), lambda l, j: (0, 0)),
                  pl.BlockSpec((1, D_MODEL, D_MODEL), lambda l, j: (l, 0, j)),
                  pl.BlockSpec((1, 1, D_MODEL), lambda l, j: (l, 0, j))],
        out_specs=pl.BlockSpec((1, n, D_MODEL), lambda l, j: (l, 0, j)),
        out_shape=jax.ShapeDtypeStruct((DEPTH, n, 6 * D_MODEL), F32),
        compiler_params=_cparams(2),
        name="ada_mod",
    )(c_all, ada_w, ada_b.reshape(DEPTH, 1, 6 * D_MODEL))


def _pa_kernel(x_ref, mod_ref, n1_ref, wt_ref, wkv_ref, bias_ref, sink_ref, lowm_ref,
               mix_ref, ko_ref, vo_ref, zt_ref, kv_ref, kprev_ref, vtprev_ref, pen_ref, *, tt):
    t = pl.program_id(1)
    nchunk = tt // WINDOW
    kw = HA_KV * HD_A
    gsz = HA_Q // HA_KV

    @pl.when(t == 0)
    def _():
        kprev_ref[...] = jnp.zeros_like(kprev_ref)
        vtprev_ref[...] = jnp.zeros_like(vtprev_ref)
        pen_ref[...] = jnp.full(pen_ref.shape, NEG, F32)

    mod = mod_ref[0]
    h = _modnorm(x_ref[0], n1_ref[...], mod[:, D_MODEL:2 * D_MODEL], mod[:, 0:D_MODEL]).astype(BF16)
    kv_ref[...] = _dot(h, wkv_ref[...])
    for cc in range(nchunk // 2):
        z2 = _dot_nt(wt_ref[...], h[2 * WINDOW * cc:2 * WINDOW * (cc + 1), :])
        zt_ref[2 * cc] = z2[:, 0:WINDOW]
        zt_ref[2 * cc + 1] = z2[:, WINDOW:2 * WINDOW]
    qw = gsz * WINDOW
    lower = (lax.broadcasted_iota(jnp.int32, (WINDOW, qw), 0)
             > (lax.broadcasted_iota(jnp.int32, (WINDOW, qw), 1) & (WINDOW - 1)))

    def chunk(c, carry):
        r0 = pl.multiple_of(c * WINDOW, WINDOW)
        rows = pl.ds(r0, WINDOW)
        kc = kv_ref[rows, 0:kw]
        vc = kv_ref[rows, kw:2 * kw]
        vt = vc.T
        kk = jnp.concatenate([kprev_ref[...], kc], axis=0).astype(BF16)
        vvt = jnp.concatenate([vtprev_ref[...], vt], axis=1).astype(BF16)
        qt = zt_ref[c, 0:D_MODEL, :].astype(BF16)
        pen = pen_ref[0:1, :]
        s_all = []
        for g in range(HA_KV):
            gs = slice(HD_A * g, HD_A * (g + 1))
            qcat = jnp.concatenate([qt[HD_A * (gsz * g + j):HD_A * (gsz * g + j + 1), :] for j in range(gsz)], axis=1)
            s_all.append(_dot(kk[:, gs], qcat))
        p_all = []
        for g in range(HA_KV):
            sg = jnp.where(lower, s_all[g][0:WINDOW, :] + pen, s_all[g][WINDOW:2 * WINDOW, :]) + bias_ref[g]
            sink = sink_ref[g]
            m = jnp.maximum(jnp.max(sg, axis=0, keepdims=True), sink)
            pw = jnp.exp(sg - m)
            den = jnp.sum(pw, axis=0, keepdims=True) + jnp.exp(sink - m)
            pb = pw.astype(BF16)
            p_prev = pb * lowm_ref[...]
            p_all.append((jnp.concatenate([p_prev, pb - p_prev], axis=0), 1.0 / den))
        pieces = []
        for g in range(HA_KV):
            gs = slice(HD_A * g, HD_A * (g + 1))
            p, rden = p_all[g]
            ot = _dot(vvt[gs, :], p) * rden
            pieces += [ot[:, WINDOW * j:WINDOW * (j + 1)] for j in range(gsz)]
        oat = jnp.concatenate(pieces, axis=0)
        mixt = _sigmoid(zt_ref[c, D_MODEL:2 * D_MODEL, :]) * oat
        mix_ref[0, rows, :] = mixt.T
        kprev_ref[...] = kc
        vtprev_ref[...] = vt
        pen_ref[...] = jnp.zeros_like(pen_ref)
        ko_ref[0] = kc
        vo_ref[0] = vc
        return carry

    lax.fori_loop(0, nchunk, chunk, 0, unroll=True)


def _pa_call(x, mod3, n1, wqgt, wkv, bias_t, sink_rows, tt):
    B, T, _ = x.shape
    kw = HA_KV * HD_A
    qw = (HA_Q // HA_KV) * WINDOW
    kern = functools.partial(_pa_kernel, tt=tt)
    lowm = (np.arange(WINDOW)[:, None] > (np.arange(qw)[None, :] % WINDOW)).astype(np.float32)
    return pl.pallas_call(
        kern,
        grid=(B, T // tt),
        in_specs=[pl.BlockSpec((1, tt, D_MODEL), lambda b, t: (b, t, 0)),
                  pl.BlockSpec((1, 1, 6 * D_MODEL), lambda b, t: (b, 0, 0)),
                  pl.BlockSpec((1, D_MODEL), lambda b, t: (0, 0)),
                  pl.BlockSpec((2 * D_MODEL, D_MODEL), lambda b, t: (0, 0)),
                  pl.BlockSpec((D_MODEL, 2 * kw), lambda b, t: (0, 0)),
                  pl.BlockSpec((HA_KV, WINDOW, qw), lambda b, t: (0, 0, 0)),
                  pl.BlockSpec((HA_KV, 1, qw), lambda b, t: (0, 0, 0)),
                  pl.BlockSpec((WINDOW, qw), lambda b, t: (0, 0))],
        out_specs=[pl.BlockSpec((1, tt, D_MODEL), lambda b, t: (b, t, 0)),
                   pl.BlockSpec((1, WINDOW, kw), lambda b, t: (b, 0, 0)),
                   pl.BlockSpec((1, WINDOW, kw), lambda b, t: (b, 0, 0))],
        out_shape=[jax.ShapeDtypeStruct((B, T, D_MODEL), F32),
                   jax.ShapeDtypeStruct((B, WINDOW, kw), F32),
                   jax.ShapeDtypeStruct((B, WINDOW, kw), F32)],
        scratch_shapes=[pltpu.VMEM((tt // WINDOW, 2 * D_MODEL, WINDOW), F32),
                        pltpu.VMEM((tt, 2 * kw), F32),
                        pltpu.VMEM((WINDOW, kw), F32),
                        pltpu.VMEM((kw, WINDOW), F32),
                        pltpu.VMEM((8, qw), F32)],
        compiler_params=_cparams(2),
        name="prompt_attn",
    )(x, mod3, n1, wqgt, wkv, bias_t, sink_rows, jnp.asarray(lowm, BF16))


def _pb_kernel(x_ref, mod_ref, n1_ref, wt_ref, cos_ref, sin_ref, qdec_ref, kdec_ref, dm_ref, mixa_ref,
               mix_ref, so_ref, zt_ref, s_ref, *, tt, glast):
    t = pl.program_id(1)
    nchunk = tt // WINDOW
    hw = HB * DK_B // 2
    hp = DK_B // 2

    @pl.when(t == 0)
    def _():
        s_ref[...] = jnp.zeros_like(s_ref)

    mod = mod_ref[0]
    h = _modnorm(x_ref[0], n1_ref[...], mod[:, D_MODEL:2 * D_MODEL], mod[:, 0:D_MODEL]).astype(BF16)
    for cc in range(nchunk // 2):
        z2 = _dot_nt(wt_ref[...], h[2 * WINDOW * cc:2 * WINDOW * (cc + 1), :])
        zt_ref[2 * cc] = z2[:, 0:WINDOW]
        zt_ref[2 * cc + 1] = z2[:, WINDOW:2 * WINDOW]

    def head_rows(pair, hh):
        return jnp.concatenate([pair[0][hp * hh:hp * (hh + 1), :], pair[1][hp * hh:hp * (hh + 1), :]], axis=0)

    def chunk(c, carry):
        r0 = pl.multiple_of(c * WINDOW, WINDOW)
        rows = pl.ds(r0, WINDOW)
        cos = cos_ref[c]
        sin = sin_ref[c]
        q1 = zt_ref[c, 0:hw, :]
        q2 = zt_ref[c, hw:2 * hw, :]
        k1 = zt_ref[c, 2 * hw:3 * hw, :]
        k2 = zt_ref[c, 3 * hw:4 * hw, :]
        rq = (q1 * cos - q2 * sin, q1 * sin + q2 * cos)
        rk = (k1 * cos - k2 * sin, k1 * sin + k2 * cos)
        qdec = qdec_ref[...]
        kdec = kdec_ref[...]
        qb = tuple(a.astype(BF16) for a in rq)
        kb = tuple(a.astype(BF16) for a in rk)
        qd = tuple((a * qdec).astype(BF16) for a in rq)
        kd = tuple((a * kdec).astype(BF16) for a in rk)
        vt = zt_ref[c, 4 * hw:4 * hw + HB * DV_B, :].astype(BF16)
        inner, cross, supd = [], [], []
        for hh in range(HB):
            s_old = s_ref[DK_B * hh:DK_B * (hh + 1), :]
            inner.append(_dot_tn(head_rows(qb, hh), head_rows(kb, hh)))
            cross.append(_dot_tn(s_old.astype(BF16), head_rows(qd, hh)))
            supd.append(glast[hh] * s_old + _dot_nt(head_rows(kd, hh), vt[DV_B * hh:DV_B * (hh + 1), :]))
        innd = [(inner[hh] * dm_ref[hh]).astype(BF16) for hh in range(HB)]
        outs = []
        for hh in range(HB):
            ot = _dot_nt(vt[DV_B * hh:DV_B * (hh + 1), :], innd[hh]) + cross[hh]
            s_ref[DK_B * hh:DK_B * (hh + 1), :] = supd[hh]
            ms = jnp.mean(ot * ot, axis=0, keepdims=True)
            outs.append(ot * (0.5 * lax.rsqrt(ms + EPS)))
        obt = jnp.concatenate(outs, axis=0)
        bgt = zt_ref[c, 4 * hw + D_MODEL:4 * hw + 2 * D_MODEL, :]
        gbt = zt_ref[c, 4 * hw + 2 * D_MODEL:4 * hw + 3 * D_MODEL, :]
        mixt = _tanh1(gbt) * (bgt * _tanh1(bgt) * obt)
        mix_ref[0, rows, :] = mixa_ref[0, rows, :] + mixt.T
        return carry

    lax.fori_loop(0, nchunk, chunk, 0, unroll=True)
    so_ref[0] = s_ref[...]


def _deinterleave_pairs(w):
    return w.reshape(w.shape[0], HB, DK_B // 2, 2).transpose(0, 3, 1, 2).reshape(w.shape[0], HB * DK_B)


def _rot_tables_t(T):
    theta = 1.0 / (10000.0 ** np.linspace(0.0, 1.0, DK_B // 2))
    ang = theta[:, None] * np.arange(T, dtype=np.float64)[None, :]

    def lay(a):
        a = np.tile(a, (HB, 1))
        return np.ascontiguousarray(a.reshape(a.shape[0], T // WINDOW, WINDOW).transpose(1, 0, 2)).astype(np.float32)

    return lay(np.cos(ang)), lay(np.sin(ang))


def _pb_call(x, mod3, n1, wbt, mixa, tt):
    B, T, _ = x.shape
    hw = HB * DK_B // 2
    cos, sin = _rot_tables_t(T)
    dm, qdec, kdec = _ret_tables()
    qdec_t = np.ascontiguousarray(qdec[:, ::2].T)
    kdec_t = np.ascontiguousarray(kdec[:, ::2].T)
    glast = tuple(float(v) for v in (_gammas() ** WINDOW))
    kern = functools.partial(_pb_kernel, tt=tt, glast=glast)
    full2 = lambda b, t: (0, 0)
    nct = tt // WINDOW
    return pl.pallas_call(
        kern,
        grid=(B, T // tt),
        in_specs=[pl.BlockSpec((1, tt, D_MODEL), lambda b, t: (b, t, 0)),
                  pl.BlockSpec((1, 1, 6 * D_MODEL), lambda b, t: (b, 0, 0)),
                  pl.BlockSpec((1, D_MODEL), full2),
                  pl.BlockSpec((SEG_B, D_MODEL), full2, pipeline_mode=pl.Buffered(1)),
                  pl.BlockSpec((nct, hw, WINDOW), lambda b, t: (t, 0, 0)),
                  pl.BlockSpec((nct, hw, WINDOW), lambda b, t: (t, 0, 0)),
                  pl.BlockSpec((hw, WINDOW), full2),
                  pl.BlockSpec((hw, WINDOW), full2),
                  pl.BlockSpec((HB, WINDOW, WINDOW), lambda b, t: (0, 0, 0)),
                  pl.BlockSpec((1, tt, D_MODEL), lambda b, t: (b, t, 0))],
        out_specs=[pl.BlockSpec((1, tt, D_MODEL), lambda b, t: (b, t, 0)),
                   pl.BlockSpec((1, HB * DK_B, DV_B), lambda b, t: (b, 0, 0))],
        out_shape=[jax.ShapeDtypeStruct((B, T, D_MODEL), F32),
                   jax.ShapeDtypeStruct((B, HB * DK_B, DV_B), F32)],
        scratch_shapes=[pltpu.VMEM((nct, SEG_B, WINDOW), F32),
                        pltpu.VMEM((HB * DK_B, DV_B), F32)],
        compiler_params=_cparams(2),
        name="prompt_retention",
    )(x, mod3, n1, wbt, jnp.asarray(cos), jnp.asarray(sin), jnp.asarray(qdec_t), jnp.asarray(kdec_t),
      jnp.asarray(dm), mixa)


def _pc_kernel(x_ref, mod_ref, n1_ref, w_ref, cw_ref, cb_ref, dtb_ref, a_ref, dsk_ref, snw_ref, tri_ref,
               mixab_ref, wo_ref,
               xo_ref, ho_ref, co_ref,
               z_ref, xbuf_ref, xbc_ref, dt_ref, hst_ref, mixs_ref, *, tt):
    t = pl.program_id(1)

    @pl.when(t == 0)
    def _():
        xbuf_ref[0:8, :] = jnp.zeros((8, CONV_DIM), F32)
        hst_ref[...] = jnp.zeros_like(hst_ref)

    x = x_ref[0]
    mod = mod_ref[0]
    h = _modnorm(x, n1_ref[...], mod[:, D_MODEL:2 * D_MODEL], mod[:, 0:D_MODEL])
    z_ref[...] = _dot(h.astype(BF16), w_ref[...])

    xbuf_ref[8:8 + tt, :] = z_ref[:, 1024:2560]
    acc = cb_ref[...]
    for i in range(CONV_W):
        acc = acc + xbuf_ref[5 + i:5 + i + tt, :] * cw_ref[i:i + 1, :]
    xbc_ref[...] = _silu(acc)
    co_ref[0] = xbuf_ref[tt + 5:tt + 8, :]
    xbuf_ref[0:8, :] = xbuf_ref[tt:tt + 8, :]
    dt_ref[...] = _softplus(z_ref[:, C_DT:C_DT + 128] + dtb_ref[...])

    ii = lax.broadcasted_iota(jnp.int32, (WINDOW, WINDOW), 0)
    jj = lax.broadcasted_iota(jnp.int32, (WINDOW, WINDOW), 1)
    causal = ii >= jj
    hpg = HC // G_C

    def chunk(c):
        rows = slice(WINDOW * c, WINDOW * (c + 1))
        xc = xbc_ref[rows, 0:1024]
        bmat = xbc_ref[rows, 1024:1280]
        cmat = xbc_ref[rows, 1280:1536]
        dtc = dt_ref[rows, :]
        acum = jnp.dot(tri_ref[...], dtc * a_ref[...], precision=lax.Precision.HIGHEST,
                       preferred_element_type=F32)
        acum = acum * LOG2E
        acum_t = acum.T
        rowp_t = acum_t - jnp.log2(dtc.T)
        x_t = xc.T
        xb = xc.astype(BF16)
        bb = bmat.astype(BF16)
        cb16 = cmat.astype(BF16)
        ys = []
        for g in range(G_C):
            ns = slice(N_C * g, N_C * (g + 1))
            cbg = _dot_nt(cb16[:, ns], bb[:, ns])
            for hh in range(hpg * g, hpg * (g + 1)):
                ps = slice(HD_C * hh, HD_C * (hh + 1))
                colb = jnp.broadcast_to(acum[:, hh:hh + 1], (WINDOW, WINDOW))
                rowp = rowp_t[hh:hh + 1, :]
                m = cbg * jnp.exp2(jnp.where(causal, colb - rowp, NEG))
                hs = hst_ref[ps, :]
                ecolb = jnp.exp2(colb)
                cs = cmat[:, ns] * ecolb
                y = _dot(m.astype(BF16), xb[:, ps]) + _dot_nt(cs.astype(BF16), hs.astype(BF16))
                wrow = jnp.exp2(colb[WINDOW - 1:WINDOW, :] - rowp)
                xw = (x_t[ps, :] * wrow).astype(BF16)
                hst_ref[ps, :] = ecolb[WINDOW - 1:WINDOW, :] * hs + _dot(xw, bb[:, ns])
                ys.append(y)
        y = jnp.concatenate(ys, axis=1) + dsk_ref[...] * xc
        yc = y * _silu(z_ref[rows, 0:1024])
        gw = D_MODEL // G_C
        oc = jnp.concatenate([_rms_lanes(yc[:, gw * g:gw * (g + 1)]) for g in range(G_C)], axis=1) * snw_ref[...]
        gc = _sigmoid(z_ref[rows, C_GC:C_GC + 1024])
        mixs_ref[rows, :] = (mixab_ref[0, rows, :] + gc * oc).astype(BF16)

    g1 = mod[:, 2 * D_MODEL:3 * D_MODEL]
    for c in range(tt // WINDOW):
        chunk(c)
        if c % 2 == 1:
            pr = slice(WINDOW * (c - 1), WINDOW * (c + 1))
            xo_ref[0, pr, :] = x_ref[0, pr, :] + g1 * _dot(mixs_ref[pr, :], wo_ref[...])
    ho_ref[0] = hst_ref[...]


def _pc_call(x, mod3, n1, wc, cw, cb, dtb_pad, a_pad, dsk_full, snw, mixab, wo, tt):
    B, T, _ = x.shape
    tri = jnp.asarray(np.tril(np.ones((WINDOW, WINDOW), np.float32)))
    kern = functools.partial(_pc_kernel, tt=tt)
    full2 = lambda b, t: (0, 0)
    return pl.pallas_call(
        kern,
        grid=(B, T // tt),
        in_specs=[pl.BlockSpec((1, tt, D_MODEL), lambda b, t: (b, t, 0)),
                  pl.BlockSpec((1, 1, 6 * D_MODEL), lambda b, t: (b, 0, 0)),
                  pl.BlockSpec((1, D_MODEL), full2),
                  pl.BlockSpec((D_MODEL, SEG_C), full2),
                  pl.BlockSpec((CONV_W, CONV_DIM), full2),
                  pl.BlockSpec((1, CONV_DIM), full2),
                  pl.BlockSpec((1, 128), full2),
                  pl.BlockSpec((1, 128), full2),
                  pl.BlockSpec((1, D_MODEL), full2),
                  pl.BlockSpec((1, D_MODEL), full2),
                  pl.BlockSpec((WINDOW, WINDOW), full2),
                  pl.BlockSpec((1, tt, D_MODEL), lambda b, t: (b, t, 0)),
                  pl.BlockSpec((D_MODEL, D_MODEL), full2)],
        out_specs=[pl.BlockSpec((1, tt, D_MODEL), lambda b, t: (b, t, 0)),
                   pl.BlockSpec((1, HC * HD_C, N_C), lambda b, t: (b, 0, 0)),
                   pl.BlockSpec((1, CONV_W - 1, CONV_DIM), lambda b, t: (b, 0, 0))],
        out_shape=[jax.ShapeDtypeStruct((B, T, D_MODEL), F32),
                   jax.ShapeDtypeStruct((B, HC * HD_C, N_C), F32),
                   jax.ShapeDtypeStruct((B, CONV_W - 1, CONV_DIM), F32)],
        scratch_shapes=[pltpu.VMEM((tt, SEG_C), F32),
                        pltpu.VMEM((tt + 8, CONV_DIM), F32),
                        pltpu.VMEM((tt, CONV_DIM), F32),
                        pltpu.VMEM((tt, 128), F32),
                        pltpu.VMEM((HC * HD_C, N_C), F32),
                        pltpu.VMEM((tt, D_MODEL), BF16)],
        compiler_params=_cparams(2),
        name="prompt_ssd_out",
    )(x, mod3, n1, wc, cw, cb, dtb_pad, a_pad, dsk_full, snw, tri, mixab, wo)


def _mlp_kernel(x_ref, sh_ref, sc_ref, g_ref, n2_ref, wu_ref, wd_ref, fn_ref, o_ref, *, tf, final, per_row):
    rd = (lambda r: r[...]) if per_row else (lambda r: r[0])
    x = x_ref[...]
    hb = _modnorm(x, n2_ref[...], rd(sc_ref), rd(sh_ref)).astype(BF16)
    acc = None
    for f in range(D_FF // tf):
        u = _dot(hb, wu_ref[0, :, tf * f:tf * (f + 1)])
        u = jnp.square(jnp.maximum(u, 0.0)).astype(BF16)
        part = _dot(u, wd_ref[0, tf * f:tf * (f + 1), :])
        acc = part if acc is None else acc + part
    y = x + rd(g_ref) * acc
    if final:
        y = _rms_lanes(y) * fn_ref[...]
    o_ref[...] = y


def _mlp_call(x2, mod, n2, wu, wd, fn, layer, tm, tf, rows_per_mod, final):
    M = x2.shape[0]
    per_row = rows_per_mod == 1
    if per_row:
        mspec = lambda j: pl.BlockSpec((tm, D_MODEL), lambda m: (m, j))
    else:
        mspec = lambda j: pl.BlockSpec((1, 1, D_MODEL), lambda m: ((m * tm) // rows_per_mod, 0, j))
    kern = functools.partial(_mlp_kernel, tf=tf, final=final, per_row=per_row)
    resident = pl.Buffered(1)
    return pl.pallas_call(
        kern,
        grid=(M // tm,),
        in_specs=[pl.BlockSpec((tm, D_MODEL), lambda m: (m, 0)),
                  mspec(3), mspec(4), mspec(5),
                  pl.BlockSpec((1, D_MODEL), lambda m: (0, 0)),
                  pl.BlockSpec((1, D_MODEL, D_FF), lambda m: (layer, 0, 0), pipeline_mode=resident),
                  pl.BlockSpec((1, D_FF, D_MODEL), lambda m: (layer, 0, 0), pipeline_mode=resident),
                  pl.BlockSpec((1, D_MODEL), lambda m: (0, 0))],
        out_specs=pl.BlockSpec((tm, D_MODEL), lambda m: (m, 0)),
        out_shape=jax.ShapeDtypeStruct((M, D_MODEL), F32),
        compiler_params=_cparams(1),
        name="mlp",
    )(x2, mod, mod, mod, n2, wu, wd, fn)


def _sproj_kernel(x_ref, sh_ref, sc_ref, n1_ref, w_ref, o_ref):
    h = _modnorm(x_ref[...], n1_ref[...], sc_ref[...], sh_ref[...])
    o_ref[...] = _dot(h.astype(BF16), w_ref[...])


def _sproj_call(xs, mod_s, n1, w):
    n, width = xs.shape[0], w.shape[1]
    return pl.pallas_call(
        _sproj_kernel,
        grid=(1,),
        in_specs=[pl.BlockSpec((n, D_MODEL), lambda i: (0, 0)),
                  pl.BlockSpec((n, D_MODEL), lambda i: (0, 0)),
                  pl.BlockSpec((n, D_MODEL), lambda i: (0, 1)),
                  pl.BlockSpec((1, D_MODEL), lambda i: (0, 0)),
                  pl.BlockSpec((D_MODEL, width), lambda i: (0, 0))],
        out_specs=pl.BlockSpec((n, width), lambda i: (0, 0)),
        out_shape=jax.ShapeDtypeStruct((n, width), F32),
        compiler_params=_cparams(1),
        name="sample_proj",
    )(xs, mod_s, mod_s, n1, w)


def _carry_outputs(prev, n_in, nsteps):
    nslab = 1 if prev else DEPTH
    extra = [pl.BlockSpec(memory_space=pl.ANY) for _ in prev]
    alias = {n_in + k: k for k in range(len(prev))}
    row = lambda p, i: jnp.where(p == 0, i, nsteps - 1)
    return nslab, extra, alias, row


def _with_fill(body, state_outs):
    def kern(*refs):
        p = pl.program_id(0)

        @pl.when(p == 0)
        def _():
            body(*refs)

        @pl.when(p != 0)
        def _():
            for k in state_outs:
                refs[k][...] = jnp.zeros_like(refs[k])

    return kern


def _sa_kernel(q_ref, kv_ref, ck_ref, cv_ref, bias_ref, sink_ref, *rest):
    ko_ref, vo_ref, oa_ref = rest[-3:]
    nb = ROWS_PER_SAMPLE_STEP
    gsz = HA_Q // HA_KV
    rg = lax.broadcasted_iota(jnp.int32, (HA_Q, HD_A), 0) // gsz
    sink = sink_ref[...]
    kw = HA_KV * HD_A
    scores, vmats = [], []
    for i in range(nb):
        ko_ref[0, i, 0:WINDOW - 1, :] = ck_ref[0, i, 1:WINDOW, :]
        ko_ref[0, i, WINDOW - 1:WINDOW, :] = kv_ref[i:i + 1, 0:kw]
        vo_ref[0, i, 0:WINDOW - 1, :] = cv_ref[0, i, 1:WINDOW, :]
        vo_ref[0, i, WINDOW - 1:WINDOW, :] = kv_ref[i:i + 1, kw:2 * kw]
        kmat = ko_ref[0, i].astype(BF16)
        vmats.append(vo_ref[0, i].astype(BF16))
        q = q_ref[i] * (HD_A ** -0.5)
        qe = jnp.concatenate([jnp.where(rg == g, q, 0.0) for g in range(HA_KV)], axis=1).astype(BF16)
        scores.append(_dot_nt(qe, kmat))
    probs = []
    for i in range(nb):
        sc = scores[i] + bias_ref[...]
        m = jnp.maximum(jnp.max(sc, axis=-1, keepdims=True), sink)
        p = jnp.exp(sc - m)
        den = jnp.sum(p, axis=-1, keepdims=True) + jnp.exp(sink - m)
        probs.append((p.astype(BF16), den))
    outs = [_dot(probs[i][0], vmats[i]) for i in range(nb)]
    for i in range(nb):
        o = outs[i] / probs[i][1]
        o16 = jnp.zeros((HA_Q, HD_A), F32)
        for g in range(HA_KV):
            o16 = o16 + jnp.where(rg == g, o[:, HD_A * g:HD_A * (g + 1)], 0.0)
        oa_ref[i] = o16


def _sa_call(q3, kv, cache_k, cache_v, bias_s, sink_col, layer, prev):
    n = q3.shape[0]
    nb = ROWS_PER_SAMPLE_STEP
    kw = HA_KV * HD_A
    nslab, extra, alias, row = _carry_outputs(prev, 6, n // nb)
    cspec = pl.BlockSpec((1, nb, WINDOW, kw), lambda p, i: (layer, row(p, i), 0, 0))
    ospec = pl.BlockSpec((1, nb, WINDOW, kw), lambda p, i: ((layer + p) % DEPTH, i, 0, 0))
    return pl.pallas_call(
        _with_fill(_sa_kernel, (-3, -2)),
        grid=(nslab, n // nb),
        in_specs=[pl.BlockSpec((nb, HA_Q, HD_A), lambda p, i: (row(p, i), 0, 0)),
                  pl.BlockSpec((nb, 2 * kw), lambda p, i: (row(p, i), 0)),
                  cspec, cspec,
                  pl.BlockSpec((HA_Q, WINDOW), lambda p, i: (0, 0)),
                  pl.BlockSpec((HA_Q, 1), lambda p, i: (0, 0))] + extra,
        out_specs=[ospec, ospec, pl.BlockSpec((nb, HA_Q, HD_A), lambda p, i: (row(p, i), 0, 0))],
        out_shape=[jax.ShapeDtypeStruct((DEPTH, n, WINDOW, kw), F32),
                   jax.ShapeDtypeStruct((DEPTH, n, WINDOW, kw), F32),
                   jax.ShapeDtypeStruct((n, HA_Q, HD_A), F32)],
        input_output_aliases=alias,
        compiler_params=_cparams(2),
        name="sample_attn",
    )(q3, kv, cache_k, cache_v, bias_s, sink_col, *prev)


def _sr_kernel(zb_ref, cos_ref, sin_ref, gcol_ref, s_ref, *rest):
    so_ref, o_ref = rest[-2:]
    nb = ROWS_PER_SAMPLE_STEP
    wq = HB * DK_B
    cos = cos_ref[...]
    sin = sin_ref[...]
    qf = zb_ref[:, 0:wq]
    kf = zb_ref[:, wq:2 * wq]
    qr = qf * cos + _pairswap(qf) * sin
    kr = (kf * cos + _pairswap(kf) * sin) * (DK_B ** -0.5)
    v = zb_ref[:, 2 * wq:2 * wq + HB * DV_B]
    r8 = lax.broadcasted_iota(jnp.int32, (HB, wq), 0)
    hl = lax.broadcasted_iota(jnp.int32, (HB, wq), 1) // DK_B
    rv = lax.broadcasted_iota(jnp.int32, (HB, DV_B), 0)
    outers, q8s = [], []
    for i in range(nb):
        k8 = jnp.where(hl == r8, jnp.broadcast_to(kr[i:i + 1, :], (HB, wq)), 0.0).astype(BF16)
        q8s.append(jnp.where(hl == r8, jnp.broadcast_to(qr[i:i + 1, :], (HB, wq)), 0.0).astype(BF16))
        v8 = jnp.zeros((HB, DV_B), F32)
        for r in range(HB):
            v8 = jnp.where(rv == r, jnp.broadcast_to(v[i:i + 1, DV_B * r:DV_B * (r + 1)], (HB, DV_B)), v8)
        outers.append(_dot_tn(k8, v8.astype(BF16)))
    s_news = []
    for i in range(nb):
        s_new = gcol_ref[...] * s_ref[0, i] + outers[i]
        so_ref[0, i] = s_new
        s_news.append(s_new.astype(BF16))
    for i in range(nb):
        o_ref[i] = _dot(q8s[i], s_news[i])


def _sr_call(zb, state, layer, prev):
    n = zb.shape[0]
    nb = ROWS_PER_SAMPLE_STEP
    wq = HB * DK_B
    cos, sin = _rot_tables(np.array([PAST_LEN]))
    gcol = np.repeat(_gammas(), DK_B)[:, None] * np.ones((1, DV_B))
    nslab, extra, alias, row = _carry_outputs(prev, 5, n // nb)
    full = lambda p, i: (0, 0)
    return pl.pallas_call(
        _with_fill(_sr_kernel, (-2,)),
        grid=(nslab, n // nb),
        in_specs=[pl.BlockSpec((nb, SEG_B), lambda p, i: (row(p, i), 0)),
                  pl.BlockSpec((1, wq), full),
                  pl.BlockSpec((1, wq), full),
                  pl.BlockSpec((wq, DV_B), full),
                  pl.BlockSpec((1, nb, wq, DV_B), lambda p, i: (layer, row(p, i), 0, 0))] + extra,
        out_specs=[pl.BlockSpec((1, nb, wq, DV_B), lambda p, i: ((layer + p) % DEPTH, i, 0, 0)),
                   pl.BlockSpec((nb, HB, DV_B), lambda p, i: (row(p, i), 0, 0))],
        out_shape=[jax.ShapeDtypeStruct((DEPTH, n, wq, DV_B), F32),
                   jax.ShapeDtypeStruct((n, HB, DV_B), F32)],
        input_output_aliases=alias,
        compiler_params=_cparams(2),
        name="sample_retention",
    )(zb, jnp.asarray(cos), jnp.asarray(sin), jnp.asarray(gcol.astype(np.float32)), state, *prev)


def _ss_kernel(zc_ref, cs_ref, cw_ref, cb_ref, dtb_ref, a_ref, e3_ref, h_ref, *rest):
    ho_ref, cso_ref, y_ref, xc_ref = rest[-4:]
    nb = ROWS_PER_SAMPLE_STEP
    cx = zc_ref[:, 1024:2560]
    taps = [cs_ref[0, :, CONV_DIM * i:CONV_DIM * (i + 1)] for i in range(CONV_W - 1)] + [cx]
    acc = cb_ref[...]
    for i in range(CONV_W):
        acc = acc + taps[i] * cw_ref[i:i + 1, :]
    xbc = _silu(acc)
    cso_ref[...] = jnp.concatenate(taps[1:], axis=1)
    xc = xbc[:, 0:1024]
    bmat = xbc[:, 1024:1280]
    cmat = xbc[:, 1280:1536]
    dt = _softplus(zc_ref[:, C_DT:C_DT + 128] + dtb_ref[...])
    da = jnp.exp(dt * a_ref[...])
    dt_e = _dot(jnp.concatenate(_split3(dt), axis=1), e3_ref[...])
    da_e = _dot(jnp.concatenate(_split3(da), axis=1), e3_ref[...])
    dtx = dt_e * xc
    gw = (HC // G_C) * HD_C
    r8 = lax.broadcasted_iota(jnp.int32, (nb, gw), 0)
    rn = lax.broadcasted_iota(jnp.int32, (nb, N_C), 0)
    ones8 = jnp.ones((nb, N_C), BF16)
    prods = {}
    for g in range(G_C):
        ws = slice(gw * g, gw * (g + 1))
        bg16 = bmat[:, N_C * g:N_C * (g + 1)].astype(BF16)
        for i in range(nb):
            x8 = jnp.where(r8 == i, dtx[:, ws], 0.0).astype(BF16)
            outer = _dot_tn(x8, bg16)
            hi, mid, lo = (p.astype(F32) for p in _split3(jnp.broadcast_to(da_e[i:i + 1, ws], (nb, gw))))
            l3 = jnp.where(r8 == 0, hi, jnp.where(r8 == 1, mid, jnp.where(r8 == 2, lo, 0.0)))
            prods[g, i] = (outer, _dot_tn(l3.astype(BF16), ones8))
    h16 = {}
    for g in range(G_C):
        ws = slice(gw * g, gw * (g + 1))
        for i in range(nb):
            outer, dacol = prods[g, i]
            h_new = dacol * h_ref[0, i, ws, :] + outer
            ho_ref[0, i, ws, :] = h_new
            h16[g, i] = h_new.astype(BF16)
    ycols = []
    for g in range(G_C):
        ns = slice(N_C * g, N_C * (g + 1))
        yacc = jnp.zeros((nb, gw), F32)
        for i in range(nb):
            c8 = jnp.where(rn == i, cmat[:, ns], 0.0).astype(BF16)
            yacc = yacc + _dot_nt(c8, h16[g, i])
        ycols.append(yacc)
    y_ref[...] = jnp.concatenate(ycols, axis=1)
    xc_ref[...] = xc


def _ss_call(zc, conv_state, hstate, cw, cb, dtb_pad, a_pad, layer, prev):
    n = zc.shape[0]
    nb = ROWS_PER_SAMPLE_STEP
    e = np.zeros((128, D_MODEL), np.float32)
    for hh in range(HC):
        e[hh, HD_C * hh:HD_C * (hh + 1)] = 1.0
    e3 = jnp.asarray(np.concatenate([e, e, e], axis=0), dtype=BF16)
    cwid = (CONV_W - 1) * CONV_DIM
    full = lambda p, i: (0, 0)
    nslab, extra, alias, row = _carry_outputs(prev, 8, n // nb)
    return pl.pallas_call(
        _with_fill(_ss_kernel, (-4,)),
        grid=(nslab, n // nb),
        in_specs=[pl.BlockSpec((nb, SEG_C), lambda p, i: (row(p, i), 0)),
                  pl.BlockSpec((1, nb, cwid), lambda p, i: (layer, row(p, i), 0)),
                  pl.BlockSpec((CONV_W, CONV_DIM), full),
                  pl.BlockSpec((1, CONV_DIM), full),
                  pl.BlockSpec((1, 128), full),
                  pl.BlockSpec((1, 128), full),
                  pl.BlockSpec((3 * 128, D_MODEL), full),
                  pl.BlockSpec((1, nb, HC * HD_C, N_C), lambda p, i: (layer, row(p, i), 0, 0))] + extra,
        out_specs=[pl.BlockSpec((1, nb, HC * HD_C, N_C), lambda p, i: ((layer + p) % DEPTH, i, 0, 0)),
                   pl.BlockSpec((nb, cwid), lambda p, i: (row(p, i), 0)),
                   pl.BlockSpec((nb, D_MODEL), lambda p, i: (row(p, i), 0)),
                   pl.BlockSpec((nb, D_MODEL), lambda p, i: (row(p, i), 0))],
        out_shape=[jax.ShapeDtypeStruct((DEPTH, n, HC * HD_C, N_C), F32),
                   jax.ShapeDtypeStruct((n, cwid), F32),
                   jax.ShapeDtypeStruct((n, D_MODEL), F32),
                   jax.ShapeDtypeStruct((n, D_MODEL), F32)],
        input_output_aliases=alias,
        compiler_params=_cparams(2),
        name="sample_ssd",
    )(zc, conv_state, cw, cb, dtb_pad, a_pad, e3, hstate, *prev)


def _sm_kernel(x_ref, g1_ref, oa_ref, oret_ref, y_ref, xc_ref, za_ref, zb_ref, zc_ref,
               dsk_ref, snw_ref, wo_ref, o_ref):
    ob = jnp.concatenate([_rms_lanes(oret_ref[:, DV_B * hh:DV_B * (hh + 1)]) for hh in range(HB)], axis=1)
    ob = _silu(zb_ref[:, 2048:3072]) * ob
    yc = (y_ref[...] + dsk_ref[...] * xc_ref[...]) * _silu(zc_ref[:, 0:1024])
    gw = D_MODEL // G_C
    oc = jnp.concatenate([_rms_lanes(yc[:, gw * g:gw * (g + 1)]) for g in range(G_C)], axis=1) * snw_ref[...]
    mix = (_sigmoid(za_ref[:, 1536:2560]) * oa_ref[...] + _sigmoid(zb_ref[:, 3072:4096]) * ob
           + _sigmoid(zc_ref[:, C_GC:C_GC + 1024]) * oc)
    o_ref[...] = x_ref[...] + g1_ref[...] * _dot(mix.astype(BF16), wo_ref[...])


def _sm_call(xs, mod_s, oa, oret, y, xc, za, zb, zc, dsk_full, snw, wo):
    n = xs.shape[0]
    full = lambda i: (0, 0)
    row = lambda w: pl.BlockSpec((n, w), full)
    return pl.pallas_call(
        _sm_kernel,
        grid=(1,),
        in_specs=[row(D_MODEL),
                  pl.BlockSpec((n, D_MODEL), lambda i: (0, 2)),
                  row(D_MODEL), row(D_MODEL), row(D_MODEL), row(D_MODEL),
                  row(SEG_A), row(SEG_B), row(SEG_C),
                  pl.BlockSpec((1, D_MODEL), full),
                  pl.BlockSpec((1, D_MODEL), full),
                  pl.BlockSpec((D_MODEL, D_MODEL), full)],
        out_specs=row(D_MODEL),
        out_shape=jax.ShapeDtypeStruct((n, D_MODEL), F32),
        compiler_params=_cparams(1),
        name="sample_merge_out",
    )(xs, mod_s, oa, oret, y, xc, za, zb, zc, dsk_full, snw, wo)


PROMPT_TILE = 512
ATTN_TILE = 1024
MLP_ROWS = 1024
MLP_FF = 1024


def _prep_w_in(w):
    wa = jnp.concatenate([w[:, O_AQ:O_BQ], w[:, O_GTS:O_GTS + 1024]], axis=1)
    wb = jnp.concatenate([w[:, O_BQ:O_CZ], w[:, O_GTS + 1024:O_GTS + 2048]], axis=1)
    wc = jnp.concatenate([w[:, O_CZ:O_CDT], jnp.pad(w[:, O_CDT:O_GTS], ((0, 0), (0, 128 - HC))),
                          w[:, O_GTS + 2048:O_GTS + 3072]], axis=1)
    return wa, wb, wc


def _forward(x_prompt, x_sample, cache_win_k, cache_win_v, state_ret, state_ssm, state_conv,
             c_prompt, c_sample, rel_bias_table, attn_sinks, norm1_w, norm2_w, ada_w, ada_b,
             w_in, conv_w, conv_b, dt_bias, A_log, D_skip, ssm_norm_w, w_out, w_up, w_down,
             final_norm_w, *, prompt_tile, attn_tile, mlp_rows, mlp_ff):
    B, T, _ = x_prompt.shape
    DB = x_sample.shape[0]
    kw = HA_KV * HD_A

    mod_all = _ada_call(jnp.concatenate([c_prompt, c_sample], axis=0), ada_w, ada_b)

    bias_t, bias_s = _bias_tables(rel_bias_table)

    w16 = w_in.astype(BF16)
    wu16 = w_up.astype(BF16)
    wd16 = w_down.astype(BF16)
    wo16 = w_out.astype(BF16)
    fn = final_norm_w.reshape(1, D_MODEL)
    ck = cache_win_k.reshape(DEPTH, DB, WINDOW, kw)
    cv = cache_win_v.reshape(DEPTH, DB, WINDOW, kw)
    sret = state_ret.reshape(DEPTH, DB, HB * DK_B, DV_B)
    sssm = state_ssm.reshape(DEPTH, DB, HC * HD_C, N_C)
    sconv = state_conv.reshape(DEPTH, DB, (CONV_W - 1) * CONV_DIM)

    xp = x_prompt
    xs = x_sample.reshape(DB, D_MODEL)
    outs_p = [[] for _ in range(5)]
    conv_s = []
    win_kv, ret_all, ssm_all = (), (), ()
    for l in range(DEPTH):
        wl = w16[l]
        wa, wb, wc = _prep_w_in(wl)
        n1 = norm1_w[l].reshape(1, D_MODEL)
        n2 = norm2_w[l].reshape(1, D_MODEL)
        cw = conv_w[l]
        cb = conv_b[l].reshape(1, CONV_DIM)
        dtb_pad = jnp.pad(dt_bias[l], (0, 128 - HC)).reshape(1, 128)
        a_pad = jnp.pad(-jnp.exp(A_log[l].astype(F32)), (0, 128 - HC)).reshape(1, 128)
        dsk_full = jnp.repeat(D_skip[l], HD_C).reshape(1, D_MODEL)
        snw = ssm_norm_w[l].reshape(1, D_MODEL)
        final = l == DEPTH - 1
        mod_p = mod_all[l, :B].reshape(B, 1, 6 * D_MODEL)
        mod_s = mod_all[l, B:]

        wqgt = jnp.concatenate([wl[:, O_AQ:O_AK] * (HD_A ** -0.5), wl[:, O_GTS:O_GTS + D_MODEL]], axis=1).T
        wkv = wl[:, O_AK:O_BQ]
        sink_rows = jnp.repeat(attn_sinks[l], WINDOW).reshape(HA_KV, 1, (HA_Q // HA_KV) * WINDOW)
        mixa, kbuf, vbuf = _pa_call(xp, mod_p, n1, wqgt, wkv, bias_t, sink_rows, attn_tile)
        wbt = jnp.concatenate([_deinterleave_pairs(wl[:, O_BQ:O_BK]),
                               _deinterleave_pairs(wl[:, O_BK:O_BV]) * (DK_B ** -0.5), wl[:, O_BV:O_BG],
                               wl[:, O_BG:O_CZ] * 0.5, wl[:, O_GTS + D_MODEL:O_GTS + 2 * D_MODEL] * 0.5], axis=1).T
        mixab, s_perm = _pb_call(xp, mod_p, n1, wbt, mixa, attn_tile)
        s_ret = s_perm.reshape(B, HB, 2, DK_B // 2, DV_B).transpose(0, 1, 3, 2, 4)
        x1, h_ssm, conv_new = _pc_call(xp, mod_p, n1, wc, cw, cb, dtb_pad, a_pad, dsk_full, snw,
                                       mixab, wo16[l], prompt_tile)
        xp = _mlp_call(x1.reshape(B * T, D_MODEL), mod_p, n2, wu16, wd16, fn, l,
                       mlp_rows, mlp_ff, T, final).reshape(B, T, D_MODEL)
        for lst, v in zip(outs_p, (kbuf.reshape(B, WINDOW, HA_KV, HD_A), vbuf.reshape(B, WINDOW, HA_KV, HD_A),
                                   s_ret.reshape(B, HB, DK_B, DV_B), h_ssm.reshape(B, HC, HD_C, N_C), conv_new)):
            lst.append(v)

        za = _sproj_call(xs, mod_s, n1, wa)
        zb = _sproj_call(xs, mod_s, n1, wb)
        zc = _sproj_call(xs, mod_s, n1, wc)
        q3 = za[:, 0:HA_Q * HD_A].reshape(DB, HA_Q, HD_A)
        kv = za[:, HA_Q * HD_A:HA_Q * HD_A + 2 * kw]
        ck_new, cv_new, oa3 = _sa_call(q3, kv, ck, cv, bias_s, attn_sinks[l].reshape(HA_Q, 1), l, win_kv)
        win_kv = (ck_new, cv_new)
        s_new, o3 = _sr_call(zb, sret, l, ret_all)
        ret_all = (s_new,)
        h_new, cs_new, y_s, xc_s = _ss_call(zc, sconv, sssm, cw, cb, dtb_pad, a_pad, l, ssm_all)
        ssm_all = (h_new,)
        xs1 = _sm_call(xs, mod_s, oa3.reshape(DB, D_MODEL), o3.reshape(DB, D_MODEL), y_s, xc_s,
                       za, zb, zc, dsk_full, snw, wo16[l])
        xs = _mlp_call(xs1, mod_s, n2, wu16, wd16, fn, l, DB, mlp_ff, 1, final)
        conv_s.append(cs_new.reshape(DB, CONV_W - 1, CONV_DIM))

    stk = lambda lst: jnp.stack(lst, axis=0)
    return (xp, xs.reshape(DB, 1, D_MODEL),
            stk(outs_p[0]), stk(outs_p[1]), stk(outs_p[2]), stk(outs_p[3]), stk(outs_p[4]),
            win_kv[0].reshape(DEPTH, DB, WINDOW, HA_KV, HD_A), win_kv[1].reshape(DEPTH, DB, WINDOW, HA_KV, HD_A),
            ret_all[0].reshape(DEPTH, DB, HB, DK_B, DV_B), ssm_all[0].reshape(DEPTH, DB, HC, HD_C, N_C),
            stk(conv_s))


def kernel(x_prompt, x_sample, cache_win_k, cache_win_v, state_ret, state_ssm, state_conv, c_prompt, c_sample,
           rel_bias_table, attn_sinks, norm1_w, norm2_w, ada_w, ada_b, w_in, conv_w, conv_b, dt_bias, A_log,
           D_skip, ssm_norm_w, w_out, w_up, w_down, final_norm_w):
    return _forward(x_prompt, x_sample, cache_win_k, cache_win_v, state_ret, state_ssm, state_conv,
                    c_prompt, c_sample, rel_bias_table, attn_sinks, norm1_w, norm2_w, ada_w, ada_b,
                    w_in, conv_w, conv_b, dt_bias, A_log, D_skip, ssm_norm_w, w_out, w_up, w_down,
                    final_norm_w, prompt_tile=PROMPT_TILE, attn_tile=ATTN_TILE, mlp_rows=MLP_ROWS, mlp_ff=MLP_FF)
```

```python
import functools
import math

import numpy as np
import jax
import jax.numpy as jnp
from jax import lax
from jax.experimental import pallas as pl
from jax.experimental.pallas import tpu as pltpu

F32 = jnp.float32
BF16 = jnp.bfloat16

D_MODEL = 1024
DEPTH = 2
PAST_LEN = 16384
WINDOW = 128
HA_Q = 16
HA_KV = 4
HD_A = 64
NUM_BUCKETS = 32
MAX_DISTANCE = WINDOW
HB = 8
DK_B = 64
DV_B = 128
HC = 16
HD_C = 64
N_C = 128
G_C = 2
CONV_W = 4
CONV_DIM = D_MODEL + 2 * G_C * N_C
D_FF = 4 * D_MODEL
EPS = 1e-6
NEG = -1e30
LOG2E = 1.4426950408889634

O_AQ, O_AK, O_AV = 0, 1024, 1280
O_BQ, O_BK, O_BV, O_BG = 1536, 2048, 2560, 3584
O_CZ, O_CXBC, O_CDT, O_GTS = 4608, 5632, 7168, 7184
SEG_A = 2560
SEG_B = 4096
SEG_C = 3712
C_DT = 2560
C_GC = 2688

VMEM_LIMIT_V7X = 56 * 1024 * 1024
ROWS_PER_SAMPLE_STEP = 8


def _cparams(n_axes):
    return pltpu.CompilerParams(dimension_semantics=("arbitrary",) * n_axes,
                                vmem_limit_bytes=VMEM_LIMIT_V7X)


def _dot(a, b):
    return jnp.dot(a, b, preferred_element_type=F32)


def _dot_nt(a, b):
    return lax.dot_general(a, b, (((1,), (1,)), ((), ())), preferred_element_type=F32)


def _dot_tn(a, b):
    return lax.dot_general(a, b, (((0,), (0,)), ((), ())), preferred_element_type=F32)


def _sigmoid(x):
    return 0.5 * (jnp.tanh(0.5 * x) + 1.0)


def _silu(x):
    return x * _sigmoid(x)


def _tanh1(half):
    return jnp.tanh(half) + 1.0


def _softplus(x):
    return jnp.maximum(x, 0.0) + jnp.log1p(jnp.exp(-jnp.abs(x)))


def _modnorm(x, nw, sc, sh):
    ms = jnp.mean(x * x, axis=-1, keepdims=True)
    return (x * lax.rsqrt(ms + EPS) * nw) * (1.0 + sc) + sh


def _rms_lanes(x):
    ms = jnp.mean(x * x, axis=-1, keepdims=True)
    return x * lax.rsqrt(ms + EPS)


def _pairswap(x):
    ax = x.ndim - 1
    n = x.shape[ax]
    lane = lax.broadcasted_iota(jnp.int32, x.shape, ax)
    nxt = pltpu.roll(x, n - 1, ax)
    prv = pltpu.roll(x, 1, ax)
    return jnp.where((lane & 1) == 0, nxt, prv)


def _split3(x):
    hi = x.astype(BF16)
    r1 = x - hi.astype(F32)
    mid = r1.astype(BF16)
    lo = (r1 - mid.astype(F32)).astype(BF16)
    return hi, mid, lo


def _gammas():
    return 1.0 - 2.0 ** (-5.0 - np.arange(HB, dtype=np.float64))


def _rot_tables(pos):
    theta = 1.0 / (10000.0 ** np.linspace(0.0, 1.0, DK_B // 2))
    ang = np.asarray(pos, np.float64)[:, None] * theta[None, :]
    cos = np.repeat(np.cos(ang), 2, axis=1)
    sin = np.repeat(np.sin(ang), 2, axis=1)
    sin[:, 0::2] *= -1.0
    return (np.tile(cos, (1, HB)).astype(np.float32), np.tile(sin, (1, HB)).astype(np.float32))


def _ret_tables():
    g = _gammas()
    L = WINDOW
    i = np.arange(L, dtype=np.float64)
    diff = i[:, None] - i[None, :]
    dm = np.where(diff >= 0, g[:, None, None] ** np.maximum(diff, 0.0), 0.0)
    qdec = np.repeat(g[None, :] ** (i[:, None] + 1.0), DK_B, axis=1)
    kdec = np.repeat(g[None, :] ** (L - 1.0 - i[:, None]), DK_B, axis=1)
    return dm.astype(np.float32), qdec.astype(np.float32), kdec.astype(np.float32)


def _t5_bucket_np(dist):
    max_exact = NUM_BUCKETS // 2
    n = np.maximum(dist, 0)
    nf = np.maximum(n, 1).astype(np.float32)
    large = max_exact + (np.log(nf / np.float32(max_exact)) / np.float32(math.log(MAX_DISTANCE / max_exact))
                         * np.float32(NUM_BUCKETS - max_exact)).astype(np.int32)
    large = np.minimum(large, NUM_BUCKETS - 1)
    return np.where(n < max_exact, n, large)


def _bias_tables(rel_table):
    gsz = HA_Q // HA_KV
    qi = np.arange(WINDOW)[None, :]
    kj = np.arange(WINDOW)[:, None]
    dist = np.where(kj > qi, qi + WINDOW - kj, qi - kj)
    onehot = _t5_bucket_np(dist)[..., None] == np.arange(NUM_BUCKETS)
    tab = rel_table.astype(F32)
    bias_kq = jnp.einsum('kqb,bh->hkq', jnp.asarray(onehot, F32), tab, precision=lax.Precision.HIGHEST)
    bias_t = bias_kq.reshape(HA_KV, gsz, WINDOW, WINDOW).transpose(0, 2, 1, 3).reshape(HA_KV, WINDOW, gsz * WINDOW)
    oh_s = _t5_bucket_np(WINDOW - 1 - np.arange(WINDOW))[:, None] == np.arange(NUM_BUCKETS)
    bias_s = jnp.einsum('jb,bh->hj', jnp.asarray(oh_s, F32), tab, precision=lax.Precision.HIGHEST)
    return bias_t, bias_s


def _ada_kernel(c_ref, w_ref, b_ref, o_ref):
    s = _silu(c_ref[...])
    o_ref[0] = _dot(s.astype(BF16), w_ref[0].astype(BF16)) + b_ref[0]


def _ada_call(c_all, ada_w, ada_b):
    n = c_all.shape[0]
    nb = 6
    return pl.pallas_call(
        _ada_kernel,
        grid=(DEPTH, nb),
        in_specs=[pl.BlockSpec((n, D_MODEL), lambda l, j: (0, 0)),
                  pl.BlockSpec((1, D_MODEL, D_MODEL), lambda l, j: (l, 0, j)),
                  pl.BlockSpec((1, 1, D_MODEL), lambda l, j: (l, 0, j))],
        out_specs=pl.BlockSpec((1, n, D_MODEL), lambda l, j: (l, 0, j)),
        out_shape=jax.ShapeDtypeStruct((DEPTH, n, 6 * D_MODEL), F32),
        compiler_params=_cparams(2),
        name="ada_mod",
    )(c_all, ada_w, ada_b.reshape(DEPTH, 1, 6 * D_MODEL))


def _pa_kernel(x_ref, mod_ref, n1_ref, wt_ref, wkv_ref, bias_ref, sink_ref, lowm_ref, kall_ref, vall_ref,
               mix_ref, ko_ref, vo_ref, zt_ref, kv_ref, kprev_ref, vtprev_ref, pen_ref, *, tt):
    t = pl.program_id(1)
    nchunk = tt // WINDOW
    kw = HA_KV * HD_A
    gsz = HA_Q // HA_KV

    @pl.when(t == 0)
    def _():
        kprev_ref[...] = jnp.zeros_like(kprev_ref)
        vtprev_ref[...] = jnp.zeros_like(vtprev_ref)
        pen_ref[...] = jnp.full(pen_ref.shape, NEG, F32)

    mod = mod_ref[0]
    h = _modnorm(x_ref[0], n1_ref[...], mod[:, D_MODEL:2 * D_MODEL], mod[:, 0:D_MODEL]).astype(BF16)
    kv_ref[...] = _dot(h, wkv_ref[...])
    for cc in range(nchunk // 2):
        z2 = _dot_nt(wt_ref[...], h[2 * WINDOW * cc:2 * WINDOW * (cc + 1), :])
        zt_ref[2 * cc] = z2[:, 0:WINDOW]
        zt_ref[2 * cc + 1] = z2[:, WINDOW:2 * WINDOW]
    qw = gsz * WINDOW
    lower = (lax.broadcasted_iota(jnp.int32, (WINDOW, qw), 0)
             > (lax.broadcasted_iota(jnp.int32, (WINDOW, qw), 1) & (WINDOW - 1)))

    def chunk(c, carry):
        r0 = pl.multiple_of(c * WINDOW, WINDOW)
        rows = pl.ds(r0, WINDOW)
        kc = kv_ref[rows, 0:kw]
        vc = kv_ref[rows, kw:2 * kw]
        vt = vc.T
        kk = jnp.concatenate([kprev_ref[...], kc], axis=0).astype(BF16)
        vvt = jnp.concatenate([vtprev_ref[...], vt], axis=1).astype(BF16)
        qt = zt_ref[c, 0:D_MODEL, :].astype(BF16)
        pen = pen_ref[0:1, :]
        s_all = []
        for g in range(HA_KV):
            gs = slice(HD_A * g, HD_A * (g + 1))
            qcat = jnp.concatenate([qt[HD_A * (gsz * g + j):HD_A * (gsz * g + j + 1), :] for j in range(gsz)], axis=1)
            s_all.append(_dot(kk[:, gs], qcat))
        p_all = []
        for g in range(HA_KV):
            sg = jnp.where(lower, s_all[g][0:WINDOW, :] + pen, s_all[g][WINDOW:2 * WINDOW, :]) + bias_ref[g]
            sink = sink_ref[g]
            m = jnp.maximum(jnp.max(sg, axis=0, keepdims=True), sink)
            pw = jnp.exp(sg - m)
            den = jnp.sum(pw, axis=0, keepdims=True) + jnp.exp(sink - m)
            pb = pw.astype(BF16)
            p_prev = pb * lowm_ref[...]
            p_all.append((jnp.concatenate([p_prev, pb - p_prev], axis=0), 1.0 / den))
        pieces = []
        for g in range(HA_KV):
            gs = slice(HD_A * g, HD_A * (g + 1))
            p, rden = p_all[g]
            ot = _dot(vvt[gs, :], p) * rden
            pieces += [ot[:, WINDOW * j:WINDOW * (j + 1)] for j in range(gsz)]
        oat = jnp.concatenate(pieces, axis=0)
        mixt = _sigmoid(zt_ref[c, D_MODEL:2 * D_MODEL, :]) * oat
        mix_ref[0, rows, :] = mixt.T
        kprev_ref[...] = kc
        vtprev_ref[...] = vt
        pen_ref[...] = jnp.zeros_like(pen_ref)
        ko_ref[0] = kc
        vo_ref[0] = vc
        return carry

    lax.fori_loop(0, nchunk, chunk, 0, unroll=True)


def _pa_call(x, mod3, n1, wqgt, wkv, bias_t, sink_rows, tt, layer, win_all):
    B, T, _ = x.shape
    kw = HA_KV * HD_A
    qw = (HA_Q // HA_KV) * WINDOW
    kern = functools.partial(_pa_kernel, tt=tt)
    lowm = (np.arange(WINDOW)[:, None] > (np.arange(qw)[None, :] % WINDOW)).astype(np.float32)
    return pl.pallas_call(
        kern,
        grid=(B, T // tt),
        in_specs=[pl.BlockSpec((1, tt, D_MODEL), lambda b, t: (b, t, 0)),
                  pl.BlockSpec((1, 1, 6 * D_MODEL), lambda b, t: (b, 0, 0)),
                  pl.BlockSpec((1, D_MODEL), lambda b, t: (0, 0)),
                  pl.BlockSpec((2 * D_MODEL, D_MODEL), lambda b, t: (0, 0)),
                  pl.BlockSpec((D_MODEL, 2 * kw), lambda b, t: (0, 0)),
                  pl.BlockSpec((HA_KV, WINDOW, qw), lambda b, t: (0, 0, 0)),
                  pl.BlockSpec((HA_KV, 1, qw), lambda b, t: (0, 0, 0)),
                  pl.BlockSpec((WINDOW, qw), lambda b, t: (0, 0)),
                  pl.BlockSpec(memory_space=pl.ANY), pl.BlockSpec(memory_space=pl.ANY)],
        out_specs=[pl.BlockSpec((1, tt, D_MODEL), lambda b, t: (b, t, 0)),
                   pl.BlockSpec((None, 1, WINDOW, kw), lambda b, t: (layer, b, 0, 0)),
                   pl.BlockSpec((None, 1, WINDOW, kw), lambda b, t: (layer, b, 0, 0))],
        out_shape=[jax.ShapeDtypeStruct((B, T, D_MODEL), F32),
                   jax.ShapeDtypeStruct((DEPTH, B, WINDOW, kw), F32),
                   jax.ShapeDtypeStruct((DEPTH, B, WINDOW, kw), F32)],
        input_output_aliases={8: 1, 9: 2},
        scratch_shapes=[pltpu.VMEM((tt // WINDOW, 2 * D_MODEL, WINDOW), F32),
                        pltpu.VMEM((tt, 2 * kw), F32),
                        pltpu.VMEM((WINDOW, kw), F32),
                        pltpu.VMEM((kw, WINDOW), F32),
                        pltpu.VMEM((8, qw), F32)],
        compiler_params=_cparams(2),
        name="prompt_attn",
    )(x, mod3, n1, wqgt, wkv, bias_t, sink_rows, jnp.asarray(lowm, BF16), *win_all)


def _pb_kernel(x_ref, mod_ref, n1_ref, wt_ref, cos_ref, sin_ref, qdec_ref, kdec_ref, dm_ref, mixa_ref, sall_ref,
               mix_ref, so_ref, zt_ref, s_ref, *, tt, glast):
    t = pl.program_id(1)
    nchunk = tt // WINDOW
    hw = HB * DK_B // 2
    hp = DK_B // 2

    @pl.when(t == 0)
    def _():
        s_ref[...] = jnp.zeros_like(s_ref)

    mod = mod_ref[0]
    h = _modnorm(x_ref[0], n1_ref[...], mod[:, D_MODEL:2 * D_MODEL], mod[:, 0:D_MODEL]).astype(BF16)
    for cc in range(nchunk // 2):
        z2 = _dot_nt(wt_ref[...], h[2 * WINDOW * cc:2 * WINDOW * (cc + 1), :])
        zt_ref[2 * cc] = z2[:, 0:WINDOW]
        zt_ref[2 * cc + 1] = z2[:, WINDOW:2 * WINDOW]

    def head_rows(pair, hh):
        return jnp.concatenate([pair[0][hp * hh:hp * (hh + 1), :], pair[1][hp * hh:hp * (hh + 1), :]], axis=0)

    def chunk(c, carry):
        r0 = pl.multiple_of(c * WINDOW, WINDOW)
        rows = pl.ds(r0, WINDOW)
        cos = cos_ref[c]
        sin = sin_ref[c]
        q1 = zt_ref[c, 0:hw, :]
        q2 = zt_ref[c, hw:2 * hw, :]
        k1 = zt_ref[c, 2 * hw:3 * hw, :]
        k2 = zt_ref[c, 3 * hw:4 * hw, :]
        rq = (q1 * cos - q2 * sin, q1 * sin + q2 * cos)
        rk = (k1 * cos - k2 * sin, k1 * sin + k2 * cos)
        qdec = qdec_ref[...]
        kdec = kdec_ref[...]
        qb = tuple(a.astype(BF16) for a in rq)
        kb = tuple(a.astype(BF16) for a in rk)
        qd = tuple((a * qdec).astype(BF16) for a in rq)
        kd = tuple((a * kdec).astype(BF16) for a in rk)
        vt = zt_ref[c, 4 * hw:4 * hw + HB * DV_B, :].astype(BF16)
        inner, cross, supd = [], [], []
        for hh in range(HB):
            s_old = s_ref[DK_B * hh:DK_B * (hh + 1), :]
            inner.append(_dot_tn(head_rows(qb, hh), head_rows(kb, hh)))
            cross.append(_dot_tn(s_old.astype(BF16), head_rows(qd, hh)))
            supd.append(glast[hh] * s_old + _dot_nt(head_rows(kd, hh), vt[DV_B * hh:DV_B * (hh + 1), :]))
        innd = [(inner[hh] * dm_ref[hh]).astype(BF16) for hh in range(HB)]
        outs = []
        for hh in range(HB):
            ot = _dot_nt(vt[DV_B * hh:DV_B * (hh + 1), :], innd[hh]) + cross[hh]
            s_ref[DK_B * hh:DK_B * (hh + 1), :] = supd[hh]
            ms = jnp.mean(ot * ot, axis=0, keepdims=True)
            outs.append(ot * (0.5 * lax.rsqrt(ms + EPS)))
        obt = jnp.concatenate(outs, axis=0)
        bgt = zt_ref[c, 4 * hw + D_MODEL:4 * hw + 2 * D_MODEL, :]
        gbt = zt_ref[c, 4 * hw + 2 * D_MODEL:4 * hw + 3 * D_MODEL, :]
        mixt = _tanh1(gbt) * (bgt * _tanh1(bgt) * obt)
        mix_ref[0, rows, :] = mixa_ref[0, rows, :] + mixt.T
        return carry

    lax.fori_loop(0, nchunk, chunk, 0, unroll=True)
    so_ref[0] = s_ref[...]


def _deinterleave_pairs(w):
    return w.reshape(w.shape[0], HB, DK_B // 2, 2).transpose(0, 3, 1, 2).reshape(w.shape[0], HB * DK_B)


def _rot_tables_t(T):
    theta = 1.0 / (10000.0 ** np.linspace(0.0, 1.0, DK_B // 2))
    ang = theta[:, None] * np.arange(T, dtype=np.float64)[None, :]

    def lay(a):
        a = np.tile(a, (HB, 1))
        return np.ascontiguousarray(a.reshape(a.shape[0], T // WINDOW, WINDOW).transpose(1, 0, 2)).astype(np.float32)

    return lay(np.cos(ang)), lay(np.sin(ang))


def _pb_call(x, mod3, n1, wbt, mixa, tt, layer, s_all):
    B, T, _ = x.shape
    hw = HB * DK_B // 2
    cos, sin = _rot_tables_t(T)
    dm, qdec, kdec = _ret_tables()
    qdec_t = np.ascontiguousarray(qdec[:, ::2].T)
    kdec_t = np.ascontiguousarray(kdec[:, ::2].T)
    glast = tuple(float(v) for v in (_gammas() ** WINDOW))
    kern = functools.partial(_pb_kernel, tt=tt, glast=glast)
    full2 = lambda b, t: (0, 0)
    nct = tt // WINDOW
    return pl.pallas_call(
        kern,
        grid=(B, T // tt),
        in_specs=[pl.BlockSpec((1, tt, D_MODEL), lambda b, t: (b, t, 0)),
                  pl.BlockSpec((1, 1, 6 * D_MODEL), lambda b, t: (b, 0, 0)),
                  pl.BlockSpec((1, D_MODEL), full2),
                  pl.BlockSpec((SEG_B, D_MODEL), full2, pipeline_mode=pl.Buffered(1)),
                  pl.BlockSpec((nct, hw, WINDOW), lambda b, t: (t, 0, 0)),
                  pl.BlockSpec((nct, hw, WINDOW), lambda b, t: (t, 0, 0)),
                  pl.BlockSpec((hw, WINDOW), full2),
                  pl.BlockSpec((hw, WINDOW), full2),
                  pl.BlockSpec((HB, WINDOW, WINDOW), lambda b, t: (0, 0, 0)),
                  pl.BlockSpec((1, tt, D_MODEL), lambda b, t: (b, t, 0)),
                  pl.BlockSpec(memory_space=pl.ANY)],
        out_specs=[pl.BlockSpec((1, tt, D_MODEL), lambda b, t: (b, t, 0)),
                   pl.BlockSpec((None, 1, HB * DK_B, DV_B), lambda b, t: (layer, b, 0, 0))],
        out_shape=[jax.ShapeDtypeStruct((B, T, D_MODEL), F32),
                   jax.ShapeDtypeStruct((DEPTH, B, HB * DK_B, DV_B), F32)],
        input_output_aliases={10: 1},
        scratch_shapes=[pltpu.VMEM((nct, SEG_B, WINDOW), F32),
                        pltpu.VMEM((HB * DK_B, DV_B), F32)],
        compiler_params=_cparams(2),
        name="prompt_retention",
    )(x, mod3, n1, wbt, jnp.asarray(cos), jnp.asarray(sin), jnp.asarray(qdec_t), jnp.asarray(kdec_t),
      jnp.asarray(dm), mixa, s_all)


def _pc_kernel(x_ref, mod_ref, n1_ref, w_ref, cw_ref, cb_ref, dtb_ref, a_ref, dsk_ref, snw_ref, tri_ref,
               mixab_ref, wo_ref, hall_ref, call_ref,
               xo_ref, ho_ref, co_ref,
               z_ref, xbuf_ref, xbc_ref, dt_ref, hst_ref, mixs_ref, *, tt):
    t = pl.program_id(1)

    @pl.when(t == 0)
    def _():
        xbuf_ref[0:8, :] = jnp.zeros((8, CONV_DIM), F32)
        hst_ref[...] = jnp.zeros_like(hst_ref)

    x = x_ref[0]
    mod = mod_ref[0]
    h = _modnorm(x, n1_ref[...], mod[:, D_MODEL:2 * D_MODEL], mod[:, 0:D_MODEL])
    z_ref[...] = _dot(h.astype(BF16), w_ref[...])

    xbuf_ref[8:8 + tt, :] = z_ref[:, 1024:2560]
    acc = cb_ref[...]
    for i in range(CONV_W):
        acc = acc + xbuf_ref[5 + i:5 + i + tt, :] * cw_ref[i:i + 1, :]
    xbc_ref[...] = _silu(acc)
    co_ref[0] = xbuf_ref[tt + 5:tt + 8, :]
    xbuf_ref[0:8, :] = xbuf_ref[tt:tt + 8, :]
    dt_ref[...] = _softplus(z_ref[:, C_DT:C_DT + 128] + dtb_ref[...])

    ii = lax.broadcasted_iota(jnp.int32, (WINDOW, WINDOW), 0)
    jj = lax.broadcasted_iota(jnp.int32, (WINDOW, WINDOW), 1)
    causal = ii >= jj
    hpg = HC // G_C

    def chunk(c):
        rows = slice(WINDOW * c, WINDOW * (c + 1))
        xc = xbc_ref[rows, 0:1024]
        bmat = xbc_ref[rows, 1024:1280]
        cmat = xbc_ref[rows, 1280:1536]
        dtc = dt_ref[rows, :]
        acum = jnp.dot(tri_ref[...], dtc * a_ref[...], precision=lax.Precision.HIGHEST,
                       preferred_element_type=F32)
        acum = acum * LOG2E
        acum_t = acum.T
        rowp_t = acum_t - jnp.log2(dtc.T)
        x_t = xc.T
        xb = xc.astype(BF16)
        bb = bmat.astype(BF16)
        cb16 = cmat.astype(BF16)
        ys = []
        for g in range(G_C):
            ns = slice(N_C * g, N_C * (g + 1))
            cbg = _dot_nt(cb16[:, ns], bb[:, ns])
            for hh in range(hpg * g, hpg * (g + 1)):
                ps = slice(HD_C * hh, HD_C * (hh + 1))
                colb = jnp.broadcast_to(acum[:, hh:hh + 1], (WINDOW, WINDOW))
                rowp = rowp_t[hh:hh + 1, :]
                m = cbg * jnp.exp2(jnp.where(causal, colb - rowp, NEG))
                hs = hst_ref[ps, :]
                ecolb = jnp.exp2(colb)
                cs = cmat[:, ns] * ecolb
                y = _dot(m.astype(BF16), xb[:, ps]) + _dot_nt(cs.astype(BF16), hs.astype(BF16))
                wrow = jnp.exp2(colb[WINDOW - 1:WINDOW, :] - rowp)
                xw = (x_t[ps, :] * wrow).astype(BF16)
                hst_ref[ps, :] = ecolb[WINDOW - 1:WINDOW, :] * hs + _dot(xw, bb[:, ns])
                ys.append(y)
        y = jnp.concatenate(ys, axis=1) + dsk_ref[...] * xc
        yc = y * _silu(z_ref[rows, 0:1024])
        gw = D_MODEL // G_C
        oc = jnp.concatenate([_rms_lanes(yc[:, gw * g:gw * (g + 1)]) for g in range(G_C)], axis=1) * snw_ref[...]
        gc = _sigmoid(z_ref[rows, C_GC:C_GC + 1024])
        mixs_ref[rows, :] = (mixab_ref[0, rows, :] + gc * oc).astype(BF16)

    g1 = mod[:, 2 * D_MODEL:3 * D_MODEL]
    for c in range(tt // WINDOW):
        chunk(c)
        if c % 2 == 1:
            pr = slice(WINDOW * (c - 1), WINDOW * (c + 1))
            xo_ref[0, pr, :] = x_ref[0, pr, :] + g1 * _dot(mixs_ref[pr, :], wo_ref[...])
    ho_ref[0] = hst_ref[...]


def _pc_call(x, mod3, n1, wc, cw, cb, dtb_pad, a_pad, dsk_full, snw, mixab, wo, tt, layer, st_all):
    B, T, _ = x.shape
    tri = jnp.asarray(np.tril(np.ones((WINDOW, WINDOW), np.float32)))
    kern = functools.partial(_pc_kernel, tt=tt)
    full2 = lambda b, t: (0, 0)
    return pl.pallas_call(
        kern,
        grid=(B, T // tt),
        in_specs=[pl.BlockSpec((1, tt, D_MODEL), lambda b, t: (b, t, 0)),
                  pl.BlockSpec((1, 1, 6 * D_MODEL), lambda b, t: (b, 0, 0)),
                  pl.BlockSpec((1, D_MODEL), full2),
                  pl.BlockSpec((D_MODEL, SEG_C), full2),
                  pl.BlockSpec((CONV_W, CONV_DIM), full2),
                  pl.BlockSpec((1, CONV_DIM), full2),
                  pl.BlockSpec((1, 128), full2),
                  pl.BlockSpec((1, 128), full2),
                  pl.BlockSpec((1, D_MODEL), full2),
                  pl.BlockSpec((1, D_MODEL), full2),
                  pl.BlockSpec((WINDOW, WINDOW), full2),
                  pl.BlockSpec((1, tt, D_MODEL), lambda b, t: (b, t, 0)),
                  pl.BlockSpec((D_MODEL, D_MODEL), full2),
                  pl.BlockSpec(memory_space=pl.ANY), pl.BlockSpec(memory_space=pl.ANY)],
        out_specs=[pl.BlockSpec((1, tt, D_MODEL), lambda b, t: (b, t, 0)),
                   pl.BlockSpec((None, 1, HC * HD_C, N_C), lambda b, t: (layer, b, 0, 0)),
                   pl.BlockSpec((None, 1, CONV_W - 1, CONV_DIM), lambda b, t: (layer, b, 0, 0))],
        out_shape=[jax.ShapeDtypeStruct((B, T, D_MODEL), F32),
                   jax.ShapeDtypeStruct((DEPTH, B, HC * HD_C, N_C), F32),
                   jax.ShapeDtypeStruct((DEPTH, B, CONV_W - 1, CONV_DIM), F32)],
        input_output_aliases={13: 1, 14: 2},
        scratch_shapes=[pltpu.VMEM((tt, SEG_C), F32),
                        pltpu.VMEM((tt + 8, CONV_DIM), F32),
                        pltpu.VMEM((tt, CONV_DIM), F32),
                        pltpu.VMEM((tt, 128), F32),
                        pltpu.VMEM((HC * HD_C, N_C), F32),
                        pltpu.VMEM((tt, D_MODEL), BF16)],
        compiler_params=_cparams(2),
        name="prompt_ssd_out",
    )(x, mod3, n1, wc, cw, cb, dtb_pad, a_pad, dsk_full, snw, tri, mixab, wo, *st_all)


def _mlp_kernel(x_ref, sh_ref, sc_ref, g_ref, n2_ref, wu_ref, wd_ref, fn_ref, o_ref, *, tf, final, per_row):
    rd = (lambda r: r[...]) if per_row else (lambda r: r[0])
    x = x_ref[...]
    hb = _modnorm(x, n2_ref[...], rd(sc_ref), rd(sh_ref)).astype(BF16)
    acc = None
    for f in range(D_FF // tf):
        u = _dot(hb, wu_ref[0, :, tf * f:tf * (f + 1)])
        u = jnp.square(jnp.maximum(u, 0.0)).astype(BF16)
        part = _dot(u, wd_ref[0, tf * f:tf * (f + 1), :])
        acc = part if acc is None else acc + part
    y = x + rd(g_ref) * acc
    if final:
        y = _rms_lanes(y) * fn_ref[...]
    o_ref[...] = y


def _mlp_call(x2, mod, n2, wu, wd, fn, layer, tm, tf, rows_per_mod, final):
    M = x2.shape[0]
    per_row = rows_per_mod == 1
    if per_row:
        mspec = lambda j: pl.BlockSpec((tm, D_MODEL), lambda m: (m, j))
    else:
        mspec = lambda j: pl.BlockSpec((1, 1, D_MODEL), lambda m: ((m * tm) // rows_per_mod, 0, j))
    kern = functools.partial(_mlp_kernel, tf=tf, final=final, per_row=per_row)
    resident = pl.Buffered(1)
    return pl.pallas_call(
        kern,
        grid=(M // tm,),
        in_specs=[pl.BlockSpec((tm, D_MODEL), lambda m: (m, 0)),
                  mspec(3), mspec(4), mspec(5),
                  pl.BlockSpec((1, D_MODEL), lambda m: (0, 0)),
                  pl.BlockSpec((1, D_MODEL, D_FF), lambda m: (layer, 0, 0), pipeline_mode=resident),
                  pl.BlockSpec((1, D_FF, D_MODEL), lambda m: (layer, 0, 0), pipeline_mode=resident),
                  pl.BlockSpec((1, D_MODEL), lambda m: (0, 0))],
        out_specs=pl.BlockSpec((tm, D_MODEL), lambda m: (m, 0)),
        out_shape=jax.ShapeDtypeStruct((M, D_MODEL), F32),
        compiler_params=_cparams(1),
        name="mlp",
    )(x2, mod, mod, mod, n2, wu, wd, fn)


def _sproj_kernel(x_ref, sh_ref, sc_ref, n1_ref, w_ref, o_ref):
    h = _modnorm(x_ref[...], n1_ref[...], sc_ref[...], sh_ref[...])
    o_ref[...] = _dot(h.astype(BF16), w_ref[...])


def _sproj_call(xs, mod_s, n1, w):
    n, width = xs.shape[0], w.shape[1]
    return pl.pallas_call(
        _sproj_kernel,
        grid=(1,),
        in_specs=[pl.BlockSpec((n, D_MODEL), lambda i: (0, 0)),
                  pl.BlockSpec((n, D_MODEL), lambda i: (0, 0)),
                  pl.BlockSpec((n, D_MODEL), lambda i: (0, 1)),
                  pl.BlockSpec((1, D_MODEL), lambda i: (0, 0)),
                  pl.BlockSpec((D_MODEL, width), lambda i: (0, 0))],
        out_specs=pl.BlockSpec((n, width), lambda i: (0, 0)),
        out_shape=jax.ShapeDtypeStruct((n, width), F32),
        compiler_params=_cparams(1),
        name="sample_proj",
    )(xs, mod_s, mod_s, n1, w)


def _carry_outputs(prev, n_in, nsteps):
    nslab = 1 if prev else DEPTH
    extra = [pl.BlockSpec(memory_space=pl.ANY) for _ in prev]
    alias = {n_in + k: k for k in range(len(prev))}
    row = lambda p, i: jnp.where(p == 0, i, nsteps - 1)
    return nslab, extra, alias, row


def _with_fill(body, state_outs):
    def kern(*refs):
        p = pl.program_id(0)

        @pl.when(p == 0)
        def _():
            body(*refs)

        @pl.when(p != 0)
        def _():
            for k in state_outs:
                refs[k][...] = jnp.zeros_like(refs[k])

    return kern


def _sa_kernel(q_ref, kv_ref, ck_ref, cv_ref, bias_ref, sink_ref, *rest):
    ko_ref, vo_ref, oa_ref = rest[-3:]
    nb = ROWS_PER_SAMPLE_STEP
    gsz = HA_Q // HA_KV
    rg = lax.broadcasted_iota(jnp.int32, (HA_Q, HD_A), 0) // gsz
    sink = sink_ref[...]
    kw = HA_KV * HD_A
    scores, vmats = [], []
    for i in range(nb):
        ko_ref[0, i, 0:WINDOW - 1, :] = ck_ref[0, i, 1:WINDOW, :]
        ko_ref[0, i, WINDOW - 1:WINDOW, :] = kv_ref[i:i + 1, 0:kw]
        vo_ref[0, i, 0:WINDOW - 1, :] = cv_ref[0, i, 1:WINDOW, :]
        vo_ref[0, i, WINDOW - 1:WINDOW, :] = kv_ref[i:i + 1, kw:2 * kw]
        kmat = ko_ref[0, i].astype(BF16)
        vmats.append(vo_ref[0, i].astype(BF16))
        q = q_ref[i] * (HD_A ** -0.5)
        qe = jnp.concatenate([jnp.where(rg == g, q, 0.0) for g in range(HA_KV)], axis=1).astype(BF16)
        scores.append(_dot_nt(qe, kmat))
    probs = []
    for i in range(nb):
        sc = scores[i] + bias_ref[...]
        m = jnp.maximum(jnp.max(sc, axis=-1, keepdims=True), sink)
        p = jnp.exp(sc - m)
        den = jnp.sum(p, axis=-1, keepdims=True) + jnp.exp(sink - m)
        probs.append((p.astype(BF16), den))
    outs = [_dot(probs[i][0], vmats[i]) for i in range(nb)]
    for i in range(nb):
        o = outs[i] / probs[i][1]
        o16 = jnp.zeros((HA_Q, HD_A), F32)
        for g in range(HA_KV):
            o16 = o16 + jnp.where(rg == g, o[:, HD_A * g:HD_A * (g + 1)], 0.0)
        oa_ref[i] = o16


def _sa_call(q3, kv, cache_k, cache_v, bias_s, sink_col, layer, prev):
    n = q3.shape[0]
    nb = ROWS_PER_SAMPLE_STEP
    kw = HA_KV * HD_A
    nslab, extra, alias, row = _carry_outputs(prev, 6, n // nb)
    cspec = pl.BlockSpec((1, nb, WINDOW, kw), lambda p, i: (layer, row(p, i), 0, 0))
    ospec = pl.BlockSpec((1, nb, WINDOW, kw), lambda p, i: ((layer + p) % DEPTH, i, 0, 0))
    return pl.pallas_call(
        _with_fill(_sa_kernel, (-3, -2)),
        grid=(nslab, n // nb),
        in_specs=[pl.BlockSpec((nb, HA_Q, HD_A), lambda p, i: (row(p, i), 0, 0)),
                  pl.BlockSpec((nb, 2 * kw), lambda p, i: (row(p, i), 0)),
                  cspec, cspec,
                  pl.BlockSpec((HA_Q, WINDOW), lambda p, i: (0, 0)),
                  pl.BlockSpec((HA_Q, 1), lambda p, i: (0, 0))] + extra,
        out_specs=[ospec, ospec, pl.BlockSpec((nb, HA_Q, HD_A), lambda p, i: (row(p, i), 0, 0))],
        out_shape=[jax.ShapeDtypeStruct((DEPTH, n, WINDOW, kw), F32),
                   jax.ShapeDtypeStruct((DEPTH, n, WINDOW, kw), F32),
                   jax.ShapeDtypeStruct((n, HA_Q, HD_A), F32)],
        input_output_aliases=alias,
        compiler_params=_cparams(2),
        name="sample_attn",
    )(q3, kv, cache_k, cache_v, bias_s, sink_col, *prev)


def _sr_kernel(zb_ref, cos_ref, sin_ref, gcol_ref, s_ref, *rest):
    so_ref, o_ref = rest[-2:]
    nb = ROWS_PER_SAMPLE_STEP
    wq = HB * DK_B
    cos = cos_ref[...]
    sin = sin_ref[...]
    qf = zb_ref[:, 0:wq]
    kf = zb_ref[:, wq:2 * wq]
    qr = qf * cos + _pairswap(qf) * sin
    kr = (kf * cos + _pairswap(kf) * sin) * (DK_B ** -0.5)
    v = zb_ref[:, 2 * wq:2 * wq + HB * DV_B]
    r8 = lax.broadcasted_iota(jnp.int32, (HB, wq), 0)
    hl = lax.broadcasted_iota(jnp.int32, (HB, wq), 1) // DK_B
    rv = lax.broadcasted_iota(jnp.int32, (HB, DV_B), 0)
    outers, q8s = [], []
    for i in range(nb):
        k8 = jnp.where(hl == r8, jnp.broadcast_to(kr[i:i + 1, :], (HB, wq)), 0.0).astype(BF16)
        q8s.append(jnp.where(hl == r8, jnp.broadcast_to(qr[i:i + 1, :], (HB, wq)), 0.0).astype(BF16))
        v8 = jnp.zeros((HB, DV_B), F32)
        for r in range(HB):
            v8 = jnp.where(rv == r, jnp.broadcast_to(v[i:i + 1, DV_B * r:DV_B * (r + 1)], (HB, DV_B)), v8)
        outers.append(_dot_tn(k8, v8.astype(BF16)))
    s_news = []
    for i in range(nb):
        s_new = gcol_ref[...] * s_ref[0, i] + outers[i]
        so_ref[0, i] = s_new
        s_news.append(s_new.astype(BF16))
    for i in range(nb):
        o_ref[i] = _dot(q8s[i], s_news[i])


def _sr_call(zb, state, layer, prev):
    n = zb.shape[0]
    nb = ROWS_PER_SAMPLE_STEP
    wq = HB * DK_B
    cos, sin = _rot_tables(np.array([PAST_LEN]))
    gcol = np.repeat(_gammas(), DK_B)[:, None] * np.ones((1, DV_B))
    nslab, extra, alias, row = _carry_outputs(prev, 5, n // nb)
    full = lambda p, i: (0, 0)
    return pl.pallas_call(
        _with_fill(_sr_kernel, (-2,)),
        grid=(nslab, n // nb),
        in_specs=[pl.BlockSpec((nb, SEG_B), lambda p, i: (row(p, i), 0)),
                  pl.BlockSpec((1, wq), full),
                  pl.BlockSpec((1, wq), full),
                  pl.BlockSpec((wq, DV_B), full),
                  pl.BlockSpec((1, nb, wq, DV_B), lambda p, i: (layer, row(p, i), 0, 0))] + extra,
        out_specs=[pl.BlockSpec((1, nb, wq, DV_B), lambda p, i: ((layer + p) % DEPTH, i, 0, 0)),
                   pl.BlockSpec((nb, HB, DV_B), lambda p, i: (row(p, i), 0, 0))],
        out_shape=[jax.ShapeDtypeStruct((DEPTH, n, wq, DV_B), F32),
                   jax.ShapeDtypeStruct((n, HB, DV_B), F32)],
        input_output_aliases=alias,
        compiler_params=_cparams(2),
        name="sample_retention",
    )(zb, jnp.asarray(cos), jnp.asarray(sin), jnp.asarray(gcol.astype(np.float32)), state, *prev)


def _ss_kernel(zc_ref, cs_ref, cw_ref, cb_ref, dtb_ref, a_ref, e3_ref, h_ref, *rest):
    ho_ref, cso_ref, y_ref, xc_ref = rest[-4:]
    nb = ROWS_PER_SAMPLE_STEP
    cx = zc_ref[:, 1024:2560]
    taps = [cs_ref[0, :, CONV_DIM * i:CONV_DIM * (i + 1)] for i in range(CONV_W - 1)] + [cx]
    acc = cb_ref[...]
    for i in range(CONV_W):
        acc = acc + taps[i] * cw_ref[i:i + 1, :]
    xbc = _silu(acc)
    cso_ref[...] = jnp.concatenate(taps[1:], axis=1)
    xc = xbc[:, 0:1024]
    bmat = xbc[:, 1024:1280]
    cmat = xbc[:, 1280:1536]
    dt = _softplus(zc_ref[:, C_DT:C_DT + 128] + dtb_ref[...])
    da = jnp.exp(dt * a_ref[...])
    dt_e = _dot(jnp.concatenate(_split3(dt), axis=1), e3_ref[...])
    da_e = _dot(jnp.concatenate(_split3(da), axis=1), e3_ref[...])
    dtx = dt_e * xc
    gw = (HC // G_C) * HD_C
    r8 = lax.broadcasted_iota(jnp.int32, (nb, gw), 0)
    rn = lax.broadcasted_iota(jnp.int32, (nb, N_C), 0)
    ones8 = jnp.ones((nb, N_C), BF16)
    prods = {}
    for g in range(G_C):
        ws = slice(gw * g, gw * (g + 1))
        bg16 = bmat[:, N_C * g:N_C * (g + 1)].astype(BF16)
        for i in range(nb):
            x8 = jnp.where(r8 == i, dtx[:, ws], 0.0).astype(BF16)
            outer = _dot_tn(x8, bg16)
            hi, mid, lo = (p.astype(F32) for p in _split3(jnp.broadcast_to(da_e[i:i + 1, ws], (nb, gw))))
            l3 = jnp.where(r8 == 0, hi, jnp.where(r8 == 1, mid, jnp.where(r8 == 2, lo, 0.0)))
            prods[g, i] = (outer, _dot_tn(l3.astype(BF16), ones8))
    h16 = {}
    for g in range(G_C):
        ws = slice(gw * g, gw * (g + 1))
        for i in range(nb):
            outer, dacol = prods[g, i]
            h_new = dacol * h_ref[0, i, ws, :] + outer
            ho_ref[0, i, ws, :] = h_new
            h16[g, i] = h_new.astype(BF16)
    ycols = []
    for g in range(G_C):
        ns = slice(N_C * g, N_C * (g + 1))
        yacc = jnp.zeros((nb, gw), F32)
        for i in range(nb):
            c8 = jnp.where(rn == i, cmat[:, ns], 0.0).astype(BF16)
            yacc = yacc + _dot_nt(c8, h16[g, i])
        ycols.append(yacc)
    y_ref[...] = jnp.concatenate(ycols, axis=1)
    xc_ref[...] = xc


def _ss_call(zc, conv_state, hstate, cw, cb, dtb_pad, a_pad, layer, prev):
    n = zc.shape[0]
    nb = ROWS_PER_SAMPLE_STEP
    e = np.zeros((128, D_MODEL), np.float32)
    for hh in range(HC):
        e[hh, HD_C * hh:HD_C * (hh + 1)] = 1.0
    e3 = jnp.asarray(np.concatenate([e, e, e], axis=0), dtype=BF16)
    cwid = (CONV_W - 1) * CONV_DIM
    full = lambda p, i: (0, 0)
    nslab, extra, alias, row = _carry_outputs(prev, 8, n // nb)
    return pl.pallas_call(
        _with_fill(_ss_kernel, (-4,)),
        grid=(nslab, n // nb),
        in_specs=[pl.BlockSpec((nb, SEG_C), lambda p, i: (row(p, i), 0)),
                  pl.BlockSpec((1, nb, cwid), lambda p, i: (layer, row(p, i), 0)),
                  pl.BlockSpec((CONV_W, CONV_DIM), full),
                  pl.BlockSpec((1, CONV_DIM), full),
                  pl.BlockSpec((1, 128), full),
                  pl.BlockSpec((1, 128), full),
                  pl.BlockSpec((3 * 128, D_MODEL), full),
                  pl.BlockSpec((1, nb, HC * HD_C, N_C), lambda p, i: (layer, row(p, i), 0, 0))] + extra,
        out_specs=[pl.BlockSpec((1, nb, HC * HD_C, N_C), lambda p, i: ((layer + p) % DEPTH, i, 0, 0)),
                   pl.BlockSpec((nb, cwid), lambda p, i: (row(p, i), 0)),
                   pl.BlockSpec((nb, D_MODEL), lambda p, i: (row(p, i), 0)),
                   pl.BlockSpec((nb, D_MODEL), lambda p, i: (row(p, i), 0))],
        out_shape=[jax.ShapeDtypeStruct((DEPTH, n, HC * HD_C, N_C), F32),
                   jax.ShapeDtypeStruct((n, cwid), F32),
                   jax.ShapeDtypeStruct((n, D_MODEL), F32),
                   jax.ShapeDtypeStruct((n, D_MODEL), F32)],
        input_output_aliases=alias,
        compiler_params=_cparams(2),
        name="sample_ssd",
    )(zc, conv_state, cw, cb, dtb_pad, a_pad, e3, hstate, *prev)


def _sm_kernel(x_ref, g1_ref, oa_ref, oret_ref, y_ref, xc_ref, za_ref, zb_ref, zc_ref,
               dsk_ref, snw_ref, wo_ref, o_ref):
    ob = jnp.concatenate([_rms_lanes(oret_ref[:, DV_B * hh:DV_B * (hh + 1)]) for hh in range(HB)], axis=1)
    ob = _silu(zb_ref[:, 2048:3072]) * ob
    yc = (y_ref[...] + dsk_ref[...] * xc_ref[...]) * _silu(zc_ref[:, 0:1024])
    gw = D_MODEL // G_C
    oc = jnp.concatenate([_rms_lanes(yc[:, gw * g:gw * (g + 1)]) for g in range(G_C)], axis=1) * snw_ref[...]
    mix = (_sigmoid(za_ref[:, 1536:2560]) * oa_ref[...] + _sigmoid(zb_ref[:, 3072:4096]) * ob
           + _sigmoid(zc_ref[:, C_GC:C_GC + 1024]) * oc)
    o_ref[...] = x_ref[...] + g1_ref[...] * _dot(mix.astype(BF16), wo_ref[...])


def _sm_call(xs, mod_s, oa, oret, y, xc, za, zb, zc, dsk_full, snw, wo):
    n = xs.shape[0]
    full = lambda i: (0, 0)
    row = lambda w: pl.BlockSpec((n, w), full)
    return pl.pallas_call(
        _sm_kernel,
        grid=(1,),
        in_specs=[row(D_MODEL),
                  pl.BlockSpec((n, D_MODEL), lambda i: (0, 2)),
                  row(D_MODEL), row(D_MODEL), row(D_MODEL), row(D_MODEL),
                  row(SEG_A), row(SEG_B), row(SEG_C),
                  pl.BlockSpec((1, D_MODEL), full),
                  pl.BlockSpec((1, D_MODEL), full),
                  pl.BlockSpec((D_MODEL, D_MODEL), full)],
        out_specs=row(D_MODEL),
        out_shape=jax.ShapeDtypeStruct((n, D_MODEL), F32),
        compiler_params=_cparams(1),
        name="sample_merge_out",
    )(xs, mod_s, oa, oret, y, xc, za, zb, zc, dsk_full, snw, wo)


PROMPT_TILE = 512
ATTN_TILE = 1024
MLP_ROWS = 1024
MLP_FF = 1024


def _prep_w_in(w):
    wa = jnp.concatenate([w[:, O_AQ:O_BQ], w[:, O_GTS:O_GTS + 1024]], axis=1)
    wb = jnp.concatenate([w[:, O_BQ:O_CZ], w[:, O_GTS + 1024:O_GTS + 2048]], axis=1)
    wc = jnp.concatenate([w[:, O_CZ:O_CDT], jnp.pad(w[:, O_CDT:O_GTS], ((0, 0), (0, 128 - HC))),
                          w[:, O_GTS + 2048:O_GTS + 3072]], axis=1)
    return wa, wb, wc


def _forward(x_prompt, x_sample, cache_win_k, cache_win_v, state_ret, state_ssm, state_conv,
             c_prompt, c_sample, rel_bias_table, attn_sinks, norm1_w, norm2_w, ada_w, ada_b,
             w_in, conv_w, conv_b, dt_bias, A_log, D_skip, ssm_norm_w, w_out, w_up, w_down,
             final_norm_w, *, prompt_tile, attn_tile, mlp_rows, mlp_ff):
    B, T, _ = x_prompt.shape
    DB = x_sample.shape[0]
    kw = HA_KV * HD_A

    mod_all = _ada_call(jnp.concatenate([c_prompt, c_sample], axis=0), ada_w, ada_b)

    bias_t, bias_s = _bias_tables(rel_bias_table)

    w16 = w_in.astype(BF16)
    wu16 = w_up.astype(BF16)
    wd16 = w_down.astype(BF16)
    wo16 = w_out.astype(BF16)
    fn = final_norm_w.reshape(1, D_MODEL)
    ck = cache_win_k.reshape(DEPTH, DB, WINDOW, kw)
    cv = cache_win_v.reshape(DEPTH, DB, WINDOW, kw)
    sret = state_ret.reshape(DEPTH, DB, HB * DK_B, DV_B)
    sssm = state_ssm.reshape(DEPTH, DB, HC * HD_C, N_C)
    sconv = state_conv.reshape(DEPTH, DB, (CONV_W - 1) * CONV_DIM)

    xp = x_prompt
    xs = x_sample.reshape(DB, D_MODEL)
    win_p = (jnp.zeros((DEPTH, B, WINDOW, kw), F32), jnp.zeros((DEPTH, B, WINDOW, kw), F32))
    ret_p = jnp.zeros((DEPTH, B, HB * DK_B, DV_B), F32)
    st_p = (jnp.zeros((DEPTH, B, HC * HD_C, N_C), F32), jnp.zeros((DEPTH, B, CONV_W - 1, CONV_DIM), F32))
    conv_s = []
    win_kv, ret_all, ssm_all = (), (), ()
    for l in range(DEPTH):
        wl = w16[l]
        wa, wb, wc = _prep_w_in(wl)
        n1 = norm1_w[l].reshape(1, D_MODEL)
        n2 = norm2_w[l].reshape(1, D_MODEL)
        cw = conv_w[l]
        cb = conv_b[l].reshape(1, CONV_DIM)
        dtb_pad = jnp.pad(dt_bias[l], (0, 128 - HC)).reshape(1, 128)
        a_pad = jnp.pad(-jnp.exp(A_log[l].astype(F32)), (0, 128 - HC)).reshape(1, 128)
        dsk_full = jnp.repeat(D_skip[l], HD_C).reshape(1, D_MODEL)
        snw = ssm_norm_w[l].reshape(1, D_MODEL)
        final = l == DEPTH - 1
        mod_p = mod_all[l, :B].reshape(B, 1, 6 * D_MODEL)
        mod_s = mod_all[l, B:]

        wqgt = jnp.concatenate([wl[:, O_AQ:O_AK] * (HD_A ** -0.5), wl[:, O_GTS:O_GTS + D_MODEL]], axis=1).T
        wkv = wl[:, O_AK:O_BQ]
        sink_rows = jnp.repeat(attn_sinks[l], WINDOW).reshape(HA_KV, 1, (HA_Q // HA_KV) * WINDOW)
        mixa, *win_p = _pa_call(xp, mod_p, n1, wqgt, wkv, bias_t, sink_rows, attn_tile, l, win_p)
        wbt = jnp.concatenate([_deinterleave_pairs(wl[:, O_BQ:O_BK]),
                               _deinterleave_pairs(wl[:, O_BK:O_BV]) * (DK_B ** -0.5), wl[:, O_BV:O_BG],
                               wl[:, O_BG:O_CZ] * 0.5, wl[:, O_GTS + D_MODEL:O_GTS + 2 * D_MODEL] * 0.5], axis=1).T
        mixab, ret_p = _pb_call(xp, mod_p, n1, wbt, mixa, attn_tile, l, ret_p)
        x1, *st_p = _pc_call(xp, mod_p, n1, wc, cw, cb, dtb_pad, a_pad, dsk_full, snw,
                             mixab, wo16[l], prompt_tile, l, st_p)
        xp = _mlp_call(x1.reshape(B * T, D_MODEL), mod_p, n2, wu16, wd16, fn, l,
                       mlp_rows, mlp_ff, T, final).reshape(B, T, D_MODEL)

        za = _sproj_call(xs, mod_s, n1, wa)
        zb = _sproj_call(xs, mod_s, n1, wb)
        zc = _sproj_call(xs, mod_s, n1, wc)
        q3 = za[:, 0:HA_Q * HD_A].reshape(DB, HA_Q, HD_A)
        kv = za[:, HA_Q * HD_A:HA_Q * HD_A + 2 * kw]
        ck_new, cv_new, oa3 = _sa_call(q3, kv, ck, cv, bias_s, attn_sinks[l].reshape(HA_Q, 1), l, win_kv)
        win_kv = (ck_new, cv_new)
        s_new, o3 = _sr_call(zb, sret, l, ret_all)
        ret_all = (s_new,)
        h_new, cs_new, y_s, xc_s = _ss_call(zc, sconv, sssm, cw, cb, dtb_pad, a_pad, l, ssm_all)
        ssm_all = (h_new,)
        xs1 = _sm_call(xs, mod_s, oa3.reshape(DB, D_MODEL), o3.reshape(DB, D_MODEL), y_s, xc_s,
                       za, zb, zc, dsk_full, snw, wo16[l])
        xs = _mlp_call(xs1, mod_s, n2, wu16, wd16, fn, l, DB, mlp_ff, 1, final)
        conv_s.append(cs_new.reshape(DB, CONV_W - 1, CONV_DIM))

    stk = lambda lst: jnp.stack(lst, axis=0)
    return (xp, xs.reshape(DB, 1, D_MODEL),
            win_p[0].reshape(DEPTH, B, WINDOW, HA_KV, HD_A), win_p[1].reshape(DEPTH, B, WINDOW, HA_KV, HD_A),
            ret_p.reshape(DEPTH, B, HB, 2, DK_B // 2, DV_B).transpose(0, 1, 2, 4, 3, 5).reshape(DEPTH, B, HB, DK_B, DV_B),
            st_p[0].reshape(DEPTH, B, HC, HD_C, N_C), st_p[1],
            win_kv[0].reshape(DEPTH, DB, WINDOW, HA_KV, HD_A), win_kv[1].reshape(DEPTH, DB, WINDOW, HA_KV, HD_A),
            ret_all[0].reshape(DEPTH, DB, HB, DK_B, DV_B), ssm_all[0].reshape(DEPTH, DB, HC, HD_C, N_C),
            stk(conv_s))


def kernel(x_prompt, x_sample, cache_win_k, cache_win_v, state_ret, state_ssm, state_conv, c_prompt, c_sample,
           rel_bias_table, attn_sinks, norm1_w, norm2_w, ada_w, ada_b, w_in, conv_w, conv_b, dt_bias, A_log,
           D_skip, ssm_norm_w, w_out, w_up, w_down, final_norm_w):
    return _forward(x_prompt, x_sample, cache_win_k, cache_win_v, state_ret, state_ssm, state_conv,
                    c_prompt, c_sample, rel_bias_table, attn_sinks, norm1_w, norm2_w, ada_w, ada_b,
                    w_in, conv_w, conv_b, dt_bias, A_log, D_skip, ssm_norm_w, w_out, w_up, w_down,
                    final_norm_w, prompt_tile=PROMPT_TILE, attn_tile=ATTN_TILE, mlp_rows=MLP_ROWS, mlp_ff=MLP_FF)
```

```python
import functools
import math

import numpy as np
import jax
import jax.numpy as jnp
from jax import lax
from jax.experimental import pallas as pl
from jax.experimental.pallas import tpu as pltpu

F32 = jnp.float32
BF16 = jnp.bfloat16

D_MODEL = 1024
DEPTH = 2
PAST_LEN = 16384
WINDOW = 128
HA_Q = 16
HA_KV = 4
HD_A = 64
NUM_BUCKETS = 32
MAX_DISTANCE = WINDOW
HB = 8
DK_B = 64
DV_B = 128
HC = 16
HD_C = 64
N_C = 128
G_C = 2
CONV_W = 4
CONV_DIM = D_MODEL + 2 * G_C * N_C
D_FF = 4 * D_MODEL
EPS = 1e-6
NEG = -1e30
LOG2E = 1.4426950408889634

O_AQ, O_AK, O_AV = 0, 1024, 1280
O_BQ, O_BK, O_BV, O_BG = 1536, 2048, 2560, 3584
O_CZ, O_CXBC, O_CDT, O_GTS = 4608, 5632, 7168, 7184
SEG_A = 2560
SEG_B = 4096
SEG_C = 3712
C_DT = 2560
C_GC = 2688

VMEM_LIMIT_V7X = 56 * 1024 * 1024
ROWS_PER_SAMPLE_STEP = 8


def _cparams(n_axes):
    return pltpu.CompilerParams(dimension_semantics=("arbitrary",) * n_axes,
                                vmem_limit_bytes=VMEM_LIMIT_V7X)


def _dot(a, b):
    return jnp.dot(a, b, preferred_element_type=F32)


def _dot_nt(a, b):
    return lax.dot_general(a, b, (((1,), (1,)), ((), ())), preferred_element_type=F32)


def _dot_tn(a, b):
    return lax.dot_general(a, b, (((0,), (0,)), ((), ())), preferred_element_type=F32)


def _sigmoid(x):
    return 0.5 * (jnp.tanh(0.5 * x) + 1.0)


def _silu(x):
    return x * _sigmoid(x)


def _tanh1(half):
    return jnp.tanh(half) + 1.0


def _softplus(x):
    return jnp.maximum(x, 0.0) + jnp.log1p(jnp.exp(-jnp.abs(x)))


def _modnorm(x, nw, sc, sh):
    ms = jnp.mean(x * x, axis=-1, keepdims=True)
    return (x * lax.rsqrt(ms + EPS) * nw) * (1.0 + sc) + sh


def _rms_lanes(x):
    ms = jnp.mean(x * x, axis=-1, keepdims=True)
    return x * lax.rsqrt(ms + EPS)


def _pairswap(x):
    ax = x.ndim - 1
    n = x.shape[ax]
    lane = lax.broadcasted_iota(jnp.int32, x.shape, ax)
    nxt = pltpu.roll(x, n - 1, ax)
    prv = pltpu.roll(x, 1, ax)
    return jnp.where((lane & 1) == 0, nxt, prv)


def _split3(x):
    hi = x.astype(BF16)
    r1 = x - hi.astype(F32)
    mid = r1.astype(BF16)
    lo = (r1 - mid.astype(F32)).astype(BF16)
    return hi, mid, lo


def _gammas():
    return 1.0 - 2.0 ** (-5.0 - np.arange(HB, dtype=np.float64))


def _rot_tables(pos):
    theta = 1.0 / (10000.0 ** np.linspace(0.0, 1.0, DK_B // 2))
    ang = np.asarray(pos, np.float64)[:, None] * theta[None, :]
    cos = np.repeat(np.cos(ang), 2, axis=1)
    sin = np.repeat(np.sin(ang), 2, axis=1)
    sin[:, 0::2] *= -1.0
    return (np.tile(cos, (1, HB)).astype(np.float32), np.tile(sin, (1, HB)).astype(np.float32))


def _ret_tables():
    g = _gammas()
    L = WINDOW
    i = np.arange(L, dtype=np.float64)
    diff = i[:, None] - i[None, :]
    dm = np.where(diff >= 0, g[:, None, None] ** np.maximum(diff, 0.0), 0.0)
    qdec = np.repeat(g[None, :] ** (i[:, None] + 1.0), DK_B, axis=1)
    kdec = np.repeat(g[None, :] ** (L - 1.0 - i[:, None]), DK_B, axis=1)
    return dm.astype(np.float32), qdec.astype(np.float32), kdec.astype(np.float32)


def _t5_bucket_np(dist):
    max_exact = NUM_BUCKETS // 2
    n = np.maximum(dist, 0)
    nf = np.maximum(n, 1).astype(np.float32)
    large = max_exact + (np.log(nf / np.float32(max_exact)) / np.float32(math.log(MAX_DISTANCE / max_exact))
                         * np.float32(NUM_BUCKETS - max_exact)).astype(np.int32)
    large = np.minimum(large, NUM_BUCKETS - 1)
    return np.where(n < max_exact, n, large)


def _bias_tables(rel_table):
    gsz = HA_Q // HA_KV
    qi = np.arange(WINDOW)[None, :]
    kj = np.arange(WINDOW)[:, None]
    dist = np.where(kj > qi, qi + WINDOW - kj, qi - kj)
    onehot = _t5_bucket_np(dist)[..., None] == np.arange(NUM_BUCKETS)
    tab = rel_table.astype(F32)
    bias_kq = jnp.einsum('kqb,bh->hkq', jnp.asarray(onehot, F32), tab, precision=lax.Precision.HIGHEST)
    bias_t = bias_kq.reshape(HA_KV, gsz, WINDOW, WINDOW).transpose(0, 2, 1, 3).reshape(HA_KV, WINDOW, gsz * WINDOW)
    oh_s = _t5_bucket_np(WINDOW - 1 - np.arange(WINDOW))[:, None] == np.arange(NUM_BUCKETS)
    bias_s = jnp.einsum('jb,bh->hj', jnp.asarray(oh_s, F32), tab, precision=lax.Precision.HIGHEST)
    return bias_t, bias_s


def _ada_kernel(c_ref, w_ref, b_ref, o_ref):
    s = _silu(c_ref[...])
    o_ref[0] = _dot(s.astype(BF16), w_ref[0].astype(BF16)) + b_ref[0]


def _ada_call(c_all, ada_w, ada_b):
    n = c_all.shape[0]
    nb = 6
    return pl.pallas_call(
        _ada_kernel,
        grid=(DEPTH, nb),
        in_specs=[pl.BlockSpec((n, D_MODEL), lambda l, j: (0, 0)),
                  pl.BlockSpec((1, D_MODEL, D_MODEL), lambda l, j: (l, 0, j)),
                  pl.BlockSpec((1, 1, D_MODEL), lambda l, j: (l, 0, j))],
        out_specs=pl.BlockSpec((1, n, D_MODEL), lambda l, j: (l, 0, j)),
        out_shape=jax.ShapeDtypeStruct((DEPTH, n, 6 * D_MODEL), F32),
        compiler_params=_cparams(2),
        name="ada_mod",
    )(c_all, ada_w, ada_b.reshape(DEPTH, 1, 6 * D_MODEL))


def _pa_kernel(x_ref, mod_ref, n1_ref, wt_ref, wkv_ref, bias_ref, sink_ref, lowm_ref,
               mix_ref, ko_ref, vo_ref, zt_ref, kv_ref, kprev_ref, vtprev_ref, pen_ref, *, tt):
    t = pl.program_id(1)
    nchunk = tt // WINDOW
    kw = HA_KV * HD_A
    gsz = HA_Q // HA_KV

    @pl.when(t == 0)
    def _():
        kprev_ref[...] = jnp.zeros_like(kprev_ref)
        vtprev_ref[...] = jnp.zeros_like(vtprev_ref)
        pen_ref[...] = jnp.full(pen_ref.shape, NEG, F32)

    mod = mod_ref[0]
    h = _modnorm(x_ref[0], n1_ref[...], mod[:, D_MODEL:2 * D_MODEL], mod[:, 0:D_MODEL]).astype(BF16)
    kv_ref[...] = _dot(h, wkv_ref[...])
    for cc in range(nchunk // 2):
        z2 = _dot_nt(wt_ref[...], h[2 * WINDOW * cc:2 * WINDOW * (cc + 1), :])
        zt_ref[2 * cc] = z2[:, 0:WINDOW]
        zt_ref[2 * cc + 1] = z2[:, WINDOW:2 * WINDOW]
    qw = gsz * WINDOW
    lower = (lax.broadcasted_iota(jnp.int32, (WINDOW, qw), 0)
             > (lax.broadcasted_iota(jnp.int32, (WINDOW, qw), 1) & (WINDOW - 1)))

    def chunk(c, carry):
        r0 = pl.multiple_of(c * WINDOW, WINDOW)
        rows = pl.ds(r0, WINDOW)
        kc = kv_ref[rows, 0:kw]
        vc = kv_ref[rows, kw:2 * kw]
        vt = vc.T
        kk = jnp.concatenate([kprev_ref[...], kc], axis=0).astype(BF16)
        vvt = jnp.concatenate([vtprev_ref[...], vt], axis=1).astype(BF16)
        qt = zt_ref[c, 0:D_MODEL, :].astype(BF16)
        pen = pen_ref[0:1, :]
        s_all = []
        for g in range(HA_KV):
            gs = slice(HD_A * g, HD_A * (g + 1))
            qcat = jnp.concatenate([qt[HD_A * (gsz * g + j):HD_A * (gsz * g + j + 1), :] for j in range(gsz)], axis=1)
            s_all.append(_dot(kk[:, gs], qcat))
        p_all = []
        for g in range(HA_KV):
            sg = jnp.where(lower, s_all[g][0:WINDOW, :] + pen, s_all[g][WINDOW:2 * WINDOW, :]) + bias_ref[g]
            sink = sink_ref[g]
            m = jnp.maximum(jnp.max(sg, axis=0, keepdims=True), sink)
            pw = jnp.exp(sg - m)
            den = jnp.sum(pw, axis=0, keepdims=True) + jnp.exp(sink - m)
            pb = pw.astype(BF16)
            p_prev = pb * lowm_ref[...]
            p_all.append((jnp.concatenate([p_prev, pb - p_prev], axis=0), 1.0 / den))
        pieces = []
        for g in range(HA_KV):
            gs = slice(HD_A * g, HD_A * (g + 1))
            p, rden = p_all[g]
            ot = _dot(vvt[gs, :], p) * rden
            pieces += [ot[:, WINDOW * j:WINDOW * (j + 1)] for j in range(gsz)]
        oat = jnp.concatenate(pieces, axis=0)
        mixt = _sigmoid(zt_ref[c, D_MODEL:2 * D_MODEL, :]) * oat
        mix_ref[0, rows, :] = mixt.T
        kprev_ref[...] = kc
        vtprev_ref[...] = vt
        pen_ref[...] = jnp.zeros_like(pen_ref)
        ko_ref[0] = kc
        vo_ref[0] = vc
        return carry

    lax.fori_loop(0, nchunk, chunk, 0, unroll=True)


def _pa_call(x, mod3, n1, wqgt, wkv, bias_t, sink_rows, tt):
    B, T, _ = x.shape
    kw = HA_KV * HD_A
    qw = (HA_Q // HA_KV) * WINDOW
    kern = functools.partial(_pa_kernel, tt=tt)
    lowm = (np.arange(WINDOW)[:, None] > (np.arange(qw)[None, :] % WINDOW)).astype(np.float32)
    return pl.pallas_call(
        kern,
        grid=(B, T // tt),
        in_specs=[pl.BlockSpec((1, tt, D_MODEL), lambda b, t: (b, t, 0)),
                  pl.BlockSpec((1, 1, 6 * D_MODEL), lambda b, t: (b, 0, 0)),
                  pl.BlockSpec((1, D_MODEL), lambda b, t: (0, 0)),
                  pl.BlockSpec((2 * D_MODEL, D_MODEL), lambda b, t: (0, 0)),
                  pl.BlockSpec((D_MODEL, 2 * kw), lambda b, t: (0, 0)),
                  pl.BlockSpec((HA_KV, WINDOW, qw), lambda b, t: (0, 0, 0)),
                  pl.BlockSpec((HA_KV, 1, qw), lambda b, t: (0, 0, 0)),
                  pl.BlockSpec((WINDOW, qw), lambda b, t: (0, 0))],
        out_specs=[pl.BlockSpec((1, tt, D_MODEL), lambda b, t: (b, t, 0)),
                   pl.BlockSpec((1, WINDOW, kw), lambda b, t: (b, 0, 0)),
                   pl.BlockSpec((1, WINDOW, kw), lambda b, t: (b, 0, 0))],
        out_shape=[jax.ShapeDtypeStruct((B, T, D_MODEL), F32),
                   jax.ShapeDtypeStruct((B, WINDOW, kw), F32),
                   jax.ShapeDtypeStruct((B, WINDOW, kw), F32)],
        scratch_shapes=[pltpu.VMEM((tt // WINDOW, 2 * D_MODEL, WINDOW), F32),
                        pltpu.VMEM((tt, 2 * kw), F32),
                        pltpu.VMEM((WINDOW, kw), F32),
                        pltpu.VMEM((kw, WINDOW), F32),
                        pltpu.VMEM((8, qw), F32)],
        compiler_params=_cparams(2),
        name="prompt_attn",
    )(x, mod3, n1, wqgt, wkv, bias_t, sink_rows, jnp.asarray(lowm, BF16))


def _pb_kernel(x_ref, mod_ref, n1_ref, wt_ref, cos_ref, sin_ref, qdec_ref, kdec_ref, dm_ref, mixa_ref,
               mix_ref, so_ref, zt_ref, s_ref, *, tt, glast):
    t = pl.program_id(1)
    nchunk = tt // WINDOW
    hw = HB * DK_B // 2
    hp = DK_B // 2

    @pl.when(t == 0)
    def _():
        s_ref[...] = jnp.zeros_like(s_ref)

    mod = mod_ref[0]
    h = _modnorm(x_ref[0], n1_ref[...], mod[:, D_MODEL:2 * D_MODEL], mod[:, 0:D_MODEL]).astype(BF16)
    for cc in range(nchunk // 2):
        z2 = _dot_nt(wt_ref[...], h[2 * WINDOW * cc:2 * WINDOW * (cc + 1), :])
        zt_ref[2 * cc] = z2[:, 0:WINDOW]
        zt_ref[2 * cc + 1] = z2[:, WINDOW:2 * WINDOW]

    def head_rows(pair, hh):
        return jnp.concatenate([pair[0][hp * hh:hp * (hh + 1), :], pair[1][hp * hh:hp * (hh + 1), :]], axis=0)

    def chunk(c, carry):
        r0 = pl.multiple_of(c * WINDOW, WINDOW)
        rows = pl.ds(r0, WINDOW)
        cos = cos_ref[c]
        sin = sin_ref[c]
        q1 = zt_ref[c, 0:hw, :]
        q2 = zt_ref[c, hw:2 * hw, :]
        k1 = zt_ref[c, 2 * hw:3 * hw, :]
        k2 = zt_ref[c, 3 * hw:4 * hw, :]
        rq = (q1 * cos - q2 * sin, q1 * sin + q2 * cos)
        rk = (k1 * cos - k2 * sin, k1 * sin + k2 * cos)
        qdec = qdec_ref[...]
        kdec = kdec_ref[...]
        qb = tuple(a.astype(BF16) for a in rq)
        kb = tuple(a.astype(BF16) for a in rk)
        qd = tuple((a * qdec).astype(BF16) for a in rq)
        kd = tuple((a * kdec).astype(BF16) for a in rk)
        vt = zt_ref[c, 4 * hw:4 * hw + HB * DV_B, :].astype(BF16)
        inner, cross, supd = [], [], []
        for hh in range(HB):
            s_old = s_ref[DK_B * hh:DK_B * (hh + 1), :]
            inner.append(_dot_tn(head_rows(qb, hh), head_rows(kb, hh)))
            cross.append(_dot_tn(s_old.astype(BF16), head_rows(qd, hh)))
            supd.append(glast[hh] * s_old + _dot_nt(head_rows(kd, hh), vt[DV_B * hh:DV_B * (hh + 1), :]))
        innd = [(inner[hh] * dm_ref[hh]).astype(BF16) for hh in range(HB)]
        outs = []
        for hh in range(HB):
            ot = _dot_nt(vt[DV_B * hh:DV_B * (hh + 1), :], innd[hh]) + cross[hh]
            s_ref[DK_B * hh:DK_B * (hh + 1), :] = supd[hh]
            ms = jnp.mean(ot * ot, axis=0, keepdims=True)
            outs.append(ot * (0.5 * lax.rsqrt(ms + EPS)))
        obt = jnp.concatenate(outs, axis=0)
        bgt = zt_ref[c, 4 * hw + D_MODEL:4 * hw + 2 * D_MODEL, :]
        gbt = zt_ref[c, 4 * hw + 2 * D_MODEL:4 * hw + 3 * D_MODEL, :]
        mixt = _tanh1(gbt) * (bgt * _tanh1(bgt) * obt)
        mix_ref[0, rows, :] = mixa_ref[0, rows, :] + mixt.T
        return carry

    lax.fori_loop(0, nchunk, chunk, 0, unroll=True)
    so_ref[0] = s_ref[...]


def _deinterleave_pairs(w):
    return w.reshape(w.shape[0], HB, DK_B // 2, 2).transpose(0, 3, 1, 2).reshape(w.shape[0], HB * DK_B)


def _rot_tables_t(T):
    theta = 1.0 / (10000.0 ** np.linspace(0.0, 1.0, DK_B // 2))
    ang = theta[:, None] * np.arange(T, dtype=np.float64)[None, :]

    def lay(a):
        a = np.tile(a, (HB, 1))
        return np.ascontiguousarray(a.reshape(a.shape[0], T // WINDOW, WINDOW).transpose(1, 0, 2)).astype(np.float32)

    return lay(np.cos(ang)), lay(np.sin(ang))


def _pb_call(x, mod3, n1, wbt, mixa, tt):
    B, T, _ = x.shape
    hw = HB * DK_B // 2
    cos, sin = _rot_tables_t(T)
    dm, qdec, kdec = _ret_tables()
    qdec_t = np.ascontiguousarray(qdec[:, ::2].T)
    kdec_t = np.ascontiguousarray(kdec[:, ::2].T)
    glast = tuple(float(v) for v in (_gammas() ** WINDOW))
    kern = functools.partial(_pb_kernel, tt=tt, glast=glast)
    full2 = lambda b, t: (0, 0)
    nct = tt // WINDOW
    return pl.pallas_call(
        kern,
        grid=(B, T // tt),
        in_specs=[pl.BlockSpec((1, tt, D_MODEL), lambda b, t: (b, t, 0)),
                  pl.BlockSpec((1, 1, 6 * D_MODEL), lambda b, t: (b, 0, 0)),
                  pl.BlockSpec((1, D_MODEL), full2),
                  pl.BlockSpec((SEG_B, D_MODEL), full2, pipeline_mode=pl.Buffered(1)),
                  pl.BlockSpec((nct, hw, WINDOW), lambda b, t: (t, 0, 0)),
                  pl.BlockSpec((nct, hw, WINDOW), lambda b, t: (t, 0, 0)),
                  pl.BlockSpec((hw, WINDOW), full2),
                  pl.BlockSpec((hw, WINDOW), full2),
                  pl.BlockSpec((HB, WINDOW, WINDOW), lambda b, t: (0, 0, 0)),
                  pl.BlockSpec((1, tt, D_MODEL), lambda b, t: (b, t, 0))],
        out_specs=[pl.BlockSpec((1, tt, D_MODEL), lambda b, t: (b, t, 0)),
                   pl.BlockSpec((1, HB * DK_B, DV_B), lambda b, t: (b, 0, 0))],
        out_shape=[jax.ShapeDtypeStruct((B, T, D_MODEL), F32),
                   jax.ShapeDtypeStruct((B, HB * DK_B, DV_B), F32)],
        scratch_shapes=[pltpu.VMEM((nct, SEG_B, WINDOW), F32),
                        pltpu.VMEM((HB * DK_B, DV_B), F32)],
        compiler_params=_cparams(2),
        name="prompt_retention",
    )(x, mod3, n1, wbt, jnp.asarray(cos), jnp.asarray(sin), jnp.asarray(qdec_t), jnp.asarray(kdec_t),
      jnp.asarray(dm), mixa)


def _pc_kernel(x_ref, mod_ref, n1_ref, w_ref, cw_ref, cb_ref, dtb_ref, a_ref, dsk_ref, snw_ref, tri_ref,
               mixab_ref, wo_ref,
               xo_ref, ho_ref, co_ref,
               z_ref, xbuf_ref, xbc_ref, dt_ref, hst_ref, mixs_ref, *, tt):
    t = pl.program_id(1)

    @pl.when(t == 0)
    def _():
        xbuf_ref[0:8, :] = jnp.zeros((8, CONV_DIM), F32)
        hst_ref[...] = jnp.zeros_like(hst_ref)

    x = x_ref[0]
    mod = mod_ref[0]
    h = _modnorm(x, n1_ref[...], mod[:, D_MODEL:2 * D_MODEL], mod[:, 0:D_MODEL])
    z_ref[...] = _dot(h.astype(BF16), w_ref[...])

    xbuf_ref[8:8 + tt, :] = z_ref[:, 1024:2560]
    acc = cb_ref[...]
    for i in range(CONV_W):
        acc = acc + xbuf_ref[5 + i:5 + i + tt, :] * cw_ref[i:i + 1, :]
    xbc_ref[...] = _silu(acc)
    co_ref[0] = xbuf_ref[tt + 5:tt + 8, :]
    xbuf_ref[0:8, :] = xbuf_ref[tt:tt + 8, :]
    dt_ref[...] = _softplus(z_ref[:, C_DT:C_DT + 128] + dtb_ref[...])

    ii = lax.broadcasted_iota(jnp.int32, (WINDOW, WINDOW), 0)
    jj = lax.broadcasted_iota(jnp.int32, (WINDOW, WINDOW), 1)
    causal = ii >= jj
    hpg = HC // G_C

    def chunk(c):
        rows = slice(WINDOW * c, WINDOW * (c + 1))
        xc = xbc_ref[rows, 0:1024]
        bmat = xbc_ref[rows, 1024:1280]
        cmat = xbc_ref[rows, 1280:1536]
        dtc = dt_ref[rows, :]
        acum = jnp.dot(tri_ref[...], dtc * a_ref[...], precision=lax.Precision.HIGHEST,
                       preferred_element_type=F32)
        acum = acum * LOG2E
        acum_t = acum.T
        rowp_t = acum_t - jnp.log2(dtc.T)
        x_t = xc.T
        xb = xc.astype(BF16)
        bb = bmat.astype(BF16)
        cb16 = cmat.astype(BF16)
        ys = []
        for g in range(G_C):
            ns = slice(N_C * g, N_C * (g + 1))
            cbg = _dot_nt(cb16[:, ns], bb[:, ns])
            for hh in range(hpg * g, hpg * (g + 1)):
                ps = slice(HD_C * hh, HD_C * (hh + 1))
                colb = jnp.broadcast_to(acum[:, hh:hh + 1], (WINDOW, WINDOW))
                rowp = rowp_t[hh:hh + 1, :]
                m = cbg * jnp.exp2(jnp.where(causal, colb - rowp, NEG))
                hs = hst_ref[ps, :]
                ecolb = jnp.exp2(colb)
                cs = cmat[:, ns] * ecolb
                y = _dot(m.astype(BF16), xb[:, ps]) + _dot_nt(cs.astype(BF16), hs.astype(BF16))
                wrow = jnp.exp2(colb[WINDOW - 1:WINDOW, :] - rowp)
                xw = (x_t[ps, :] * wrow).astype(BF16)
                hst_ref[ps, :] = ecolb[WINDOW - 1:WINDOW, :] * hs + _dot(xw, bb[:, ns])
                ys.append(y)
        y = jnp.concatenate(ys, axis=1) + dsk_ref[...] * xc
        yc = y * _silu(z_ref[rows, 0:1024])
        gw = D_MODEL // G_C
        oc = jnp.concatenate([_rms_lanes(yc[:, gw * g:gw * (g + 1)]) for g in range(G_C)], axis=1) * snw_ref[...]
        gc = _sigmoid(z_ref[rows, C_GC:C_GC + 1024])
        mixs_ref[rows, :] = (mixab_ref[0, rows, :] + gc * oc).astype(BF16)

    g1 = mod[:, 2 * D_MODEL:3 * D_MODEL]
    for c in range(tt // WINDOW):
        chunk(c)
        if c % 2 == 1:
            pr = slice(WINDOW * (c - 1), WINDOW * (c + 1))
            xo_ref[0, pr, :] = x_ref[0, pr, :] + g1 * _dot(mixs_ref[pr, :], wo_ref[...])
    ho_ref[0] = hst_ref[...]


def _pc_call(x, mod3, n1, wc, cw, cb, dtb_pad, a_pad, dsk_full, snw, mixab, wo, tt):
    B, T, _ = x.shape
    tri = jnp.asarray(np.tril(np.ones((WINDOW, WINDOW), np.float32)))
    kern = functools.partial(_pc_kernel, tt=tt)
    full2 = lambda b, t: (0, 0)
    return pl.pallas_call(
        kern,
        grid=(B, T // tt),
        in_specs=[pl.BlockSpec((1, tt, D_MODEL), lambda b, t: (b, t, 0)),
                  pl.BlockSpec((1, 1, 6 * D_MODEL), lambda b, t: (b, 0, 0)),
                  pl.BlockSpec((1, D_MODEL), full2),
                  pl.BlockSpec((D_MODEL, SEG_C), full2),
                  pl.BlockSpec((CONV_W, CONV_DIM), full2),
                  pl.BlockSpec((1, CONV_DIM), full2),
                  pl.BlockSpec((1, 128), full2),
                  pl.BlockSpec((1, 128), full2),
                  pl.BlockSpec((1, D_MODEL), full2),
                  pl.BlockSpec((1, D_MODEL), full2),
                  pl.BlockSpec((WINDOW, WINDOW), full2),
                  pl.BlockSpec((1, tt, D_MODEL), lambda b, t: (b, t, 0)),
                  pl.BlockSpec((D_MODEL, D_MODEL), full2)],
        out_specs=[pl.BlockSpec((1, tt, D_MODEL), lambda b, t: (b, t, 0)),
                   pl.BlockSpec((1, HC * HD_C, N_C), lambda b, t: (b, 0, 0)),
                   pl.BlockSpec((1, CONV_W - 1, CONV_DIM), lambda b, t: (b, 0, 0))],
        out_shape=[jax.ShapeDtypeStruct((B, T, D_MODEL), F32),
                   jax.ShapeDtypeStruct((B, HC * HD_C, N_C), F32),
                   jax.ShapeDtypeStruct((B, CONV_W - 1, CONV_DIM), F32)],
        scratch_shapes=[pltpu.VMEM((tt, SEG_C), F32),
                        pltpu.VMEM((tt + 8, CONV_DIM), F32),
                        pltpu.VMEM((tt, CONV_DIM), F32),
                        pltpu.VMEM((tt, 128), F32),
                        pltpu.VMEM((HC * HD_C, N_C), F32),
                        pltpu.VMEM((tt, D_MODEL), BF16)],
        compiler_params=_cparams(2),
        name="prompt_ssd_out",
    )(x, mod3, n1, wc, cw, cb, dtb_pad, a_pad, dsk_full, snw, tri, mixab, wo)


def _mlp_kernel(x_ref, sh_ref, sc_ref, g_ref, n2_ref, wu_ref, wd_ref, fn_ref, o_ref, *, tf, final, per_row):
    rd = (lambda r: r[...]) if per_row else (lambda r: r[0])
    x = x_ref[...]
    hb = _modnorm(x, n2_ref[...], rd(sc_ref), rd(sh_ref)).astype(BF16)
    acc = None
    for f in range(D_FF // tf):
        u = _dot(hb, wu_ref[0, :, tf * f:tf * (f + 1)])
        u = jnp.square(jnp.maximum(u, 0.0)).astype(BF16)
        part = _dot(u, wd_ref[0, tf * f:tf * (f + 1), :])
        acc = part if acc is None else acc + part
    y = x + rd(g_ref) * acc
    if final:
        y = _rms_lanes(y) * fn_ref[...]
    o_ref[...] = y


def _mlp_call(x2, mod, n2, wu, wd, fn, layer, tm, tf, rows_per_mod, final):
    M = x2.shape[0]
    per_row = rows_per_mod == 1
    if per_row:
        mspec = lambda j: pl.BlockSpec((tm, D_MODEL), lambda m: (m, j))
    else:
        mspec = lambda j: pl.BlockSpec((1, 1, D_MODEL), lambda m: ((m * tm) // rows_per_mod, 0, j))
    kern = functools.partial(_mlp_kernel, tf=tf, final=final, per_row=per_row)
    resident = pl.Buffered(1)
    return pl.pallas_call(
        kern,
        grid=(M // tm,),
        in_specs=[pl.BlockSpec((tm, D_MODEL), lambda m: (m, 0)),
                  mspec(3), mspec(4), mspec(5),
                  pl.BlockSpec((1, D_MODEL), lambda m: (0, 0)),
                  pl.BlockSpec((1, D_MODEL, D_FF), lambda m: (layer, 0, 0), pipeline_mode=resident),
                  pl.BlockSpec((1, D_FF, D_MODEL), lambda m: (layer, 0, 0), pipeline_mode=resident),
                  pl.BlockSpec((1, D_MODEL), lambda m: (0, 0))],
        out_specs=pl.BlockSpec((tm, D_MODEL), lambda m: (m, 0)),
        out_shape=jax.ShapeDtypeStruct((M, D_MODEL), F32),
        compiler_params=_cparams(1),
        name="mlp",
    )(x2, mod, mod, mod, n2, wu, wd, fn)


def _sproj_kernel(x_ref, sh_ref, sc_ref, n1_ref, wa_ref, wb_ref, wc_ref, oa_ref, ob_ref, oc_ref):
    h = _modnorm(x_ref[...], n1_ref[...], sc_ref[...], sh_ref[...]).astype(BF16)
    oa_ref[...] = _dot(h, wa_ref[...])
    ob_ref[...] = _dot(h, wb_ref[...])
    oc_ref[...] = _dot(h, wc_ref[...])


def _sproj_call(xs, mod_s, n1, wa, wb, wc):
    n = xs.shape[0]
    full = lambda i: (0, 0)
    once = pl.Buffered(1)
    wspec = lambda w: pl.BlockSpec(w.shape, full, pipeline_mode=once)
    return pl.pallas_call(
        _sproj_kernel,
        grid=(1,),
        in_specs=[pl.BlockSpec((n, D_MODEL), full),
                  pl.BlockSpec((n, D_MODEL), full),
                  pl.BlockSpec((n, D_MODEL), lambda i: (0, 1)),
                  pl.BlockSpec((1, D_MODEL), full),
                  wspec(wa), wspec(wb), wspec(wc)],
        out_specs=[pl.BlockSpec((n, w.shape[1]), full) for w in (wa, wb, wc)],
        out_shape=[jax.ShapeDtypeStruct((n, w.shape[1]), F32) for w in (wa, wb, wc)],
        compiler_params=_cparams(1),
        name="sample_proj",
    )(xs, mod_s, mod_s, n1, wa, wb, wc)


def _carry_outputs(prev, n_in, nsteps):
    nslab = 1 if prev else DEPTH
    extra = [pl.BlockSpec(memory_space=pl.ANY) for _ in prev]
    alias = {n_in + k: k for k in range(len(prev))}
    row = lambda p, i: jnp.where(p == 0, i, nsteps - 1)
    return nslab, extra, alias, row


def _with_fill(body, state_outs):
    def kern(*refs):
        p = pl.program_id(0)

        @pl.when(p == 0)
        def _():
            body(*refs)

        @pl.when(p != 0)
        def _():
            for k in state_outs:
                refs[k][...] = jnp.zeros_like(refs[k])

    return kern


def _sa_kernel(q_ref, kv_ref, ck_ref, cv_ref, bias_ref, sink_ref, *rest):
    ko_ref, vo_ref, oa_ref = rest[-3:]
    nb = ROWS_PER_SAMPLE_STEP
    gsz = HA_Q // HA_KV
    rg = lax.broadcasted_iota(jnp.int32, (HA_Q, HD_A), 0) // gsz
    sink = sink_ref[...]
    kw = HA_KV * HD_A
    scores, vmats = [], []
    for i in range(nb):
        ko_ref[0, i, 0:WINDOW - 1, :] = ck_ref[0, i, 1:WINDOW, :]
        ko_ref[0, i, WINDOW - 1:WINDOW, :] = kv_ref[i:i + 1, 0:kw]
        vo_ref[0, i, 0:WINDOW - 1, :] = cv_ref[0, i, 1:WINDOW, :]
        vo_ref[0, i, WINDOW - 1:WINDOW, :] = kv_ref[i:i + 1, kw:2 * kw]
        kmat = ko_ref[0, i].astype(BF16)
        vmats.append(vo_ref[0, i].astype(BF16))
        q = q_ref[i] * (HD_A ** -0.5)
        qe = jnp.concatenate([jnp.where(rg == g, q, 0.0) for g in range(HA_KV)], axis=1).astype(BF16)
        scores.append(_dot_nt(qe, kmat))
    probs = []
    for i in range(nb):
        sc = scores[i] + bias_ref[...]
        m = jnp.maximum(jnp.max(sc, axis=-1, keepdims=True), sink)
        p = jnp.exp(sc - m)
        den = jnp.sum(p, axis=-1, keepdims=True) + jnp.exp(sink - m)
        probs.append((p.astype(BF16), den))
    outs = [_dot(probs[i][0], vmats[i]) for i in range(nb)]
    for i in range(nb):
        o = outs[i] / probs[i][1]
        o16 = jnp.zeros((HA_Q, HD_A), F32)
        for g in range(HA_KV):
            o16 = o16 + jnp.where(rg == g, o[:, HD_A * g:HD_A * (g + 1)], 0.0)
        oa_ref[i] = o16


def _sa_call(q3, kv, cache_k, cache_v, bias_s, sink_col, layer, prev):
    n = q3.shape[0]
    nb = ROWS_PER_SAMPLE_STEP
    kw = HA_KV * HD_A
    nslab, extra, alias, row = _carry_outputs(prev, 6, n // nb)
    cspec = pl.BlockSpec((1, nb, WINDOW, kw), lambda p, i: (layer, row(p, i), 0, 0))
    ospec = pl.BlockSpec((1, nb, WINDOW, kw), lambda p, i: ((layer + p) % DEPTH, i, 0, 0))
    return pl.pallas_call(
        _with_fill(_sa_kernel, (-3, -2)),
        grid=(nslab, n // nb),
        in_specs=[pl.BlockSpec((nb, HA_Q, HD_A), lambda p, i: (row(p, i), 0, 0)),
                  pl.BlockSpec((nb, 2 * kw), lambda p, i: (row(p, i), 0)),
                  cspec, cspec,
                  pl.BlockSpec((HA_Q, WINDOW), lambda p, i: (0, 0)),
                  pl.BlockSpec((HA_Q, 1), lambda p, i: (0, 0))] + extra,
        out_specs=[ospec, ospec, pl.BlockSpec((nb, HA_Q, HD_A), lambda p, i: (row(p, i), 0, 0))],
        out_shape=[jax.ShapeDtypeStruct((DEPTH, n, WINDOW, kw), F32),
                   jax.ShapeDtypeStruct((DEPTH, n, WINDOW, kw), F32),
                   jax.ShapeDtypeStruct((n, HA_Q, HD_A), F32)],
        input_output_aliases=alias,
        compiler_params=_cparams(2),
        name="sample_attn",
    )(q3, kv, cache_k, cache_v, bias_s, sink_col, *prev)


def _sr_kernel(zb_ref, cos_ref, sin_ref, gcol_ref, s_ref, *rest):
    so_ref, o_ref = rest[-2:]
    nb = ROWS_PER_SAMPLE_STEP
    wq = HB * DK_B
    cos = cos_ref[...]
    sin = sin_ref[...]
    qf = zb_ref[:, 0:wq]
    kf = zb_ref[:, wq:2 * wq]
    qr = qf * cos + _pairswap(qf) * sin
    kr = (kf * cos + _pairswap(kf) * sin) * (DK_B ** -0.5)
    v = zb_ref[:, 2 * wq:2 * wq + HB * DV_B]
    r8 = lax.broadcasted_iota(jnp.int32, (HB, wq), 0)
    hl = lax.broadcasted_iota(jnp.int32, (HB, wq), 1) // DK_B
    rv = lax.broadcasted_iota(jnp.int32, (HB, DV_B), 0)
    outers, q8s = [], []
    for i in range(nb):
        k8 = jnp.where(hl == r8, jnp.broadcast_to(kr[i:i + 1, :], (HB, wq)), 0.0).astype(BF16)
        q8s.append(jnp.where(hl == r8, jnp.broadcast_to(qr[i:i + 1, :], (HB, wq)), 0.0).astype(BF16))
        v8 = jnp.zeros((HB, DV_B), F32)
        for r in range(HB):
            v8 = jnp.where(rv == r, jnp.broadcast_to(v[i:i + 1, DV_B * r:DV_B * (r + 1)], (HB, DV_B)), v8)
        outers.append(_dot_tn(k8, v8.astype(BF16)))
    s_news = []
    for i in range(nb):
        s_new = gcol_ref[...] * s_ref[0, i] + outers[i]
        so_ref[0, i] = s_new
        s_news.append(s_new.astype(BF16))
    for i in range(nb):
        o_ref[i] = _dot(q8s[i], s_news[i])


def _sr_call(zb, state, layer, prev):
    n = zb.shape[0]
    nb = ROWS_PER_SAMPLE_STEP
    wq = HB * DK_B
    cos, sin = _rot_tables(np.array([PAST_LEN]))
    gcol = np.repeat(_gammas(), DK_B)[:, None] * np.ones((1, DV_B))
    nslab, extra, alias, row = _carry_outputs(prev, 5, n // nb)
    full = lambda p, i: (0, 0)
    return pl.pallas_call(
        _with_fill(_sr_kernel, (-2,)),
        grid=(nslab, n // nb),
        in_specs=[pl.BlockSpec((nb, SEG_B), lambda p, i: (row(p, i), 0)),
                  pl.BlockSpec((1, wq), full),
                  pl.BlockSpec((1, wq), full),
                  pl.BlockSpec((wq, DV_B), full),
                  pl.BlockSpec((1, nb, wq, DV_B), lambda p, i: (layer, row(p, i), 0, 0))] + extra,
        out_specs=[pl.BlockSpec((1, nb, wq, DV_B), lambda p, i: ((layer + p) % DEPTH, i, 0, 0)),
                   pl.BlockSpec((nb, HB, DV_B), lambda p, i: (row(p, i), 0, 0))],
        out_shape=[jax.ShapeDtypeStruct((DEPTH, n, wq, DV_B), F32),
                   jax.ShapeDtypeStruct((n, HB, DV_B), F32)],
        input_output_aliases=alias,
        compiler_params=_cparams(2),
        name="sample_retention",
    )(zb, jnp.asarray(cos), jnp.asarray(sin), jnp.asarray(gcol.astype(np.float32)), state, *prev)


def _ss_kernel(zc_ref, cs_ref, cw_ref, cb_ref, dtb_ref, a_ref, e3_ref, h_ref, *rest):
    ho_ref, cso_ref, y_ref, xc_ref = rest[-4:]
    nb = ROWS_PER_SAMPLE_STEP
    cx = zc_ref[:, 1024:2560]
    taps = [cs_ref[0, :, CONV_DIM * i:CONV_DIM * (i + 1)] for i in range(CONV_W - 1)] + [cx]
    acc = cb_ref[...]
    for i in range(CONV_W):
        acc = acc + taps[i] * cw_ref[i:i + 1, :]
    xbc = _silu(acc)
    cso_ref[...] = jnp.concatenate(taps[1:], axis=1)
    xc = xbc[:, 0:1024]
    bmat = xbc[:, 1024:1280]
    cmat = xbc[:, 1280:1536]
    dt = _softplus(zc_ref[:, C_DT:C_DT + 128] + dtb_ref[...])
    da = jnp.exp(dt * a_ref[...])
    dt_e = _dot(jnp.concatenate(_split3(dt), axis=1), e3_ref[...])
    da_e = _dot(jnp.concatenate(_split3(da), axis=1), e3_ref[...])
    dtx = dt_e * xc
    gw = (HC // G_C) * HD_C
    r8 = lax.broadcasted_iota(jnp.int32, (nb, gw), 0)
    rn = lax.broadcasted_iota(jnp.int32, (nb, N_C), 0)
    ones8 = jnp.ones((nb, N_C), BF16)
    prods = {}
    for g in range(G_C):
        ws = slice(gw * g, gw * (g + 1))
        bg16 = bmat[:, N_C * g:N_C * (g + 1)].astype(BF16)
        for i in range(nb):
            x8 = jnp.where(r8 == i, dtx[:, ws], 0.0).astype(BF16)
            outer = _dot_tn(x8, bg16)
            hi, mid, lo = (p.astype(F32) for p in _split3(jnp.broadcast_to(da_e[i:i + 1, ws], (nb, gw))))
            l3 = jnp.where(r8 == 0, hi, jnp.where(r8 == 1, mid, jnp.where(r8 == 2, lo, 0.0)))
            prods[g, i] = (outer, _dot_tn(l3.astype(BF16), ones8))
    h16 = {}
    for g in range(G_C):
        ws = slice(gw * g, gw * (g + 1))
        for i in range(nb):
            outer, dacol = prods[g, i]
            h_new = dacol * h_ref[0, i, ws, :] + outer
            ho_ref[0, i, ws, :] = h_new
            h16[g, i] = h_new.astype(BF16)
    ycols = []
    for g in range(G_C):
        ns = slice(N_C * g, N_C * (g + 1))
        yacc = jnp.zeros((nb, gw), F32)
        for i in range(nb):
            c8 = jnp.where(rn == i, cmat[:, ns], 0.0).astype(BF16)
            yacc = yacc + _dot_nt(c8, h16[g, i])
        ycols.append(yacc)
    y_ref[...] = jnp.concatenate(ycols, axis=1)
    xc_ref[...] = xc


def _ss_call(zc, conv_state, hstate, cw, cb, dtb_pad, a_pad, layer, prev):
    n = zc.shape[0]
    nb = ROWS_PER_SAMPLE_STEP
    e = np.zeros((128, D_MODEL), np.float32)
    for hh in range(HC):
        e[hh, HD_C * hh:HD_C * (hh + 1)] = 1.0
    e3 = jnp.asarray(np.concatenate([e, e, e], axis=0), dtype=BF16)
    cwid = (CONV_W - 1) * CONV_DIM
    full = lambda p, i: (0, 0)
    nslab, extra, alias, row = _carry_outputs(prev, 8, n // nb)
    return pl.pallas_call(
        _with_fill(_ss_kernel, (-4,)),
        grid=(nslab, n // nb),
        in_specs=[pl.BlockSpec((nb, SEG_C), lambda p, i: (row(p, i), 0)),
                  pl.BlockSpec((1, nb, cwid), lambda p, i: (layer, row(p, i), 0)),
                  pl.BlockSpec((CONV_W, CONV_DIM), full),
                  pl.BlockSpec((1, CONV_DIM), full),
                  pl.BlockSpec((1, 128), full),
                  pl.BlockSpec((1, 128), full),
                  pl.BlockSpec((3 * 128, D_MODEL), full),
                  pl.BlockSpec((1, nb, HC * HD_C, N_C), lambda p, i: (layer, row(p, i), 0, 0))] + extra,
        out_specs=[pl.BlockSpec((1, nb, HC * HD_C, N_C), lambda p, i: ((layer + p) % DEPTH, i, 0, 0)),
                   pl.BlockSpec((nb, cwid), lambda p, i: (row(p, i), 0)),
                   pl.BlockSpec((nb, D_MODEL), lambda p, i: (row(p, i), 0)),
                   pl.BlockSpec((nb, D_MODEL), lambda p, i: (row(p, i), 0))],
        out_shape=[jax.ShapeDtypeStruct((DEPTH, n, HC * HD_C, N_C), F32),
                   jax.ShapeDtypeStruct((n, cwid), F32),
                   jax.ShapeDtypeStruct((n, D_MODEL), F32),
                   jax.ShapeDtypeStruct((n, D_MODEL), F32)],
        input_output_aliases=alias,
        compiler_params=_cparams(2),
        name="sample_ssd",
    )(zc, conv_state, cw, cb, dtb_pad, a_pad, e3, hstate, *prev)


def _sm_kernel(x_ref, g1_ref, oa_ref, oret_ref, y_ref, xc_ref, za_ref, zb_ref, zc_ref,
               dsk_ref, snw_ref, wo_ref, sh2_ref, sc2_ref, g2_ref, n2_ref, wu_ref, wd_ref, fn_ref, o_ref, *, tf, final):
    ob = jnp.concatenate([_rms_lanes(oret_ref[:, DV_B * hh:DV_B * (hh + 1)]) for hh in range(HB)], axis=1)
    ob = _silu(zb_ref[:, 2048:3072]) * ob
    yc = (y_ref[...] + dsk_ref[...] * xc_ref[...]) * _silu(zc_ref[:, 0:1024])
    gw = D_MODEL // G_C
    oc = jnp.concatenate([_rms_lanes(yc[:, gw * g:gw * (g + 1)]) for g in range(G_C)], axis=1) * snw_ref[...]
    mix = (_sigmoid(za_ref[:, 1536:2560]) * oa_ref[...] + _sigmoid(zb_ref[:, 3072:4096]) * ob
           + _sigmoid(zc_ref[:, C_GC:C_GC + 1024]) * oc)
    x1 = x_ref[...] + g1_ref[...] * _dot(mix.astype(BF16), wo_ref[...])
    hb = _modnorm(x1, n2_ref[...], sc2_ref[...], sh2_ref[...]).astype(BF16)
    acc = None
    for f in range(D_FF // tf):
        u = _dot(hb, wu_ref[0, :, tf * f:tf * (f + 1)])
        u = jnp.square(jnp.maximum(u, 0.0)).astype(BF16)
        part = _dot(u, wd_ref[0, tf * f:tf * (f + 1), :])
        acc = part if acc is None else acc + part
    y = x1 + g2_ref[...] * acc
    if final:
        y = _rms_lanes(y) * fn_ref[...]
    o_ref[...] = y


def _sm_call(xs, mod_s, oa, oret, y, xc, za, zb, zc, dsk_full, snw, wo, n2, wu, wd, fn, layer, tf, final):
    n = xs.shape[0]
    full = lambda i: (0, 0)
    row = lambda w: pl.BlockSpec((n, w), full)
    modcol = lambda j: pl.BlockSpec((n, D_MODEL), lambda i: (0, j))
    once = pl.Buffered(1)
    return pl.pallas_call(
        functools.partial(_sm_kernel, tf=tf, final=final),
        grid=(1,),
        in_specs=[row(D_MODEL), modcol(2),
                  row(D_MODEL), row(D_MODEL), row(D_MODEL), row(D_MODEL),
                  row(SEG_A), row(SEG_B), row(SEG_C),
                  pl.BlockSpec((1, D_MODEL), full),
                  pl.BlockSpec((1, D_MODEL), full),
                  pl.BlockSpec((D_MODEL, D_MODEL), full, pipeline_mode=once),
                  modcol(3), modcol(4), modcol(5),
                  pl.BlockSpec((1, D_MODEL), full),
                  pl.BlockSpec((1, D_MODEL, D_FF), lambda i: (layer, 0, 0), pipeline_mode=once),
                  pl.BlockSpec((1, D_FF, D_MODEL), lambda i: (layer, 0, 0), pipeline_mode=once),
                  pl.BlockSpec((1, D_MODEL), full)],
        out_specs=row(D_MODEL),
        out_shape=jax.ShapeDtypeStruct((n, D_MODEL), F32),
        compiler_params=_cparams(1),
        name="sample_merge_mlp",
    )(xs, mod_s, oa, oret, y, xc, za, zb, zc, dsk_full, snw, wo, mod_s, mod_s, mod_s, n2, wu, wd, fn)


PROMPT_TILE = 512
ATTN_TILE = 1024
MLP_ROWS = 1024
MLP_FF = 1024


def _prep_w_in(w):
    wa = jnp.concatenate([w[:, O_AQ:O_BQ], w[:, O_GTS:O_GTS + 1024]], axis=1)
    wb = jnp.concatenate([w[:, O_BQ:O_CZ], w[:, O_GTS + 1024:O_GTS + 2048]], axis=1)
    wc = jnp.concatenate([w[:, O_CZ:O_CDT], jnp.pad(w[:, O_CDT:O_GTS], ((0, 0), (0, 128 - HC))),
                          w[:, O_GTS + 2048:O_GTS + 3072]], axis=1)
    return wa, wb, wc


def _forward(x_prompt, x_sample, cache_win_k, cache_win_v, state_ret, state_ssm, state_conv,
             c_prompt, c_sample, rel_bias_table, attn_sinks, norm1_w, norm2_w, ada_w, ada_b,
             w_in, conv_w, conv_b, dt_bias, A_log, D_skip, ssm_norm_w, w_out, w_up, w_down,
             final_norm_w, *, prompt_tile, attn_tile, mlp_rows, mlp_ff):
    B, T, _ = x_prompt.shape
    DB = x_sample.shape[0]
    kw = HA_KV * HD_A

    mod_all = _ada_call(jnp.concatenate([c_prompt, c_sample], axis=0), ada_w, ada_b)

    bias_t, bias_s = _bias_tables(rel_bias_table)

    w16 = w_in.astype(BF16)
    wu16 = w_up.astype(BF16)
    wd16 = w_down.astype(BF16)
    wo16 = w_out.astype(BF16)
    fn = final_norm_w.reshape(1, D_MODEL)
    ck = cache_win_k.reshape(DEPTH, DB, WINDOW, kw)
    cv = cache_win_v.reshape(DEPTH, DB, WINDOW, kw)
    sret = state_ret.reshape(DEPTH, DB, HB * DK_B, DV_B)
    sssm = state_ssm.reshape(DEPTH, DB, HC * HD_C, N_C)
    sconv = state_conv.reshape(DEPTH, DB, (CONV_W - 1) * CONV_DIM)

    xp = x_prompt
    xs = x_sample.reshape(DB, D_MODEL)
    outs_p = [[] for _ in range(5)]
    conv_s = []
    win_kv, ret_all, ssm_all = (), (), ()
    for l in range(DEPTH):
        wl = w16[l]
        wa, wb, wc = _prep_w_in(wl)
        n1 = norm1_w[l].reshape(1, D_MODEL)
        n2 = norm2_w[l].reshape(1, D_MODEL)
        cw = conv_w[l]
        cb = conv_b[l].reshape(1, CONV_DIM)
        dtb_pad = jnp.pad(dt_bias[l], (0, 128 - HC)).reshape(1, 128)
        a_pad = jnp.pad(-jnp.exp(A_log[l].astype(F32)), (0, 128 - HC)).reshape(1, 128)
        dsk_full = jnp.repeat(D_skip[l], HD_C).reshape(1, D_MODEL)
        snw = ssm_norm_w[l].reshape(1, D_MODEL)
        final = l == DEPTH - 1
        mod_p = mod_all[l, :B].reshape(B, 1, 6 * D_MODEL)
        mod_s = mod_all[l, B:]

        wqgt = jnp.concatenate([wl[:, O_AQ:O_AK] * (HD_A ** -0.5), wl[:, O_GTS:O_GTS + D_MODEL]], axis=1).T
        wkv = wl[:, O_AK:O_BQ]
        sink_rows = jnp.repeat(attn_sinks[l], WINDOW).reshape(HA_KV, 1, (HA_Q // HA_KV) * WINDOW)
        mixa, kbuf, vbuf = _pa_call(xp, mod_p, n1, wqgt, wkv, bias_t, sink_rows, attn_tile)
        wbt = jnp.concatenate([_deinterleave_pairs(wl[:, O_BQ:O_BK]),
                               _deinterleave_pairs(wl[:, O_BK:O_BV]) * (DK_B ** -0.5), wl[:, O_BV:O_BG],
                               wl[:, O_BG:O_CZ] * 0.5, wl[:, O_GTS + D_MODEL:O_GTS + 2 * D_MODEL] * 0.5], axis=1).T
        mixab, s_perm = _pb_call(xp, mod_p, n1, wbt, mixa, attn_tile)
        s_ret = s_perm.reshape(B, HB, 2, DK_B // 2, DV_B).transpose(0, 1, 3, 2, 4)
        x1, h_ssm, conv_new = _pc_call(xp, mod_p, n1, wc, cw, cb, dtb_pad, a_pad, dsk_full, snw,
                                       mixab, wo16[l], prompt_tile)
        xp = _mlp_call(x1.reshape(B * T, D_MODEL), mod_p, n2, wu16, wd16, fn, l,
                       mlp_rows, mlp_ff, T, final).reshape(B, T, D_MODEL)
        for lst, v in zip(outs_p, (kbuf.reshape(B, WINDOW, HA_KV, HD_A), vbuf.reshape(B, WINDOW, HA_KV, HD_A),
                                   s_ret.reshape(B, HB, DK_B, DV_B), h_ssm.reshape(B, HC, HD_C, N_C), conv_new)):
            lst.append(v)

        za, zb, zc = _sproj_call(xs, mod_s, n1, wa, wb, wc)
        q3 = za[:, 0:HA_Q * HD_A].reshape(DB, HA_Q, HD_A)
        kv = za[:, HA_Q * HD_A:HA_Q * HD_A + 2 * kw]
        ck_new, cv_new, oa3 = _sa_call(q3, kv, ck, cv, bias_s, attn_sinks[l].reshape(HA_Q, 1), l, win_kv)
        win_kv = (ck_new, cv_new)
        s_new, o3 = _sr_call(zb, sret, l, ret_all)
        ret_all = (s_new,)
        h_new, cs_new, y_s, xc_s = _ss_call(zc, sconv, sssm, cw, cb, dtb_pad, a_pad, l, ssm_all)
        ssm_all = (h_new,)
        xs = _sm_call(xs, mod_s, oa3.reshape(DB, D_MODEL), o3.reshape(DB, D_MODEL), y_s, xc_s,
                      za, zb, zc, dsk_full, snw, wo16[l], n2, wu16, wd16, fn, l, mlp_ff, final)
        conv_s.append(cs_new.reshape(DB, CONV_W - 1, CONV_DIM))

    stk = lambda lst: jnp.stack(lst, axis=0)
    return (xp, xs.reshape(DB, 1, D_MODEL),
            stk(outs_p[0]), stk(outs_p[1]), stk(outs_p[2]), stk(outs_p[3]), stk(outs_p[4]),
            win_kv[0].reshape(DEPTH, DB, WINDOW, HA_KV, HD_A), win_kv[1].reshape(DEPTH, DB, WINDOW, HA_KV, HD_A),
            ret_all[0].reshape(DEPTH, DB, HB, DK_B, DV_B), ssm_all[0].reshape(DEPTH, DB, HC, HD_C, N_C),
            stk(conv_s))


def kernel(x_prompt, x_sample, cache_win_k, cache_win_v, state_ret, state_ssm, state_conv, c_prompt, c_sample,
           rel_bias_table, attn_sinks, norm1_w, norm2_w, ada_w, ada_b, w_in, conv_w, conv_b, dt_bias, A_log,
           D_skip, ssm_norm_w, w_out, w_up, w_down, final_norm_w):
    return _forward(x_prompt, x_sample, cache_win_k, cache_win_v, state_ret, state_ssm, state_conv,
                    c_prompt, c_sample, rel_bias_table, attn_sinks, norm1_w, norm2_w, ada_w, ada_b,
                    w_in, conv_w, conv_b, dt_bias, A_log, D_skip, ssm_norm_w, w_out, w_up, w_down,
                    final_norm_w, prompt_tile=PROMPT_TILE, attn_tile=ATTN_TILE, mlp_rows=MLP_ROWS, mlp_ff=MLP_FF)
```

```python
import functools
import math

import numpy as np
import jax
import jax.numpy as jnp
from jax import lax
from jax.experimental import pallas as pl
from jax.experimental.pallas import tpu as pltpu

F32 = jnp.float32
BF16 = jnp.bfloat16

D_MODEL = 1024
DEPTH = 2
PAST_LEN = 16384
WINDOW = 128
HA_Q = 16
HA_KV = 4
HD_A = 64
NUM_BUCKETS = 32
MAX_DISTANCE = WINDOW
HB = 8
DK_B = 64
DV_B = 128
HC = 16
HD_C = 64
N_C = 128
G_C = 2
CONV_W = 4
CONV_DIM = D_MODEL + 2 * G_C * N_C
D_FF = 4 * D_MODEL
EPS = 1e-6
NEG = -1e30
LOG2E = 1.4426950408889634

O_AQ, O_AK, O_AV = 0, 1024, 1280
O_BQ, O_BK, O_BV, O_BG = 1536, 2048, 2560, 3584
O_CZ, O_CXBC, O_CDT, O_GTS = 4608, 5632, 7168, 7184
SEG_A = 2560
SEG_B = 4096
SEG_C = 3712
C_DT = 2560
C_GC = 2688

VMEM_LIMIT_V7X = 56 * 1024 * 1024
ROWS_PER_SAMPLE_STEP = 16


def _cparams(n_axes):
    return pltpu.CompilerParams(dimension_semantics=("arbitrary",) * n_axes,
                                vmem_limit_bytes=VMEM_LIMIT_V7X)


def _dot(a, b):
    return jnp.dot(a, b, preferred_element_type=F32)


def _dot_nt(a, b):
    return lax.dot_general(a, b, (((1,), (1,)), ((), ())), preferred_element_type=F32)


def _dot_tn(a, b):
    return lax.dot_general(a, b, (((0,), (0,)), ((), ())), preferred_element_type=F32)


def _sigmoid(x):
    return 0.5 * (jnp.tanh(0.5 * x) + 1.0)


def _silu(x):
    return x * _sigmoid(x)


def _tanh1(half):
    return jnp.tanh(half) + 1.0


def _softplus(x):
    return jnp.maximum(x, 0.0) + jnp.log1p(jnp.exp(-jnp.abs(x)))


def _modnorm(x, nw, sc, sh):
    ms = jnp.mean(x * x, axis=-1, keepdims=True)
    return (x * lax.rsqrt(ms + EPS) * nw) * (1.0 + sc) + sh


def _rms_lanes(x):
    ms = jnp.mean(x * x, axis=-1, keepdims=True)
    return x * lax.rsqrt(ms + EPS)


def _pairswap(x):
    ax = x.ndim - 1
    n = x.shape[ax]
    lane = lax.broadcasted_iota(jnp.int32, x.shape, ax)
    nxt = pltpu.roll(x, n - 1, ax)
    prv = pltpu.roll(x, 1, ax)
    return jnp.where((lane & 1) == 0, nxt, prv)


def _split3(x):
    hi = x.astype(BF16)
    r1 = x - hi.astype(F32)
    mid = r1.astype(BF16)
    lo = (r1 - mid.astype(F32)).astype(BF16)
    return hi, mid, lo


def _gammas():
    return 1.0 - 2.0 ** (-5.0 - np.arange(HB, dtype=np.float64))


def _rot_tables(pos):
    theta = 1.0 / (10000.0 ** np.linspace(0.0, 1.0, DK_B // 2))
    ang = np.asarray(pos, np.float64)[:, None] * theta[None, :]
    cos = np.repeat(np.cos(ang), 2, axis=1)
    sin = np.repeat(np.sin(ang), 2, axis=1)
    sin[:, 0::2] *= -1.0
    return (np.tile(cos, (1, HB)).astype(np.float32), np.tile(sin, (1, HB)).astype(np.float32))


def _ret_tables():
    g = _gammas()
    L = WINDOW
    i = np.arange(L, dtype=np.float64)
    diff = i[:, None] - i[None, :]
    dm = np.where(diff >= 0, g[:, None, None] ** np.maximum(diff, 0.0), 0.0)
    qdec = np.repeat(g[None, :] ** (i[:, None] + 1.0), DK_B, axis=1)
    kdec = np.repeat(g[None, :] ** (L - 1.0 - i[:, None]), DK_B, axis=1)
    return dm.astype(np.float32), qdec.astype(np.float32), kdec.astype(np.float32)


def _t5_bucket_np(dist):
    max_exact = NUM_BUCKETS // 2
    n = np.maximum(dist, 0)
    nf = np.maximum(n, 1).astype(np.float32)
    large = max_exact + (np.log(nf / np.float32(max_exact)) / np.float32(math.log(MAX_DISTANCE / max_exact))
                         * np.float32(NUM_BUCKETS - max_exact)).astype(np.int32)
    large = np.minimum(large, NUM_BUCKETS - 1)
    return np.where(n < max_exact, n, large)


def _bias_tables(rel_table):
    gsz = HA_Q // HA_KV
    qi = np.arange(WINDOW)[None, :]
    kj = np.arange(WINDOW)[:, None]
    dist = np.where(kj > qi, qi + WINDOW - kj, qi - kj)
    onehot = _t5_bucket_np(dist)[..., None] == np.arange(NUM_BUCKETS)
    tab = rel_table.astype(F32)
    bias_kq = jnp.einsum('kqb,bh->hkq', jnp.asarray(onehot, F32), tab, precision=lax.Precision.HIGHEST)
    bias_t = bias_kq.reshape(HA_KV, gsz, WINDOW, WINDOW).transpose(0, 2, 1, 3).reshape(HA_KV, WINDOW, gsz * WINDOW)
    oh_s = _t5_bucket_np(WINDOW - 1 - np.arange(WINDOW))[:, None] == np.arange(NUM_BUCKETS)
    bias_s = jnp.einsum('jb,bh->hj', jnp.asarray(oh_s, F32), tab, precision=lax.Precision.HIGHEST)
    return bias_t, bias_s


def _ada_kernel(c_ref, w_ref, b_ref, o_ref):
    s = _silu(c_ref[...])
    o_ref[0] = _dot(s.astype(BF16), w_ref[0].astype(BF16)) + b_ref[0]


def _ada_call(c_all, ada_w, ada_b):
    n = c_all.shape[0]
    nb = 6
    return pl.pallas_call(
        _ada_kernel,
        grid=(DEPTH, nb),
        in_specs=[pl.BlockSpec((n, D_MODEL), lambda l, j: (0, 0)),
                  pl.BlockSpec((1, D_MODEL, D_MODEL), lambda l, j: (l, 0, j)),
                  pl.BlockSpec((1, 1, D_MODEL), lambda l, j: (l, 0, j))],
        out_specs=pl.BlockSpec((1, n, D_MODEL), lambda l, j: (l, 0, j)),
        out_shape=jax.ShapeDtypeStruct((DEPTH, n, 6 * D_MODEL), F32),
        compiler_params=_cparams(2),
        name="ada_mod",
    )(c_all, ada_w, ada_b.reshape(DEPTH, 1, 6 * D_MODEL))


def _pa_kernel(x_ref, mod_ref, n1_ref, wt_ref, wkv_ref, bias_ref, sink_ref, lowm_ref,
               mix_ref, ko_ref, vo_ref, zt_ref, kv_ref, kprev_ref, vtprev_ref, pen_ref, *, tt):
    t = pl.program_id(1)
    nchunk = tt // WINDOW
    kw = HA_KV * HD_A
    gsz = HA_Q // HA_KV

    @pl.when(t == 0)
    def _():
        kprev_ref[...] = jnp.zeros_like(kprev_ref)
        vtprev_ref[...] = jnp.zeros_like(vtprev_ref)
        pen_ref[...] = jnp.full(pen_ref.shape, NEG, F32)

    mod = mod_ref[0]
    h = _modnorm(x_ref[0], n1_ref[...], mod[:, D_MODEL:2 * D_MODEL], mod[:, 0:D_MODEL]).astype(BF16)
    kv_ref[...] = _dot(h, wkv_ref[...])
    for cc in range(nchunk // 2):
        z2 = _dot_nt(wt_ref[...], h[2 * WINDOW * cc:2 * WINDOW * (cc + 1), :])
        zt_ref[2 * cc] = z2[:, 0:WINDOW]
        zt_ref[2 * cc + 1] = z2[:, WINDOW:2 * WINDOW]
    qw = gsz * WINDOW
    lower = (lax.broadcasted_iota(jnp.int32, (WINDOW, qw), 0)
             > (lax.broadcasted_iota(jnp.int32, (WINDOW, qw), 1) & (WINDOW - 1)))

    def chunk(c, carry):
        r0 = pl.multiple_of(c * WINDOW, WINDOW)
        rows = pl.ds(r0, WINDOW)
        kc = kv_ref[rows, 0:kw]
        vc = kv_ref[rows, kw:2 * kw]
        vt = vc.T
        kk = jnp.concatenate([kprev_ref[...], kc], axis=0).astype(BF16)
        vvt = jnp.concatenate([vtprev_ref[...], vt], axis=1).astype(BF16)
        qt = zt_ref[c, 0:D_MODEL, :].astype(BF16)
        pen = pen_ref[0:1, :]
        s_all = []
        for g in range(HA_KV):
            gs = slice(HD_A * g, HD_A * (g + 1))
            qcat = jnp.concatenate([qt[HD_A * (gsz * g + j):HD_A * (gsz * g + j + 1), :] for j in range(gsz)], axis=1)
            s_all.append(_dot(kk[:, gs], qcat))
        p_all = []
        for g in range(HA_KV):
            sg = jnp.where(lower, s_all[g][0:WINDOW, :] + pen, s_all[g][WINDOW:2 * WINDOW, :]) + bias_ref[g]
            sink = sink_ref[g]
            m = jnp.maximum(jnp.max(sg, axis=0, keepdims=True), sink)
            pw = jnp.exp(sg - m)
            den = jnp.sum(pw, axis=0, keepdims=True) + jnp.exp(sink - m)
            pb = pw.astype(BF16)
            p_prev = pb * lowm_ref[...]
            p_all.append((jnp.concatenate([p_prev, pb - p_prev], axis=0), 1.0 / den))
        pieces = []
        for g in range(HA_KV):
            gs = slice(HD_A * g, HD_A * (g + 1))
            p, rden = p_all[g]
            ot = _dot(vvt[gs, :], p) * rden
            pieces += [ot[:, WINDOW * j:WINDOW * (j + 1)] for j in range(gsz)]
        oat = jnp.concatenate(pieces, axis=0)
        mixt = _sigmoid(zt_ref[c, D_MODEL:2 * D_MODEL, :]) * oat
        mix_ref[0, rows, :] = mixt.T
        kprev_ref[...] = kc
        vtprev_ref[...] = vt
        pen_ref[...] = jnp.zeros_like(pen_ref)
        ko_ref[0] = kc
        vo_ref[0] = vc
        return carry

    lax.fori_loop(0, nchunk, chunk, 0, unroll=True)


def _pa_call(x, mod3, n1, wqgt, wkv, bias_t, sink_rows, tt):
    B, T, _ = x.shape
    kw = HA_KV * HD_A
    qw = (HA_Q // HA_KV) * WINDOW
    kern = functools.partial(_pa_kernel, tt=tt)
    lowm = (np.arange(WINDOW)[:, None] > (np.arange(qw)[None, :] % WINDOW)).astype(np.float32)
    return pl.pallas_call(
        kern,
        grid=(B, T // tt),
        in_specs=[pl.BlockSpec((1, tt, D_MODEL), lambda b, t: (b, t, 0)),
                  pl.BlockSpec((1, 1, 6 * D_MODEL), lambda b, t: (b, 0, 0)),
                  pl.BlockSpec((1, D_MODEL), lambda b, t: (0, 0)),
                  pl.BlockSpec((2 * D_MODEL, D_MODEL), lambda b, t: (0, 0)),
                  pl.BlockSpec((D_MODEL, 2 * kw), lambda b, t: (0, 0)),
                  pl.BlockSpec((HA_KV, WINDOW, qw), lambda b, t: (0, 0, 0)),
                  pl.BlockSpec((HA_KV, 1, qw), lambda b, t: (0, 0, 0)),
                  pl.BlockSpec((WINDOW, qw), lambda b, t: (0, 0))],
        out_specs=[pl.BlockSpec((1, tt, D_MODEL), lambda b, t: (b, t, 0)),
                   pl.BlockSpec((1, WINDOW, kw), lambda b, t: (b, 0, 0)),
                   pl.BlockSpec((1, WINDOW, kw), lambda b, t: (b, 0, 0))],
        out_shape=[jax.ShapeDtypeStruct((B, T, D_MODEL), F32),
                   jax.ShapeDtypeStruct((B, WINDOW, kw), F32),
                   jax.ShapeDtypeStruct((B, WINDOW, kw), F32)],
        scratch_shapes=[pltpu.VMEM((tt // WINDOW, 2 * D_MODEL, WINDOW), F32),
                        pltpu.VMEM((tt, 2 * kw), F32),
                        pltpu.VMEM((WINDOW, kw), F32),
                        pltpu.VMEM((kw, WINDOW), F32),
                        pltpu.VMEM((8, qw), F32)],
        compiler_params=_cparams(2),
        name="prompt_attn",
    )(x, mod3, n1, wqgt, wkv, bias_t, sink_rows, jnp.asarray(lowm, BF16))


def _pb_kernel(x_ref, mod_ref, n1_ref, wt_ref, cos_ref, sin_ref, qdec_ref, kdec_ref, dm_ref, mixa_ref,
               mix_ref, so_ref, zt_ref, s_ref, *, tt, glast):
    t = pl.program_id(1)
    nchunk = tt // WINDOW
    hw = HB * DK_B // 2
    hp = DK_B // 2

    @pl.when(t == 0)
    def _():
        s_ref[...] = jnp.zeros_like(s_ref)

    mod = mod_ref[0]
    h = _modnorm(x_ref[0], n1_ref[...], mod[:, D_MODEL:2 * D_MODEL], mod[:, 0:D_MODEL]).astype(BF16)
    for cc in range(nchunk // 2):
        z2 = _dot_nt(wt_ref[...], h[2 * WINDOW * cc:2 * WINDOW * (cc + 1), :])
        zt_ref[2 * cc] = z2[:, 0:WINDOW]
        zt_ref[2 * cc + 1] = z2[:, WINDOW:2 * WINDOW]

    def head_rows(pair, hh):
        return jnp.concatenate([pair[0][hp * hh:hp * (hh + 1), :], pair[1][hp * hh:hp * (hh + 1), :]], axis=0)

    def chunk(c, carry):
        r0 = pl.multiple_of(c * WINDOW, WINDOW)
        rows = pl.ds(r0, WINDOW)
        cos = cos_ref[c]
        sin = sin_ref[c]
        q1 = zt_ref[c, 0:hw, :]
        q2 = zt_ref[c, hw:2 * hw, :]
        k1 = zt_ref[c, 2 * hw:3 * hw, :]
        k2 = zt_ref[c, 3 * hw:4 * hw, :]
        rq = (q1 * cos - q2 * sin, q1 * sin + q2 * cos)
        rk = (k1 * cos - k2 * sin, k1 * sin + k2 * cos)
        qdec = qdec_ref[...]
        kdec = kdec_ref[...]
        qb = tuple(a.astype(BF16) for a in rq)
        kb = tuple(a.astype(BF16) for a in rk)
        qd = tuple((a * qdec).astype(BF16) for a in rq)
        kd = tuple((a * kdec).astype(BF16) for a in rk)
        vt = zt_ref[c, 4 * hw:4 * hw + HB * DV_B, :].astype(BF16)
        inner, cross, supd = [], [], []
        for hh in range(HB):
            s_old = s_ref[DK_B * hh:DK_B * (hh + 1), :]
            inner.append(_dot_tn(head_rows(qb, hh), head_rows(kb, hh)))
            cross.append(_dot_tn(s_old.astype(BF16), head_rows(qd, hh)))
            supd.append(glast[hh] * s_old + _dot_nt(head_rows(kd, hh), vt[DV_B * hh:DV_B * (hh + 1), :]))
        innd = [(inner[hh] * dm_ref[hh]).astype(BF16) for hh in range(HB)]
        outs = []
        for hh in range(HB):
            ot = _dot_nt(vt[DV_B * hh:DV_B * (hh + 1), :], innd[hh]) + cross[hh]
            s_ref[DK_B * hh:DK_B * (hh + 1), :] = supd[hh]
            ms = jnp.mean(ot * ot, axis=0, keepdims=True)
            outs.append(ot * (0.5 * lax.rsqrt(ms + EPS)))
        obt = jnp.concatenate(outs, axis=0)
        bgt = zt_ref[c, 4 * hw + D_MODEL:4 * hw + 2 * D_MODEL, :]
        gbt = zt_ref[c, 4 * hw + 2 * D_MODEL:4 * hw + 3 * D_MODEL, :]
        mixt = _tanh1(gbt) * (bgt * _tanh1(bgt) * obt)
        mix_ref[0, rows, :] = mixa_ref[0, rows, :] + mixt.T
        return carry

    lax.fori_loop(0, nchunk, chunk, 0, unroll=True)
    so_ref[0] = s_ref[...]


def _deinterleave_pairs(w):
    return w.reshape(w.shape[0], HB, DK_B // 2, 2).transpose(0, 3, 1, 2).reshape(w.shape[0], HB * DK_B)


def _rot_tables_t(T):
    theta = 1.0 / (10000.0 ** np.linspace(0.0, 1.0, DK_B // 2))
    ang = theta[:, None] * np.arange(T, dtype=np.float64)[None, :]

    def lay(a):
        a = np.tile(a, (HB, 1))
        return np.ascontiguousarray(a.reshape(a.shape[0], T // WINDOW, WINDOW).transpose(1, 0, 2)).astype(np.float32)

    return lay(np.cos(ang)), lay(np.sin(ang))


def _pb_call(x, mod3, n1, wbt, mixa, tt):
    B, T, _ = x.shape
    hw = HB * DK_B // 2
    cos, sin = _rot_tables_t(T)
    dm, qdec, kdec = _ret_tables()
    qdec_t = np.ascontiguousarray(qdec[:, ::2].T)
    kdec_t = np.ascontiguousarray(kdec[:, ::2].T)
    glast = tuple(float(v) for v in (_gammas() ** WINDOW))
    kern = functools.partial(_pb_kernel, tt=tt, glast=glast)
    full2 = lambda b, t: (0, 0)
    nct = tt // WINDOW
    return pl.pallas_call(
        kern,
        grid=(B, T // tt),
        in_specs=[pl.BlockSpec((1, tt, D_MODEL), lambda b, t: (b, t, 0)),
                  pl.BlockSpec((1, 1, 6 * D_MODEL), lambda b, t: (b, 0, 0)),
                  pl.BlockSpec((1, D_MODEL), full2),
                  pl.BlockSpec((SEG_B, D_MODEL), full2, pipeline_mode=pl.Buffered(1)),
                  pl.BlockSpec((nct, hw, WINDOW), lambda b, t: (t, 0, 0)),
                  pl.BlockSpec((nct, hw, WINDOW), lambda b, t: (t, 0, 0)),
                  pl.BlockSpec((hw, WINDOW), full2),
                  pl.BlockSpec((hw, WINDOW), full2),
                  pl.BlockSpec((HB, WINDOW, WINDOW), lambda b, t: (0, 0, 0)),
                  pl.BlockSpec((1, tt, D_MODEL), lambda b, t: (b, t, 0))],
        out_specs=[pl.BlockSpec((1, tt, D_MODEL), lambda b, t: (b, t, 0)),
                   pl.BlockSpec((1, HB * DK_B, DV_B), lambda b, t: (b, 0, 0))],
        out_shape=[jax.ShapeDtypeStruct((B, T, D_MODEL), F32),
                   jax.ShapeDtypeStruct((B, HB * DK_B, DV_B), F32)],
        scratch_shapes=[pltpu.VMEM((nct, SEG_B, WINDOW), F32),
                        pltpu.VMEM((HB * DK_B, DV_B), F32)],
        compiler_params=_cparams(2),
        name="prompt_retention",
    )(x, mod3, n1, wbt, jnp.asarray(cos), jnp.asarray(sin), jnp.asarray(qdec_t), jnp.asarray(kdec_t),
      jnp.asarray(dm), mixa)


def _pc_kernel(x_ref, mod_ref, n1_ref, w_ref, cw_ref, cb_ref, dtb_ref, a_ref, dsk_ref, snw_ref, tri_ref,
               mixab_ref, wo_ref,
               xo_ref, ho_ref, co_ref,
               z_ref, xbuf_ref, xbc_ref, dt_ref, hst_ref, mixs_ref, *, tt):
    t = pl.program_id(1)

    @pl.when(t == 0)
    def _():
        xbuf_ref[0:8, :] = jnp.zeros((8, CONV_DIM), F32)
        hst_ref[...] = jnp.zeros_like(hst_ref)

    x = x_ref[0]
    mod = mod_ref[0]
    h = _modnorm(x, n1_ref[...], mod[:, D_MODEL:2 * D_MODEL], mod[:, 0:D_MODEL])
    z_ref[...] = _dot(h.astype(BF16), w_ref[...])

    xbuf_ref[8:8 + tt, :] = z_ref[:, 1024:2560]
    acc = cb_ref[...]
    for i in range(CONV_W):
        acc = acc + xbuf_ref[5 + i:5 + i + tt, :] * cw_ref[i:i + 1, :]
    xbc_ref[...] = _silu(acc)
    co_ref[0] = xbuf_ref[tt + 5:tt + 8, :]
    xbuf_ref[0:8, :] = xbuf_ref[tt:tt + 8, :]
    dt_ref[...] = _softplus(z_ref[:, C_DT:C_DT + 128] + dtb_ref[...])

    ii = lax.broadcasted_iota(jnp.int32, (WINDOW, WINDOW), 0)
    jj = lax.broadcasted_iota(jnp.int32, (WINDOW, WINDOW), 1)
    causal = ii >= jj
    hpg = HC // G_C

    def chunk(c):
        rows = slice(WINDOW * c, WINDOW * (c + 1))
        xc = xbc_ref[rows, 0:1024]
        bmat = xbc_ref[rows, 1024:1280]
        cmat = xbc_ref[rows, 1280:1536]
        dtc = dt_ref[rows, :]
        acum = jnp.dot(tri_ref[...], dtc * a_ref[...], precision=lax.Precision.HIGHEST,
                       preferred_element_type=F32)
        acum = acum * LOG2E
        acum_t = acum.T
        rowp_t = acum_t - jnp.log2(dtc.T)
        x_t = xc.T
        xb = xc.astype(BF16)
        bb = bmat.astype(BF16)
        cb16 = cmat.astype(BF16)
        ys = []
        for g in range(G_C):
            ns = slice(N_C * g, N_C * (g + 1))
            cbg = _dot_nt(cb16[:, ns], bb[:, ns])
            for hh in range(hpg * g, hpg * (g + 1)):
                ps = slice(HD_C * hh, HD_C * (hh + 1))
                colb = jnp.broadcast_to(acum[:, hh:hh + 1], (WINDOW, WINDOW))
                rowp = rowp_t[hh:hh + 1, :]
                m = cbg * jnp.exp2(jnp.where(causal, colb - rowp, NEG))
                hs = hst_ref[ps, :]
                ecolb = jnp.exp2(colb)
                cs = cmat[:, ns] * ecolb
                y = _dot(m.astype(BF16), xb[:, ps]) + _dot_nt(cs.astype(BF16), hs.astype(BF16))
                wrow = jnp.exp2(colb[WINDOW - 1:WINDOW, :] - rowp)
                xw = (x_t[ps, :] * wrow).astype(BF16)
                hst_ref[ps, :] = ecolb[WINDOW - 1:WINDOW, :] * hs + _dot(xw, bb[:, ns])
                ys.append(y)
        y = jnp.concatenate(ys, axis=1) + dsk_ref[...] * xc
        yc = y * _silu(z_ref[rows, 0:1024])
        gw = D_MODEL // G_C
        oc = jnp.concatenate([_rms_lanes(yc[:, gw * g:gw * (g + 1)]) for g in range(G_C)], axis=1) * snw_ref[...]
        gc = _sigmoid(z_ref[rows, C_GC:C_GC + 1024])
        mixs_ref[rows, :] = (mixab_ref[0, rows, :] + gc * oc).astype(BF16)

    g1 = mod[:, 2 * D_MODEL:3 * D_MODEL]
    for c in range(tt // WINDOW):
        chunk(c)
        if c % 2 == 1:
            pr = slice(WINDOW * (c - 1), WINDOW * (c + 1))
            xo_ref[0, pr, :] = x_ref[0, pr, :] + g1 * _dot(mixs_ref[pr, :], wo_ref[...])
    ho_ref[0] = hst_ref[...]


def _pc_call(x, mod3, n1, wc, cw, cb, dtb_pad, a_pad, dsk_full, snw, mixab, wo, tt):
    B, T, _ = x.shape
    tri = jnp.asarray(np.tril(np.ones((WINDOW, WINDOW), np.float32)))
    kern = functools.partial(_pc_kernel, tt=tt)
    full2 = lambda b, t: (0, 0)
    return pl.pallas_call(
        kern,
        grid=(B, T // tt),
        in_specs=[pl.BlockSpec((1, tt, D_MODEL), lambda b, t: (b, t, 0)),
                  pl.BlockSpec((1, 1, 6 * D_MODEL), lambda b, t: (b, 0, 0)),
                  pl.BlockSpec((1, D_MODEL), full2),
                  pl.BlockSpec((D_MODEL, SEG_C), full2),
                  pl.BlockSpec((CONV_W, CONV_DIM), full2),
                  pl.BlockSpec((1, CONV_DIM), full2),
                  pl.BlockSpec((1, 128), full2),
                  pl.BlockSpec((1, 128), full2),
                  pl.BlockSpec((1, D_MODEL), full2),
                  pl.BlockSpec((1, D_MODEL), full2),
                  pl.BlockSpec((WINDOW, WINDOW), full2),
                  pl.BlockSpec((1, tt, D_MODEL), lambda b, t: (b, t, 0)),
                  pl.BlockSpec((D_MODEL, D_MODEL), full2)],
        out_specs=[pl.BlockSpec((1, tt, D_MODEL), lambda b, t: (b, t, 0)),
                   pl.BlockSpec((1, HC * HD_C, N_C), lambda b, t: (b, 0, 0)),
                   pl.BlockSpec((1, CONV_W - 1, CONV_DIM), lambda b, t: (b, 0, 0))],
        out_shape=[jax.ShapeDtypeStruct((B, T, D_MODEL), F32),
                   jax.ShapeDtypeStruct((B, HC * HD_C, N_C), F32),
                   jax.ShapeDtypeStruct((B, CONV_W - 1, CONV_DIM), F32)],
        scratch_shapes=[pltpu.VMEM((tt, SEG_C), F32),
                        pltpu.VMEM((tt + 8, CONV_DIM), F32),
                        pltpu.VMEM((tt, CONV_DIM), F32),
                        pltpu.VMEM((tt, 128), F32),
                        pltpu.VMEM((HC * HD_C, N_C), F32),
                        pltpu.VMEM((tt, D_MODEL), BF16)],
        compiler_params=_cparams(2),
        name="prompt_ssd_out",
    )(x, mod3, n1, wc, cw, cb, dtb_pad, a_pad, dsk_full, snw, tri, mixab, wo)


def _mlp_kernel(x_ref, sh_ref, sc_ref, g_ref, n2_ref, wu_ref, wd_ref, fn_ref, o_ref, *, tf, final, per_row):
    rd = (lambda r: r[...]) if per_row else (lambda r: r[0])
    x = x_ref[...]
    hb = _modnorm(x, n2_ref[...], rd(sc_ref), rd(sh_ref)).astype(BF16)
    acc = None
    for f in range(D_FF // tf):
        u = _dot(hb, wu_ref[0, :, tf * f:tf * (f + 1)])
        u = jnp.square(jnp.maximum(u, 0.0)).astype(BF16)
        part = _dot(u, wd_ref[0, tf * f:tf * (f + 1), :])
        acc = part if acc is None else acc + part
    y = x + rd(g_ref) * acc
    if final:
        y = _rms_lanes(y) * fn_ref[...]
    o_ref[...] = y


def _mlp_call(x2, mod, n2, wu, wd, fn, layer, tm, tf, rows_per_mod, final):
    M = x2.shape[0]
    per_row = rows_per_mod == 1
    if per_row:
        mspec = lambda j: pl.BlockSpec((tm, D_MODEL), lambda m: (m, j))
    else:
        mspec = lambda j: pl.BlockSpec((1, 1, D_MODEL), lambda m: ((m * tm) // rows_per_mod, 0, j))
    kern = functools.partial(_mlp_kernel, tf=tf, final=final, per_row=per_row)
    resident = pl.Buffered(1)
    return pl.pallas_call(
        kern,
        grid=(M // tm,),
        in_specs=[pl.BlockSpec((tm, D_MODEL), lambda m: (m, 0)),
                  mspec(3), mspec(4), mspec(5),
                  pl.BlockSpec((1, D_MODEL), lambda m: (0, 0)),
                  pl.BlockSpec((1, D_MODEL, D_FF), lambda m: (layer, 0, 0), pipeline_mode=resident),
                  pl.BlockSpec((1, D_FF, D_MODEL), lambda m: (layer, 0, 0), pipeline_mode=resident),
                  pl.BlockSpec((1, D_MODEL), lambda m: (0, 0))],
        out_specs=pl.BlockSpec((tm, D_MODEL), lambda m: (m, 0)),
        out_shape=jax.ShapeDtypeStruct((M, D_MODEL), F32),
        compiler_params=_cparams(1),
        name="mlp",
    )(x2, mod, mod, mod, n2, wu, wd, fn)


def _sproj_kernel(x_ref, sh_ref, sc_ref, n1_ref, wa_ref, wb_ref, wc_ref, oa_ref, ob_ref, oc_ref):
    h = _modnorm(x_ref[...], n1_ref[...], sc_ref[...], sh_ref[...]).astype(BF16)
    oa_ref[...] = _dot(h, wa_ref[...])
    ob_ref[...] = _dot(h, wb_ref[...])
    oc_ref[...] = _dot(h, wc_ref[...])


def _sproj_call(xs, mod_s, n1, wa, wb, wc):
    n = xs.shape[0]
    full = lambda i: (0, 0)
    once = pl.Buffered(1)
    wspec = lambda w: pl.BlockSpec(w.shape, full, pipeline_mode=once)
    return pl.pallas_call(
        _sproj_kernel,
        grid=(1,),
        in_specs=[pl.BlockSpec((n, D_MODEL), full),
                  pl.BlockSpec((n, D_MODEL), full),
                  pl.BlockSpec((n, D_MODEL), lambda i: (0, 1)),
                  pl.BlockSpec((1, D_MODEL), full),
                  wspec(wa), wspec(wb), wspec(wc)],
        out_specs=[pl.BlockSpec((n, w.shape[1]), full) for w in (wa, wb, wc)],
        out_shape=[jax.ShapeDtypeStruct((n, w.shape[1]), F32) for w in (wa, wb, wc)],
        compiler_params=_cparams(1),
        name="sample_proj",
    )(xs, mod_s, mod_s, n1, wa, wb, wc)


def _carry_outputs(prev, n_in, nsteps):
    nslab = 1 if prev else DEPTH
    extra = [pl.BlockSpec(memory_space=pl.ANY) for _ in prev]
    alias = {n_in + k: k for k in range(len(prev))}
    row = lambda p, i: jnp.where(p == 0, i, nsteps - 1)
    return nslab, extra, alias, row


def _with_fill(body, state_outs):
    def kern(*refs):
        p = pl.program_id(0)

        @pl.when(p == 0)
        def _():
            body(*refs)

        @pl.when(p != 0)
        def _():
            for k in state_outs:
                refs[k][...] = jnp.zeros_like(refs[k])

    return kern


def _sa_kernel(q_ref, kv_ref, ck_ref, cv_ref, bias_ref, sink_ref, *rest):
    ko_ref, vo_ref, oa_ref = rest[-3:]
    nb = ROWS_PER_SAMPLE_STEP
    gsz = HA_Q // HA_KV
    rg = lax.broadcasted_iota(jnp.int32, (HA_Q, HD_A), 0) // gsz
    sink = sink_ref[...]
    kw = HA_KV * HD_A
    scores, vmats = [], []
    for i in range(nb):
        ko_ref[0, i, 0:WINDOW - 1, :] = ck_ref[0, i, 1:WINDOW, :]
        ko_ref[0, i, WINDOW - 1:WINDOW, :] = kv_ref[i:i + 1, 0:kw]
        vo_ref[0, i, 0:WINDOW - 1, :] = cv_ref[0, i, 1:WINDOW, :]
        vo_ref[0, i, WINDOW - 1:WINDOW, :] = kv_ref[i:i + 1, kw:2 * kw]
        kmat = ko_ref[0, i].astype(BF16)
        vmats.append(vo_ref[0, i].astype(BF16))
        q = q_ref[i] * (HD_A ** -0.5)
        qe = jnp.concatenate([jnp.where(rg == g, q, 0.0) for g in range(HA_KV)], axis=1).astype(BF16)
        scores.append(_dot_nt(qe, kmat))
    probs = []
    for i in range(nb):
        sc = scores[i] + bias_ref[...]
        m = jnp.maximum(jnp.max(sc, axis=-1, keepdims=True), sink)
        p = jnp.exp(sc - m)
        den = jnp.sum(p, axis=-1, keepdims=True) + jnp.exp(sink - m)
        probs.append((p.astype(BF16), den))
    outs = [_dot(probs[i][0], vmats[i]) for i in range(nb)]
    for i in range(nb):
        o = outs[i] / probs[i][1]
        o16 = jnp.zeros((HA_Q, HD_A), F32)
        for g in range(HA_KV):
            o16 = o16 + jnp.where(rg == g, o[:, HD_A * g:HD_A * (g + 1)], 0.0)
        oa_ref[i] = o16


def _sa_call(q3, kv, cache_k, cache_v, bias_s, sink_col, layer, prev):
    n = q3.shape[0]
    nb = ROWS_PER_SAMPLE_STEP
    kw = HA_KV * HD_A
    nslab, extra, alias, row = _carry_outputs(prev, 6, n // nb)
    cspec = pl.BlockSpec((1, nb, WINDOW, kw), lambda p, i: (layer, row(p, i), 0, 0))
    ospec = pl.BlockSpec((1, nb, WINDOW, kw), lambda p, i: ((layer + p) % DEPTH, i, 0, 0))
    return pl.pallas_call(
        _with_fill(_sa_kernel, (-3, -2)),
        grid=(nslab, n // nb),
        in_specs=[pl.BlockSpec((nb, HA_Q, HD_A), lambda p, i: (row(p, i), 0, 0)),
                  pl.BlockSpec((nb, 2 * kw), lambda p, i: (row(p, i), 0)),
                  cspec, cspec,
                  pl.BlockSpec((HA_Q, WINDOW), lambda p, i: (0, 0)),
                  pl.BlockSpec((HA_Q, 1), lambda p, i: (0, 0))] + extra,
        out_specs=[ospec, ospec, pl.BlockSpec((nb, HA_Q, HD_A), lambda p, i: (row(p, i), 0, 0))],
        out_shape=[jax.ShapeDtypeStruct((DEPTH, n, WINDOW, kw), F32),
                   jax.ShapeDtypeStruct((DEPTH, n, WINDOW, kw), F32),
                   jax.ShapeDtypeStruct((n, HA_Q, HD_A), F32)],
        input_output_aliases=alias,
        compiler_params=_cparams(2),
        name="sample_attn",
    )(q3, kv, cache_k, cache_v, bias_s, sink_col, *prev)


def _sr_kernel(zb_ref, cos_ref, sin_ref, gcol_ref, s_ref, *rest):
    so_ref, o_ref = rest[-2:]
    nb = ROWS_PER_SAMPLE_STEP
    wq = HB * DK_B
    cos = cos_ref[...]
    sin = sin_ref[...]
    qf = zb_ref[:, 0:wq]
    kf = zb_ref[:, wq:2 * wq]
    qr = qf * cos + _pairswap(qf) * sin
    kr = (kf * cos + _pairswap(kf) * sin) * (DK_B ** -0.5)
    v = zb_ref[:, 2 * wq:2 * wq + HB * DV_B]
    r8 = lax.broadcasted_iota(jnp.int32, (HB, wq), 0)
    hl = lax.broadcasted_iota(jnp.int32, (HB, wq), 1) // DK_B
    rv = lax.broadcasted_iota(jnp.int32, (HB, DV_B), 0)
    outers, q8s = [], []
    for i in range(nb):
        k8 = jnp.where(hl == r8, jnp.broadcast_to(kr[i:i + 1, :], (HB, wq)), 0.0).astype(BF16)
        q8s.append(jnp.where(hl == r8, jnp.broadcast_to(qr[i:i + 1, :], (HB, wq)), 0.0).astype(BF16))
        v8 = jnp.zeros((HB, DV_B), F32)
        for r in range(HB):
            v8 = jnp.where(rv == r, jnp.broadcast_to(v[i:i + 1, DV_B * r:DV_B * (r + 1)], (HB, DV_B)), v8)
        outers.append(_dot_tn(k8, v8.astype(BF16)))
    s_news = []
    for i in range(nb):
        s_new = gcol_ref[...] * s_ref[0, i] + outers[i]
        so_ref[0, i] = s_new
        s_news.append(s_new.astype(BF16))
    for i in range(nb):
        o_ref[i] = _dot(q8s[i], s_news[i])


def _sr_call(zb, state, layer, prev):
    n = zb.shape[0]
    nb = ROWS_PER_SAMPLE_STEP
    wq = HB * DK_B
    cos, sin = _rot_tables(np.array([PAST_LEN]))
    gcol = np.repeat(_gammas(), DK_B)[:, None] * np.ones((1, DV_B))
    nslab, extra, alias, row = _carry_outputs(prev, 5, n // nb)
    full = lambda p, i: (0, 0)
    return pl.pallas_call(
        _with_fill(_sr_kernel, (-2,)),
        grid=(nslab, n // nb),
        in_specs=[pl.BlockSpec((nb, SEG_B), lambda p, i: (row(p, i), 0)),
                  pl.BlockSpec((1, wq), full),
                  pl.BlockSpec((1, wq), full),
                  pl.BlockSpec((wq, DV_B), full),
                  pl.BlockSpec((1, nb, wq, DV_B), lambda p, i: (layer, row(p, i), 0, 0))] + extra,
        out_specs=[pl.BlockSpec((1, nb, wq, DV_B), lambda p, i: ((layer + p) % DEPTH, i, 0, 0)),
                   pl.BlockSpec((nb, HB, DV_B), lambda p, i: (row(p, i), 0, 0))],
        out_shape=[jax.ShapeDtypeStruct((DEPTH, n, wq, DV_B), F32),
                   jax.ShapeDtypeStruct((n, HB, DV_B), F32)],
        input_output_aliases=alias,
        compiler_params=_cparams(2),
        name="sample_retention",
    )(zb, jnp.asarray(cos), jnp.asarray(sin), jnp.asarray(gcol.astype(np.float32)), state, *prev)


def _ss_kernel(zc_ref, cs_ref, cw_ref, cb_ref, dtb_ref, a_ref, e3_ref, h_ref, *rest):
    ho_ref, cso_ref, y_ref, xc_ref = rest[-4:]
    nb = ROWS_PER_SAMPLE_STEP
    cx = zc_ref[:, 1024:2560]
    taps = [cs_ref[0, :, CONV_DIM * i:CONV_DIM * (i + 1)] for i in range(CONV_W - 1)] + [cx]
    acc = cb_ref[...]
    for i in range(CONV_W):
        acc = acc + taps[i] * cw_ref[i:i + 1, :]
    xbc = _silu(acc)
    cso_ref[...] = jnp.concatenate(taps[1:], axis=1)
    xc = xbc[:, 0:1024]
    bmat = xbc[:, 1024:1280]
    cmat = xbc[:, 1280:1536]
    dt = _softplus(zc_ref[:, C_DT:C_DT + 128] + dtb_ref[...])
    da = jnp.exp(dt * a_ref[...])
    dt_e = _dot(jnp.concatenate(_split3(dt), axis=1), e3_ref[...])
    da_e = _dot(jnp.concatenate(_split3(da), axis=1), e3_ref[...])
    dtx = dt_e * xc
    gw = (HC // G_C) * HD_C
    r8 = lax.broadcasted_iota(jnp.int32, (nb, gw), 0)
    rn = lax.broadcasted_iota(jnp.int32, (nb, N_C), 0)
    ones8 = jnp.ones((nb, N_C), BF16)
    prods = {}
    for g in range(G_C):
        ws = slice(gw * g, gw * (g + 1))
        bg16 = bmat[:, N_C * g:N_C * (g + 1)].astype(BF16)
        for i in range(nb):
            x8 = jnp.where(r8 == i, dtx[:, ws], 0.0).astype(BF16)
            outer = _dot_tn(x8, bg16)
            hi, mid, lo = (p.astype(F32) for p in _split3(jnp.broadcast_to(da_e[i:i + 1, ws], (nb, gw))))
            l3 = jnp.where(r8 == 0, hi, jnp.where(r8 == 1, mid, jnp.where(r8 == 2, lo, 0.0)))
            prods[g, i] = (outer, _dot_tn(l3.astype(BF16), ones8))
    h16 = {}
    for g in range(G_C):
        ws = slice(gw * g, gw * (g + 1))
        for i in range(nb):
            outer, dacol = prods[g, i]
            h_new = dacol * h_ref[0, i, ws, :] + outer
            ho_ref[0, i, ws, :] = h_new
            h16[g, i] = h_new.astype(BF16)
    ycols = []
    for g in range(G_C):
        ns = slice(N_C * g, N_C * (g + 1))
        yacc = jnp.zeros((nb, gw), F32)
        for i in range(nb):
            c8 = jnp.where(rn == i, cmat[:, ns], 0.0).astype(BF16)
            yacc = yacc + _dot_nt(c8, h16[g, i])
        ycols.append(yacc)
    y_ref[...] = jnp.concatenate(ycols, axis=1)
    xc_ref[...] = xc


def _ss_call(zc, conv_state, hstate, cw, cb, dtb_pad, a_pad, layer, prev):
    n = zc.shape[0]
    nb = ROWS_PER_SAMPLE_STEP
    e = np.zeros((128, D_MODEL), np.float32)
    for hh in range(HC):
        e[hh, HD_C * hh:HD_C * (hh + 1)] = 1.0
    e3 = jnp.asarray(np.concatenate([e, e, e], axis=0), dtype=BF16)
    cwid = (CONV_W - 1) * CONV_DIM
    full = lambda p, i: (0, 0)
    nslab, extra, alias, row = _carry_outputs(prev, 8, n // nb)
    return pl.pallas_call(
        _with_fill(_ss_kernel, (-4,)),
        grid=(nslab, n // nb),
        in_specs=[pl.BlockSpec((nb, SEG_C), lambda p, i: (row(p, i), 0)),
                  pl.BlockSpec((1, nb, cwid), lambda p, i: (layer, row(p, i), 0)),
                  pl.BlockSpec((CONV_W, CONV_DIM), full),
                  pl.BlockSpec((1, CONV_DIM), full),
                  pl.BlockSpec((1, 128), full),
                  pl.BlockSpec((1, 128), full),
                  pl.BlockSpec((3 * 128, D_MODEL), full),
                  pl.BlockSpec((1, nb, HC * HD_C, N_C), lambda p, i: (layer, row(p, i), 0, 0))] + extra,
        out_specs=[pl.BlockSpec((1, nb, HC * HD_C, N_C), lambda p, i: ((layer + p) % DEPTH, i, 0, 0)),
                   pl.BlockSpec((nb, cwid), lambda p, i: (row(p, i), 0)),
                   pl.BlockSpec((nb, D_MODEL), lambda p, i: (row(p, i), 0)),
                   pl.BlockSpec((nb, D_MODEL), lambda p, i: (row(p, i), 0))],
        out_shape=[jax.ShapeDtypeStruct((DEPTH, n, HC * HD_C, N_C), F32),
                   jax.ShapeDtypeStruct((n, cwid), F32),
                   jax.ShapeDtypeStruct((n, D_MODEL), F32),
                   jax.ShapeDtypeStruct((n, D_MODEL), F32)],
        input_output_aliases=alias,
        compiler_params=_cparams(2),
        name="sample_ssd",
    )(zc, conv_state, cw, cb, dtb_pad, a_pad, e3, hstate, *prev)


def _sm_kernel(x_ref, g1_ref, oa_ref, oret_ref, y_ref, xc_ref, za_ref, zb_ref, zc_ref,
               dsk_ref, snw_ref, wo_ref, sh2_ref, sc2_ref, g2_ref, n2_ref, wu_ref, wd_ref, fn_ref, o_ref, *, tf, final):
    ob = jnp.concatenate([_rms_lanes(oret_ref[:, DV_B * hh:DV_B * (hh + 1)]) for hh in range(HB)], axis=1)
    ob = _silu(zb_ref[:, 2048:3072]) * ob
    yc = (y_ref[...] + dsk_ref[...] * xc_ref[...]) * _silu(zc_ref[:, 0:1024])
    gw = D_MODEL // G_C
    oc = jnp.concatenate([_rms_lanes(yc[:, gw * g:gw * (g + 1)]) for g in range(G_C)], axis=1) * snw_ref[...]
    mix = (_sigmoid(za_ref[:, 1536:2560]) * oa_ref[...] + _sigmoid(zb_ref[:, 3072:4096]) * ob
           + _sigmoid(zc_ref[:, C_GC:C_GC + 1024]) * oc)
    x1 = x_ref[...] + g1_ref[...] * _dot(mix.astype(BF16), wo_ref[...])
    hb = _modnorm(x1, n2_ref[...], sc2_ref[...], sh2_ref[...]).astype(BF16)
    acc = None
    for f in range(D_FF // tf):
        u = _dot(hb, wu_ref[0, :, tf * f:tf * (f + 1)])
        u = jnp.square(jnp.maximum(u, 0.0)).astype(BF16)
        part = _dot(u, wd_ref[0, tf * f:tf * (f + 1), :])
        acc = part if acc is None else acc + part
    y = x1 + g2_ref[...] * acc
    if final:
        y = _rms_lanes(y) * fn_ref[...]
    o_ref[...] = y


def _sm_call(xs, mod_s, oa, oret, y, xc, za, zb, zc, dsk_full, snw, wo, n2, wu, wd, fn, layer, tf, final):
    n = xs.shape[0]
    full = lambda i: (0, 0)
    row = lambda w: pl.BlockSpec((n, w), full)
    modcol = lambda j: pl.BlockSpec((n, D_MODEL), lambda i: (0, j))
    once = pl.Buffered(1)
    return pl.pallas_call(
        functools.partial(_sm_kernel, tf=tf, final=final),
        grid=(1,),
        in_specs=[row(D_MODEL), modcol(2),
                  row(D_MODEL), row(D_MODEL), row(D_MODEL), row(D_MODEL),
                  row(SEG_A), row(SEG_B), row(SEG_C),
                  pl.BlockSpec((1, D_MODEL), full),
                  pl.BlockSpec((1, D_MODEL), full),
                  pl.BlockSpec((D_MODEL, D_MODEL), full, pipeline_mode=once),
                  modcol(3), modcol(4), modcol(5),
                  pl.BlockSpec((1, D_MODEL), full),
                  pl.BlockSpec((1, D_MODEL, D_FF), lambda i: (layer, 0, 0), pipeline_mode=once),
                  pl.BlockSpec((1, D_FF, D_MODEL), lambda i: (layer, 0, 0), pipeline_mode=once),
                  pl.BlockSpec((1, D_MODEL), full)],
        out_specs=row(D_MODEL),
        out_shape=jax.ShapeDtypeStruct((n, D_MODEL), F32),
        compiler_params=_cparams(1),
        name="sample_merge_mlp",
    )(xs, mod_s, oa, oret, y, xc, za, zb, zc, dsk_full, snw, wo, mod_s, mod_s, mod_s, n2, wu, wd, fn)


PROMPT_TILE = 512
ATTN_TILE = 1024
MLP_ROWS = 1024
MLP_FF = 1024


def _prep_w_in(w):
    wa = jnp.concatenate([w[:, O_AQ:O_BQ], w[:, O_GTS:O_GTS + 1024]], axis=1)
    wb = jnp.concatenate([w[:, O_BQ:O_CZ], w[:, O_GTS + 1024:O_GTS + 2048]], axis=1)
    wc = jnp.concatenate([w[:, O_CZ:O_CDT], jnp.pad(w[:, O_CDT:O_GTS], ((0, 0), (0, 128 - HC))),
                          w[:, O_GTS + 2048:O_GTS + 3072]], axis=1)
    return wa, wb, wc


def _forward(x_prompt, x_sample, cache_win_k, cache_win_v, state_ret, state_ssm, state_conv,
             c_prompt, c_sample, rel_bias_table, attn_sinks, norm1_w, norm2_w, ada_w, ada_b,
             w_in, conv_w, conv_b, dt_bias, A_log, D_skip, ssm_norm_w, w_out, w_up, w_down,
             final_norm_w, *, prompt_tile, attn_tile, mlp_rows, mlp_ff):
    B, T, _ = x_prompt.shape
    DB = x_sample.shape[0]
    kw = HA_KV * HD_A

    mod_all = _ada_call(jnp.concatenate([c_prompt, c_sample], axis=0), ada_w, ada_b)

    bias_t, bias_s = _bias_tables(rel_bias_table)

    w16 = w_in.astype(BF16)
    wu16 = w_up.astype(BF16)
    wd16 = w_down.astype(BF16)
    wo16 = w_out.astype(BF16)
    fn = final_norm_w.reshape(1, D_MODEL)
    ck = cache_win_k.reshape(DEPTH, DB, WINDOW, kw)
    cv = cache_win_v.reshape(DEPTH, DB, WINDOW, kw)
    sret = state_ret.reshape(DEPTH, DB, HB * DK_B, DV_B)
    sssm = state_ssm.reshape(DEPTH, DB, HC * HD_C, N_C)
    sconv = state_conv.reshape(DEPTH, DB, (CONV_W - 1) * CONV_DIM)

    xp = x_prompt
    xs = x_sample.reshape(DB, D_MODEL)
    outs_p = [[] for _ in range(5)]
    conv_s = []
    win_kv, ret_all, ssm_all = (), (), ()
    for l in range(DEPTH):
        wl = w16[l]
        wa, wb, wc = _prep_w_in(wl)
        n1 = norm1_w[l].reshape(1, D_MODEL)
        n2 = norm2_w[l].reshape(1, D_MODEL)
        cw = conv_w[l]
        cb = conv_b[l].reshape(1, CONV_DIM)
        dtb_pad = jnp.pad(dt_bias[l], (0, 128 - HC)).reshape(1, 128)
        a_pad = jnp.pad(-jnp.exp(A_log[l].astype(F32)), (0, 128 - HC)).reshape(1, 128)
        dsk_full = jnp.repeat(D_skip[l], HD_C).reshape(1, D_MODEL)
        snw = ssm_norm_w[l].reshape(1, D_MODEL)
        final = l == DEPTH - 1
        mod_p = mod_all[l, :B].reshape(B, 1, 6 * D_MODEL)
        mod_s = mod_all[l, B:]

        wqgt = jnp.concatenate([wl[:, O_AQ:O_AK] * (HD_A ** -0.5), wl[:, O_GTS:O_GTS + D_MODEL]], axis=1).T
        wkv = wl[:, O_AK:O_BQ]
        sink_rows = jnp.repeat(attn_sinks[l], WINDOW).reshape(HA_KV, 1, (HA_Q // HA_KV) * WINDOW)
        mixa, kbuf, vbuf = _pa_call(xp, mod_p, n1, wqgt, wkv, bias_t, sink_rows, attn_tile)
        wbt = jnp.concatenate([_deinterleave_pairs(wl[:, O_BQ:O_BK]),
                               _deinterleave_pairs(wl[:, O_BK:O_BV]) * (DK_B ** -0.5), wl[:, O_BV:O_BG],
                               wl[:, O_BG:O_CZ] * 0.5, wl[:, O_GTS + D_MODEL:O_GTS + 2 * D_MODEL] * 0.5], axis=1).T
        mixab, s_perm = _pb_call(xp, mod_p, n1, wbt, mixa, attn_tile)
        s_ret = s_perm.reshape(B, HB, 2, DK_B // 2, DV_B).transpose(0, 1, 3, 2, 4)
        x1, h_ssm, conv_new = _pc_call(xp, mod_p, n1, wc, cw, cb, dtb_pad, a_pad, dsk_full, snw,
                                       mixab, wo16[l], prompt_tile)
        xp = _mlp_call(x1.reshape(B * T, D_MODEL), mod_p, n2, wu16, wd16, fn, l,
                       mlp_rows, mlp_ff, T, final).reshape(B, T, D_MODEL)
        for lst, v in zip(outs_p, (kbuf.reshape(B, WINDOW, HA_KV, HD_A), vbuf.reshape(B, WINDOW, HA_KV, HD_A),
                                   s_ret.reshape(B, HB, DK_B, DV_B), h_ssm.reshape(B, HC, HD_C, N_C), conv_new)):
            lst.append(v)

        za, zb, zc = _sproj_call(xs, mod_s, n1, wa, wb, wc)
        q3 = za[:, 0:HA_Q * HD_A].reshape(DB, HA_Q, HD_A)
        kv = za[:, HA_Q * HD_A:HA_Q * HD_A + 2 * kw]
        ck_new, cv_new, oa3 = _sa_call(q3, kv, ck, cv, bias_s, attn_sinks[l].reshape(HA_Q, 1), l, win_kv)
        win_kv = (ck_new, cv_new)
        s_new, o3 = _sr_call(zb, sret, l, ret_all)
        ret_all = (s_new,)
        h_new, cs_new, y_s, xc_s = _ss_call(zc, sconv, sssm, cw, cb, dtb_pad, a_pad, l, ssm_all)
        ssm_all = (h_new,)
        xs = _sm_call(xs, mod_s, oa3.reshape(DB, D_MODEL), o3.reshape(DB, D_MODEL), y_s, xc_s,
                      za, zb, zc, dsk_full, snw, wo16[l], n2, wu16, wd16, fn, l, mlp_ff, final)
        conv_s.append(cs_new.reshape(DB, CONV_W - 1, CONV_DIM))

    stk = lambda lst: jnp.stack(lst, axis=0)
    return (xp, xs.reshape(DB, 1, D_MODEL),
            stk(outs_p[0]), stk(outs_p[1]), stk(outs_p[2]), stk(outs_p[3]), stk(outs_p[4]),
            win_kv[0].reshape(DEPTH, DB, WINDOW, HA_KV, HD_A), win_kv[1].reshape(DEPTH, DB, WINDOW, HA_KV, HD_A),
            ret_all[0].reshape(DEPTH, DB, HB, DK_B, DV_B), ssm_all[0].reshape(DEPTH, DB, HC, HD_C, N_C),
            stk(conv_s))


def kernel(x_prompt, x_sample, cache_win_k, cache_win_v, state_ret, state_ssm, state_conv, c_prompt, c_sample,
           rel_bias_table, attn_sinks, norm1_w, norm2_w, ada_w, ada_b, w_in, conv_w, conv_b, dt_bias, A_log,
           D_skip, ssm_norm_w, w_out, w_up, w_down, final_norm_w):
    return _forward(x_prompt, x_sample, cache_win_k, cache_win_v, state_ret, state_ssm, state_conv,
                    c_prompt, c_sample, rel_bias_table, attn_sinks, norm1_w, norm2_w, ada_w, ada_b,
                    w_in, conv_w, conv_b, dt_bias, A_log, D_skip, ssm_norm_w, w_out, w_up, w_down,
                    final_norm_w, prompt_tile=PROMPT_TILE, attn_tile=ATTN_TILE, mlp_rows=MLP_ROWS, mlp_ff=MLP_FF)
```

```python
import functools
import math

import numpy as np
import jax
import jax.numpy as jnp
from jax import lax
from jax.experimental import pallas as pl
from jax.experimental.pallas import tpu as pltpu

F32 = jnp.float32
BF16 = jnp.bfloat16

D_MODEL = 1024
DEPTH = 2
PAST_LEN = 16384
WINDOW = 128
HA_Q = 16
HA_KV = 4
HD_A = 64
NUM_BUCKETS = 32
MAX_DISTANCE = WINDOW
HB = 8
DK_B = 64
DV_B = 128
HC = 16
HD_C = 64
N_C = 128
G_C = 2
CONV_W = 4
CONV_DIM = D_MODEL + 2 * G_C * N_C
D_FF = 4 * D_MODEL
EPS = 1e-6
NEG = -1e30
LOG2E = 1.4426950408889634

O_AQ, O_AK, O_AV = 0, 1024, 1280
O_BQ, O_BK, O_BV, O_BG = 1536, 2048, 2560, 3584
O_CZ, O_CXBC, O_CDT, O_GTS = 4608, 5632, 7168, 7184
SEG_A = 2560
SEG_B = 4096
SEG_C = 3712
C_DT = 2560
C_GC = 2688

VMEM_LIMIT_V7X = 56 * 1024 * 1024
ROWS_PER_SAMPLE_STEP = 16
ROWS_PER_SAMPLE_STEP_SMALL = 32


def _cparams(n_axes):
    return pltpu.CompilerParams(dimension_semantics=("arbitrary",) * n_axes,
                                vmem_limit_bytes=VMEM_LIMIT_V7X)


def _dot(a, b):
    return jnp.dot(a, b, preferred_element_type=F32)


def _dot_nt(a, b):
    return lax.dot_general(a, b, (((1,), (1,)), ((), ())), preferred_element_type=F32)


def _dot_tn(a, b):
    return lax.dot_general(a, b, (((0,), (0,)), ((), ())), preferred_element_type=F32)


def _sigmoid(x):
    return 0.5 * (jnp.tanh(0.5 * x) + 1.0)


def _silu(x):
    return x * _sigmoid(x)


def _tanh1(half):
    return jnp.tanh(half) + 1.0


def _softplus(x):
    return jnp.maximum(x, 0.0) + jnp.log1p(jnp.exp(-jnp.abs(x)))


def _modnorm(x, nw, sc, sh):
    ms = jnp.mean(x * x, axis=-1, keepdims=True)
    return (x * lax.rsqrt(ms + EPS) * nw) * (1.0 + sc) + sh


def _rms_lanes(x):
    ms = jnp.mean(x * x, axis=-1, keepdims=True)
    return x * lax.rsqrt(ms + EPS)


def _pairswap(x):
    ax = x.ndim - 1
    n = x.shape[ax]
    lane = lax.broadcasted_iota(jnp.int32, x.shape, ax)
    nxt = pltpu.roll(x, n - 1, ax)
    prv = pltpu.roll(x, 1, ax)
    return jnp.where((lane & 1) == 0, nxt, prv)


def _split3(x):
    hi = x.astype(BF16)
    r1 = x - hi.astype(F32)
    mid = r1.astype(BF16)
    lo = (r1 - mid.astype(F32)).astype(BF16)
    return hi, mid, lo


def _gammas():
    return 1.0 - 2.0 ** (-5.0 - np.arange(HB, dtype=np.float64))


def _rot_tables(pos):
    theta = 1.0 / (10000.0 ** np.linspace(0.0, 1.0, DK_B // 2))
    ang = np.asarray(pos, np.float64)[:, None] * theta[None, :]
    cos = np.repeat(np.cos(ang), 2, axis=1)
    sin = np.repeat(np.sin(ang), 2, axis=1)
    sin[:, 0::2] *= -1.0
    return (np.tile(cos, (1, HB)).astype(np.float32), np.tile(sin, (1, HB)).astype(np.float32))


def _ret_tables():
    g = _gammas()
    L = WINDOW
    i = np.arange(L, dtype=np.float64)
    diff = i[:, None] - i[None, :]
    dm = np.where(diff >= 0, g[:, None, None] ** np.maximum(diff, 0.0), 0.0)
    qdec = np.repeat(g[None, :] ** (i[:, None] + 1.0), DK_B, axis=1)
    kdec = np.repeat(g[None, :] ** (L - 1.0 - i[:, None]), DK_B, axis=1)
    return dm.astype(np.float32), qdec.astype(np.float32), kdec.astype(np.float32)


def _t5_bucket_np(dist):
    max_exact = NUM_BUCKETS // 2
    n = np.maximum(dist, 0)
    nf = np.maximum(n, 1).astype(np.float32)
    large = max_exact + (np.log(nf / np.float32(max_exact)) / np.float32(math.log(MAX_DISTANCE / max_exact))
                         * np.float32(NUM_BUCKETS - max_exact)).astype(np.int32)
    large = np.minimum(large, NUM_BUCKETS - 1)
    return np.where(n < max_exact, n, large)


def _bias_tables(rel_table):
    gsz = HA_Q // HA_KV
    qi = np.arange(WINDOW)[None, :]
    kj = np.arange(WINDOW)[:, None]
    dist = np.where(kj > qi, qi + WINDOW - kj, qi - kj)
    onehot = _t5_bucket_np(dist)[..., None] == np.arange(NUM_BUCKETS)
    tab = rel_table.astype(F32)
    bias_kq = jnp.einsum('kqb,bh->hkq', jnp.asarray(onehot, F32), tab, precision=lax.Precision.HIGHEST)
    bias_t = bias_kq.reshape(HA_KV, gsz, WINDOW, WINDOW).transpose(0, 2, 1, 3).reshape(HA_KV, WINDOW, gsz * WINDOW)
    oh_s = _t5_bucket_np(WINDOW - 1 - np.arange(WINDOW))[:, None] == np.arange(NUM_BUCKETS)
    bias_s = jnp.einsum('jb,bh->hj', jnp.asarray(oh_s, F32), tab, precision=lax.Precision.HIGHEST)
    return bias_t, bias_s


def _ada_kernel(c_ref, w_ref, b_ref, o_ref):
    s = _silu(c_ref[...])
    o_ref[0] = _dot(s.astype(BF16), w_ref[0].astype(BF16)) + b_ref[0]


def _ada_call(c_all, ada_w, ada_b):
    n = c_all.shape[0]
    nb = 6
    return pl.pallas_call(
        _ada_kernel,
        grid=(DEPTH, nb),
        in_specs=[pl.BlockSpec((n, D_MODEL), lambda l, j: (0, 0)),
                  pl.BlockSpec((1, D_MODEL, D_MODEL), lambda l, j: (l, 0, j)),
                  pl.BlockSpec((1, 1, D_MODEL), lambda l, j: (l, 0, j))],
        out_specs=pl.BlockSpec((1, n, D_MODEL), lambda l, j: (l, 0, j)),
        out_shape=jax.ShapeDtypeStruct((DEPTH, n, 6 * D_MODEL), F32),
        compiler_params=_cparams(2),
        name="ada_mod",
    )(c_all, ada_w, ada_b.reshape(DEPTH, 1, 6 * D_MODEL))


def _pa_kernel(x_ref, mod_ref, n1_ref, wt_ref, wkv_ref, bias_ref, sink_ref, lowm_ref,
               mix_ref, ko_ref, vo_ref, zt_ref, kv_ref, kprev_ref, vtprev_ref, pen_ref, *, tt):
    t = pl.program_id(1)
    nchunk = tt // WINDOW
    kw = HA_KV * HD_A
    gsz = HA_Q // HA_KV

    @pl.when(t == 0)
    def _():
        kprev_ref[...] = jnp.zeros_like(kprev_ref)
        vtprev_ref[...] = jnp.zeros_like(vtprev_ref)
        pen_ref[...] = jnp.full(pen_ref.shape, NEG, F32)

    mod = mod_ref[0]
    h = _modnorm(x_ref[0], n1_ref[...], mod[:, D_MODEL:2 * D_MODEL], mod[:, 0:D_MODEL]).astype(BF16)
    kv_ref[...] = _dot(h, wkv_ref[...])
    for cc in range(nchunk // 2):
        z2 = _dot_nt(wt_ref[...], h[2 * WINDOW * cc:2 * WINDOW * (cc + 1), :])
        zt_ref[2 * cc] = z2[:, 0:WINDOW]
        zt_ref[2 * cc + 1] = z2[:, WINDOW:2 * WINDOW]
    qw = gsz * WINDOW
    lower = (lax.broadcasted_iota(jnp.int32, (WINDOW, qw), 0)
             > (lax.broadcasted_iota(jnp.int32, (WINDOW, qw), 1) & (WINDOW - 1)))

    def chunk(c, carry):
        r0 = pl.multiple_of(c * WINDOW, WINDOW)
        rows = pl.ds(r0, WINDOW)
        kc = kv_ref[rows, 0:kw]
        vc = kv_ref[rows, kw:2 * kw]
        vt = vc.T
        kk = jnp.concatenate([kprev_ref[...], kc], axis=0).astype(BF16)
        vvt = jnp.concatenate([vtprev_ref[...], vt], axis=1).astype(BF16)
        qt = zt_ref[c, 0:D_MODEL, :].astype(BF16)
        pen = pen_ref[0:1, :]
        s_all = []
        for g in range(HA_KV):
            gs = slice(HD_A * g, HD_A * (g + 1))
            qcat = jnp.concatenate([qt[HD_A * (gsz * g + j):HD_A * (gsz * g + j + 1), :] for j in range(gsz)], axis=1)
            s_all.append(_dot(kk[:, gs], qcat))
        p_all = []
        for g in range(HA_KV):
            sg = jnp.where(lower, s_all[g][0:WINDOW, :] + pen, s_all[g][WINDOW:2 * WINDOW, :]) + bias_ref[g]
            sink = sink_ref[g]
            m = jnp.maximum(jnp.max(sg, axis=0, keepdims=True), sink)
            pw = jnp.exp(sg - m)
            den = jnp.sum(pw, axis=0, keepdims=True) + jnp.exp(sink - m)
            pb = pw.astype(BF16)
            p_prev = pb * lowm_ref[...]
            p_all.append((jnp.concatenate([p_prev, pb - p_prev], axis=0), 1.0 / den))
        pieces = []
        for g in range(HA_KV):
            gs = slice(HD_A * g, HD_A * (g + 1))
            p, rden = p_all[g]
            ot = _dot(vvt[gs, :], p) * rden
            pieces += [ot[:, WINDOW * j:WINDOW * (j + 1)] for j in range(gsz)]
        oat = jnp.concatenate(pieces, axis=0)
        mixt = _sigmoid(zt_ref[c, D_MODEL:2 * D_MODEL, :]) * oat
        mix_ref[0, rows, :] = mixt.T
        kprev_ref[...] = kc
        vtprev_ref[...] = vt
        pen_ref[...] = jnp.zeros_like(pen_ref)
        ko_ref[0] = kc
        vo_ref[0] = vc
        return carry

    lax.fori_loop(0, nchunk, chunk, 0, unroll=True)


def _pa_call(x, mod3, n1, wqgt, wkv, bias_t, sink_rows, tt):
    B, T, _ = x.shape
    kw = HA_KV * HD_A
    qw = (HA_Q // HA_KV) * WINDOW
    kern = functools.partial(_pa_kernel, tt=tt)
    lowm = (np.arange(WINDOW)[:, None] > (np.arange(qw)[None, :] % WINDOW)).astype(np.float32)
    return pl.pallas_call(
        kern,
        grid=(B, T // tt),
        in_specs=[pl.BlockSpec((1, tt, D_MODEL), lambda b, t: (b, t, 0)),
                  pl.BlockSpec((1, 1, 6 * D_MODEL), lambda b, t: (b, 0, 0)),
                  pl.BlockSpec((1, D_MODEL), lambda b, t: (0, 0)),
                  pl.BlockSpec((2 * D_MODEL, D_MODEL), lambda b, t: (0, 0)),
                  pl.BlockSpec((D_MODEL, 2 * kw), lambda b, t: (0, 0)),
                  pl.BlockSpec((HA_KV, WINDOW, qw), lambda b, t: (0, 0, 0)),
                  pl.BlockSpec((HA_KV, 1, qw), lambda b, t: (0, 0, 0)),
                  pl.BlockSpec((WINDOW, qw), lambda b, t: (0, 0))],
        out_specs=[pl.BlockSpec((1, tt, D_MODEL), lambda b, t: (b, t, 0)),
                   pl.BlockSpec((1, WINDOW, kw), lambda b, t: (b, 0, 0)),
                   pl.BlockSpec((1, WINDOW, kw), lambda b, t: (b, 0, 0))],
        out_shape=[jax.ShapeDtypeStruct((B, T, D_MODEL), F32),
                   jax.ShapeDtypeStruct((B, WINDOW, kw), F32),
                   jax.ShapeDtypeStruct((B, WINDOW, kw), F32)],
        scratch_shapes=[pltpu.VMEM((tt // WINDOW, 2 * D_MODEL, WINDOW), F32),
                        pltpu.VMEM((tt, 2 * kw), F32),
                        pltpu.VMEM((WINDOW, kw), F32),
                        pltpu.VMEM((kw, WINDOW), F32),
                        pltpu.VMEM((8, qw), F32)],
        compiler_params=_cparams(2),
        name="prompt_attn",
    )(x, mod3, n1, wqgt, wkv, bias_t, sink_rows, jnp.asarray(lowm, BF16))


def _pb_kernel(x_ref, mod_ref, n1_ref, wt_ref, cos_ref, sin_ref, qdec_ref, kdec_ref, dm_ref, mixa_ref,
               mix_ref, so_ref, zt_ref, s_ref, *, tt, glast):
    t = pl.program_id(1)
    nchunk = tt // WINDOW
    hw = HB * DK_B // 2
    hp = DK_B // 2

    @pl.when(t == 0)
    def _():
        s_ref[...] = jnp.zeros_like(s_ref)

    mod = mod_ref[0]
    h = _modnorm(x_ref[0], n1_ref[...], mod[:, D_MODEL:2 * D_MODEL], mod[:, 0:D_MODEL]).astype(BF16)
    for cc in range(nchunk // 2):
        z2 = _dot_nt(wt_ref[...], h[2 * WINDOW * cc:2 * WINDOW * (cc + 1), :])
        zt_ref[2 * cc] = z2[:, 0:WINDOW]
        zt_ref[2 * cc + 1] = z2[:, WINDOW:2 * WINDOW]

    def head_rows(pair, hh):
        return jnp.concatenate([pair[0][hp * hh:hp * (hh + 1), :], pair[1][hp * hh:hp * (hh + 1), :]], axis=0)

    def chunk(c, carry):
        r0 = pl.multiple_of(c * WINDOW, WINDOW)
        rows = pl.ds(r0, WINDOW)
        cos = cos_ref[c]
        sin = sin_ref[c]
        q1 = zt_ref[c, 0:hw, :]
        q2 = zt_ref[c, hw:2 * hw, :]
        k1 = zt_ref[c, 2 * hw:3 * hw, :]
        k2 = zt_ref[c, 3 * hw:4 * hw, :]
        rq = (q1 * cos - q2 * sin, q1 * sin + q2 * cos)
        rk = (k1 * cos - k2 * sin, k1 * sin + k2 * cos)
        qdec = qdec_ref[...]
        kdec = kdec_ref[...]
        qb = tuple(a.astype(BF16) for a in rq)
        kb = tuple(a.astype(BF16) for a in rk)
        qd = tuple((a * qdec).astype(BF16) for a in rq)
        kd = tuple((a * kdec).astype(BF16) for a in rk)
        vt = zt_ref[c, 4 * hw:4 * hw + HB * DV_B, :].astype(BF16)
        inner, cross, supd = [], [], []
        for hh in range(HB):
            s_old = s_ref[DK_B * hh:DK_B * (hh + 1), :]
            inner.append(_dot_tn(head_rows(qb, hh), head_rows(kb, hh)))
            cross.append(_dot_tn(s_old.astype(BF16), head_rows(qd, hh)))
            supd.append(glast[hh] * s_old + _dot_nt(head_rows(kd, hh), vt[DV_B * hh:DV_B * (hh + 1), :]))
        innd = [(inner[hh] * dm_ref[hh]).astype(BF16) for hh in range(HB)]
        outs = []
        for hh in range(HB):
            ot = _dot_nt(vt[DV_B * hh:DV_B * (hh + 1), :], innd[hh]) + cross[hh]
            s_ref[DK_B * hh:DK_B * (hh + 1), :] = supd[hh]
            ms = jnp.mean(ot * ot, axis=0, keepdims=True)
            outs.append(ot * (0.5 * lax.rsqrt(ms + EPS)))
        obt = jnp.concatenate(outs, axis=0)
        bgt = zt_ref[c, 4 * hw + D_MODEL:4 * hw + 2 * D_MODEL, :]
        gbt = zt_ref[c, 4 * hw + 2 * D_MODEL:4 * hw + 3 * D_MODEL, :]
        mixt = _tanh1(gbt) * (bgt * _tanh1(bgt) * obt)
        mix_ref[0, rows, :] = mixa_ref[0, rows, :] + mixt.T
        return carry

    lax.fori_loop(0, nchunk, chunk, 0, unroll=True)
    so_ref[0] = s_ref[...]


def _deinterleave_pairs(w):
    return w.reshape(w.shape[0], HB, DK_B // 2, 2).transpose(0, 3, 1, 2).reshape(w.shape[0], HB * DK_B)


def _rot_tables_t(T):
    theta = 1.0 / (10000.0 ** np.linspace(0.0, 1.0, DK_B // 2))
    ang = theta[:, None] * np.arange(T, dtype=np.float64)[None, :]

    def lay(a):
        a = np.tile(a, (HB, 1))
        return np.ascontiguousarray(a.reshape(a.shape[0], T // WINDOW, WINDOW).transpose(1, 0, 2)).astype(np.float32)

    return lay(np.cos(ang)), lay(np.sin(ang))


def _pb_call(x, mod3, n1, wbt, mixa, tt):
    B, T, _ = x.shape
    hw = HB * DK_B // 2
    cos, sin = _rot_tables_t(T)
    dm, qdec, kdec = _ret_tables()
    qdec_t = np.ascontiguousarray(qdec[:, ::2].T)
    kdec_t = np.ascontiguousarray(kdec[:, ::2].T)
    glast = tuple(float(v) for v in (_gammas() ** WINDOW))
    kern = functools.partial(_pb_kernel, tt=tt, glast=glast)
    full2 = lambda b, t: (0, 0)
    nct = tt // WINDOW
    return pl.pallas_call(
        kern,
        grid=(B, T // tt),
        in_specs=[pl.BlockSpec((1, tt, D_MODEL), lambda b, t: (b, t, 0)),
                  pl.BlockSpec((1, 1, 6 * D_MODEL), lambda b, t: (b, 0, 0)),
                  pl.BlockSpec((1, D_MODEL), full2),
                  pl.BlockSpec((SEG_B, D_MODEL), full2, pipeline_mode=pl.Buffered(1)),
                  pl.BlockSpec((nct, hw, WINDOW), lambda b, t: (t, 0, 0)),
                  pl.BlockSpec((nct, hw, WINDOW), lambda b, t: (t, 0, 0)),
                  pl.BlockSpec((hw, WINDOW), full2),
                  pl.BlockSpec((hw, WINDOW), full2),
                  pl.BlockSpec((HB, WINDOW, WINDOW), lambda b, t: (0, 0, 0)),
                  pl.BlockSpec((1, tt, D_MODEL), lambda b, t: (b, t, 0))],
        out_specs=[pl.BlockSpec((1, tt, D_MODEL), lambda b, t: (b, t, 0)),
                   pl.BlockSpec((1, HB * DK_B, DV_B), lambda b, t: (b, 0, 0))],
        out_shape=[jax.ShapeDtypeStruct((B, T, D_MODEL), F32),
                   jax.ShapeDtypeStruct((B, HB * DK_B, DV_B), F32)],
        scratch_shapes=[pltpu.VMEM((nct, SEG_B, WINDOW), F32),
                        pltpu.VMEM((HB * DK_B, DV_B), F32)],
        compiler_params=_cparams(2),
        name="prompt_retention",
    )(x, mod3, n1, wbt, jnp.asarray(cos), jnp.asarray(sin), jnp.asarray(qdec_t), jnp.asarray(kdec_t),
      jnp.asarray(dm), mixa)


def _pc_kernel(x_ref, mod_ref, n1_ref, w_ref, cw_ref, cb_ref, dtb_ref, a_ref, dsk_ref, snw_ref, tri_ref,
               mixab_ref, wo_ref,
               xo_ref, ho_ref, co_ref,
               z_ref, xbuf_ref, xbc_ref, dt_ref, hst_ref, mixs_ref, *, tt):
    t = pl.program_id(1)

    @pl.when(t == 0)
    def _():
        xbuf_ref[0:8, :] = jnp.zeros((8, CONV_DIM), F32)
        hst_ref[...] = jnp.zeros_like(hst_ref)

    x = x_ref[0]
    mod = mod_ref[0]
    h = _modnorm(x, n1_ref[...], mod[:, D_MODEL:2 * D_MODEL], mod[:, 0:D_MODEL])
    z_ref[...] = _dot(h.astype(BF16), w_ref[...])

    xbuf_ref[8:8 + tt, :] = z_ref[:, 1024:2560]
    acc = cb_ref[...]
    for i in range(CONV_W):
        acc = acc + xbuf_ref[5 + i:5 + i + tt, :] * cw_ref[i:i + 1, :]
    xbc_ref[...] = _silu(acc)
    co_ref[0] = xbuf_ref[tt + 5:tt + 8, :]
    xbuf_ref[0:8, :] = xbuf_ref[tt:tt + 8, :]
    dt_ref[...] = _softplus(z_ref[:, C_DT:C_DT + 128] + dtb_ref[...])

    ii = lax.broadcasted_iota(jnp.int32, (WINDOW, WINDOW), 0)
    jj = lax.broadcasted_iota(jnp.int32, (WINDOW, WINDOW), 1)
    causal = ii >= jj
    hpg = HC // G_C

    def chunk(c):
        rows = slice(WINDOW * c, WINDOW * (c + 1))
        xc = xbc_ref[rows, 0:1024]
        bmat = xbc_ref[rows, 1024:1280]
        cmat = xbc_ref[rows, 1280:1536]
        dtc = dt_ref[rows, :]
        acum = jnp.dot(tri_ref[...], dtc * a_ref[...], precision=lax.Precision.HIGHEST,
                       preferred_element_type=F32)
        acum = acum * LOG2E
        acum_t = acum.T
        rowp_t = acum_t - jnp.log2(dtc.T)
        x_t = xc.T
        xb = xc.astype(BF16)
        bb = bmat.astype(BF16)
        cb16 = cmat.astype(BF16)
        ys = []
        for g in range(G_C):
            ns = slice(N_C * g, N_C * (g + 1))
            cbg = _dot_nt(cb16[:, ns], bb[:, ns])
            for hh in range(hpg * g, hpg * (g + 1)):
                ps = slice(HD_C * hh, HD_C * (hh + 1))
                colb = jnp.broadcast_to(acum[:, hh:hh + 1], (WINDOW, WINDOW))
                rowp = rowp_t[hh:hh + 1, :]
                m = cbg * jnp.exp2(jnp.where(causal, colb - rowp, NEG))
                hs = hst_ref[ps, :]
                ecolb = jnp.exp2(colb)
                cs = cmat[:, ns] * ecolb
                y = _dot(m.astype(BF16), xb[:, ps]) + _dot_nt(cs.astype(BF16), hs.astype(BF16))
                wrow = jnp.exp2(colb[WINDOW - 1:WINDOW, :] - rowp)
                xw = (x_t[ps, :] * wrow).astype(BF16)
                hst_ref[ps, :] = ecolb[WINDOW - 1:WINDOW, :] * hs + _dot(xw, bb[:, ns])
                ys.append(y)
        y = jnp.concatenate(ys, axis=1) + dsk_ref[...] * xc
        yc = y * _silu(z_ref[rows, 0:1024])
        gw = D_MODEL // G_C
        oc = jnp.concatenate([_rms_lanes(yc[:, gw * g:gw * (g + 1)]) for g in range(G_C)], axis=1) * snw_ref[...]
        gc = _sigmoid(z_ref[rows, C_GC:C_GC + 1024])
        mixs_ref[rows, :] = (mixab_ref[0, rows, :] + gc * oc).astype(BF16)

    g1 = mod[:, 2 * D_MODEL:3 * D_MODEL]
    for c in range(tt // WINDOW):
        chunk(c)
        if c % 2 == 1:
            pr = slice(WINDOW * (c - 1), WINDOW * (c + 1))
            xo_ref[0, pr, :] = x_ref[0, pr, :] + g1 * _dot(mixs_ref[pr, :], wo_ref[...])
    ho_ref[0] = hst_ref[...]


def _pc_call(x, mod3, n1, wc, cw, cb, dtb_pad, a_pad, dsk_full, snw, mixab, wo, tt):
    B, T, _ = x.shape
    tri = jnp.asarray(np.tril(np.ones((WINDOW, WINDOW), np.float32)))
    kern = functools.partial(_pc_kernel, tt=tt)
    full2 = lambda b, t: (0, 0)
    return pl.pallas_call(
        kern,
        grid=(B, T // tt),
        in_specs=[pl.BlockSpec((1, tt, D_MODEL), lambda b, t: (b, t, 0)),
                  pl.BlockSpec((1, 1, 6 * D_MODEL), lambda b, t: (b, 0, 0)),
                  pl.BlockSpec((1, D_MODEL), full2),
                  pl.BlockSpec((D_MODEL, SEG_C), full2),
                  pl.BlockSpec((CONV_W, CONV_DIM), full2),
                  pl.BlockSpec((1, CONV_DIM), full2),
                  pl.BlockSpec((1, 128), full2),
                  pl.BlockSpec((1, 128), full2),
                  pl.BlockSpec((1, D_MODEL), full2),
                  pl.BlockSpec((1, D_MODEL), full2),
                  pl.BlockSpec((WINDOW, WINDOW), full2),
                  pl.BlockSpec((1, tt, D_MODEL), lambda b, t: (b, t, 0)),
                  pl.BlockSpec((D_MODEL, D_MODEL), full2)],
        out_specs=[pl.BlockSpec((1, tt, D_MODEL), lambda b, t: (b, t, 0)),
                   pl.BlockSpec((1, HC * HD_C, N_C), lambda b, t: (b, 0, 0)),
                   pl.BlockSpec((1, CONV_W - 1, CONV_DIM), lambda b, t: (b, 0, 0))],
        out_shape=[jax.ShapeDtypeStruct((B, T, D_MODEL), F32),
                   jax.ShapeDtypeStruct((B, HC * HD_C, N_C), F32),
                   jax.ShapeDtypeStruct((B, CONV_W - 1, CONV_DIM), F32)],
        scratch_shapes=[pltpu.VMEM((tt, SEG_C), F32),
                        pltpu.VMEM((tt + 8, CONV_DIM), F32),
                        pltpu.VMEM((tt, CONV_DIM), F32),
                        pltpu.VMEM((tt, 128), F32),
                        pltpu.VMEM((HC * HD_C, N_C), F32),
                        pltpu.VMEM((tt, D_MODEL), BF16)],
        compiler_params=_cparams(2),
        name="prompt_ssd_out",
    )(x, mod3, n1, wc, cw, cb, dtb_pad, a_pad, dsk_full, snw, tri, mixab, wo)


def _mlp_kernel(x_ref, sh_ref, sc_ref, g_ref, n2_ref, wu_ref, wd_ref, fn_ref, o_ref, *, tf, final, per_row):
    rd = (lambda r: r[...]) if per_row else (lambda r: r[0])
    x = x_ref[...]
    hb = _modnorm(x, n2_ref[...], rd(sc_ref), rd(sh_ref)).astype(BF16)
    acc = None
    for f in range(D_FF // tf):
        u = _dot(hb, wu_ref[0, :, tf * f:tf * (f + 1)])
        u = jnp.square(jnp.maximum(u, 0.0)).astype(BF16)
        part = _dot(u, wd_ref[0, tf * f:tf * (f + 1), :])
        acc = part if acc is None else acc + part
    y = x + rd(g_ref) * acc
    if final:
        y = _rms_lanes(y) * fn_ref[...]
    o_ref[...] = y


def _mlp_call(x2, mod, n2, wu, wd, fn, layer, tm, tf, rows_per_mod, final):
    M = x2.shape[0]
    per_row = rows_per_mod == 1
    if per_row:
        mspec = lambda j: pl.BlockSpec((tm, D_MODEL), lambda m: (m, j))
    else:
        mspec = lambda j: pl.BlockSpec((1, 1, D_MODEL), lambda m: ((m * tm) // rows_per_mod, 0, j))
    kern = functools.partial(_mlp_kernel, tf=tf, final=final, per_row=per_row)
    resident = pl.Buffered(1)
    return pl.pallas_call(
        kern,
        grid=(M // tm,),
        in_specs=[pl.BlockSpec((tm, D_MODEL), lambda m: (m, 0)),
                  mspec(3), mspec(4), mspec(5),
                  pl.BlockSpec((1, D_MODEL), lambda m: (0, 0)),
                  pl.BlockSpec((1, D_MODEL, D_FF), lambda m: (layer, 0, 0), pipeline_mode=resident),
                  pl.BlockSpec((1, D_FF, D_MODEL), lambda m: (layer, 0, 0), pipeline_mode=resident),
                  pl.BlockSpec((1, D_MODEL), lambda m: (0, 0))],
        out_specs=pl.BlockSpec((tm, D_MODEL), lambda m: (m, 0)),
        out_shape=jax.ShapeDtypeStruct((M, D_MODEL), F32),
        compiler_params=_cparams(1),
        name="mlp",
    )(x2, mod, mod, mod, n2, wu, wd, fn)


def _sproj_kernel(x_ref, sh_ref, sc_ref, n1_ref, wa_ref, wb_ref, wc_ref, oa_ref, ob_ref, oc_ref):
    h = _modnorm(x_ref[...], n1_ref[...], sc_ref[...], sh_ref[...]).astype(BF16)
    oa_ref[...] = _dot(h, wa_ref[...])
    ob_ref[...] = _dot(h, wb_ref[...])
    oc_ref[...] = _dot(h, wc_ref[...])


def _sproj_call(xs, mod_s, n1, wa, wb, wc):
    n = xs.shape[0]
    full = lambda i: (0, 0)
    once = pl.Buffered(1)
    wspec = lambda w: pl.BlockSpec(w.shape, full, pipeline_mode=once)
    return pl.pallas_call(
        _sproj_kernel,
        grid=(1,),
        in_specs=[pl.BlockSpec((n, D_MODEL), full),
                  pl.BlockSpec((n, D_MODEL), full),
                  pl.BlockSpec((n, D_MODEL), lambda i: (0, 1)),
                  pl.BlockSpec((1, D_MODEL), full),
                  wspec(wa), wspec(wb), wspec(wc)],
        out_specs=[pl.BlockSpec((n, w.shape[1]), full) for w in (wa, wb, wc)],
        out_shape=[jax.ShapeDtypeStruct((n, w.shape[1]), F32) for w in (wa, wb, wc)],
        compiler_params=_cparams(1),
        name="sample_proj",
    )(xs, mod_s, mod_s, n1, wa, wb, wc)


def _carry_outputs(prev, n_in, nsteps):
    nslab = 1 if prev else DEPTH
    extra = [pl.BlockSpec(memory_space=pl.ANY) for _ in prev]
    alias = {n_in + k: k for k in range(len(prev))}
    row = lambda p, i: jnp.where(p == 0, i, nsteps - 1)
    return nslab, extra, alias, row


def _with_fill(body, state_outs):
    def kern(*refs):
        p = pl.program_id(0)

        @pl.when(p == 0)
        def _():
            body(*refs)

        @pl.when(p != 0)
        def _():
            for k in state_outs:
                refs[k][...] = jnp.zeros_like(refs[k])

    return kern


def _sa_kernel(q_ref, kv_ref, ck_ref, cv_ref, bias_ref, sink_ref, *rest):
    ko_ref, vo_ref, oa_ref = rest[-3:]
    nb = ROWS_PER_SAMPLE_STEP_SMALL
    gsz = HA_Q // HA_KV
    rg = lax.broadcasted_iota(jnp.int32, (HA_Q, HD_A), 0) // gsz
    sink = sink_ref[...]
    kw = HA_KV * HD_A
    scores, vmats = [], []
    for i in range(nb):
        ko_ref[0, i, 0:WINDOW - 1, :] = ck_ref[0, i, 1:WINDOW, :]
        ko_ref[0, i, WINDOW - 1:WINDOW, :] = kv_ref[i:i + 1, 0:kw]
        vo_ref[0, i, 0:WINDOW - 1, :] = cv_ref[0, i, 1:WINDOW, :]
        vo_ref[0, i, WINDOW - 1:WINDOW, :] = kv_ref[i:i + 1, kw:2 * kw]
        kmat = ko_ref[0, i].astype(BF16)
        vmats.append(vo_ref[0, i].astype(BF16))
        q = q_ref[i] * (HD_A ** -0.5)
        qe = jnp.concatenate([jnp.where(rg == g, q, 0.0) for g in range(HA_KV)], axis=1).astype(BF16)
        scores.append(_dot_nt(qe, kmat))
    probs = []
    for i in range(nb):
        sc = scores[i] + bias_ref[...]
        m = jnp.maximum(jnp.max(sc, axis=-1, keepdims=True), sink)
        p = jnp.exp(sc - m)
        den = jnp.sum(p, axis=-1, keepdims=True) + jnp.exp(sink - m)
        probs.append((p.astype(BF16), den))
    outs = [_dot(probs[i][0], vmats[i]) for i in range(nb)]
    for i in range(nb):
        o = outs[i] / probs[i][1]
        o16 = jnp.zeros((HA_Q, HD_A), F32)
        for g in range(HA_KV):
            o16 = o16 + jnp.where(rg == g, o[:, HD_A * g:HD_A * (g + 1)], 0.0)
        oa_ref[i] = o16


def _sa_call(q3, kv, cache_k, cache_v, bias_s, sink_col, layer, prev):
    n = q3.shape[0]
    nb = ROWS_PER_SAMPLE_STEP_SMALL
    kw = HA_KV * HD_A
    nslab, extra, alias, row = _carry_outputs(prev, 6, n // nb)
    cspec = pl.BlockSpec((1, nb, WINDOW, kw), lambda p, i: (layer, row(p, i), 0, 0))
    ospec = pl.BlockSpec((1, nb, WINDOW, kw), lambda p, i: ((layer + p) % DEPTH, i, 0, 0))
    return pl.pallas_call(
        _with_fill(_sa_kernel, (-3, -2)),
        grid=(nslab, n // nb),
        in_specs=[pl.BlockSpec((nb, HA_Q, HD_A), lambda p, i: (row(p, i), 0, 0)),
                  pl.BlockSpec((nb, 2 * kw), lambda p, i: (row(p, i), 0)),
                  cspec, cspec,
                  pl.BlockSpec((HA_Q, WINDOW), lambda p, i: (0, 0)),
                  pl.BlockSpec((HA_Q, 1), lambda p, i: (0, 0))] + extra,
        out_specs=[ospec, ospec, pl.BlockSpec((nb, HA_Q, HD_A), lambda p, i: (row(p, i), 0, 0))],
        out_shape=[jax.ShapeDtypeStruct((DEPTH, n, WINDOW, kw), F32),
                   jax.ShapeDtypeStruct((DEPTH, n, WINDOW, kw), F32),
                   jax.ShapeDtypeStruct((n, HA_Q, HD_A), F32)],
        input_output_aliases=alias,
        compiler_params=_cparams(2),
        name="sample_attn",
    )(q3, kv, cache_k, cache_v, bias_s, sink_col, *prev)


def _sr_kernel(zb_ref, cos_ref, sin_ref, gcol_ref, s_ref, *rest):
    so_ref, o_ref = rest[-2:]
    nb = ROWS_PER_SAMPLE_STEP_SMALL
    wq = HB * DK_B
    cos = cos_ref[...]
    sin = sin_ref[...]
    qf = zb_ref[:, 0:wq]
    kf = zb_ref[:, wq:2 * wq]
    qr = qf * cos + _pairswap(qf) * sin
    kr = (kf * cos + _pairswap(kf) * sin) * (DK_B ** -0.5)
    v = zb_ref[:, 2 * wq:2 * wq + HB * DV_B]
    r8 = lax.broadcasted_iota(jnp.int32, (HB, wq), 0)
    hl = lax.broadcasted_iota(jnp.int32, (HB, wq), 1) // DK_B
    rv = lax.broadcasted_iota(jnp.int32, (HB, DV_B), 0)
    outers, q8s = [], []
    for i in range(nb):
        k8 = jnp.where(hl == r8, jnp.broadcast_to(kr[i:i + 1, :], (HB, wq)), 0.0).astype(BF16)
        q8s.append(jnp.where(hl == r8, jnp.broadcast_to(qr[i:i + 1, :], (HB, wq)), 0.0).astype(BF16))
        v8 = jnp.zeros((HB, DV_B), F32)
        for r in range(HB):
            v8 = jnp.where(rv == r, jnp.broadcast_to(v[i:i + 1, DV_B * r:DV_B * (r + 1)], (HB, DV_B)), v8)
        outers.append(_dot_tn(k8, v8.astype(BF16)))
    s_news = []
    for i in range(nb):
        s_new = gcol_ref[...] * s_ref[0, i] + outers[i]
        so_ref[0, i] = s_new
        s_news.append(s_new.astype(BF16))
    for i in range(nb):
        o_ref[i] = _dot(q8s[i], s_news[i])


def _sr_call(zb, state, layer, prev):
    n = zb.shape[0]
    nb = ROWS_PER_SAMPLE_STEP_SMALL
    wq = HB * DK_B
    cos, sin = _rot_tables(np.array([PAST_LEN]))
    gcol = np.repeat(_gammas(), DK_B)[:, None] * np.ones((1, DV_B))
    nslab, extra, alias, row = _carry_outputs(prev, 5, n // nb)
    full = lambda p, i: (0, 0)
    return pl.pallas_call(
        _with_fill(_sr_kernel, (-2,)),
        grid=(nslab, n // nb),
        in_specs=[pl.BlockSpec((nb, SEG_B), lambda p, i: (row(p, i), 0)),
                  pl.BlockSpec((1, wq), full),
                  pl.BlockSpec((1, wq), full),
                  pl.BlockSpec((wq, DV_B), full),
                  pl.BlockSpec((1, nb, wq, DV_B), lambda p, i: (layer, row(p, i), 0, 0))] + extra,
        out_specs=[pl.BlockSpec((1, nb, wq, DV_B), lambda p, i: ((layer + p) % DEPTH, i, 0, 0)),
                   pl.BlockSpec((nb, HB, DV_B), lambda p, i: (row(p, i), 0, 0))],
        out_shape=[jax.ShapeDtypeStruct((DEPTH, n, wq, DV_B), F32),
                   jax.ShapeDtypeStruct((n, HB, DV_B), F32)],
        input_output_aliases=alias,
        compiler_params=_cparams(2),
        name="sample_retention",
    )(zb, jnp.asarray(cos), jnp.asarray(sin), jnp.asarray(gcol.astype(np.float32)), state, *prev)


def _ss_kernel(zc_ref, cs_ref, cw_ref, cb_ref, dtb_ref, a_ref, e3_ref, h_ref, *rest):
    ho_ref, cso_ref, y_ref, xc_ref = rest[-4:]
    nb = ROWS_PER_SAMPLE_STEP
    cx = zc_ref[:, 1024:2560]
    taps = [cs_ref[0, :, CONV_DIM * i:CONV_DIM * (i + 1)] for i in range(CONV_W - 1)] + [cx]
    acc = cb_ref[...]
    for i in range(CONV_W):
        acc = acc + taps[i] * cw_ref[i:i + 1, :]
    xbc = _silu(acc)
    cso_ref[...] = jnp.concatenate(taps[1:], axis=1)
    xc = xbc[:, 0:1024]
    bmat = xbc[:, 1024:1280]
    cmat = xbc[:, 1280:1536]
    dt = _softplus(zc_ref[:, C_DT:C_DT + 128] + dtb_ref[...])
    da = jnp.exp(dt * a_ref[...])
    dt_e = _dot(jnp.concatenate(_split3(dt), axis=1), e3_ref[...])
    da_e = _dot(jnp.concatenate(_split3(da), axis=1), e3_ref[...])
    dtx = dt_e * xc
    gw = (HC // G_C) * HD_C
    r8 = lax.broadcasted_iota(jnp.int32, (nb, gw), 0)
    rn = lax.broadcasted_iota(jnp.int32, (nb, N_C), 0)
    ones8 = jnp.ones((nb, N_C), BF16)
    prods = {}
    for g in range(G_C):
        ws = slice(gw * g, gw * (g + 1))
        bg16 = bmat[:, N_C * g:N_C * (g + 1)].astype(BF16)
        for i in range(nb):
            x8 = jnp.where(r8 == i, dtx[:, ws], 0.0).astype(BF16)
            outer = _dot_tn(x8, bg16)
            hi, mid, lo = (p.astype(F32) for p in _split3(jnp.broadcast_to(da_e[i:i + 1, ws], (nb, gw))))
            l3 = jnp.where(r8 == 0, hi, jnp.where(r8 == 1, mid, jnp.where(r8 == 2, lo, 0.0)))
            prods[g, i] = (outer, _dot_tn(l3.astype(BF16), ones8))
    h16 = {}
    for g in range(G_C):
        ws = slice(gw * g, gw * (g + 1))
        for i in range(nb):
            outer, dacol = prods[g, i]
            h_new = dacol * h_ref[0, i, ws, :] + outer
            ho_ref[0, i, ws, :] = h_new
            h16[g, i] = h_new.astype(BF16)
    ycols = []
    for g in range(G_C):
        ns = slice(N_C * g, N_C * (g + 1))
        yacc = jnp.zeros((nb, gw), F32)
        for i in range(nb):
            c8 = jnp.where(rn == i, cmat[:, ns], 0.0).astype(BF16)
            yacc = yacc + _dot_nt(c8, h16[g, i])
        ycols.append(yacc)
    y_ref[...] = jnp.concatenate(ycols, axis=1)
    xc_ref[...] = xc


def _ss_call(zc, conv_state, hstate, cw, cb, dtb_pad, a_pad, layer, prev):
    n = zc.shape[0]
    nb = ROWS_PER_SAMPLE_STEP
    e = np.zeros((128, D_MODEL), np.float32)
    for hh in range(HC):
        e[hh, HD_C * hh:HD_C * (hh + 1)] = 1.0
    e3 = jnp.asarray(np.concatenate([e, e, e], axis=0), dtype=BF16)
    cwid = (CONV_W - 1) * CONV_DIM
    full = lambda p, i: (0, 0)
    nslab, extra, alias, row = _carry_outputs(prev, 8, n // nb)
    return pl.pallas_call(
        _with_fill(_ss_kernel, (-4,)),
        grid=(nslab, n // nb),
        in_specs=[pl.BlockSpec((nb, SEG_C), lambda p, i: (row(p, i), 0)),
                  pl.BlockSpec((1, nb, cwid), lambda p, i: (layer, row(p, i), 0)),
                  pl.BlockSpec((CONV_W, CONV_DIM), full),
                  pl.BlockSpec((1, CONV_DIM), full),
                  pl.BlockSpec((1, 128), full),
                  pl.BlockSpec((1, 128), full),
                  pl.BlockSpec((3 * 128, D_MODEL), full),
                  pl.BlockSpec((1, nb, HC * HD_C, N_C), lambda p, i: (layer, row(p, i), 0, 0))] + extra,
        out_specs=[pl.BlockSpec((1, nb, HC * HD_C, N_C), lambda p, i: ((layer + p) % DEPTH, i, 0, 0)),
                   pl.BlockSpec((nb, cwid), lambda p, i: (row(p, i), 0)),
                   pl.BlockSpec((nb, D_MODEL), lambda p, i: (row(p, i), 0)),
                   pl.BlockSpec((nb, D_MODEL), lambda p, i: (row(p, i), 0))],
        out_shape=[jax.ShapeDtypeStruct((DEPTH, n, HC * HD_C, N_C), F32),
                   jax.ShapeDtypeStruct((n, cwid), F32),
                   jax.ShapeDtypeStruct((n, D_MODEL), F32),
                   jax.ShapeDtypeStruct((n, D_MODEL), F32)],
        input_output_aliases=alias,
        compiler_params=_cparams(2),
        name="sample_ssd",
    )(zc, conv_state, cw, cb, dtb_pad, a_pad, e3, hstate, *prev)


def _sm_kernel(x_ref, g1_ref, oa_ref, oret_ref, y_ref, xc_ref, za_ref, zb_ref, zc_ref,
               dsk_ref, snw_ref, wo_ref, sh2_ref, sc2_ref, g2_ref, n2_ref, wu_ref, wd_ref, fn_ref, o_ref, *, tf, final):
    ob = jnp.concatenate([_rms_lanes(oret_ref[:, DV_B * hh:DV_B * (hh + 1)]) for hh in range(HB)], axis=1)
    ob = _silu(zb_ref[:, 2048:3072]) * ob
    yc = (y_ref[...] + dsk_ref[...] * xc_ref[...]) * _silu(zc_ref[:, 0:1024])
    gw = D_MODEL // G_C
    oc = jnp.concatenate([_rms_lanes(yc[:, gw * g:gw * (g + 1)]) for g in range(G_C)], axis=1) * snw_ref[...]
    mix = (_sigmoid(za_ref[:, 1536:2560]) * oa_ref[...] + _sigmoid(zb_ref[:, 3072:4096]) * ob
           + _sigmoid(zc_ref[:, C_GC:C_GC + 1024]) * oc)
    x1 = x_ref[...] + g1_ref[...] * _dot(mix.astype(BF16), wo_ref[...])
    hb = _modnorm(x1, n2_ref[...], sc2_ref[...], sh2_ref[...]).astype(BF16)
    acc = None
    for f in range(D_FF // tf):
        u = _dot(hb, wu_ref[0, :, tf * f:tf * (f + 1)])
        u = jnp.square(jnp.maximum(u, 0.0)).astype(BF16)
        part = _dot(u, wd_ref[0, tf * f:tf * (f + 1), :])
        acc = part if acc is None else acc + part
    y = x1 + g2_ref[...] * acc
    if final:
        y = _rms_lanes(y) * fn_ref[...]
    o_ref[...] = y


def _sm_call(xs, mod_s, oa, oret, y, xc, za, zb, zc, dsk_full, snw, wo, n2, wu, wd, fn, layer, tf, final):
    n = xs.shape[0]
    full = lambda i: (0, 0)
    row = lambda w: pl.BlockSpec((n, w), full)
    modcol = lambda j: pl.BlockSpec((n, D_MODEL), lambda i: (0, j))
    once = pl.Buffered(1)
    return pl.pallas_call(
        functools.partial(_sm_kernel, tf=tf, final=final),
        grid=(1,),
        in_specs=[row(D_MODEL), modcol(2),
                  row(D_MODEL), row(D_MODEL), row(D_MODEL), row(D_MODEL),
                  row(SEG_A), row(SEG_B), row(SEG_C),
                  pl.BlockSpec((1, D_MODEL), full),
                  pl.BlockSpec((1, D_MODEL), full),
                  pl.BlockSpec((D_MODEL, D_MODEL), full, pipeline_mode=once),
                  modcol(3), modcol(4), modcol(5),
                  pl.BlockSpec((1, D_MODEL), full),
                  pl.BlockSpec((1, D_MODEL, D_FF), lambda i: (layer, 0, 0), pipeline_mode=once),
                  pl.BlockSpec((1, D_FF, D_MODEL), lambda i: (layer, 0, 0), pipeline_mode=once),
                  pl.BlockSpec((1, D_MODEL), full)],
        out_specs=row(D_MODEL),
        out_shape=jax.ShapeDtypeStruct((n, D_MODEL), F32),
        compiler_params=_cparams(1),
        name="sample_merge_mlp",
    )(xs, mod_s, oa, oret, y, xc, za, zb, zc, dsk_full, snw, wo, mod_s, mod_s, mod_s, n2, wu, wd, fn)


PROMPT_TILE = 512
ATTN_TILE = 1024
MLP_ROWS = 1024
MLP_FF = 1024


def _prep_w_in(w):
    wa = jnp.concatenate([w[:, O_AQ:O_BQ], w[:, O_GTS:O_GTS + 1024]], axis=1)
    wb = jnp.concatenate([w[:, O_BQ:O_CZ], w[:, O_GTS + 1024:O_GTS + 2048]], axis=1)
    wc = jnp.concatenate([w[:, O_CZ:O_CDT], jnp.pad(w[:, O_CDT:O_GTS], ((0, 0), (0, 128 - HC))),
                          w[:, O_GTS + 2048:O_GTS + 3072]], axis=1)
    return wa, wb, wc


def _forward(x_prompt, x_sample, cache_win_k, cache_win_v, state_ret, state_ssm, state_conv,
             c_prompt, c_sample, rel_bias_table, attn_sinks, norm1_w, norm2_w, ada_w, ada_b,
             w_in, conv_w, conv_b, dt_bias, A_log, D_skip, ssm_norm_w, w_out, w_up, w_down,
             final_norm_w, *, prompt_tile, attn_tile, mlp_rows, mlp_ff):
    B, T, _ = x_prompt.shape
    DB = x_sample.shape[0]
    kw = HA_KV * HD_A

    mod_all = _ada_call(jnp.concatenate([c_prompt, c_sample], axis=0), ada_w, ada_b)

    bias_t, bias_s = _bias_tables(rel_bias_table)

    w16 = w_in.astype(BF16)
    wu16 = w_up.astype(BF16)
    wd16 = w_down.astype(BF16)
    wo16 = w_out.astype(BF16)
    fn = final_norm_w.reshape(1, D_MODEL)
    ck = cache_win_k.reshape(DEPTH, DB, WINDOW, kw)
    cv = cache_win_v.reshape(DEPTH, DB, WINDOW, kw)
    sret = state_ret.reshape(DEPTH, DB, HB * DK_B, DV_B)
    sssm = state_ssm.reshape(DEPTH, DB, HC * HD_C, N_C)
    sconv = state_conv.reshape(DEPTH, DB, (CONV_W - 1) * CONV_DIM)

    xp = x_prompt
    xs = x_sample.reshape(DB, D_MODEL)
    outs_p = [[] for _ in range(5)]
    conv_s = []
    win_kv, ret_all, ssm_all = (), (), ()
    for l in range(DEPTH):
        wl = w16[l]
        wa, wb, wc = _prep_w_in(wl)
        n1 = norm1_w[l].reshape(1, D_MODEL)
        n2 = norm2_w[l].reshape(1, D_MODEL)
        cw = conv_w[l]
        cb = conv_b[l].reshape(1, CONV_DIM)
        dtb_pad = jnp.pad(dt_bias[l], (0, 128 - HC)).reshape(1, 128)
        a_pad = jnp.pad(-jnp.exp(A_log[l].astype(F32)), (0, 128 - HC)).reshape(1, 128)
        dsk_full = jnp.repeat(D_skip[l], HD_C).reshape(1, D_MODEL)
        snw = ssm_norm_w[l].reshape(1, D_MODEL)
        final = l == DEPTH - 1
        mod_p = mod_all[l, :B].reshape(B, 1, 6 * D_MODEL)
        mod_s = mod_all[l, B:]

        wqgt = jnp.concatenate([wl[:, O_AQ:O_AK] * (HD_A ** -0.5), wl[:, O_GTS:O_GTS + D_MODEL]], axis=1).T
        wkv = wl[:, O_AK:O_BQ]
        sink_rows = jnp.repeat(attn_sinks[l], WINDOW).reshape(HA_KV, 1, (HA_Q // HA_KV) * WINDOW)
        mixa, kbuf, vbuf = _pa_call(xp, mod_p, n1, wqgt, wkv, bias_t, sink_rows, attn_tile)
        wbt = jnp.concatenate([_deinterleave_pairs(wl[:, O_BQ:O_BK]),
                               _deinterleave_pairs(wl[:, O_BK:O_BV]) * (DK_B ** -0.5), wl[:, O_BV:O_BG],
                               wl[:, O_BG:O_CZ] * 0.5, wl[:, O_GTS + D_MODEL:O_GTS + 2 * D_MODEL] * 0.5], axis=1).T
        mixab, s_perm = _pb_call(xp, mod_p, n1, wbt, mixa, attn_tile)
        s_ret = s_perm.reshape(B, HB, 2, DK_B // 2, DV_B).transpose(0, 1, 3, 2, 4)
        x1, h_ssm, conv_new = _pc_call(xp, mod_p, n1, wc, cw, cb, dtb_pad, a_pad, dsk_full, snw,
                                       mixab, wo16[l], prompt_tile)
        xp = _mlp_call(x1.reshape(B * T, D_MODEL), mod_p, n2, wu16, wd16, fn, l,
                       mlp_rows, mlp_ff, T, final).reshape(B, T, D_MODEL)
        for lst, v in zip(outs_p, (kbuf.reshape(B, WINDOW, HA_KV, HD_A), vbuf.reshape(B, WINDOW, HA_KV, HD_A),
                                   s_ret.reshape(B, HB, DK_B, DV_B), h_ssm.reshape(B, HC, HD_C, N_C), conv_new)):
            lst.append(v)

        za, zb, zc = _sproj_call(xs, mod_s, n1, wa, wb, wc)
        q3 = za[:, 0:HA_Q * HD_A].reshape(DB, HA_Q, HD_A)
        kv = za[:, HA_Q * HD_A:HA_Q * HD_A + 2 * kw]
        ck_new, cv_new, oa3 = _sa_call(q3, kv, ck, cv, bias_s, attn_sinks[l].reshape(HA_Q, 1), l, win_kv)
        win_kv = (ck_new, cv_new)
        s_new, o3 = _sr_call(zb, sret, l, ret_all)
        ret_all = (s_new,)
        h_new, cs_new, y_s, xc_s = _ss_call(zc, sconv, sssm, cw, cb, dtb_pad, a_pad, l, ssm_all)
        ssm_all = (h_new,)
        xs = _sm_call(xs, mod_s, oa3.reshape(DB, D_MODEL), o3.reshape(DB, D_MODEL), y_s, xc_s,
                      za, zb, zc, dsk_full, snw, wo16[l], n2, wu16, wd16, fn, l, mlp_ff, final)
        conv_s.append(cs_new.reshape(DB, CONV_W - 1, CONV_DIM))

    stk = lambda lst: jnp.stack(lst, axis=0)
    return (xp, xs.reshape(DB, 1, D_MODEL),
            stk(outs_p[0]), stk(outs_p[1]), stk(outs_p[2]), stk(outs_p[3]), stk(outs_p[4]),
            win_kv[0].reshape(DEPTH, DB, WINDOW, HA_KV, HD_A), win_kv[1].reshape(DEPTH, DB, WINDOW, HA_KV, HD_A),
            ret_all[0].reshape(DEPTH, DB, HB, DK_B, DV_B), ssm_all[0].reshape(DEPTH, DB, HC, HD_C, N_C),
            stk(conv_s))


def kernel(x_prompt, x_sample, cache_win_k, cache_win_v, state_ret, state_ssm, state_conv, c_prompt, c_sample,
           rel_bias_table, attn_sinks, norm1_w, norm2_w, ada_w, ada_b, w_in, conv_w, conv_b, dt_bias, A_log,
           D_skip, ssm_norm_w, w_out, w_up, w_down, final_norm_w):
    return _forward(x_prompt, x_sample, cache_win_k, cache_win_v, state_ret, state_ssm, state_conv,
                    c_prompt, c_sample, rel_bias_table, attn_sinks, norm1_w, norm2_w, ada_w, ada_b,
                    w_in, conv_w, conv_b, dt_bias, A_log, D_skip, ssm_norm_w, w_out, w_up, w_down,
                    final_norm_w, prompt_tile=PROMPT_TILE, attn_tile=ATTN_TILE, mlp_rows=MLP_ROWS, mlp_ff=MLP_FF)
```
